```python
import math
import jax
import jax.numpy as jnp
from jax import lax
import numpy as np

D_MODEL = 1024
BATCH = 8
SEQ = 16384
DEPTH = 2

GRID_W = 64
CTX_LEN = 256
HEAD_DIM = 64
N_BRANCH = 4
BRANCH_W = D_MODEL // N_BRANCH
BLOCK = 128
WINDOW = 128
ROPE_THETA = 10000.0
EPS = 1e-6
D_FF = 2816
N_SUB = 3
POOL_WINDOWS = (2, 4, 8, 16)
POOL_GROUP = BRANCH_W // len(POOL_WINDOWS)
WIN_HEADS = BRANCH_W // HEAD_DIM
WIN_KV_HEADS = WIN_HEADS // 2
GLB_HEADS = BRANCH_W // HEAD_DIM
GLB_KV_HEADS = GLB_HEADS // 2
SSM_GROUP = 16
SSM_GROUPS = BRANCH_W // SSM_GROUP
SSM_STATE = 64

O_KB = 0
O_VB = O_KB + WIN_KV_HEADS * HEAD_DIM
O_UC = O_VB + WIN_KV_HEADS * HEAD_DIM
O_KD = O_UC + BRANCH_W
O_VD = O_KD + GLB_KV_HEADS * HEAD_DIM
CTX_COLS = O_VD + GLB_KV_HEADS * HEAD_DIM
O_QB = CTX_COLS
O_QD = O_QB + WIN_HEADS * HEAD_DIM
O_XA = O_QD + GLB_HEADS * HEAD_DIM
O_GATE = O_XA + BRANCH_W
IN_W = O_GATE + N_BRANCH * D_MODEL

kernel_name = 'hybrid_gated_pool_swa_s5_gqa_flow'


def rms_norm(x, g):
    x32 = x.astype(jnp.float32)
    y = x32 * lax.rsqrt(jnp.mean(x32 * x32, axis=-1, keepdims=True) + EPS)
    return (y * g.astype(jnp.float32)).astype(x.dtype)


def modulate(h, shift, scale):
    return h * (1.0 + scale) + shift


def swiglu(h, w_in, w_out):
    gate, up = jnp.split(h @ w_in, 2, axis=-1)
    return (jax.nn.silu(gate) * up) @ w_out


def split_heads(z, n_heads):
    return z.reshape(z.shape[:-1] + (n_heads, HEAD_DIM))


def rope_tables(rows):
    n_freq = HEAD_DIM // 4
    row = jnp.repeat(jnp.arange(rows), GRID_W)
    col = jnp.tile(jnp.arange(GRID_W), rows)
    inv = ROPE_THETA ** (-jnp.arange(n_freq, dtype=jnp.float32) / n_freq)
    ang = jnp.stack([row, col], axis=-1).astype(jnp.float32)[..., None] * inv
    return jnp.cos(ang), jnp.sin(ang)


def apply_rope(x, cos, sin):
    xs = x.astype(jnp.float32).reshape(x.shape[:-1] + (2, 2, HEAD_DIM // 4))
    x1, x2 = xs[..., 0, :], xs[..., 1, :]
    c = cos[None, :, None]
    s = sin[None, :, None]
    out = jnp.stack([x1 * c - x2 * s, x2 * c + x1 * s], axis=-2)
    return out.reshape(x.shape).astype(x.dtype)


def multi_scale_pool(xa, w_pool, pool_scale):
    b, n, _ = xa.shape
    xg = xa.astype(jnp.float32).reshape(b, n, len(POOL_WINDOWS), POOL_GROUP)
    csum = jnp.concatenate([jnp.zeros((b, 1, len(POOL_WINDOWS), POOL_GROUP), jnp.float32),
                            jnp.cumsum(xg, axis=1)], axis=1)
    t = jnp.arange(n)
    pooled = []
    for gi, w in enumerate(POOL_WINDOWS):
        lo = jnp.clip(t - w // 2, 0, n)
        hi = jnp.clip(t - w // 2 + w, 0, n)
        cg = csum[:, :, gi]
        pooled.append((cg[:, hi] - cg[:, lo]) / (hi - lo).astype(jnp.float32)[:, None])
    diff = jnp.stack(pooled, axis=2) - xg
    y = jnp.einsum('bngc,gcd->bngd', diff, w_pool.astype(jnp.float32))
    return (y.reshape(b, n, BRANCH_W) * pool_scale.astype(jnp.float32)).astype(xa.dtype)


def window_attention(q, k, v, kc, vc, sink):
    b, n, h, dh = q.shape
    kvh = k.shape[2]
    g = h // kvh
    nb = n // BLOCK
    n_ctx = kc.shape[1]
    qb = q.reshape(b, nb, BLOCK, kvh, g, dh)
    pad = ((0, 0), (BLOCK, BLOCK), (0, 0), (0, 0))

    def band(z):
        zp = jnp.pad(z, pad).reshape(b, nb + 2, BLOCK, kvh, dh)
        return jnp.concatenate([zp[:, :-2], zp[:, 1:-1], zp[:, 2:]], axis=2)

    kw, vw = band(k), band(v)
    scale = dh ** -0.5
    s_lat = jnp.einsum('bnqhgd,bnkhd->bnhgqk', qb, kw, preferred_element_type=jnp.float32) * scale
    rel = (jnp.arange(3 * BLOCK)[None, :] - BLOCK) - jnp.arange(BLOCK)[:, None]
    kpos = jnp.arange(nb)[:, None] * BLOCK - BLOCK + jnp.arange(3 * BLOCK)[None, :]
    mask = (jnp.abs(rel) <= WINDOW)[None] & ((kpos >= 0) & (kpos < n))[:, None, :]
    s_lat = jnp.where(mask[None, :, None, None], s_lat, -jnp.inf)
    s_ctx = jnp.einsum('bnqhgd,bchd->bnhgqc', qb, kc, preferred_element_type=jnp.float32) * scale
    sink_col = jnp.broadcast_to(sink.astype(jnp.float32).reshape(1, 1, kvh, g, 1, 1), s_lat.shape[:-1] + (1,))
    p = jax.nn.softmax(jnp.concatenate([s_lat, s_ctx, sink_col], axis=-1), axis=-1)
    p_lat = p[..., :3 * BLOCK].astype(v.dtype)
    p_ctx = p[..., 3 * BLOCK:3 * BLOCK + n_ctx].astype(v.dtype)
    o = (jnp.einsum('bnhgqk,bnkhd->bnqhgd', p_lat, vw)
         + jnp.einsum('bnhgqc,bchd->bnqhgd', p_ctx, vc))
    return o.reshape(b, n, h * dh)


def context_attention(q, k, v, sink):
    b, c, h, dh = q.shape
    kvh = k.shape[2]
    g = h // kvh
    s = jnp.einsum('bqhgd,bkhd->bhgqk', q.reshape(b, c, kvh, g, dh), k,
                   preferred_element_type=jnp.float32) * dh ** -0.5
    if sink is not None:
        sink_col = jnp.broadcast_to(sink.astype(jnp.float32).reshape(1, kvh, g, 1, 1), s.shape[:-1] + (1,))
        s = jnp.concatenate([s, sink_col], axis=-1)
    p = jax.nn.softmax(s, axis=-1)[..., :k.shape[1]]
    o = jnp.einsum('bhgqk,bkhd->bqhgd', p.astype(v.dtype), v)
    return o.reshape(b, c, h * dh)


def global_attention(q, k_all, v_all):
    b, n, h, dh = q.shape
    kvh = k_all.shape[2]
    g = h // kvh
    nb = n // BLOCK
    qb = jnp.moveaxis(q.reshape(b, nb, BLOCK, kvh, g, dh), 1, 0)
    scale = dh ** -0.5

    def one_block(qblk):
        s = jnp.einsum('bqhgd,bkhd->bhgqk', qblk, k_all, preferred_element_type=jnp.float32) * scale
        p = jax.nn.softmax(s, axis=-1)
        return jnp.einsum('bhgqk,bkhd->bqhgd', p.astype(v_all.dtype), v_all)

    o = lax.map(one_block, qb)
    return jnp.moveaxis(o, 0, 1).reshape(b, n, h * dh)


def s5_discretize(a_re, a_im, log_dt, b_re, b_im):
    lam = lax.complex(a_re.astype(jnp.float32), a_im.astype(jnp.float32))
    dt = jnp.exp(log_dt.astype(jnp.float32))[:, None]
    lam_bar = jnp.exp(lam * dt)
    b_mat = lax.complex(b_re.astype(jnp.float32), b_im.astype(jnp.float32))
    return lam_bar, ((lam_bar - 1.0) / lam)[..., None] * b_mat


def _ssm_combine(e_i, e_j):
    a_i, b_i = e_i
    a_j, b_j = e_j
    return a_j * a_i, a_j * b_i + b_j


def s5_scan(u, lam_bar, b_bar, h0):
    bu = jnp.einsum('btgh,gph->btgp', u.astype(jnp.complex64), b_bar)
    bu = bu.at[:, 0].add(lam_bar * h0)
    a = jnp.broadcast_to(lam_bar, bu.shape)
    _, states = lax.associative_scan(_ssm_combine, (a, bu), axis=1)
    return states


def s5_readout(states, c_re, c_im):
    return (jnp.einsum('btgp,ghp->btgh', states.real, c_re.astype(jnp.float32))
            - jnp.einsum('btgp,ghp->btgh', states.imag, c_im.astype(jnp.float32)))


def s5_output(y, u, d_skip, w_glu):
    b, n = y.shape[:2]
    y = (y + d_skip.astype(jnp.float32).reshape(SSM_GROUPS, SSM_GROUP) * u).reshape(b, n, BRANCH_W)
    z = jax.nn.gelu(y) @ w_glu.astype(jnp.float32)
    return z[..., :BRANCH_W] * jax.nn.sigmoid(z[..., BRANCH_W:])


def maybe_flip(z, rev):
    return jnp.flip(z, axis=1) if rev else z


def s5_branch(u_lat, u_ctx, a_re, a_im, log_dt, b_re, b_im, c_re, c_im, d_skip, w_glu, with_ctx_out):
    b, n, _ = u_lat.shape
    ul = u_lat.astype(jnp.float32).reshape(b, n, SSM_GROUPS, SSM_GROUP)
    uc = u_ctx.astype(jnp.float32).reshape(b, u_ctx.shape[1], SSM_GROUPS, SSM_GROUP)
    h_zero = jnp.zeros((b, SSM_GROUPS, SSM_STATE), jnp.complex64)
    ys_lat, ys_ctx = [], []
    for d in range(2):
        rev = d == 1
        lam_bar, b_bar = s5_discretize(a_re[d], a_im[d], log_dt[d], b_re[d], b_im[d])
        st_ctx = s5_scan(maybe_flip(uc, rev), lam_bar, b_bar, h_zero)
        st_lat = s5_scan(maybe_flip(ul, rev), lam_bar, b_bar, st_ctx[:, -1])
        ys_lat.append(maybe_flip(s5_readout(st_lat, c_re[d], c_im[d]), rev))
        if with_ctx_out:
            ys_ctx.append(maybe_flip(s5_readout(st_ctx, c_re[d], c_im[d]), rev))
    y_lat = s5_output(ys_lat[0] + ys_lat[1], ul, d_skip, w_glu).astype(u_lat.dtype)
    if not with_ctx_out:
        return y_lat, None
    y_ctx = s5_output(ys_ctx[0] + ys_ctx[1], uc, d_skip, w_glu).astype(u_ctx.dtype)
    return y_lat, y_ctx


def merge_branches(branches, gate_logits, w_branch, w_out):
    g = jax.nn.sigmoid(gate_logits.reshape(gate_logits.shape[:-1] + (N_BRANCH, D_MODEL)))
    terms = [g[..., k, :] * (y_k @ w_branch[k]) for k, y_k in enumerate(branches)]
    return sum(terms[1:], terms[0]) @ w_out


def token_mixer(h, hc, cos, sin, w_in, sink, qk_g, pool_w, pool_scale, a_re, a_im, log_dt,
                b_re, b_im, c_re, c_im, d_skip, w_glu, w_branch, w_out, with_ctx_out):
    p = h @ w_in
    pc = hc @ (w_in if with_ctx_out else w_in[:, :CTX_COLS])
    q_g, k_g = qk_g[0], qk_g[1]
    k_win_c = split_heads(pc[..., O_KB:O_VB], WIN_KV_HEADS)
    v_win_c = split_heads(pc[..., O_VB:O_UC], WIN_KV_HEADS)
    k_glb_c = rms_norm(split_heads(pc[..., O_KD:O_VD], GLB_KV_HEADS), k_g)
    v_glb_c = split_heads(pc[..., O_VD:CTX_COLS], GLB_KV_HEADS)
    y_a = multi_scale_pool(p[..., O_XA:O_GATE], pool_w, pool_scale)
    q_win = apply_rope(split_heads(p[..., O_QB:O_QD], WIN_HEADS), cos, sin)
    k_win = apply_rope(split_heads(p[..., O_KB:O_VB], WIN_KV_HEADS), cos, sin)
    v_win = split_heads(p[..., O_VB:O_UC], WIN_KV_HEADS)
    y_b = window_attention(q_win, k_win, v_win, k_win_c, v_win_c, sink)
    y_c, y_c_ctx = s5_branch(p[..., O_UC:O_KD], pc[..., O_UC:O_KD], a_re, a_im, log_dt,
                             b_re, b_im, c_re, c_im, d_skip, w_glu, with_ctx_out)
    q_glb = apply_rope(rms_norm(split_heads(p[..., O_QD:O_XA], GLB_HEADS), q_g), cos, sin)
    k_glb = apply_rope(rms_norm(split_heads(p[..., O_KD:O_VD], GLB_KV_HEADS), k_g), cos, sin)
    v_glb = split_heads(p[..., O_VD:CTX_COLS], GLB_KV_HEADS)
    y_d = global_attention(q_glb, jnp.concatenate([k_glb_c, k_glb], axis=1),
                           jnp.concatenate([v_glb_c, v_glb], axis=1))
    y = merge_branches((y_a, y_b, y_c, y_d), p[..., O_GATE:], w_branch, w_out)
    if not with_ctx_out:
        return y, None
    y_a_c = multi_scale_pool(pc[..., O_XA:O_GATE], pool_w, pool_scale)
    y_b_c = context_attention(split_heads(pc[..., O_QB:O_QD], WIN_HEADS), k_win_c, v_win_c, sink)
    q_glb_c = rms_norm(split_heads(pc[..., O_QD:O_XA], GLB_HEADS), q_g)
    y_d_c = context_attention(q_glb_c, k_glb_c, v_glb_c, None)
    y_ctx = merge_branches((y_a_c, y_b_c, y_c_ctx, y_d_c), pc[..., O_GATE:], w_branch, w_out)
    return y, y_ctx


def _fwd_setup_inputs(seed: int = 0) -> dict:
    key = jax.random.key(seed)
    ks = jax.random.split(key, 26)
    L = DEPTH
    G, P, H = SSM_GROUPS, SSM_STATE, SSM_GROUP

    def nrm(k, shape, s):
        return jax.random.normal(k, shape, jnp.float32) * s

    return {
        'x': nrm(ks[0], (BATCH, SEQ, D_MODEL), 1.0),
        'c': nrm(ks[1], (BATCH, D_MODEL), 1.0),
        'ctx': nrm(ks[2], (BATCH, CTX_LEN, D_MODEL), 1.0),
        'c_ctx': nrm(ks[3], (D_MODEL,), 1.0),
        'w_mod': nrm(ks[4], (L, D_MODEL, N_SUB * 3 * D_MODEL), 0.5 * D_MODEL ** -0.5),
        'b_mod': nrm(ks[5], (L, N_SUB * 3 * D_MODEL), 0.02),
        'norm_g': 1.0 + nrm(ks[6], (L, N_SUB, D_MODEL), 0.02),
        'ffn_in': nrm(ks[7], (L, 2, D_MODEL, 2 * D_FF), D_MODEL ** -0.5),
        'ffn_out': nrm(ks[8], (L, 2, D_FF, D_MODEL), D_FF ** -0.5),
        'w_in': nrm(ks[9], (L, D_MODEL, IN_W), D_MODEL ** -0.5),
        'win_sink': nrm(ks[10], (L, WIN_HEADS), 0.5),
        'qk_norm': 1.0 + nrm(ks[11], (L, 2, HEAD_DIM), 0.02),
        'pool_w': nrm(ks[12], (L, len(POOL_WINDOWS), POOL_GROUP, POOL_GROUP), POOL_GROUP ** -0.5),
        'pool_scale': 1.0 + nrm(ks[13], (L, BRANCH_W), 0.1),
        'ssm_a_re': -0.5 + nrm(ks[14], (L, 2, G, P), 0.01),
        'ssm_a_im': math.pi * jnp.arange(P, dtype=jnp.float32) + nrm(ks[15], (L, 2, G, P), 0.01),
        'ssm_log_dt': jax.random.uniform(ks[16], (L, 2, G), jnp.float32, math.log(1e-3), math.log(1e-1)),
        'ssm_b_re': nrm(ks[17], (L, 2, G, P, H), (2 * H) ** -0.5),
        'ssm_b_im': nrm(ks[18], (L, 2, G, P, H), (2 * H) ** -0.5),
        'ssm_c_re': nrm(ks[19], (L, 2, G, H, P), P ** -0.5),
        'ssm_c_im': nrm(ks[20], (L, 2, G, H, P), P ** -0.5),
        'ssm_d': nrm(ks[21], (L, BRANCH_W), 1.0),
        'glu_w': nrm(ks[22], (L, BRANCH_W, 2 * BRANCH_W), BRANCH_W ** -0.5),
        'branch_w': nrm(ks[23], (L, N_BRANCH, BRANCH_W, D_MODEL), BRANCH_W ** -0.5),
        'out_w': nrm(ks[24], (L, D_MODEL, D_MODEL), D_MODEL ** -0.5),
        'final_g': 1.0 + nrm(ks[25], (D_MODEL,), 0.02),
    }


def _fwd_reference(x, c, ctx, c_ctx, w_mod, b_mod, norm_g, ffn_in, ffn_out, w_in, win_sink, qk_norm,
              pool_w, pool_scale, ssm_a_re, ssm_a_im, ssm_log_dt, ssm_b_re, ssm_b_im, ssm_c_re,
              ssm_c_im, ssm_d, glu_w, branch_w, out_w, final_g):
    b = x.shape[0]
    rows = x.shape[1] // GRID_W
    cos, sin = rope_tables(rows)
    s_lat = jax.nn.silu(c)
    s_ctx = jax.nn.silu(c_ctx)
    for l in range(DEPTH):
        last = l == DEPTH - 1
        m = (s_lat @ w_mod[l] + b_mod[l]).reshape(b, N_SUB, 3, D_MODEL)[:, :, :, None, :]
        mc = (s_ctx @ w_mod[l] + b_mod[l]).reshape(N_SUB, 3, D_MODEL)
        h = modulate(rms_norm(x, norm_g[l, 0]), m[:, 0, 0], m[:, 0, 1])
        x = x + 0.5 * m[:, 0, 2] * swiglu(h, ffn_in[l, 0], ffn_out[l, 0])
        hc = modulate(rms_norm(ctx, norm_g[l, 0]), mc[0, 0], mc[0, 1])
        ctx = ctx + 0.5 * mc[0, 2] * swiglu(hc, ffn_in[l, 0], ffn_out[l, 0])
        h = modulate(rms_norm(x, norm_g[l, 1]), m[:, 1, 0], m[:, 1, 1])
        hc = modulate(rms_norm(ctx, norm_g[l, 1]), mc[1, 0], mc[1, 1])
        y, y_ctx = token_mixer(h, hc, cos, sin, w_in[l], win_sink[l], qk_norm[l], pool_w[l],
                               pool_scale[l], ssm_a_re[l], ssm_a_im[l], ssm_log_dt[l], ssm_b_re[l],
                               ssm_b_im[l], ssm_c_re[l], ssm_c_im[l], ssm_d[l], glu_w[l],
                               branch_w[l], out_w[l], not last)
        x = x + m[:, 1, 2] * y
        if not last:
            ctx = ctx + mc[1, 2] * y_ctx
        h = modulate(rms_norm(x, norm_g[l, 2]), m[:, 2, 0], m[:, 2, 1])
        x = x + 0.5 * m[:, 2, 2] * swiglu(h, ffn_in[l, 1], ffn_out[l, 1])
        if not last:
            hc = modulate(rms_norm(ctx, norm_g[l, 2]), mc[2, 0], mc[2, 1])
            ctx = ctx + 0.5 * mc[2, 2] * swiglu(hc, ffn_in[l, 1], ffn_out[l, 1])
    return rms_norm(x, final_g)


import jax as _jax
import jax.numpy as _jnp

TWIN_FORMAT = 'train_step'
FWD_PARAMS = ['x', 'c', 'ctx', 'c_ctx', 'w_mod', 'b_mod', 'norm_g', 'ffn_in', 'ffn_out', 'w_in', 'win_sink', 'qk_norm', 'pool_w', 'pool_scale', 'ssm_a_re', 'ssm_a_im', 'ssm_log_dt', 'ssm_b_re', 'ssm_b_im', 'ssm_c_re', 'ssm_c_im', 'ssm_d', 'glu_w', 'branch_w', 'out_w', 'final_g']
TWIN_WEIGHTS = ['c_ctx', 'w_mod', 'b_mod', 'norm_g', 'ffn_in', 'ffn_out', 'w_in', 'win_sink', 'qk_norm', 'pool_w', 'pool_scale', 'ssm_a_re', 'ssm_a_im', 'ssm_log_dt', 'ssm_b_re', 'ssm_b_im', 'ssm_c_re', 'ssm_c_im', 'ssm_d', 'glu_w', 'branch_w', 'out_w', 'final_g']
TWIN_DIFF_INPUT = 'x'
TWIN_INPUTS = ['x', 'c', 'ctx', 'c_ctx', 'w_mod', 'b_mod', 'norm_g', 'ffn_in', 'ffn_out', 'w_in', 'win_sink', 'qk_norm', 'pool_w', 'pool_scale', 'ssm_a_re', 'ssm_a_im', 'ssm_log_dt', 'ssm_b_re', 'ssm_b_im', 'ssm_c_re', 'ssm_c_im', 'ssm_d', 'glu_w', 'branch_w', 'out_w', 'final_g', 'loss_target', 'm_c_ctx', 'm_w_mod', 'm_b_mod', 'm_norm_g', 'm_ffn_in', 'm_ffn_out', 'm_w_in', 'm_win_sink', 'm_qk_norm', 'm_pool_w', 'm_pool_scale', 'm_ssm_a_re', 'm_ssm_a_im', 'm_ssm_log_dt', 'm_ssm_b_re', 'm_ssm_b_im', 'm_ssm_c_re', 'm_ssm_c_im', 'm_ssm_d', 'm_glu_w', 'm_branch_w', 'm_out_w', 'm_final_g', 'v_c_ctx', 'v_w_mod', 'v_b_mod', 'v_norm_g', 'v_ffn_in', 'v_ffn_out', 'v_w_in', 'v_win_sink', 'v_qk_norm', 'v_pool_w', 'v_pool_scale', 'v_ssm_a_re', 'v_ssm_a_im', 'v_ssm_log_dt', 'v_ssm_b_re', 'v_ssm_b_im', 'v_ssm_c_re', 'v_ssm_c_im', 'v_ssm_d', 'v_glu_w', 'v_branch_w', 'v_out_w', 'v_final_g']
TWIN_OUTPUTS = ['loss', 'grad_x', 'grad_c_ctx', 'grad_w_mod', 'grad_b_mod', 'grad_norm_g', 'grad_ffn_in', 'grad_ffn_out', 'grad_w_in', 'grad_win_sink', 'grad_qk_norm', 'grad_pool_w', 'grad_pool_scale', 'grad_ssm_a_re', 'grad_ssm_a_im', 'grad_ssm_log_dt', 'grad_ssm_b_re', 'grad_ssm_b_im', 'grad_ssm_c_re', 'grad_ssm_c_im', 'grad_ssm_d', 'grad_glu_w', 'grad_branch_w', 'grad_out_w', 'grad_final_g', 'delta_c_ctx', 'delta_w_mod', 'delta_b_mod', 'delta_norm_g', 'delta_ffn_in', 'delta_ffn_out', 'delta_w_in', 'delta_win_sink', 'delta_qk_norm', 'delta_pool_w', 'delta_pool_scale', 'delta_ssm_a_re', 'delta_ssm_a_im', 'delta_ssm_log_dt', 'delta_ssm_b_re', 'delta_ssm_b_im', 'delta_ssm_c_re', 'delta_ssm_c_im', 'delta_ssm_d', 'delta_glu_w', 'delta_branch_w', 'delta_out_w', 'delta_final_g', 'new_m_c_ctx', 'new_m_w_mod', 'new_m_b_mod', 'new_m_norm_g', 'new_m_ffn_in', 'new_m_ffn_out', 'new_m_w_in', 'new_m_win_sink', 'new_m_qk_norm', 'new_m_pool_w', 'new_m_pool_scale', 'new_m_ssm_a_re', 'new_m_ssm_a_im', 'new_m_ssm_log_dt', 'new_m_ssm_b_re', 'new_m_ssm_b_im', 'new_m_ssm_c_re', 'new_m_ssm_c_im', 'new_m_ssm_d', 'new_m_glu_w', 'new_m_branch_w', 'new_m_out_w', 'new_m_final_g', 'new_v_c_ctx', 'new_v_w_mod', 'new_v_b_mod', 'new_v_norm_g', 'new_v_ffn_in', 'new_v_ffn_out', 'new_v_w_in', 'new_v_win_sink', 'new_v_qk_norm', 'new_v_pool_w', 'new_v_pool_scale', 'new_v_ssm_a_re', 'new_v_ssm_a_im', 'new_v_ssm_log_dt', 'new_v_ssm_b_re', 'new_v_ssm_b_im', 'new_v_ssm_c_re', 'new_v_ssm_c_im', 'new_v_ssm_d', 'new_v_glu_w', 'new_v_branch_w', 'new_v_out_w', 'new_v_final_g']
TWIN_LEAF_KINDS = {'loss': 'loss', 'grad_x': 'grad_x', 'grad_c_ctx': 'grad_w', 'grad_w_mod': 'grad_w', 'grad_b_mod': 'grad_w', 'grad_norm_g': 'grad_w', 'grad_ffn_in': 'grad_w', 'grad_ffn_out': 'grad_w', 'grad_w_in': 'grad_w', 'grad_win_sink': 'grad_w', 'grad_qk_norm': 'grad_w', 'grad_pool_w': 'grad_w', 'grad_pool_scale': 'grad_w', 'grad_ssm_a_re': 'grad_w', 'grad_ssm_a_im': 'grad_w', 'grad_ssm_log_dt': 'grad_w', 'grad_ssm_b_re': 'grad_w', 'grad_ssm_b_im': 'grad_w', 'grad_ssm_c_re': 'grad_w', 'grad_ssm_c_im': 'grad_w', 'grad_ssm_d': 'grad_w', 'grad_glu_w': 'grad_w', 'grad_branch_w': 'grad_w', 'grad_out_w': 'grad_w', 'grad_final_g': 'grad_w', 'delta_c_ctx': 'delta_w', 'delta_w_mod': 'delta_w', 'delta_b_mod': 'delta_w', 'delta_norm_g': 'delta_w', 'delta_ffn_in': 'delta_w', 'delta_ffn_out': 'delta_w', 'delta_w_in': 'delta_w', 'delta_win_sink': 'delta_w', 'delta_qk_norm': 'delta_w', 'delta_pool_w': 'delta_w', 'delta_pool_scale': 'delta_w', 'delta_ssm_a_re': 'delta_w', 'delta_ssm_a_im': 'delta_w', 'delta_ssm_log_dt': 'delta_w', 'delta_ssm_b_re': 'delta_w', 'delta_ssm_b_im': 'delta_w', 'delta_ssm_c_re': 'delta_w', 'delta_ssm_c_im': 'delta_w', 'delta_ssm_d': 'delta_w', 'delta_glu_w': 'delta_w', 'delta_branch_w': 'delta_w', 'delta_out_w': 'delta_w', 'delta_final_g': 'delta_w', 'new_m_c_ctx': 'new_m', 'new_m_w_mod': 'new_m', 'new_m_b_mod': 'new_m', 'new_m_norm_g': 'new_m', 'new_m_ffn_in': 'new_m', 'new_m_ffn_out': 'new_m', 'new_m_w_in': 'new_m', 'new_m_win_sink': 'new_m', 'new_m_qk_norm': 'new_m', 'new_m_pool_w': 'new_m', 'new_m_pool_scale': 'new_m', 'new_m_ssm_a_re': 'new_m', 'new_m_ssm_a_im': 'new_m', 'new_m_ssm_log_dt': 'new_m', 'new_m_ssm_b_re': 'new_m', 'new_m_ssm_b_im': 'new_m', 'new_m_ssm_c_re': 'new_m', 'new_m_ssm_c_im': 'new_m', 'new_m_ssm_d': 'new_m', 'new_m_glu_w': 'new_m', 'new_m_branch_w': 'new_m', 'new_m_out_w': 'new_m', 'new_m_final_g': 'new_m', 'new_v_c_ctx': 'new_v', 'new_v_w_mod': 'new_v', 'new_v_b_mod': 'new_v', 'new_v_norm_g': 'new_v', 'new_v_ffn_in': 'new_v', 'new_v_ffn_out': 'new_v', 'new_v_w_in': 'new_v', 'new_v_win_sink': 'new_v', 'new_v_qk_norm': 'new_v', 'new_v_pool_w': 'new_v', 'new_v_pool_scale': 'new_v', 'new_v_ssm_a_re': 'new_v', 'new_v_ssm_a_im': 'new_v', 'new_v_ssm_log_dt': 'new_v', 'new_v_ssm_b_re': 'new_v', 'new_v_ssm_b_im': 'new_v', 'new_v_ssm_c_re': 'new_v', 'new_v_ssm_c_im': 'new_v', 'new_v_ssm_d': 'new_v', 'new_v_glu_w': 'new_v', 'new_v_branch_w': 'new_v', 'new_v_out_w': 'new_v', 'new_v_final_g': 'new_v'}


def _forward(args):
    return _fwd_reference(*[args[k] for k in FWD_PARAMS])


def _output_shape():
    def fwd():
        inp = _fwd_setup_inputs(0)
        return _fwd_reference(*[inp[k] for k in FWD_PARAMS])
    out = _jax.eval_shape(fwd)
    return out.shape, out.dtype

N_MICROBATCH = 1
ADAM_LR = 0.001
ADAM_B1 = 0.9
ADAM_B2 = 0.999
ADAM_EPS = 1e-08
ADAM_WD = 0.01
ADAM_STEP = 10
PER_EXAMPLE_BATCH_AXIS = {'x': 0, 'c': 0, 'ctx': 0, 'loss_target': 0}
SHARED_INPUTS = []
_WEIGHT_DTYPES = {'c_ctx': _jnp.float32, 'w_mod': _jnp.float32, 'b_mod': _jnp.float32, 'norm_g': _jnp.float32, 'ffn_in': _jnp.float32, 'ffn_out': _jnp.float32, 'w_in': _jnp.float32, 'win_sink': _jnp.float32, 'qk_norm': _jnp.float32, 'pool_w': _jnp.float32, 'pool_scale': _jnp.float32, 'ssm_a_re': _jnp.float32, 'ssm_a_im': _jnp.float32, 'ssm_log_dt': _jnp.float32, 'ssm_b_re': _jnp.float32, 'ssm_b_im': _jnp.float32, 'ssm_c_re': _jnp.float32, 'ssm_c_im': _jnp.float32, 'ssm_d': _jnp.float32, 'glu_w': _jnp.float32, 'branch_w': _jnp.float32, 'out_w': _jnp.float32, 'final_g': _jnp.float32}
MOMENT_SCALE = {'c_ctx': 2.046932e-02, 'w_mod': 6.474186e-02, 'b_mod': 1.114743e-01, 'norm_g': 5.727264e-02, 'ffn_in': 2.305874e-02, 'ffn_out': 3.763446e-02, 'w_in': 2.875976e-02, 'win_sink': 5.716368e-04, 'qk_norm': 1.798577e-02, 'pool_w': 1.048226e-01, 'pool_scale': 1.064226e-01, 'ssm_a_re': 3.694902e-03, 'ssm_a_im': 3.647032e-03, 'ssm_log_dt': 2.460012e+00, 'ssm_b_re': 2.510683e-03, 'ssm_b_im': 2.334273e-03, 'ssm_c_re': 3.694983e-03, 'ssm_c_im': 3.691555e-03, 'ssm_d': 4.842441e-02, 'glu_w': 3.270096e-02, 'branch_w': 3.085518e-02, 'out_w': 6.170070e-02, 'final_g': 1.278820e+02}


def _to_microbatches(a, axis):
    t = _jnp.moveaxis(a, axis, 0)
    t = t.reshape((N_MICROBATCH, t.shape[0] // N_MICROBATCH) + t.shape[1:])
    return _jnp.moveaxis(t, 1, axis + 1)


def setup_inputs(seed: int = 0) -> dict:
    inp = _fwd_setup_inputs(seed)
    key = _jax.random.fold_in(_jax.random.key(seed), 7919)
    shape, _ = _output_shape()
    out = dict(inp)
    out["loss_target"] = _jax.random.normal(_jax.random.fold_in(key, 0), shape, _jnp.float32)
    for i, name in enumerate(TWIN_WEIGHTS):
        w = inp[name].astype(_jnp.float32)
        if MOMENT_SCALE is None:
            s = _jnp.sqrt(_jnp.mean(_jnp.square(w)) + 1e-30)
        else:
            s = MOMENT_SCALE[name]
        km, kv = _jax.random.split(_jax.random.fold_in(key, i + 1))
        out[name] = w
        out["m_" + name] = s * _jax.random.normal(km, w.shape, _jnp.float32)
        out["v_" + name] = (s * s) * _jax.random.uniform(kv, w.shape, _jnp.float32, 0.5, 1.5)
    if N_MICROBATCH > 1:
        for name, axis in PER_EXAMPLE_BATCH_AXIS.items():
            out[name] = _to_microbatches(out[name], axis)
    return {'x': out['x'], 'c': out['c'], 'ctx': out['ctx'], 'c_ctx': out['c_ctx'], 'w_mod': out['w_mod'], 'b_mod': out['b_mod'], 'norm_g': out['norm_g'], 'ffn_in': out['ffn_in'], 'ffn_out': out['ffn_out'], 'w_in': out['w_in'], 'win_sink': out['win_sink'], 'qk_norm': out['qk_norm'], 'pool_w': out['pool_w'], 'pool_scale': out['pool_scale'], 'ssm_a_re': out['ssm_a_re'], 'ssm_a_im': out['ssm_a_im'], 'ssm_log_dt': out['ssm_log_dt'], 'ssm_b_re': out['ssm_b_re'], 'ssm_b_im': out['ssm_b_im'], 'ssm_c_re': out['ssm_c_re'], 'ssm_c_im': out['ssm_c_im'], 'ssm_d': out['ssm_d'], 'glu_w': out['glu_w'], 'branch_w': out['branch_w'], 'out_w': out['out_w'], 'final_g': out['final_g'], 'loss_target': out['loss_target'], 'm_c_ctx': out['m_c_ctx'], 'm_w_mod': out['m_w_mod'], 'm_b_mod': out['m_b_mod'], 'm_norm_g': out['m_norm_g'], 'm_ffn_in': out['m_ffn_in'], 'm_ffn_out': out['m_ffn_out'], 'm_w_in': out['m_w_in'], 'm_win_sink': out['m_win_sink'], 'm_qk_norm': out['m_qk_norm'], 'm_pool_w': out['m_pool_w'], 'm_pool_scale': out['m_pool_scale'], 'm_ssm_a_re': out['m_ssm_a_re'], 'm_ssm_a_im': out['m_ssm_a_im'], 'm_ssm_log_dt': out['m_ssm_log_dt'], 'm_ssm_b_re': out['m_ssm_b_re'], 'm_ssm_b_im': out['m_ssm_b_im'], 'm_ssm_c_re': out['m_ssm_c_re'], 'm_ssm_c_im': out['m_ssm_c_im'], 'm_ssm_d': out['m_ssm_d'], 'm_glu_w': out['m_glu_w'], 'm_branch_w': out['m_branch_w'], 'm_out_w': out['m_out_w'], 'm_final_g': out['m_final_g'], 'v_c_ctx': out['v_c_ctx'], 'v_w_mod': out['v_w_mod'], 'v_b_mod': out['v_b_mod'], 'v_norm_g': out['v_norm_g'], 'v_ffn_in': out['v_ffn_in'], 'v_ffn_out': out['v_ffn_out'], 'v_w_in': out['v_w_in'], 'v_win_sink': out['v_win_sink'], 'v_qk_norm': out['v_qk_norm'], 'v_pool_w': out['v_pool_w'], 'v_pool_scale': out['v_pool_scale'], 'v_ssm_a_re': out['v_ssm_a_re'], 'v_ssm_a_im': out['v_ssm_a_im'], 'v_ssm_log_dt': out['v_ssm_log_dt'], 'v_ssm_b_re': out['v_ssm_b_re'], 'v_ssm_b_im': out['v_ssm_b_im'], 'v_ssm_c_re': out['v_ssm_c_re'], 'v_ssm_c_im': out['v_ssm_c_im'], 'v_ssm_d': out['v_ssm_d'], 'v_glu_w': out['v_glu_w'], 'v_branch_w': out['v_branch_w'], 'v_out_w': out['v_out_w'], 'v_final_g': out['v_final_g']}


def _loss(weights, diff, rest, loss_target):
    with _jax.named_scope("forward"):
        args = {**rest, TWIN_DIFF_INPUT: diff, **{k: w.astype(_WEIGHT_DTYPES[k]) for k, w in weights.items()}}
        y = _forward(args)
    with _jax.named_scope("loss_head"):
        err = _jnp.square(y.astype(_jnp.float32) - loss_target)
        return 0.5 * _jnp.sum(_jnp.mean(err, axis=-1)) if err.ndim else 0.5 * err


def _adamw(w, g, m, v):
    m = ADAM_B1 * m + (1.0 - ADAM_B1) * g
    v = ADAM_B2 * v + (1.0 - ADAM_B2) * _jnp.square(g)
    m_hat = m / (1.0 - ADAM_B1 ** ADAM_STEP)
    v_hat = v / (1.0 - ADAM_B2 ** ADAM_STEP)
    delta = -ADAM_LR * (m_hat / (_jnp.sqrt(v_hat) + ADAM_EPS) + ADAM_WD * w)
    return delta, m, v


def reference(x, c, ctx, c_ctx, w_mod, b_mod, norm_g, ffn_in, ffn_out, w_in, win_sink, qk_norm, pool_w, pool_scale, ssm_a_re, ssm_a_im, ssm_log_dt, ssm_b_re, ssm_b_im, ssm_c_re, ssm_c_im, ssm_d, glu_w, branch_w, out_w, final_g, loss_target, m_c_ctx, m_w_mod, m_b_mod, m_norm_g, m_ffn_in, m_ffn_out, m_w_in, m_win_sink, m_qk_norm, m_pool_w, m_pool_scale, m_ssm_a_re, m_ssm_a_im, m_ssm_log_dt, m_ssm_b_re, m_ssm_b_im, m_ssm_c_re, m_ssm_c_im, m_ssm_d, m_glu_w, m_branch_w, m_out_w, m_final_g, v_c_ctx, v_w_mod, v_b_mod, v_norm_g, v_ffn_in, v_ffn_out, v_w_in, v_win_sink, v_qk_norm, v_pool_w, v_pool_scale, v_ssm_a_re, v_ssm_a_im, v_ssm_log_dt, v_ssm_b_re, v_ssm_b_im, v_ssm_c_re, v_ssm_c_im, v_ssm_d, v_glu_w, v_branch_w, v_out_w, v_final_g):
    given = dict(x=x, c=c, ctx=ctx, c_ctx=c_ctx, w_mod=w_mod, b_mod=b_mod, norm_g=norm_g, ffn_in=ffn_in, ffn_out=ffn_out, w_in=w_in, win_sink=win_sink, qk_norm=qk_norm, pool_w=pool_w, pool_scale=pool_scale, ssm_a_re=ssm_a_re, ssm_a_im=ssm_a_im, ssm_log_dt=ssm_log_dt, ssm_b_re=ssm_b_re, ssm_b_im=ssm_b_im, ssm_c_re=ssm_c_re, ssm_c_im=ssm_c_im, ssm_d=ssm_d, glu_w=glu_w, branch_w=branch_w, out_w=out_w, final_g=final_g, loss_target=loss_target, m_c_ctx=m_c_ctx, m_w_mod=m_w_mod, m_b_mod=m_b_mod, m_norm_g=m_norm_g, m_ffn_in=m_ffn_in, m_ffn_out=m_ffn_out, m_w_in=m_w_in, m_win_sink=m_win_sink, m_qk_norm=m_qk_norm, m_pool_w=m_pool_w, m_pool_scale=m_pool_scale, m_ssm_a_re=m_ssm_a_re, m_ssm_a_im=m_ssm_a_im, m_ssm_log_dt=m_ssm_log_dt, m_ssm_b_re=m_ssm_b_re, m_ssm_b_im=m_ssm_b_im, m_ssm_c_re=m_ssm_c_re, m_ssm_c_im=m_ssm_c_im, m_ssm_d=m_ssm_d, m_glu_w=m_glu_w, m_branch_w=m_branch_w, m_out_w=m_out_w, m_final_g=m_final_g, v_c_ctx=v_c_ctx, v_w_mod=v_w_mod, v_b_mod=v_b_mod, v_norm_g=v_norm_g, v_ffn_in=v_ffn_in, v_ffn_out=v_ffn_out, v_w_in=v_w_in, v_win_sink=v_win_sink, v_qk_norm=v_qk_norm, v_pool_w=v_pool_w, v_pool_scale=v_pool_scale, v_ssm_a_re=v_ssm_a_re, v_ssm_a_im=v_ssm_a_im, v_ssm_log_dt=v_ssm_log_dt, v_ssm_b_re=v_ssm_b_re, v_ssm_b_im=v_ssm_b_im, v_ssm_c_re=v_ssm_c_re, v_ssm_c_im=v_ssm_c_im, v_ssm_d=v_ssm_d, v_glu_w=v_glu_w, v_branch_w=v_branch_w, v_out_w=v_out_w, v_final_g=v_final_g)
    weights = {n: given[n] for n in TWIN_WEIGHTS}
    shared = {n: given[n] for n in SHARED_INPUTS}
    per_example = {n: given[n] for n in ['x', 'c', 'ctx']}
    grad_fn = _jax.value_and_grad(_loss, argnums=(0, 1))

    def one_microbatch(ex, loss_target):
        ex = dict(ex)
        diff = ex.pop(TWIN_DIFF_INPUT)
        return grad_fn(weights, diff, {**shared, **ex}, loss_target)

    if N_MICROBATCH == 1:
        loss, (grad_w, grad_x) = one_microbatch(per_example, given["loss_target"])
    else:
        def body(carry, xs):
            loss_sum, grad_sum = carry
            l_k, (gw_k, gx_k) = one_microbatch(xs[0], xs[1])
            with _jax.named_scope("update"):
                return (loss_sum + l_k, _jax.tree.map(_jnp.add, grad_sum, gw_k)), gx_k

        init = (_jnp.zeros((), _jnp.float32), _jax.tree.map(_jnp.zeros_like, weights))
        (loss, grad_w), grad_x = _jax.lax.scan(body, init, (per_example, given["loss_target"]))
    with _jax.named_scope("update"):
        delta_w, new_m, new_v = {}, {}, {}
        for n in TWIN_WEIGHTS:
            delta_w[n], new_m[n], new_v[n] = _adamw(weights[n], grad_w[n], given["m_" + n], given["v_" + n])
    return (loss, grad_x, *[grad_w[n] for n in TWIN_WEIGHTS], *[delta_w[n] for n in TWIN_WEIGHTS],
            *[new_m[n] for n in TWIN_WEIGHTS], *[new_v[n] for n in TWIN_WEIGHTS])
```

```python
import functools
import math

import jax
import jax.numpy as jnp
from jax import lax
from jax.experimental import pallas as pl
from jax.experimental.pallas import tpu as pltpu

F32 = jnp.float32
BF16 = jnp.bfloat16

D_MODEL = 1024
GRID_W = 64
HEAD_DIM = 64
N_BRANCH = 4
BRANCH_W = D_MODEL // N_BRANCH
WINDOW = 128
ROPE_THETA = 10000.0
EPS = 1e-6
D_FF = 2816
N_SUB = 3
POOL_WINDOWS = (2, 4, 8, 16)
POOL_GROUP = BRANCH_W // len(POOL_WINDOWS)
KV_HEADS = 2
Q_PER_KV = 2
SSM_GROUP = 16
SSM_GROUPS = BRANCH_W // SSM_GROUP
SSM_STATE = 64
SSM_LANES = SSM_GROUPS * SSM_STATE
O_KB, O_VB, O_UC, O_KD, O_VD, CTX_COLS = 0, 128, 256, 512, 640, 768
O_QB, O_QD, O_XA, O_GATE = 768, 1024, 1280, 1536
IN_W = O_GATE + N_BRANCH * D_MODEL

ADAM_LR, ADAM_B1, ADAM_B2, ADAM_EPS, ADAM_WD, ADAM_STEP = 0.001, 0.9, 0.999, 1e-08, 0.01, 10

N_DEV = 8
MESH = pl.DeviceIdType.MESH

V7X_VMEM_BYTES = 64 * 1024 * 1024
SUBLANES = 8
LANES = 128
NEG_BIG = -1e30
COND_ROWS = 128

SHARDED = ("w_mod", "ffn_in", "ffn_out", "w_in", "glu_w", "branch_w", "out_w")
SHARD_AXIS = {"w_mod": 2, "ffn_in": 3, "ffn_out": 2, "w_in": 2, "glu_w": 2, "branch_w": 3, "out_w": 1}
SMALL = ("c_ctx", "b_mod", "win_sink", "qk_norm", "pool_w", "pool_scale", "ssm_a_re", "ssm_a_im",
         "ssm_log_dt", "ssm_b_re", "ssm_b_im", "ssm_c_re", "ssm_c_im", "ssm_d", "final_g")
WEIGHTS = ("c_ctx", "w_mod", "b_mod", "norm_g", "ffn_in", "ffn_out", "w_in", "win_sink", "qk_norm",
           "pool_w", "pool_scale", "ssm_a_re", "ssm_a_im", "ssm_log_dt", "ssm_b_re", "ssm_b_im",
           "ssm_c_re", "ssm_c_im", "ssm_d", "glu_w", "branch_w", "out_w", "final_g")


def _tile(n, cap, mult):
    if n <= cap:
        return n
    t = (cap // mult) * mult
    while t >= mult:
        if n % t == 0:
            return t
        t -= mult
    return n


def _cparams(sem, tile_bytes):
    limit = int(min(V7X_VMEM_BYTES - 8 * 2 ** 20, max(32 * 2 ** 20, 3 * tile_bytes + 8 * 2 ** 20)))
    return pltpu.CompilerParams(dimension_semantics=sem, vmem_limit_bytes=limit)


def _mm(a, b, ta=False, tb=False, name="mm"):
    m, k = (a.shape[1], a.shape[0]) if ta else a.shape
    n = b.shape[0] if tb else b.shape[1]
    assert (b.shape[1] if tb else b.shape[0]) == k
    tm, tn, tk = _tile(m, 1024, LANES), _tile(n, 512, LANES), _tile(k, 1536, LANES)
    nk = k // tk
    dims = (((0 if ta else 1,), (1 if tb else 0,)), ((), ()))

    def body(a_ref, b_ref, o_ref, acc_ref):
        kk = pl.program_id(2)

        @pl.when(kk == 0)
        def _():
            acc_ref[...] = jnp.zeros_like(acc_ref)

        acc_ref[...] += lax.dot_general(a_ref[...].astype(BF16), b_ref[...].astype(BF16), dims,
                                        preferred_element_type=F32)

        @pl.when(kk == nk - 1)
        def _():
            o_ref[...] = acc_ref[...]

    a_spec = (pl.BlockSpec((tk, tm), lambda i, j, kk: (kk, i)) if ta
              else pl.BlockSpec((tm, tk), lambda i, j, kk: (i, kk)))
    b_spec = (pl.BlockSpec((tn, tk), lambda i, j, kk: (j, kk)) if tb
              else pl.BlockSpec((tk, tn), lambda i, j, kk: (kk, j)))
    tile_bytes = 4 * (tm * tk + tk * tn + 2 * tm * tn)
    return pl.pallas_call(
        body, name=name, grid=(m // tm, n // tn, nk),
        in_specs=[a_spec, b_spec], out_specs=pl.BlockSpec((tm, tn), lambda i, j, kk: (i, j)),
        out_shape=jax.ShapeDtypeStruct((m, n), F32),
        scratch_shapes=[pltpu.VMEM((tm, tn), F32)],
        compiler_params=_cparams(("parallel", "parallel", "arbitrary"), tile_bytes),
    )(a, b)


@jax.custom_vjp
def linear(x, w):
    return _mm(x, w, name="linear_fwd")


def _linear_fwd(x, w):
    return linear(x, w), (x, w)


def _linear_bwd(res, dy):
    x, w = res
    return _mm(dy, w, tb=True, name="linear_dx"), _mm(x, dy, ta=True, name="linear_dw")


linear.defvjp(_linear_fwd, _linear_bwd)


ROW_TILE_BYTES = 6 * 2 ** 20


def _row_tile(t, cols):
    tm = 1024
    while tm > SUBLANES and 4 * tm * cols > ROW_TILE_BYTES:
        tm //= 2
    return _tile(t, tm, SUBLANES)


def _rowwise_fwd_call(fn, rows, params, name):
    t = rows[0].shape[0]
    out_avals = jax.eval_shape(fn, *rows, *params)
    cols = sum(r.shape[1] for r in rows) + sum(o.shape[1] for o in out_avals)
    tm = _row_tile(t, cols)
    nr, npar = len(rows), len(params)

    def body(*refs):
        ins = [r[...] for r in refs[:nr + npar]]
        outs = fn(*ins)
        for o_ref, o in zip(refs[nr + npar:], outs):
            o_ref[...] = o.astype(o_ref.dtype)

    in_specs = ([pl.BlockSpec((tm, r.shape[1]), lambda i: (i, 0)) for r in rows]
                + [pl.BlockSpec(p.shape, lambda i: (0, 0)) for p in params])
    out_specs = [pl.BlockSpec((tm, o.shape[1]), lambda i: (i, 0)) for o in out_avals]
    return pl.pallas_call(
        body, name=name, grid=(t // tm,), in_specs=in_specs, out_specs=out_specs,
        out_shape=[jax.ShapeDtypeStruct((t, o.shape[1]), o.dtype) for o in out_avals],
        compiler_params=_cparams(("parallel",), 4 * tm * cols),
    )(*rows, *params)


def _rowwise_bwd_call(fn, rows, params, cts, name):
    t = rows[0].shape[0]
    cols = 2 * sum(r.shape[1] for r in rows) + sum(c.shape[1] for c in cts)
    tm = _row_tile(t, cols)
    nr, npar, nct = len(rows), len(params), len(cts)

    def body(*refs):
        ins = [r[...] for r in refs[:nr + npar]]
        ct = tuple(r[...] for r in refs[nr + npar:nr + npar + nct])
        d_refs = refs[nr + npar + nct:]
        _, vjp = jax.vjp(fn, *ins)
        grads = vjp(ct)
        for r, g in zip(d_refs[:nr], grads[:nr]):
            r[...] = g

        @pl.when(pl.program_id(0) == 0)
        def _():
            for r in d_refs[nr:]:
                r[...] = jnp.zeros_like(r)

        for r, g in zip(d_refs[nr:], grads[nr:]):
            r[...] += g

    in_specs = ([pl.BlockSpec((tm, r.shape[1]), lambda i: (i, 0)) for r in rows]
                + [pl.BlockSpec(p.shape, lambda i: (0, 0)) for p in params]
                + [pl.BlockSpec((tm, c.shape[1]), lambda i: (i, 0)) for c in cts])
    out_specs = ([pl.BlockSpec((tm, r.shape[1]), lambda i: (i, 0)) for r in rows]
                 + [pl.BlockSpec(p.shape, lambda i: (0, 0)) for p in params])
    return pl.pallas_call(
        body, name=name, grid=(t // tm,), in_specs=in_specs, out_specs=out_specs,
        out_shape=[jax.ShapeDtypeStruct(a.shape, F32) for a in (*rows, *params)],
        compiler_params=_cparams(("arbitrary",), 4 * tm * cols),
    )(*rows, *params, *cts)


def rowwise(fn, n_rows, name):
    @jax.custom_vjp
    def op(*args):
        return tuple(_rowwise_fwd_call(fn, args[:n_rows], args[n_rows:], name + "_fwd"))

    def fwd(*args):
        return op(*args), args

    def bwd(args, cts):
        return tuple(_rowwise_bwd_call(fn, args[:n_rows], args[n_rows:], cts, name + "_bwd"))

    op.defvjp(fwd, bwd)
    return op


@functools.partial(jax.custom_vjp, nondiff_argnums=(1,))
def _swap_lanes(x, k):
    n = x.shape[-1]
    lane = lax.broadcasted_iota(jnp.int32, x.shape, x.ndim - 1)
    return jnp.where((lane & k) == 0, pltpu.roll(x, n - k, x.ndim - 1), pltpu.roll(x, k, x.ndim - 1))


def _swap_lanes_fwd(x, k):
    return _swap_lanes(x, k), None


def _swap_lanes_bwd(k, _, g):
    return (_swap_lanes(g, k),)


_swap_lanes.defvjp(_swap_lanes_fwd, _swap_lanes_bwd)


def _head_sum(x):
    s = x
    k = 1
    while k < HEAD_DIM:
        s = s + _swap_lanes(s, k)
        k *= 2
    return s


def _rms(x):
    return x * lax.rsqrt(jnp.mean(x * x, axis=-1, keepdims=True) + EPS)


def _norm_mod_fn(x, g, shift, scale):
    return ((_rms(x) * g) * (1.0 + scale) + shift,)


def _swiglu_fn(u):
    gate, up = u[:, :D_FF], u[:, D_FF:]
    return (jax.nn.silu(gate) * up,)


def _resid_fn(coef, x, y, gate):
    return (x + (coef * gate) * y,)


def _scale_fn(y, s):
    return (y * s,)


def _tile_lanes(tab, width):
    return tab if tab.shape[1] == width else jnp.concatenate([tab] * (width // tab.shape[1]), axis=1)


def _rope_fn(x, cos, sin):
    w = x.shape[1]
    return (x * _tile_lanes(cos, w) + _swap_lanes(x, 16) * _tile_lanes(sin, w),)


def _head_norm(x, g):
    ms = _head_sum(x * x) * (1.0 / HEAD_DIM)
    return x * lax.rsqrt(ms + EPS) * g


def _norm_rope_fn(x, cos, sin, g):
    return _rope_fn(_head_norm(x, g), cos, sin)


def _head_norm_fn(x, g):
    return (_head_norm(x, g),)


def _merge_fn(gl, z0, z1, z2, z3):
    zs = (z0, z1, z2, z3)
    terms = [jax.nn.sigmoid(gl[:, k * D_MODEL:(k + 1) * D_MODEL]) * zs[k] for k in range(N_BRANCH)]
    return (sum(terms[1:], terms[0]),)


def _s5_pre_fn(y0r, y0i, y1r, y1i, u, d):
    return (jax.nn.gelu(((y0r - y0i) + (y1r - y1i)) + d * u),)


def _glu_fn(z):
    return (z[:, :BRANCH_W] * jax.nn.sigmoid(z[:, BRANCH_W:]),)


def _silu_fn(x):
    return (jax.nn.silu(x),)


def _loss_fn(x, tgt, g):
    err = jnp.square(_rms(x) * g - tgt)
    return (0.5 * jnp.mean(err, axis=-1, keepdims=True),)


norm_mod = rowwise(_norm_mod_fn, 1, "norm_mod")
swiglu_act = rowwise(_swiglu_fn, 1, "swiglu")
resid_half = rowwise(functools.partial(_resid_fn, 0.5), 2, "resid_half")
resid_full = rowwise(functools.partial(_resid_fn, 1.0), 2, "resid_full")
scale_rows = rowwise(_scale_fn, 1, "pool_scale")
rope = rowwise(_rope_fn, 3, "rope")
norm_rope = rowwise(_norm_rope_fn, 3, "norm_rope")
head_norm = rowwise(_head_norm_fn, 1, "head_norm")
merge = rowwise(_merge_fn, 5, "merge")
s5_pre = rowwise(_s5_pre_fn, 5, "s5_pre")
glu = rowwise(_glu_fn, 1, "glu")
silu_rows = rowwise(_silu_fn, 1, "silu")
loss_rows = rowwise(_loss_fn, 2, "loss_head")


POOL_HALO = 16


def _pool_call(xa, adjoint, name):
    n, width = xa.shape
    tm = _tile(n, 512, POOL_HALO)
    halo_blocks = tm // POOL_HALO
    last_halo = n // POOL_HALO - 1
    ext_rows = tm + 2 * POOL_HALO

    def body(prev_ref, cur_ref, next_ref, o_ref, ext_ref):
        i = pl.program_id(0)
        ext_ref[0:POOL_HALO] = prev_ref[...]
        ext_ref[POOL_HALO:POOL_HALO + tm] = cur_ref[...]
        ext_ref[POOL_HALO + tm:ext_rows] = next_ref[...]
        e = ext_ref[...]
        row = lax.broadcasted_iota(jnp.int32, e.shape, 0) + (i * tm - POOL_HALO)
        grp = lax.broadcasted_iota(jnp.int32, e.shape, 1) // POOL_GROUP
        win = jnp.where(grp == 0, POOL_WINDOWS[0],
                        jnp.where(grp == 1, POOL_WINDOWS[1], jnp.where(grp == 2, POOL_WINDOWS[2], POOL_WINDOWS[3])))
        valid = (row >= 0) & (row < n)
        lo = jnp.clip(row - win // 2, 0, n)
        hi = jnp.clip(row - win // 2 + win, 0, n)
        cnt = jnp.maximum((hi - lo).astype(F32), 1.0)
        e0 = jnp.where(valid, e / cnt if adjoint else e, 0.0)

        def shift(z, s):
            return pltpu.roll(z, s % ext_rows, 0)

        s2 = e0 + shift(e0, -1 if adjoint else 1)
        s4 = shift(s2, 1) + shift(s2, -1)
        s8 = shift(s4, 2) + shift(s4, -2)
        s16 = shift(s8, 4) + shift(s8, -4)
        s = jnp.where(grp == 0, s2, jnp.where(grp == 1, s4, jnp.where(grp == 2, s8, s16)))
        out = (s - e) if adjoint else (s / cnt - e)
        o_ref[...] = out[POOL_HALO:POOL_HALO + tm]

    return pl.pallas_call(
        body, name=name, grid=(n // tm,),
        in_specs=[pl.BlockSpec((POOL_HALO, width), lambda i: (jnp.maximum(i * halo_blocks - 1, 0), 0)),
                  pl.BlockSpec((tm, width), lambda i: (i, 0)),
                  pl.BlockSpec((POOL_HALO, width), lambda i: (jnp.minimum((i + 1) * halo_blocks, last_halo), 0))],
        out_specs=pl.BlockSpec((tm, width), lambda i: (i, 0)),
        out_shape=jax.ShapeDtypeStruct((n, width), F32),
        scratch_shapes=[pltpu.VMEM((ext_rows, width), F32)],
        compiler_params=_cparams(("parallel",), 4 * 4 * ext_rows * width),
    )(xa, xa, xa)


@jax.custom_vjp
def pool_diff(xa):
    return _pool_call(xa, False, "pool_fwd")


pool_diff.defvjp(lambda xa: (pool_diff(xa), None), lambda _, g: (_pool_call(g, True, "pool_bwd"),))


ATT_SCALE = HEAD_DIM ** -0.5


def _attn_blocks(t, band):
    if band:
        b = _tile(t, 256, LANES)
        return b, b
    return _tile(t, 512, LANES), _tile(t, 1024, LANES)


def _qk_scores(q, k):
    return lax.dot_general(q.astype(BF16), k.astype(BF16), (((1,), (1,)), ((), ())),
                           preferred_element_type=F32) * ATT_SCALE


def _band_mask(qi, kblk, bq, bk, n_blocks, rows):
    qpos = qi * bq + lax.broadcasted_iota(jnp.int32, (rows, bk), 0) % bq
    kpos = kblk * bk + lax.broadcasted_iota(jnp.int32, (rows, bk), 1)
    return (kblk >= 0) & (kblk < n_blocks) & (jnp.abs(kpos - qpos) <= WINDOW)


def _sink_rows(sink_ref, h, bq):
    r = lax.broadcasted_iota(jnp.int32, (Q_PER_KV * bq, 1), 0)
    return jnp.where(r < bq, sink_ref[h * Q_PER_KV], sink_ref[h * Q_PER_KV + 1])


def _lat_index(band, nb):
    if band:
        return lambda h, i, j: (h, jnp.clip(i - 1 + j, 0, nb - 1), 0)
    return lambda h, i, j: (h, j, 0)


def _flash_fwd(q, kl, vl, kc, vc, sink, band):
    kvh, g, t, dh = q.shape
    c = kc.shape[1]
    has_lat = kl is not None
    has_sink = sink is not None
    bq, bk = _attn_blocks(t, band)
    nb = t // bk
    nkv = (3 if band else nb) if has_lat else 1
    rows = g * bq

    def body(*refs):
        it = iter(refs)
        q_ref, kc_ref, vc_ref = next(it), next(it), next(it)
        kl_ref, vl_ref = (next(it), next(it)) if has_lat else (None, None)
        sink_ref = next(it) if has_sink else None
        o_ref, lse_ref, m_ref, l_ref, acc_ref = next(it), next(it), next(it), next(it), next(it)
        h, qi, kj = pl.program_id(0), pl.program_id(1), pl.program_id(2)
        qv = q_ref[0].reshape(rows, dh)

        def update(s, v):
            m_old = m_ref[...]
            m_new = jnp.maximum(m_old, jnp.max(s, axis=-1, keepdims=True))
            p = jnp.exp(s - m_new)
            alpha = jnp.exp(m_old - m_new)
            l_ref[...] = alpha * l_ref[...] + jnp.sum(p, axis=-1, keepdims=True)
            acc_ref[...] = alpha * acc_ref[...] + jnp.dot(p.astype(BF16), v.astype(BF16),
                                                          preferred_element_type=F32)
            m_ref[...] = m_new

        @pl.when(kj == 0)
        def _():
            if has_sink:
                m_ref[...] = _sink_rows(sink_ref, h, bq)
                l_ref[...] = jnp.ones_like(l_ref)
            else:
                m_ref[...] = jnp.full_like(m_ref, NEG_BIG)
                l_ref[...] = jnp.zeros_like(l_ref)
            acc_ref[...] = jnp.zeros_like(acc_ref)
            update(_qk_scores(qv, kc_ref[0]), vc_ref[0])

        if has_lat:
            s = _qk_scores(qv, kl_ref[0])
            if band:
                s = jnp.where(_band_mask(qi, qi - 1 + kj, bq, bk, nb, rows), s, NEG_BIG)
            update(s, vl_ref[0])

        @pl.when(kj == nkv - 1)
        def _():
            o_ref[0] = (acc_ref[...] / l_ref[...]).reshape(g, bq, dh)
            lse_ref[0] = (m_ref[...] + jnp.log(l_ref[...])).reshape(g, bq, 1)

    q_spec = pl.BlockSpec((1, g, bq, dh), lambda h, i, j: (h, 0, i, 0))
    r_spec = pl.BlockSpec((1, g, bq, 1), lambda h, i, j: (h, 0, i, 0))
    c_spec = pl.BlockSpec((1, c, dh), lambda h, i, j: (h, 0, 0))
    in_specs, args = [q_spec, c_spec, c_spec], [q, kc, vc]
    if has_lat:
        l_spec = pl.BlockSpec((1, bk, dh), _lat_index(band, nb))
        in_specs += [l_spec, l_spec]
        args += [kl, vl]
    if has_sink:
        in_specs.append(pl.BlockSpec(memory_space=pltpu.SMEM))
        args.append(sink)
    return pl.pallas_call(
        body, name="attn_fwd_band" if band else "attn_fwd", grid=(kvh, t // bq, nkv),
        in_specs=in_specs, out_specs=[q_spec, r_spec],
        out_shape=[jax.ShapeDtypeStruct(q.shape, F32), jax.ShapeDtypeStruct((kvh, g, t, 1), F32)],
        scratch_shapes=[pltpu.VMEM((rows, 1), F32), pltpu.VMEM((rows, 1), F32), pltpu.VMEM((rows, dh), F32)],
        compiler_params=_cparams(("parallel", "parallel", "arbitrary"), 4 * 4 * rows * max(bk, c)),
    )(*args)


def _flash_dq(q, kl, vl, kc, vc, sink, o, do, lse, band):
    kvh, g, t, dh = q.shape
    c = kc.shape[1]
    has_lat = kl is not None
    has_sink = sink is not None
    bq, bk = _attn_blocks(t, band)
    nb = t // bk
    nkv = (3 if band else nb) if has_lat else 1
    rows = g * bq

    def body(*refs):
        it = iter(refs)
        q_ref, o_ref, do_ref, lse_ref, kc_ref, vc_ref = (next(it) for _ in range(6))
        kl_ref, vl_ref = (next(it), next(it)) if has_lat else (None, None)
        sink_ref = next(it) if has_sink else None
        dq_ref, delta_ref, dsink_ref, acc_ref, dl_ref = (next(it) for _ in range(5))
        h, qi, kj = pl.program_id(0), pl.program_id(1), pl.program_id(2)
        qv = q_ref[0].reshape(rows, dh)
        dov = do_ref[0].reshape(rows, dh)
        lse_v = lse_ref[0].reshape(rows, 1)

        def update(s, k, v):
            p = jnp.exp(s - lse_v)
            dp = lax.dot_general(dov.astype(BF16), v.astype(BF16), (((1,), (1,)), ((), ())),
                                 preferred_element_type=F32)
            ds = p * (dp - dl_ref[...])
            acc_ref[...] += jnp.dot(ds.astype(BF16), k.astype(BF16), preferred_element_type=F32)

        @pl.when(kj == 0)
        def _():
            dl_ref[...] = jnp.sum(o_ref[0].reshape(rows, dh) * dov, axis=-1, keepdims=True)
            acc_ref[...] = jnp.zeros_like(acc_ref)
            update(_qk_scores(qv, kc_ref[0]), kc_ref[0], vc_ref[0])

        if has_lat:
            s = _qk_scores(qv, kl_ref[0])
            if band:
                s = jnp.where(_band_mask(qi, qi - 1 + kj, bq, bk, nb, rows), s, NEG_BIG)
            update(s, kl_ref[0], vl_ref[0])

        @pl.when(kj == nkv - 1)
        def _():
            dq_ref[0] = (acc_ref[...] * ATT_SCALE).reshape(g, bq, dh)
            delta_ref[0] = dl_ref[...].reshape(g, bq, 1)
            if has_sink:
                p_sink = jnp.exp(_sink_rows(sink_ref, h, bq) - lse_v)
                dsink_ref[0] = (-p_sink * dl_ref[...]).reshape(g, bq, 1)
            else:
                dsink_ref[0] = jnp.zeros((g, bq, 1), F32)

    q_spec = pl.BlockSpec((1, g, bq, dh), lambda h, i, j: (h, 0, i, 0))
    r_spec = pl.BlockSpec((1, g, bq, 1), lambda h, i, j: (h, 0, i, 0))
    c_spec = pl.BlockSpec((1, c, dh), lambda h, i, j: (h, 0, 0))
    in_specs, args = [q_spec, q_spec, q_spec, r_spec, c_spec, c_spec], [q, o, do, lse, kc, vc]
    if has_lat:
        l_spec = pl.BlockSpec((1, bk, dh), _lat_index(band, nb))
        in_specs += [l_spec, l_spec]
        args += [kl, vl]
    if has_sink:
        in_specs.append(pl.BlockSpec(memory_space=pltpu.SMEM))
        args.append(sink)
    row_shape = jax.ShapeDtypeStruct((kvh, g, t, 1), F32)
    return pl.pallas_call(
        body, name="attn_dq_band" if band else "attn_dq", grid=(kvh, t // bq, nkv),
        in_specs=in_specs, out_specs=[q_spec, r_spec, r_spec],
        out_shape=[jax.ShapeDtypeStruct(q.shape, F32), row_shape, row_shape],
        scratch_shapes=[pltpu.VMEM((rows, dh), F32), pltpu.VMEM((rows, 1), F32)],
        compiler_params=_cparams(("parallel", "parallel", "arbitrary"), 4 * 6 * rows * max(bk, c)),
    )(*args)


def _flash_dkv(q, do, lse, delta, k, v, band):
    kvh, g, t, dh = q.shape
    nk_rows = k.shape[1]
    if band:
        bq, bk = _attn_blocks(t, True)
    else:
        bq = _tile(t, 512, LANES)
        bk = _tile(nk_rows, 1024, LANES)
    nbq = t // bq
    nq = 3 if band else nbq
    rows = g * bq

    def body(q_ref, do_ref, lse_ref, delta_ref, k_ref, v_ref, dk_ref, dv_ref, dk_acc, dv_acc):
        ki, qj = pl.program_id(1), pl.program_id(2)
        qv = q_ref[0].reshape(rows, dh)
        dov = do_ref[0].reshape(rows, dh)

        @pl.when(qj == 0)
        def _():
            dk_acc[...] = jnp.zeros_like(dk_acc)
            dv_acc[...] = jnp.zeros_like(dv_acc)

        s = _qk_scores(qv, k_ref[0])
        if band:
            qblk = ki - 1 + qj
            qpos = qblk * bq + lax.broadcasted_iota(jnp.int32, (rows, bk), 0) % bq
            kpos = ki * bk + lax.broadcasted_iota(jnp.int32, (rows, bk), 1)
            s = jnp.where((qblk >= 0) & (qblk < nbq) & (jnp.abs(kpos - qpos) <= WINDOW), s, NEG_BIG)
        p = jnp.exp(s - lse_ref[0].reshape(rows, 1))
        dp = lax.dot_general(dov.astype(BF16), v_ref[0].astype(BF16), (((1,), (1,)), ((), ())),
                             preferred_element_type=F32)
        ds = p * (dp - delta_ref[0].reshape(rows, 1))
        tn = (((0,), (0,)), ((), ()))
        dv_acc[...] += lax.dot_general(p.astype(BF16), dov.astype(BF16), tn, preferred_element_type=F32)
        dk_acc[...] += lax.dot_general(ds.astype(BF16), qv.astype(BF16), tn, preferred_element_type=F32)

        @pl.when(qj == nq - 1)
        def _():
            dk_ref[0] = dk_acc[...] * ATT_SCALE
            dv_ref[0] = dv_acc[...]

    if band:
        q_index = lambda h, i, j: (h, 0, jnp.clip(i - 1 + j, 0, nbq - 1), 0)
    else:
        q_index = lambda h, i, j: (h, 0, j, 0)
    q_spec = pl.BlockSpec((1, g, bq, dh), q_index)
    r_spec = pl.BlockSpec((1, g, bq, 1), q_index)
    k_spec = pl.BlockSpec((1, bk, dh), lambda h, i, j: (h, i, 0))
    return pl.pallas_call(
        body, name="attn_dkv_band" if band else "attn_dkv", grid=(kvh, nk_rows // bk, nq),
        in_specs=[q_spec, q_spec, r_spec, r_spec, k_spec, k_spec], out_specs=[k_spec, k_spec],
        out_shape=[jax.ShapeDtypeStruct(k.shape, F32), jax.ShapeDtypeStruct(k.shape, F32)],
        scratch_shapes=[pltpu.VMEM((bk, dh), F32), pltpu.VMEM((bk, dh), F32)],
        compiler_params=_cparams(("parallel", "parallel", "arbitrary"), 4 * 6 * rows * bk),
    )(q, do, lse, delta, k, v)


def make_attention(band, has_lat, has_sink):
    def unpack(args):
        it = iter(args)
        q, kc, vc = next(it), next(it), next(it)
        kl, vl = (next(it), next(it)) if has_lat else (None, None)
        sink = next(it) if has_sink else None
        return q, kl, vl, kc, vc, sink

    @jax.custom_vjp
    def op(*args):
        q, kl, vl, kc, vc, sink = unpack(args)
        return _flash_fwd(q, kl, vl, kc, vc, sink, band)[0]

    def fwd(*args):
        q, kl, vl, kc, vc, sink = unpack(args)
        o, lse = _flash_fwd(q, kl, vl, kc, vc, sink, band)
        return o, (args, o, lse)

    def bwd(res, do):
        args, o, lse = res
        q, kl, vl, kc, vc, sink = unpack(args)
        dq, delta, dsink_rows = _flash_dq(q, kl, vl, kc, vc, sink, o, do, lse, band)
        dkc, dvc = _flash_dkv(q, do, lse, delta, kc, vc, False)
        grads = [dq, dkc, dvc]
        if has_lat:
            grads += list(_flash_dkv(q, do, lse, delta, kl, vl, band))
        if has_sink:
            grads.append(jnp.sum(dsink_rows, axis=(2, 3)).reshape(-1))
        return tuple(grads)

    op.defvjp(fwd, bwd)
    return op


attn_window = make_attention(True, True, True)
attn_global = make_attention(False, True, False)
attn_ctx_sink = make_attention(False, False, True)
attn_ctx = make_attention(False, False, False)


def _to_heads(z, n_heads):
    return z.reshape(z.shape[0], n_heads, HEAD_DIM).transpose(1, 0, 2)


def _q_heads(z):
    return z.reshape(z.shape[0], KV_HEADS, Q_PER_KV, HEAD_DIM).transpose(1, 2, 0, 3)


def _from_q_heads(o):
    return o.transpose(2, 0, 1, 3).reshape(o.shape[2], KV_HEADS * Q_PER_KV * HEAD_DIM)


def _scan_tile(t):
    return _tile(t, 512, SUBLANES)


def _scan_fwd_call(bre, bim, lre, lim, h0re, h0im, rev):
    t = bre.shape[0]
    tt = _scan_tile(t)
    nb = t // tt
    plane = bre.shape[1:]

    def body(bre_ref, bim_ref, lre_ref, lim_ref, h0re_ref, h0im_ref, sre_ref, sim_ref, h_ref):
        @pl.when(pl.program_id(0) == 0)
        def _():
            h_ref[0] = h0re_ref[...]
            h_ref[1] = h0im_ref[...]

        ar, ai = lre_ref[...], lim_ref[...]

        def step(j, carry):
            hr, hi = carry
            tj = (tt - 1 - j) if rev else j
            nr = ar * hr - ai * hi + bre_ref[tj]
            ni = ar * hi + ai * hr + bim_ref[tj]
            sre_ref[tj] = nr
            sim_ref[tj] = ni
            return nr, ni

        hr, hi = lax.fori_loop(0, tt, step, (h_ref[0], h_ref[1]), unroll=8)
        h_ref[0] = hr
        h_ref[1] = hi

    blk = pl.BlockSpec((tt,) + plane, (lambda i: (nb - 1 - i, 0, 0)) if rev else (lambda i: (i, 0, 0)))
    par = pl.BlockSpec(plane, lambda i: (0, 0))
    return pl.pallas_call(
        body, name="s5_scan_fwd", grid=(nb,), in_specs=[blk, blk, par, par, par, par], out_specs=[blk, blk],
        out_shape=[jax.ShapeDtypeStruct(bre.shape, F32)] * 2,
        scratch_shapes=[pltpu.VMEM((2,) + plane, F32)],
        compiler_params=_cparams(("arbitrary",), 4 * 4 * tt * plane[0] * plane[1]),
    )(bre, bim, lre, lim, h0re, h0im)


def _scan_bwd_call(gre, gim, sre, sim, lre, lim, h0re, h0im, rev):
    t = gre.shape[0]
    tt = _scan_tile(t)
    nb = t // tt
    plane = gre.shape[1:]
    down = not rev

    def body(gre_ref, gim_ref, sre_ref, sim_ref, lre_ref, lim_ref, h0re_ref, h0im_ref,
             dbre_ref, dbim_ref, dare_ref, daim_ref, dh0re_ref, dh0im_ref, carry_ref):
        i = pl.program_id(0)

        @pl.when(i == 0)
        def _():
            carry_ref[...] = jnp.zeros_like(carry_ref)

        ar, ai = lre_ref[...], lim_ref[...]

        def step(j, carry):
            gr, gi, dar, dai = carry
            tj = (tt - 1 - j) if down else j
            hr, hi = sre_ref[tj], sim_ref[tj]
            dar = dar + hr * gr + hi * gi
            dai = dai + hr * gi - hi * gr
            ngr = gre_ref[tj] + ar * gr + ai * gi
            ngi = gim_ref[tj] + ar * gi - ai * gr
            dbre_ref[tj] = ngr
            dbim_ref[tj] = ngi
            return ngr, ngi, dar, dai

        gr, gi, dar, dai = lax.fori_loop(
            0, tt, step, (carry_ref[0], carry_ref[1], carry_ref[2], carry_ref[3]), unroll=8)
        carry_ref[0] = gr
        carry_ref[1] = gi
        carry_ref[2] = dar
        carry_ref[3] = dai

        @pl.when(i == nb - 1)
        def _():
            hr, hi = h0re_ref[...], h0im_ref[...]
            dare_ref[...] = dar + hr * gr + hi * gi
            daim_ref[...] = dai + hr * gi - hi * gr
            dh0re_ref[...] = ar * gr + ai * gi
            dh0im_ref[...] = ar * gi - ai * gr

    blk = pl.BlockSpec((tt,) + plane, (lambda i: (nb - 1 - i, 0, 0)) if down else (lambda i: (i, 0, 0)))
    par = pl.BlockSpec(plane, lambda i: (0, 0))
    return pl.pallas_call(
        body, name="s5_scan_bwd", grid=(nb,), in_specs=[blk, blk, blk, blk, par, par, par, par],
        out_specs=[blk, blk, par, par, par, par],
        out_shape=[jax.ShapeDtypeStruct(gre.shape, F32)] * 2 + [jax.ShapeDtypeStruct(plane, F32)] * 4,
        scratch_shapes=[pltpu.VMEM((4,) + plane, F32)],
        compiler_params=_cparams(("arbitrary",), 4 * 6 * tt * plane[0] * plane[1]),
    )(gre, gim, sre, sim, lre, lim, h0re, h0im)


def make_scan(rev):
    @jax.custom_vjp
    def op(bre, bim, lre, lim, h0re, h0im):
        return tuple(_scan_fwd_call(bre, bim, lre, lim, h0re, h0im, rev))

    def fwd(bre, bim, lre, lim, h0re, h0im):
        sre, sim = _scan_fwd_call(bre, bim, lre, lim, h0re, h0im, rev)
        return (sre, sim), (sre, sim, lre, lim, h0re, h0im)

    def bwd(res, cts):
        sre, sim, lre, lim, h0re, h0im = res
        return tuple(_scan_bwd_call(cts[0], cts[1], sre, sim, lre, lim, h0re, h0im, rev))

    op.defvjp(fwd, bwd)
    return op


scan_up = make_scan(False)
scan_down = make_scan(True)


def _adamw_call(parts, w, m, v, name):
    r, c = w.shape
    tr = _tile(r, 256, SUBLANES)
    nparts = parts.shape[0]
    c1 = 1.0 - ADAM_B1 ** ADAM_STEP
    c2 = 1.0 - ADAM_B2 ** ADAM_STEP

    def body(p_ref, w_ref, m_ref, v_ref, g_ref, d_ref, nm_ref, nv_ref):
        g = p_ref[0]
        for s in range(1, nparts):
            g = g + p_ref[s]
        m1 = ADAM_B1 * m_ref[...] + (1.0 - ADAM_B1) * g
        v1 = ADAM_B2 * v_ref[...] + (1.0 - ADAM_B2) * jnp.square(g)
        g_ref[...] = g
        nm_ref[...] = m1
        nv_ref[...] = v1
        d_ref[...] = -ADAM_LR * ((m1 / c1) / (jnp.sqrt(v1 / c2) + ADAM_EPS) + ADAM_WD * w_ref[...])

    blk = pl.BlockSpec((tr, c), lambda i: (i, 0))
    return pl.pallas_call(
        body, name=name, grid=(r // tr,),
        in_specs=[pl.BlockSpec((nparts, tr, c), lambda i: (0, i, 0)), blk, blk, blk], out_specs=[blk] * 4,
        out_shape=[jax.ShapeDtypeStruct((r, c), F32)] * 4,
        compiler_params=_cparams(("parallel",), 4 * tr * c * (nparts + 7)),
    )(parts, w, m, v)


def _peer(k):
    x, y, c = lax.axis_index("x"), lax.axis_index("y"), lax.axis_index("c")
    px = 1 - x if k & 4 else x
    py = 1 - y if k & 2 else y
    pc = 1 - c if k & 1 else c
    return (px, py, pc), 4 * px + 2 * py + pc


def _my_slot():
    return 4 * lax.axis_index("x") + 2 * lax.axis_index("y") + lax.axis_index("c")


def _exchange_call(x, gather, name):
    slab = x.shape if gather else x.shape[1:]

    def body(x_ref, out_ref, send_sems, recv_sems, local_sem):
        me = _my_slot()
        mine = pltpu.make_async_copy(x_ref if gather else x_ref.at[me], out_ref.at[me], local_sem)
        mine.start()
        sends = []
        for k in range(1, N_DEV):
            peer, slot = _peer(k)
            cp = pltpu.make_async_remote_copy(
                src_ref=x_ref if gather else x_ref.at[slot], dst_ref=out_ref.at[me],
                send_sem=send_sems.at[k - 1], recv_sem=recv_sems.at[k - 1],
                device_id=peer, device_id_type=MESH)
            cp.start()
            sends.append(cp)
        for k in range(1, N_DEV):
            peer, slot = _peer(k)
            pltpu.make_async_remote_copy(
                src_ref=x_ref if gather else x_ref.at[slot], dst_ref=out_ref.at[slot],
                send_sem=send_sems.at[k - 1], recv_sem=recv_sems.at[k - 1],
                device_id=peer, device_id_type=MESH).wait_recv()
        for cp in sends:
            cp.wait_send()
        mine.wait()

    return pl.pallas_call(
        body, name=name,
        in_specs=[pl.BlockSpec(memory_space=pl.ANY)], out_specs=pl.BlockSpec(memory_space=pl.ANY),
        out_shape=jax.ShapeDtypeStruct((N_DEV,) + tuple(slab), x.dtype),
        scratch_shapes=[pltpu.SemaphoreType.DMA((N_DEV - 1,)), pltpu.SemaphoreType.DMA((N_DEV - 1,)),
                        pltpu.SemaphoreType.DMA],
    )(x)


def all_gather(x, name):
    return _exchange_call(x, True, name)


def all_to_all(x, name):
    return _exchange_call(x, False, name)


def _rope_tables(t):
    n_freq = HEAD_DIM // 4
    tok = jnp.arange(t)
    inv = ROPE_THETA ** (-jnp.arange(n_freq, dtype=F32) / n_freq)
    a_row = (tok // GRID_W).astype(F32)[:, None] * inv
    a_col = (tok % GRID_W).astype(F32)[:, None] * inv
    cos = jnp.concatenate([jnp.cos(a_row)] * 2 + [jnp.cos(a_col)] * 2, axis=1)
    sin = jnp.concatenate([-jnp.sin(a_row), jnp.sin(a_row), -jnp.sin(a_col), jnp.sin(a_col)], axis=1)
    return jnp.concatenate([cos, cos], axis=1), jnp.concatenate([sin, sin], axis=1)


def _block_diag(blocks):
    g, a, b = blocks.shape
    eye = jnp.eye(g, dtype=blocks.dtype)
    return jnp.einsum("gab,gk->gakb", blocks, eye).reshape(g * a, g * b)


def _ffn_half(x, mod, g, w_in, w_out):
    h, = norm_mod(x, g, mod[0:1], mod[1:2])
    a, = swiglu_act(linear(h, w_in))
    return resid_half(x, linear(a, w_out), mod[2:3])[0]


def _planes(z):
    return z.reshape(z.shape[0], SUBLANES, SSM_LANES // SUBLANES)


def _s5_discretize(a_re, a_im, log_dt, b_re, b_im):
    lam = lax.complex(a_re, a_im)
    dt = jnp.exp(log_dt)[:, None]
    lam_bar = jnp.exp(lam * dt)
    b_bar = ((lam_bar - 1.0) / lam)[..., None] * lax.complex(b_re, b_im)
    return lam_bar, b_bar


def _s5_branch(u_lat, u_ctx, w, l, with_ctx_out):
    zero = jnp.zeros((SUBLANES, SSM_LANES // SUBLANES), F32)
    lat_terms, ctx_terms = [], []
    for d, scan in enumerate((scan_up, scan_down)):
        lam_bar, b_bar = _s5_discretize(w["ssm_a_re"][l, d], w["ssm_a_im"][l, d], w["ssm_log_dt"][l, d],
                                        w["ssm_b_re"][l, d], w["ssm_b_im"][l, d])
        lre = jnp.real(lam_bar).reshape(zero.shape)
        lim = jnp.imag(lam_bar).reshape(zero.shape)
        b_t = jnp.swapaxes(b_bar, 1, 2)
        b_mat = jnp.concatenate([_block_diag(jnp.real(b_t)), _block_diag(jnp.imag(b_t))], axis=1)
        c_re = _block_diag(jnp.swapaxes(w["ssm_c_re"][l, d], 1, 2))
        c_im = _block_diag(jnp.swapaxes(w["ssm_c_im"][l, d], 1, 2))
        bu_c = linear(u_ctx, b_mat)
        sc_re, sc_im = scan(_planes(bu_c[:, :SSM_LANES]), _planes(bu_c[:, SSM_LANES:]), lre, lim, zero, zero)
        last = 0 if d == 1 else u_ctx.shape[0] - 1
        bu_l = linear(u_lat, b_mat)
        sl_re, sl_im = scan(_planes(bu_l[:, :SSM_LANES]), _planes(bu_l[:, SSM_LANES:]), lre, lim,
                            sc_re[last], sc_im[last])
        flat = lambda s: s.reshape(s.shape[0], SSM_LANES)
        lat_terms += [linear(flat(sl_re), c_re), linear(flat(sl_im), c_im)]
        if with_ctx_out:
            ctx_terms += [linear(flat(sc_re), c_re), linear(flat(sc_im), c_im)]
    d_skip = w["ssm_d"][l][None, :]

    def out(terms, u):
        y, = s5_pre(*terms, u, d_skip)
        return glu(linear(y, w["glu_w"][l]))[0]

    return out(lat_terms, u_lat), (out(ctx_terms, u_ctx) if with_ctx_out else None)


def _pool_branch(xa, w, l):
    y = linear(pool_diff(xa), _block_diag(w["pool_w"][l]))
    return scale_rows(y, w["pool_scale"][l][None, :])[0]


def _merge_branches(branches, gate_logits, w, l):
    zs = [linear(y, w["branch_w"][l, k]) for k, y in enumerate(branches)]
    return linear(merge(gate_logits, *zs)[0], w["out_w"][l])


def _token_mixer(h, hc, cos, sin, w, l, with_ctx_out):
    w_in = w["w_in"][l]
    p = linear(h, w_in)
    pc = linear(hc, w_in if with_ctx_out else w_in[:, :CTX_COLS])
    sink = w["win_sink"][l]
    q_g = jnp.tile(w["qk_norm"][l, 0], KV_HEADS * Q_PER_KV)[None, :]
    k_g = jnp.tile(w["qk_norm"][l, 1], KV_HEADS)[None, :]
    k_win_c = _to_heads(pc[:, O_KB:O_VB], KV_HEADS)
    v_win_c = _to_heads(pc[:, O_VB:O_UC], KV_HEADS)
    k_glb_c = _to_heads(head_norm(pc[:, O_KD:O_VD], k_g)[0], KV_HEADS)
    v_glb_c = _to_heads(pc[:, O_VD:CTX_COLS], KV_HEADS)
    y_a = _pool_branch(p[:, O_XA:O_GATE], w, l)
    q_win = _q_heads(rope(p[:, O_QB:O_QD], cos, sin)[0])
    k_win = _to_heads(rope(p[:, O_KB:O_VB], cos, sin)[0], KV_HEADS)
    v_win = _to_heads(p[:, O_VB:O_UC], KV_HEADS)
    y_b = _from_q_heads(attn_window(q_win, k_win_c, v_win_c, k_win, v_win, sink))
    y_c, y_c_ctx = _s5_branch(p[:, O_UC:O_KD], pc[:, O_UC:O_KD], w, l, with_ctx_out)
    q_glb = _q_heads(norm_rope(p[:, O_QD:O_XA], cos, sin, q_g)[0])
    k_glb = _to_heads(norm_rope(p[:, O_KD:O_VD], cos, sin, k_g)[0], KV_HEADS)
    v_glb = _to_heads(p[:, O_VD:CTX_COLS], KV_HEADS)
    y_d = _from_q_heads(attn_global(q_glb, k_glb_c, v_glb_c, k_glb, v_glb))
    y = _merge_branches((y_a, y_b, y_c, y_d), p[:, O_GATE:], w, l)
    if not with_ctx_out:
        return y, None
    y_a_c = _pool_branch(pc[:, O_XA:O_GATE], w, l)
    y_b_c = _from_q_heads(attn_ctx_sink(_q_heads(pc[:, O_QB:O_QD]), k_win_c, v_win_c, sink))
    q_glb_c = _q_heads(head_norm(pc[:, O_QD:O_XA], q_g)[0])
    y_d_c = _from_q_heads(attn_ctx(q_glb_c, k_glb_c, v_glb_c))
    return y, _merge_branches((y_a_c, y_b_c, y_c_ctx, y_d_c), pc[:, O_GATE:], w, l)


def local_loss(w, x, c, ctx, target):
    depth = w["w_mod"].shape[0]
    cos, sin = _rope_tables(x.shape[0])
    cond = jnp.concatenate([c, w["c_ctx"][None, :], jnp.zeros((COND_ROWS - 2, D_MODEL), F32)], axis=0)
    s_all, = silu_rows(cond)
    for l in range(depth):
        last = l == depth - 1
        m_all = (linear(s_all, w["w_mod"][l]) + w["b_mod"][l][None, :]).reshape(COND_ROWS, N_SUB, 3, D_MODEL)
        m, mc = m_all[0], m_all[1]
        g = w["norm_g"][l][:, None, :]
        x = _ffn_half(x, m[0], g[0], w["ffn_in"][l, 0], w["ffn_out"][l, 0])
        ctx = _ffn_half(ctx, mc[0], g[0], w["ffn_in"][l, 0], w["ffn_out"][l, 0])
        h, = norm_mod(x, g[1], m[1, 0:1], m[1, 1:2])
        hc, = norm_mod(ctx, g[1], mc[1, 0:1], mc[1, 1:2])
        y, y_ctx = _token_mixer(h, hc, cos, sin, w, l, not last)
        x, = resid_full(x, y, m[1, 2:3])
        if not last:
            ctx, = resid_full(ctx, y_ctx, mc[1, 2:3])
        x = _ffn_half(x, m[2], g[2], w["ffn_in"][l, 1], w["ffn_out"][l, 1])
        if not last:
            ctx = _ffn_half(ctx, mc[2], g[2], w["ffn_in"][l, 1], w["ffn_out"][l, 1])
    return jnp.sum(loss_rows(x, target, w["final_g"][None, :])[0])


PACK_COLS = 1024


def _pack(arrays):
    flat = jnp.concatenate([a.reshape(-1) for a in arrays])
    pad = (-flat.shape[0]) % (PACK_COLS * 16)
    return jnp.pad(flat, (0, pad)).reshape(-1, PACK_COLS)


def _unpack(slab, shapes):
    flat = slab.reshape(-1)
    out, off = [], 0
    for s in shapes:
        n = math.prod(s)
        out.append(flat[off:off + n].reshape(s))
        off += n
    return out


def _full_from_shards(gathered, shard_shape, axis):
    z = jnp.moveaxis(gathered.reshape((N_DEV,) + tuple(shard_shape)), 0, axis)
    shape = list(shard_shape)
    shape[axis] *= N_DEV
    return z.reshape(shape)


def _shards_from_full(full, axis):
    shape = list(full.shape)
    shape[axis:axis + 1] = [N_DEV, shape[axis] // N_DEV]
    return jnp.moveaxis(full.reshape(shape), axis, 0)


def kernel(x, c, ctx, c_ctx, w_mod, b_mod, norm_g, ffn_in, ffn_out, w_in, win_sink, qk_norm, pool_w, pool_scale, ssm_a_re, ssm_a_im, ssm_log_dt, ssm_b_re, ssm_b_im, ssm_c_re, ssm_c_im, ssm_d, glu_w, branch_w, out_w, final_g, loss_target, m_c_ctx, m_w_mod, m_b_mod, m_norm_g, m_ffn_in, m_ffn_out, m_w_in, m_win_sink, m_qk_norm, m_pool_w, m_pool_scale, m_ssm_a_re, m_ssm_a_im, m_ssm_log_dt, m_ssm_b_re, m_ssm_b_im, m_ssm_c_re, m_ssm_c_im, m_ssm_d, m_glu_w, m_branch_w, m_out_w, m_final_g, v_c_ctx, v_w_mod, v_b_mod, v_norm_g, v_ffn_in, v_ffn_out, v_w_in, v_win_sink, v_qk_norm, v_pool_w, v_pool_scale, v_ssm_a_re, v_ssm_a_im, v_ssm_log_dt, v_ssm_b_re, v_ssm_b_im, v_ssm_c_re, v_ssm_c_im, v_ssm_d, v_glu_w, v_branch_w, v_out_w, v_final_g):
    given = dict(locals())
    wts = {n: given[n] for n in WEIGHTS}
    mom = {n: given["m_" + n] for n in WEIGHTS}
    var = {n: given["v_" + n] for n in WEIGHTS}
    me = _my_slot()

    shard_shapes = [wts[n].shape for n in SHARDED]
    w_slab = _pack([wts[n] for n in SHARDED])
    gathered = all_gather(w_slab.astype(BF16), "gather_weights").astype(F32)
    parts = [_unpack(gathered[s], shard_shapes) for s in range(N_DEV)]
    full = dict(wts)
    for i, n in enumerate(SHARDED):
        full[n] = _full_from_shards(jnp.stack([parts[s][i] for s in range(N_DEV)]), wts[n].shape, SHARD_AXIS[n])
    g_slab = _pack([norm_g])
    g_all = all_gather(g_slab, "gather_norm_g")
    full["norm_g"] = _full_from_shards(
        jnp.stack([_unpack(g_all[s], [norm_g.shape])[0] for s in range(N_DEV)]), norm_g.shape, 2)

    loss, (gw, gx) = jax.value_and_grad(local_loss, argnums=(0, 1))(full, x[0], c, ctx[0], loss_target[0])
    loss = lax.psum(loss, ("x", "y", "c"))

    dest = [_shards_from_full(gw[n], SHARD_AXIS[n]) for n in SHARDED]
    send = jnp.stack([_pack([d[s] for d in dest]) for s in range(N_DEV)])
    big_parts = all_to_all(send, "exchange_grads")
    small_names = SMALL + ("norm_g",)
    small_shapes = [gw[n].shape for n in small_names]
    small_parts = all_gather(_pack([gw[n] for n in small_names]), "gather_small_grads")

    big = _adamw_call(big_parts, w_slab, _pack([mom[n] for n in SHARDED]), _pack([var[n] for n in SHARDED]),
                      "adamw_sharded")
    big = [_unpack(b, shard_shapes) for b in big]
    col = me * norm_g.shape[2]

    def small_slab(src, shard_src):
        padded = jnp.zeros((norm_g.shape[0], norm_g.shape[1], norm_g.shape[2] * N_DEV), F32)
        padded = lax.dynamic_update_slice(padded, shard_src, (0, 0, col))
        return _pack([src[n] for n in SMALL] + [padded])

    small = _adamw_call(small_parts, small_slab(wts, norm_g), small_slab(mom, m_norm_g),
                        small_slab(var, v_norm_g), "adamw_small")
    small = [_unpack(s, small_shapes) for s in small]

    outs = {}
    for kind in range(4):
        for i, n in enumerate(SHARDED):
            outs[(kind, n)] = big[kind][i]
        for i, n in enumerate(small_names):
            val = small[kind][i]
            if n == "norm_g":
                val = lax.dynamic_slice(val, (0, 0, col), norm_g.shape)
            outs[(kind, n)] = val
    return (loss, gx[None], *[outs[(k, n)] for k in range(4) for n in WEIGHTS])
```

```python
import functools
import math

import jax
import jax.numpy as jnp
from jax import lax
from jax.experimental import pallas as pl
from jax.experimental.pallas import tpu as pltpu

F32 = jnp.float32
BF16 = jnp.bfloat16

D_MODEL = 1024
GRID_W = 64
HEAD_DIM = 64
N_BRANCH = 4
BRANCH_W = D_MODEL // N_BRANCH
WINDOW = 128
ROPE_THETA = 10000.0
EPS = 1e-6
D_FF = 2816
N_SUB = 3
POOL_WINDOWS = (2, 4, 8, 16)
POOL_GROUP = BRANCH_W // len(POOL_WINDOWS)
KV_HEADS = 2
Q_PER_KV = 2
SSM_GROUP = 16
SSM_GROUPS = BRANCH_W // SSM_GROUP
SSM_STATE = 64
SSM_LANES = SSM_GROUPS * SSM_STATE
O_KB, O_VB, O_UC, O_KD, O_VD, CTX_COLS = 0, 128, 256, 512, 640, 768
O_QB, O_QD, O_XA, O_GATE = 768, 1024, 1280, 1536
IN_W = O_GATE + N_BRANCH * D_MODEL

ADAM_LR, ADAM_B1, ADAM_B2, ADAM_EPS, ADAM_WD, ADAM_STEP = 0.001, 0.9, 0.999, 1e-08, 0.01, 10

N_DEV = 8
MESH = pl.DeviceIdType.MESH

V7X_VMEM_BYTES = 64 * 1024 * 1024
SUBLANES = 8
LANES = 128
NEG_BIG = -1e30
COND_ROWS = 128

SHARDED = ("w_mod", "ffn_in", "ffn_out", "w_in", "glu_w", "branch_w", "out_w")
SHARD_AXIS = {"w_mod": 2, "ffn_in": 3, "ffn_out": 2, "w_in": 2, "glu_w": 2, "branch_w": 3, "out_w": 1}
SMALL = ("c_ctx", "b_mod", "win_sink", "qk_norm", "pool_w", "pool_scale", "ssm_a_re", "ssm_a_im",
         "ssm_log_dt", "ssm_b_re", "ssm_b_im", "ssm_c_re", "ssm_c_im", "ssm_d", "final_g")
WEIGHTS = ("c_ctx", "w_mod", "b_mod", "norm_g", "ffn_in", "ffn_out", "w_in", "win_sink", "qk_norm",
           "pool_w", "pool_scale", "ssm_a_re", "ssm_a_im", "ssm_log_dt", "ssm_b_re", "ssm_b_im",
           "ssm_c_re", "ssm_c_im", "ssm_d", "glu_w", "branch_w", "out_w", "final_g")


def _tile(n, cap, mult):
    if n <= cap:
        return n
    t = (cap // mult) * mult
    while t >= mult:
        if n % t == 0:
            return t
        t -= mult
    return n


def _cparams(sem, tile_bytes):
    limit = int(min(V7X_VMEM_BYTES - 8 * 2 ** 20, max(32 * 2 ** 20, 3 * tile_bytes + 8 * 2 ** 20)))
    return pltpu.CompilerParams(dimension_semantics=sem, vmem_limit_bytes=limit)


def _mm(a, b, ta=False, tb=False, out_dtype=F32, name="mm"):
    m, k = (a.shape[1], a.shape[0]) if ta else a.shape
    n = b.shape[0] if tb else b.shape[1]
    assert (b.shape[1] if tb else b.shape[0]) == k
    tm, tn, tk = _tile(m, 1024, LANES), _tile(n, 1536, LANES), _tile(k, 1536, LANES)
    nk = k // tk
    dims = (((0 if ta else 1,), (1 if tb else 0,)), ((), ()))

    def body(a_ref, b_ref, o_ref, acc_ref):
        kk = pl.program_id(2)

        @pl.when(kk == 0)
        def _():
            acc_ref[...] = jnp.zeros_like(acc_ref)

        acc_ref[...] += lax.dot_general(a_ref[...].astype(BF16), b_ref[...].astype(BF16), dims,
                                        preferred_element_type=F32)

        @pl.when(kk == nk - 1)
        def _():
            o_ref[...] = acc_ref[...].astype(o_ref.dtype)

    a_spec = (pl.BlockSpec((tk, tm), lambda i, j, kk: (kk, i)) if ta
              else pl.BlockSpec((tm, tk), lambda i, j, kk: (i, kk)))
    b_spec = (pl.BlockSpec((tn, tk), lambda i, j, kk: (j, kk)) if tb
              else pl.BlockSpec((tk, tn), lambda i, j, kk: (kk, j)))
    tile_bytes = (a.dtype.itemsize * tm * tk + b.dtype.itemsize * tk * tn
                  + jnp.dtype(out_dtype).itemsize * tm * tn + 2 * tm * tn)
    return pl.pallas_call(
        body, name=name, grid=(m // tm, n // tn, nk),
        in_specs=[a_spec, b_spec], out_specs=pl.BlockSpec((tm, tn), lambda i, j, kk: (i, j)),
        out_shape=jax.ShapeDtypeStruct((m, n), out_dtype),
        scratch_shapes=[pltpu.VMEM((tm, tn), F32)],
        compiler_params=_cparams(("parallel", "parallel", "arbitrary"), tile_bytes),
    )(a, b)


def make_linear(out_dtype):
    @jax.custom_vjp
    def op(x, w):
        return _mm(x, w, out_dtype=out_dtype, name="linear_fwd")

    def fwd(x, w):
        return op(x, w), (x, w)

    def bwd(res, dy):
        x, w = res
        return (_mm(dy, w, tb=True, out_dtype=x.dtype, name="linear_dx"),
                _mm(x, dy, ta=True, out_dtype=w.dtype, name="linear_dw"))

    op.defvjp(fwd, bwd)
    return op


linear = make_linear(F32)
linear_b = make_linear(BF16)


ROW_TILE_BYTES = 6 * 2 ** 20


def _row_tile(t, row_bytes):
    tm = 1024
    while tm > 2 * SUBLANES and tm * row_bytes > ROW_TILE_BYTES:
        tm //= 2
    return _tile(t, tm, 2 * SUBLANES)


def _row_call(fn, rows, params, n_reduce, name):
    t = rows[0].shape[0]
    out_avals = jax.eval_shape(fn, *rows, *params)
    n_out = len(out_avals) - n_reduce
    row_avals, red_avals = out_avals[:n_out], out_avals[n_out:]
    row_bytes = sum(r.shape[1] * r.dtype.itemsize for r in (*rows, *row_avals))
    tm = _row_tile(t, row_bytes)
    n_in = len(rows) + len(params)

    def body(*refs):
        outs = fn(*[r[...] for r in refs[:n_in]])
        o_refs = refs[n_in:]
        for o_ref, o in zip(o_refs[:n_out], outs[:n_out]):
            o_ref[...] = o.astype(o_ref.dtype)
        if n_reduce:
            @pl.when(pl.program_id(0) == 0)
            def _():
                for r in o_refs[n_out:]:
                    r[...] = jnp.zeros_like(r)

            for r, o in zip(o_refs[n_out:], outs[n_out:]):
                r[...] += o.astype(r.dtype)

    in_specs = ([pl.BlockSpec((tm, r.shape[1]), lambda i: (i, 0)) for r in rows]
                + [pl.BlockSpec(p.shape, lambda i: (0, 0)) for p in params])
    out_specs = ([pl.BlockSpec((tm, o.shape[1]), lambda i: (i, 0)) for o in row_avals]
                 + [pl.BlockSpec(o.shape, lambda i: (0, 0)) for o in red_avals])
    return pl.pallas_call(
        body, name=name, grid=(t // tm,), in_specs=in_specs, out_specs=out_specs,
        out_shape=[jax.ShapeDtypeStruct(o.shape, o.dtype) for o in out_avals],
        compiler_params=_cparams(("arbitrary",) if n_reduce else ("parallel",), tm * row_bytes),
    )(*rows, *params)


def rowwise(fn, n_rows, name):
    @jax.custom_vjp
    def op(*args):
        return tuple(_row_call(fn, args[:n_rows], args[n_rows:], 0, name + "_fwd"))

    def fwd(*args):
        return op(*args), args

    def bwd(args, cts):
        n_ct = len(cts)

        def bwd_fn(*a):
            r, ct, p = a[:n_rows], a[n_rows:n_rows + n_ct], a[n_rows + n_ct:]
            return jax.vjp(fn, *r, *p)[1](tuple(ct))

        return tuple(_row_call(bwd_fn, (*args[:n_rows], *cts), args[n_rows:], len(args) - n_rows,
                               name + "_bwd"))

    op.defvjp(fwd, bwd)
    return op


@functools.partial(jax.custom_vjp, nondiff_argnums=(1,))
def _swap_lanes(x, k):
    n = x.shape[-1]
    lane = lax.broadcasted_iota(jnp.int32, x.shape, x.ndim - 1)
    return jnp.where((lane & k) == 0, pltpu.roll(x, n - k, x.ndim - 1), pltpu.roll(x, k, x.ndim - 1))


def _swap_lanes_fwd(x, k):
    return _swap_lanes(x, k), None


def _swap_lanes_bwd(k, _, g):
    return (_swap_lanes(g, k),)


_swap_lanes.defvjp(_swap_lanes_fwd, _swap_lanes_bwd)


def _head_sum(x):
    s = x
    k = 1
    while k < HEAD_DIM:
        s = s + _swap_lanes(s, k)
        k *= 2
    return s


def _rms(x):
    return x * lax.rsqrt(jnp.mean(x * x, axis=-1, keepdims=True) + EPS)


def _norm_mod_fn(x, g, shift, scale):
    return ((_rms(x) * g) * (1.0 + scale) + shift,)


def _swiglu_fn(u):
    gate, up = u[:, :D_FF], u[:, D_FF:]
    return (jax.nn.silu(gate) * up,)


def _resid_fn(coef, x, y, gate):
    return (x + (coef * gate) * y,)


def _scale_fn(y, s):
    return (y * s,)


def _tile_lanes(tab, width):
    return tab if tab.shape[1] == width else jnp.concatenate([tab] * (width // tab.shape[1]), axis=1)


def _rope_fn(x, cos, sin):
    w = x.shape[1]
    return (x * _tile_lanes(cos, w) + _swap_lanes(x, 16) * _tile_lanes(sin, w),)


def _head_norm(x, g):
    ms = _head_sum(x * x) * (1.0 / HEAD_DIM)
    return x * lax.rsqrt(ms + EPS) * g


def _norm_rope_fn(x, cos, sin, g):
    return _rope_fn(_head_norm(x, g), cos, sin)


def _head_norm_fn(x, g):
    return (_head_norm(x, g),)


def _merge_fn(gl, z0, z1, z2, z3):
    zs = (z0, z1, z2, z3)
    terms = [jax.nn.sigmoid(gl[:, k * D_MODEL:(k + 1) * D_MODEL].astype(F32)) * zs[k].astype(F32)
             for k in range(N_BRANCH)]
    return (sum(terms[1:], terms[0]),)


def _s5_pre_fn(y0r, y0i, y1r, y1i, u, d):
    return (jax.nn.gelu(((y0r - y0i) + (y1r - y1i)) + d * u),)


def _glu_fn(z):
    return (z[:, :BRANCH_W] * jax.nn.sigmoid(z[:, BRANCH_W:]),)


def _silu_fn(x):
    return (jax.nn.silu(x),)


def _loss_fn(x, tgt, g):
    err = jnp.square(_rms(x) * g - tgt)
    return (0.5 * jnp.mean(err, axis=-1, keepdims=True),)


norm_mod = rowwise(_norm_mod_fn, 1, "norm_mod")
swiglu_act = rowwise(_swiglu_fn, 1, "swiglu")
resid_half = rowwise(functools.partial(_resid_fn, 0.5), 2, "resid_half")
resid_full = rowwise(functools.partial(_resid_fn, 1.0), 2, "resid_full")
scale_rows = rowwise(_scale_fn, 1, "pool_scale")
rope = rowwise(_rope_fn, 3, "rope")
norm_rope = rowwise(_norm_rope_fn, 3, "norm_rope")
head_norm = rowwise(_head_norm_fn, 1, "head_norm")
merge = rowwise(_merge_fn, 5, "merge")
s5_pre = rowwise(_s5_pre_fn, 5, "s5_pre")
glu = rowwise(_glu_fn, 1, "glu")
silu_rows = rowwise(_silu_fn, 1, "silu")
loss_rows = rowwise(_loss_fn, 2, "loss_head")


POOL_HALO = 16


def _pool_call(xa, adjoint, name):
    n, width = xa.shape
    tm = _tile(n, 512, POOL_HALO)
    halo_blocks = tm // POOL_HALO
    last_halo = n // POOL_HALO - 1
    ext_rows = tm + 2 * POOL_HALO

    def body(prev_ref, cur_ref, next_ref, o_ref, ext_ref):
        i = pl.program_id(0)
        ext_ref[0:POOL_HALO] = prev_ref[...]
        ext_ref[POOL_HALO:POOL_HALO + tm] = cur_ref[...]
        ext_ref[POOL_HALO + tm:ext_rows] = next_ref[...]
        e = ext_ref[...]
        row = lax.broadcasted_iota(jnp.int32, e.shape, 0) + (i * tm - POOL_HALO)
        grp = lax.broadcasted_iota(jnp.int32, e.shape, 1) // POOL_GROUP
        win = jnp.where(grp == 0, POOL_WINDOWS[0],
                        jnp.where(grp == 1, POOL_WINDOWS[1], jnp.where(grp == 2, POOL_WINDOWS[2], POOL_WINDOWS[3])))
        valid = (row >= 0) & (row < n)
        lo = jnp.clip(row - win // 2, 0, n)
        hi = jnp.clip(row - win // 2 + win, 0, n)
        cnt = jnp.maximum((hi - lo).astype(F32), 1.0)
        e0 = jnp.where(valid, e / cnt if adjoint else e, 0.0)

        def shift(z, s):
            return pltpu.roll(z, s % ext_rows, 0)

        s2 = e0 + shift(e0, -1 if adjoint else 1)
        s4 = shift(s2, 1) + shift(s2, -1)
        s8 = shift(s4, 2) + shift(s4, -2)
        s16 = shift(s8, 4) + shift(s8, -4)
        s = jnp.where(grp == 0, s2, jnp.where(grp == 1, s4, jnp.where(grp == 2, s8, s16)))
        out = (s - e) if adjoint else (s / cnt - e)
        o_ref[...] = out[POOL_HALO:POOL_HALO + tm]

    return pl.pallas_call(
        body, name=name, grid=(n // tm,),
        in_specs=[pl.BlockSpec((POOL_HALO, width), lambda i: (jnp.maximum(i * halo_blocks - 1, 0), 0)),
                  pl.BlockSpec((tm, width), lambda i: (i, 0)),
                  pl.BlockSpec((POOL_HALO, width), lambda i: (jnp.minimum((i + 1) * halo_blocks, last_halo), 0))],
        out_specs=pl.BlockSpec((tm, width), lambda i: (i, 0)),
        out_shape=jax.ShapeDtypeStruct((n, width), F32),
        scratch_shapes=[pltpu.VMEM((ext_rows, width), F32)],
        compiler_params=_cparams(("parallel",), 4 * 4 * ext_rows * width),
    )(xa, xa, xa)


@jax.custom_vjp
def pool_diff(xa):
    return _pool_call(xa, False, "pool_fwd")


pool_diff.defvjp(lambda xa: (pool_diff(xa), None), lambda _, g: (_pool_call(g, True, "pool_bwd"),))


ATT_SCALE = HEAD_DIM ** -0.5


def _attn_blocks(t, band):
    if band:
        b = _tile(t, 256, LANES)
        return b, b
    return _tile(t, 512, LANES), _tile(t, 1024, LANES)


def _qk_scores(q, k):
    return lax.dot_general((q * ATT_SCALE).astype(BF16), k.astype(BF16), (((1,), (1,)), ((), ())),
                           preferred_element_type=F32)


def _band_mask(qi, kblk, bq, bk, n_blocks, rows):
    qpos = qi * bq + lax.broadcasted_iota(jnp.int32, (rows, bk), 0) % bq
    kpos = kblk * bk + lax.broadcasted_iota(jnp.int32, (rows, bk), 1)
    return (kblk >= 0) & (kblk < n_blocks) & (jnp.abs(kpos - qpos) <= WINDOW)


def _sink_rows(sink_ref, h, bq):
    r = lax.broadcasted_iota(jnp.int32, (Q_PER_KV * bq, 1), 0)
    return jnp.where(r < bq, sink_ref[h * Q_PER_KV], sink_ref[h * Q_PER_KV + 1])


def _lat_index(band, nb):
    if band:
        return lambda h, i, j: (h, jnp.clip(i - 1 + j, 0, nb - 1), 0)
    return lambda h, i, j: (h, j, 0)


def _flash_fwd(q, kl, vl, kc, vc, sink, band):
    kvh, g, t, dh = q.shape
    c = kc.shape[1]
    has_lat = kl is not None
    has_sink = sink is not None
    bq, bk = _attn_blocks(t, band)
    nb = t // bk
    nkv = (3 if band else nb) if has_lat else 1
    rows = g * bq

    def body(*refs):
        it = iter(refs)
        q_ref, kc_ref, vc_ref = next(it), next(it), next(it)
        kl_ref, vl_ref = (next(it), next(it)) if has_lat else (None, None)
        sink_ref = next(it) if has_sink else None
        o_ref, lse_ref, m_ref, l_ref, acc_ref = next(it), next(it), next(it), next(it), next(it)
        h, qi, kj = pl.program_id(0), pl.program_id(1), pl.program_id(2)
        qv = q_ref[0].reshape(rows, dh)

        def update(s, v):
            m_old = m_ref[...]
            m_new = jnp.maximum(m_old, jnp.max(s, axis=-1, keepdims=True))
            p = jnp.exp(s - m_new)
            alpha = jnp.exp(m_old - m_new)
            l_ref[...] = alpha * l_ref[...] + jnp.sum(p, axis=-1, keepdims=True)
            acc_ref[...] = alpha * acc_ref[...] + jnp.dot(p.astype(BF16), v.astype(BF16),
                                                          preferred_element_type=F32)
            m_ref[...] = m_new

        @pl.when(kj == 0)
        def _():
            if has_sink:
                m_ref[...] = _sink_rows(sink_ref, h, bq)
                l_ref[...] = jnp.ones_like(l_ref)
            else:
                m_ref[...] = jnp.full_like(m_ref, NEG_BIG)
                l_ref[...] = jnp.zeros_like(l_ref)
            acc_ref[...] = jnp.zeros_like(acc_ref)
            update(_qk_scores(qv, kc_ref[0]), vc_ref[0])

        if has_lat:
            s = _qk_scores(qv, kl_ref[0])
            if band:
                s = jnp.where(_band_mask(qi, qi - 1 + kj, bq, bk, nb, rows), s, NEG_BIG)
            update(s, vl_ref[0])

        @pl.when(kj == nkv - 1)
        def _():
            o_ref[0] = (acc_ref[...] / l_ref[...]).reshape(g, bq, dh)
            lse_ref[0] = (m_ref[...] + jnp.log(l_ref[...])).reshape(g, bq, 1)

    q_spec = pl.BlockSpec((1, g, bq, dh), lambda h, i, j: (h, 0, i, 0))
    r_spec = pl.BlockSpec((1, g, bq, 1), lambda h, i, j: (h, 0, i, 0))
    c_spec = pl.BlockSpec((1, c, dh), lambda h, i, j: (h, 0, 0))
    in_specs, args = [q_spec, c_spec, c_spec], [q, kc, vc]
    if has_lat:
        l_spec = pl.BlockSpec((1, bk, dh), _lat_index(band, nb))
        in_specs += [l_spec, l_spec]
        args += [kl, vl]
    if has_sink:
        in_specs.append(pl.BlockSpec(memory_space=pltpu.SMEM))
        args.append(sink)
    return pl.pallas_call(
        body, name="attn_fwd_band" if band else "attn_fwd", grid=(kvh, t // bq, nkv),
        in_specs=in_specs, out_specs=[q_spec, r_spec],
        out_shape=[jax.ShapeDtypeStruct(q.shape, F32), jax.ShapeDtypeStruct((kvh, g, t, 1), F32)],
        scratch_shapes=[pltpu.VMEM((rows, 1), F32), pltpu.VMEM((rows, 1), F32), pltpu.VMEM((rows, dh), F32)],
        compiler_params=_cparams(("parallel", "parallel", "arbitrary"), 4 * 4 * rows * max(bk, c)),
    )(*args)


def _flash_dq(q, kl, vl, kc, vc, sink, o, do, lse, band):
    kvh, g, t, dh = q.shape
    c = kc.shape[1]
    has_lat = kl is not None
    has_sink = sink is not None
    bq, bk = _attn_blocks(t, band)
    nb = t // bk
    nkv = (3 if band else nb) if has_lat else 1
    rows = g * bq

    def body(*refs):
        it = iter(refs)
        q_ref, o_ref, do_ref, lse_ref, kc_ref, vc_ref = (next(it) for _ in range(6))
        kl_ref, vl_ref = (next(it), next(it)) if has_lat else (None, None)
        sink_ref = next(it) if has_sink else None
        dq_ref, delta_ref, dsink_ref, acc_ref, dl_ref = (next(it) for _ in range(5))
        h, qi, kj = pl.program_id(0), pl.program_id(1), pl.program_id(2)
        qv = q_ref[0].reshape(rows, dh)
        dov = do_ref[0].reshape(rows, dh)
        lse_v = lse_ref[0].reshape(rows, 1)

        def update(s, k, v):
            p = jnp.exp(s - lse_v)
            dp = lax.dot_general(dov.astype(BF16), v.astype(BF16), (((1,), (1,)), ((), ())),
                                 preferred_element_type=F32)
            ds = p * (dp - dl_ref[...])
            acc_ref[...] += jnp.dot(ds.astype(BF16), k.astype(BF16), preferred_element_type=F32)

        @pl.when(kj == 0)
        def _():
            dl_ref[...] = jnp.sum(o_ref[0].reshape(rows, dh) * dov, axis=-1, keepdims=True)
            acc_ref[...] = jnp.zeros_like(acc_ref)
            update(_qk_scores(qv, kc_ref[0]), kc_ref[0], vc_ref[0])

        if has_lat:
            s = _qk_scores(qv, kl_ref[0])
            if band:
                s = jnp.where(_band_mask(qi, qi - 1 + kj, bq, bk, nb, rows), s, NEG_BIG)
            update(s, kl_ref[0], vl_ref[0])

        @pl.when(kj == nkv - 1)
        def _():
            dq_ref[0] = (acc_ref[...] * ATT_SCALE).reshape(g, bq, dh)
            delta_ref[0] = dl_ref[...].reshape(g, bq, 1)
            if has_sink:
                p_sink = jnp.exp(_sink_rows(sink_ref, h, bq) - lse_v)
                dsink_ref[0] = (-p_sink * dl_ref[...]).reshape(g, bq, 1)
            else:
                dsink_ref[0] = jnp.zeros((g, bq, 1), F32)

    q_spec = pl.BlockSpec((1, g, bq, dh), lambda h, i, j: (h, 0, i, 0))
    r_spec = pl.BlockSpec((1, g, bq, 1), lambda h, i, j: (h, 0, i, 0))
    c_spec = pl.BlockSpec((1, c, dh), lambda h, i, j: (h, 0, 0))
    in_specs, args = [q_spec, q_spec, q_spec, r_spec, c_spec, c_spec], [q, o, do, lse, kc, vc]
    if has_lat:
        l_spec = pl.BlockSpec((1, bk, dh), _lat_index(band, nb))
        in_specs += [l_spec, l_spec]
        args += [kl, vl]
    if has_sink:
        in_specs.append(pl.BlockSpec(memory_space=pltpu.SMEM))
        args.append(sink)
    row_shape = jax.ShapeDtypeStruct((kvh, g, t, 1), F32)
    return pl.pallas_call(
        body, name="attn_dq_band" if band else "attn_dq", grid=(kvh, t // bq, nkv),
        in_specs=in_specs, out_specs=[q_spec, r_spec, r_spec],
        out_shape=[jax.ShapeDtypeStruct(q.shape, F32), row_shape, row_shape],
        scratch_shapes=[pltpu.VMEM((rows, dh), F32), pltpu.VMEM((rows, 1), F32)],
        compiler_params=_cparams(("parallel", "parallel", "arbitrary"), 4 * 6 * rows * max(bk, c)),
    )(*args)


def _flash_dkv(q, do, lse, delta, k, v, band):
    kvh, g, t, dh = q.shape
    nk_rows = k.shape[1]
    if band:
        bq, bk = _attn_blocks(t, True)
    else:
        bq = _tile(t, 512, LANES)
        bk = _tile(nk_rows, 1024, LANES)
    nbq = t // bq
    nq = 3 if band else nbq
    rows = g * bq

    def body(q_ref, do_ref, lse_ref, delta_ref, k_ref, v_ref, dk_ref, dv_ref, dk_acc, dv_acc):
        ki, qj = pl.program_id(1), pl.program_id(2)
        qv = q_ref[0].reshape(rows, dh)
        dov = do_ref[0].reshape(rows, dh)

        @pl.when(qj == 0)
        def _():
            dk_acc[...] = jnp.zeros_like(dk_acc)
            dv_acc[...] = jnp.zeros_like(dv_acc)

        s = _qk_scores(qv, k_ref[0])
        if band:
            qblk = ki - 1 + qj
            qpos = qblk * bq + lax.broadcasted_iota(jnp.int32, (rows, bk), 0) % bq
            kpos = ki * bk + lax.broadcasted_iota(jnp.int32, (rows, bk), 1)
            s = jnp.where((qblk >= 0) & (qblk < nbq) & (jnp.abs(kpos - qpos) <= WINDOW), s, NEG_BIG)
        p = jnp.exp(s - lse_ref[0].reshape(rows, 1))
        dp = lax.dot_general(dov.astype(BF16), v_ref[0].astype(BF16), (((1,), (1,)), ((), ())),
                             preferred_element_type=F32)
        ds = p * (dp - delta_ref[0].reshape(rows, 1))
        tn = (((0,), (0,)), ((), ()))
        dv_acc[...] += lax.dot_general(p.astype(BF16), dov.astype(BF16), tn, preferred_element_type=F32)
        dk_acc[...] += lax.dot_general(ds.astype(BF16), qv.astype(BF16), tn, preferred_element_type=F32)

        @pl.when(qj == nq - 1)
        def _():
            dk_ref[0] = dk_acc[...] * ATT_SCALE
            dv_ref[0] = dv_acc[...]

    if band:
        q_index = lambda h, i, j: (h, 0, jnp.clip(i - 1 + j, 0, nbq - 1), 0)
    else:
        q_index = lambda h, i, j: (h, 0, j, 0)
    q_spec = pl.BlockSpec((1, g, bq, dh), q_index)
    r_spec = pl.BlockSpec((1, g, bq, 1), q_index)
    k_spec = pl.BlockSpec((1, bk, dh), lambda h, i, j: (h, i, 0))
    return pl.pallas_call(
        body, name="attn_dkv_band" if band else "attn_dkv", grid=(kvh, nk_rows // bk, nq),
        in_specs=[q_spec, q_spec, r_spec, r_spec, k_spec, k_spec], out_specs=[k_spec, k_spec],
        out_shape=[jax.ShapeDtypeStruct(k.shape, F32), jax.ShapeDtypeStruct(k.shape, F32)],
        scratch_shapes=[pltpu.VMEM((bk, dh), F32), pltpu.VMEM((bk, dh), F32)],
        compiler_params=_cparams(("parallel", "parallel", "arbitrary"), 4 * 6 * rows * bk),
    )(q, do, lse, delta, k, v)


def make_attention(band, has_lat, has_sink):
    def unpack(args):
        it = iter(args)
        q, kc, vc = next(it), next(it), next(it)
        kl, vl = (next(it), next(it)) if has_lat else (None, None)
        sink = next(it) if has_sink else None
        return q, kl, vl, kc, vc, sink

    @jax.custom_vjp
    def op(*args):
        q, kl, vl, kc, vc, sink = unpack(args)
        return _flash_fwd(q, kl, vl, kc, vc, sink, band)[0]

    def fwd(*args):
        q, kl, vl, kc, vc, sink = unpack(args)
        o, lse = _flash_fwd(q, kl, vl, kc, vc, sink, band)
        return o, (args, o, lse)

    def bwd(res, do):
        args, o, lse = res
        q, kl, vl, kc, vc, sink = unpack(args)
        dq, delta, dsink_rows = _flash_dq(q, kl, vl, kc, vc, sink, o, do, lse, band)
        dkc, dvc = _flash_dkv(q, do, lse, delta, kc, vc, False)
        grads = [dq, dkc, dvc]
        if has_lat:
            grads += list(_flash_dkv(q, do, lse, delta, kl, vl, band))
        if has_sink:
            grads.append(jnp.sum(dsink_rows, axis=(2, 3)).reshape(-1))
        return tuple(grads)

    op.defvjp(fwd, bwd)
    return op


attn_window = make_attention(True, True, True)
attn_global = make_attention(False, True, False)
attn_ctx_sink = make_attention(False, False, True)
attn_ctx = make_attention(False, False, False)


def _to_heads(z, n_heads):
    return z.reshape(z.shape[0], n_heads, HEAD_DIM).transpose(1, 0, 2)


def _q_heads(z):
    return z.reshape(z.shape[0], KV_HEADS, Q_PER_KV, HEAD_DIM).transpose(1, 2, 0, 3)


def _from_q_heads(o):
    return o.transpose(2, 0, 1, 3).reshape(o.shape[2], KV_HEADS * Q_PER_KV * HEAD_DIM)


def _scan_tile(t):
    return _tile(t, 512, SUBLANES)


def _scan_fwd_call(bre, bim, lre, lim, h0re, h0im, rev):
    t = bre.shape[0]
    tt = _scan_tile(t)
    nb = t // tt
    plane = bre.shape[1:]

    def body(bre_ref, bim_ref, lre_ref, lim_ref, h0re_ref, h0im_ref, sre_ref, sim_ref, h_ref):
        @pl.when(pl.program_id(0) == 0)
        def _():
            h_ref[0] = h0re_ref[...]
            h_ref[1] = h0im_ref[...]

        ar, ai = lre_ref[...], lim_ref[...]

        def step(j, carry):
            hr, hi = carry
            tj = (tt - 1 - j) if rev else j
            nr = ar * hr - ai * hi + bre_ref[tj]
            ni = ar * hi + ai * hr + bim_ref[tj]
            sre_ref[tj] = nr
            sim_ref[tj] = ni
            return nr, ni

        hr, hi = lax.fori_loop(0, tt, step, (h_ref[0], h_ref[1]), unroll=8)
        h_ref[0] = hr
        h_ref[1] = hi

    blk = pl.BlockSpec((tt,) + plane, (lambda i: (nb - 1 - i, 0, 0)) if rev else (lambda i: (i, 0, 0)))
    par = pl.BlockSpec(plane, lambda i: (0, 0))
    return pl.pallas_call(
        body, name="s5_scan_fwd", grid=(nb,), in_specs=[blk, blk, par, par, par, par], out_specs=[blk, blk],
        out_shape=[jax.ShapeDtypeStruct(bre.shape, F32)] * 2,
        scratch_shapes=[pltpu.VMEM((2,) + plane, F32)],
        compiler_params=_cparams(("arbitrary",), 4 * 4 * tt * plane[0] * plane[1]),
    )(bre, bim, lre, lim, h0re, h0im)


def _scan_bwd_call(gre, gim, sre, sim, lre, lim, h0re, h0im, rev):
    t = gre.shape[0]
    tt = _scan_tile(t)
    nb = t // tt
    plane = gre.shape[1:]
    down = not rev

    def body(gre_ref, gim_ref, sre_ref, sim_ref, lre_ref, lim_ref, h0re_ref, h0im_ref,
             dbre_ref, dbim_ref, dare_ref, daim_ref, dh0re_ref, dh0im_ref, carry_ref):
        i = pl.program_id(0)

        @pl.when(i == 0)
        def _():
            carry_ref[...] = jnp.zeros_like(carry_ref)

        ar, ai = lre_ref[...], lim_ref[...]

        def step(j, carry):
            gr, gi, dar, dai = carry
            tj = (tt - 1 - j) if down else j
            hr, hi = sre_ref[tj], sim_ref[tj]
            dar = dar + hr * gr + hi * gi
            dai = dai + hr * gi - hi * gr
            ngr = gre_ref[tj] + ar * gr + ai * gi
            ngi = gim_ref[tj] + ar * gi - ai * gr
            dbre_ref[tj] = ngr
            dbim_ref[tj] = ngi
            return ngr, ngi, dar, dai

        gr, gi, dar, dai = lax.fori_loop(
            0, tt, step, (carry_ref[0], carry_ref[1], carry_ref[2], carry_ref[3]), unroll=8)
        carry_ref[0] = gr
        carry_ref[1] = gi
        carry_ref[2] = dar
        carry_ref[3] = dai

        @pl.when(i == nb - 1)
        def _():
            hr, hi = h0re_ref[...], h0im_ref[...]
            dare_ref[...] = dar + hr * gr + hi * gi
            daim_ref[...] = dai + hr * gi - hi * gr
            dh0re_ref[...] = ar * gr + ai * gi
            dh0im_ref[...] = ar * gi - ai * gr

    blk = pl.BlockSpec((tt,) + plane, (lambda i: (nb - 1 - i, 0, 0)) if down else (lambda i: (i, 0, 0)))
    par = pl.BlockSpec(plane, lambda i: (0, 0))
    return pl.pallas_call(
        body, name="s5_scan_bwd", grid=(nb,), in_specs=[blk, blk, blk, blk, par, par, par, par],
        out_specs=[blk, blk, par, par, par, par],
        out_shape=[jax.ShapeDtypeStruct(gre.shape, F32)] * 2 + [jax.ShapeDtypeStruct(plane, F32)] * 4,
        scratch_shapes=[pltpu.VMEM((4,) + plane, F32)],
        compiler_params=_cparams(("arbitrary",), 4 * 6 * tt * plane[0] * plane[1]),
    )(gre, gim, sre, sim, lre, lim, h0re, h0im)


def make_scan(rev):
    @jax.custom_vjp
    def op(bre, bim, lre, lim, h0re, h0im):
        return tuple(_scan_fwd_call(bre, bim, lre, lim, h0re, h0im, rev))

    def fwd(bre, bim, lre, lim, h0re, h0im):
        sre, sim = _scan_fwd_call(bre, bim, lre, lim, h0re, h0im, rev)
        return (sre, sim), (sre, sim, lre, lim, h0re, h0im)

    def bwd(res, cts):
        sre, sim, lre, lim, h0re, h0im = res
        return tuple(_scan_bwd_call(cts[0], cts[1], sre, sim, lre, lim, h0re, h0im, rev))

    op.defvjp(fwd, bwd)
    return op


scan_up = make_scan(False)
scan_down = make_scan(True)


def _adamw_call(parts, w, m, v, name):
    r, c = w.shape
    tr = _tile(r, 256, SUBLANES)
    nparts = parts.shape[0]
    c1 = 1.0 - ADAM_B1 ** ADAM_STEP
    c2 = 1.0 - ADAM_B2 ** ADAM_STEP

    def body(p_ref, w_ref, m_ref, v_ref, g_ref, d_ref, nm_ref, nv_ref):
        g = p_ref[0].astype(F32)
        for s in range(1, nparts):
            g = g + p_ref[s].astype(F32)
        m1 = ADAM_B1 * m_ref[...] + (1.0 - ADAM_B1) * g
        v1 = ADAM_B2 * v_ref[...] + (1.0 - ADAM_B2) * jnp.square(g)
        g_ref[...] = g
        nm_ref[...] = m1
        nv_ref[...] = v1
        d_ref[...] = -ADAM_LR * ((m1 / c1) / (jnp.sqrt(v1 / c2) + ADAM_EPS) + ADAM_WD * w_ref[...])

    blk = pl.BlockSpec((tr, c), lambda i: (i, 0))
    return pl.pallas_call(
        body, name=name, grid=(r // tr,),
        in_specs=[pl.BlockSpec((nparts, tr, c), lambda i: (0, i, 0)), blk, blk, blk], out_specs=[blk] * 4,
        out_shape=[jax.ShapeDtypeStruct((r, c), F32)] * 4,
        compiler_params=_cparams(("parallel",), 4 * tr * c * (nparts + 7)),
    )(parts, w, m, v)


def _peer(k):
    x, y, c = lax.axis_index("x"), lax.axis_index("y"), lax.axis_index("c")
    px = 1 - x if k & 4 else x
    py = 1 - y if k & 2 else y
    pc = 1 - c if k & 1 else c
    return (px, py, pc), 4 * px + 2 * py + pc


def _my_slot():
    return 4 * lax.axis_index("x") + 2 * lax.axis_index("y") + lax.axis_index("c")


def _exchange_call(x, gather, name):
    slab = x.shape if gather else x.shape[1:]

    def body(x_ref, out_ref, send_sems, recv_sems, local_sem):
        me = _my_slot()
        mine = pltpu.make_async_copy(x_ref if gather else x_ref.at[me], out_ref.at[me], local_sem)
        mine.start()
        sends = []
        for k in range(1, N_DEV):
            peer, slot = _peer(k)
            cp = pltpu.make_async_remote_copy(
                src_ref=x_ref if gather else x_ref.at[slot], dst_ref=out_ref.at[me],
                send_sem=send_sems.at[k - 1], recv_sem=recv_sems.at[k - 1],
                device_id=peer, device_id_type=MESH)
            cp.start()
            sends.append(cp)
        for k in range(1, N_DEV):
            peer, slot = _peer(k)
            pltpu.make_async_remote_copy(
                src_ref=x_ref if gather else x_ref.at[slot], dst_ref=out_ref.at[slot],
                send_sem=send_sems.at[k - 1], recv_sem=recv_sems.at[k - 1],
                device_id=peer, device_id_type=MESH).wait_recv()
        for cp in sends:
            cp.wait_send()
        mine.wait()

    return pl.pallas_call(
        body, name=name,
        in_specs=[pl.BlockSpec(memory_space=pl.ANY)], out_specs=pl.BlockSpec(memory_space=pl.ANY),
        out_shape=jax.ShapeDtypeStruct((N_DEV,) + tuple(slab), x.dtype),
        scratch_shapes=[pltpu.SemaphoreType.DMA((N_DEV - 1,)), pltpu.SemaphoreType.DMA((N_DEV - 1,)),
                        pltpu.SemaphoreType.DMA],
    )(x)


def all_gather(x, name):
    return _exchange_call(x, True, name)


def all_to_all(x, name):
    return _exchange_call(x, False, name)


def _rope_tables(t):
    n_freq = HEAD_DIM // 4
    tok = jnp.arange(t)
    inv = ROPE_THETA ** (-jnp.arange(n_freq, dtype=F32) / n_freq)
    a_row = (tok // GRID_W).astype(F32)[:, None] * inv
    a_col = (tok % GRID_W).astype(F32)[:, None] * inv
    cos = jnp.concatenate([jnp.cos(a_row)] * 2 + [jnp.cos(a_col)] * 2, axis=1)
    sin = jnp.concatenate([-jnp.sin(a_row), jnp.sin(a_row), -jnp.sin(a_col), jnp.sin(a_col)], axis=1)
    return jnp.concatenate([cos, cos], axis=1), jnp.concatenate([sin, sin], axis=1)


def _block_diag(blocks):
    g, a, b = blocks.shape
    eye = jnp.eye(g, dtype=blocks.dtype)
    return jnp.einsum("gab,gk->gakb", blocks, eye).reshape(g * a, g * b)


def _ffn_fwd_calls(x, mod, g, w_in, w_out):
    shift, scale, gate = mod[0:1], mod[1:2], mod[2:3]
    h, = _row_call(lambda xt, gt, sh, sc: (_norm_mod_fn(xt, gt, sh, sc)[0].astype(BF16),),
                   [x], [g, shift, scale], 0, "ffn_norm")
    u = _mm(h, w_in, out_dtype=BF16, name="ffn_up")
    a, = _row_call(lambda ut: (_swiglu_fn(ut.astype(F32))[0].astype(BF16),), [u], [], 0, "ffn_act")
    y = _mm(a, w_out, name="ffn_down")
    out, = _row_call(functools.partial(_resid_fn, 0.5), [x, y], [gate], 0, "ffn_resid")
    return out, (h, u, a, y)


@jax.custom_vjp
def _ffn_half(x, mod, g, w_in, w_out):
    return _ffn_fwd_calls(x, mod, g, w_in, w_out)[0]


def _ffn_half_fwd(x, mod, g, w_in, w_out):
    out, saved = _ffn_fwd_calls(x, mod, g, w_in, w_out)
    return out, (x, mod, g, w_in, w_out, saved)


def _ffn_half_bwd(res, dxn):
    x, mod, g, w_in, w_out, (h, u, a, y) = res
    shift, scale, gate = mod[0:1], mod[1:2], mod[2:3]

    def resid_bwd(dt, yt, gt):
        return (0.5 * gt * dt).astype(BF16), 0.5 * jnp.sum(dt * yt, axis=0, keepdims=True)

    dy, dgate = _row_call(resid_bwd, [dxn, y], [gate], 1, "ffn_resid_bwd")
    da = _mm(dy, w_out, tb=True, out_dtype=BF16, name="ffn_down_dx")
    dw_out = _mm(a, dy, ta=True, out_dtype=w_out.dtype, name="ffn_down_dw")

    def act_bwd(ut, dat):
        return (jax.vjp(_swiglu_fn, ut.astype(F32))[1]((dat.astype(F32),))[0].astype(BF16),)

    du, = _row_call(act_bwd, [u, da], [], 0, "ffn_act_bwd")
    dh = _mm(du, w_in, tb=True, name="ffn_up_dx")
    dw_in = _mm(h, du, ta=True, out_dtype=w_in.dtype, name="ffn_up_dw")

    def norm_bwd(xt, dht, dt, gt, sh, sc):
        dx, dg, dsh, dsc = jax.vjp(_norm_mod_fn, xt, gt, sh, sc)[1]((dht,))
        return dx + dt, dg, dsh, dsc

    dx, dg, dshift, dscale = _row_call(norm_bwd, [x, dh, dxn], [g, shift, scale], 3, "ffn_norm_bwd")
    return dx, jnp.concatenate([dshift, dscale, dgate], axis=0), dg, dw_in, dw_out


_ffn_half.defvjp(_ffn_half_fwd, _ffn_half_bwd)


def _planes(z):
    return z.reshape(z.shape[0], SUBLANES, SSM_LANES // SUBLANES)


def _s5_discretize(a_re, a_im, log_dt, b_re, b_im):
    lam = lax.complex(a_re, a_im)
    dt = jnp.exp(log_dt)[:, None]
    lam_bar = jnp.exp(lam * dt)
    b_bar = ((lam_bar - 1.0) / lam)[..., None] * lax.complex(b_re, b_im)
    return lam_bar, b_bar


def _s5_branch(u_lat, u_ctx, w, l, with_ctx_out):
    zero = jnp.zeros((SUBLANES, SSM_LANES // SUBLANES), F32)
    lat_terms, ctx_terms = [], []
    for d, scan in enumerate((scan_up, scan_down)):
        lam_bar, b_bar = _s5_discretize(w["ssm_a_re"][l, d], w["ssm_a_im"][l, d], w["ssm_log_dt"][l, d],
                                        w["ssm_b_re"][l, d], w["ssm_b_im"][l, d])
        lre = jnp.real(lam_bar).reshape(zero.shape)
        lim = jnp.imag(lam_bar).reshape(zero.shape)
        b_t = jnp.swapaxes(b_bar, 1, 2)
        b_mat = jnp.concatenate([_block_diag(jnp.real(b_t)), _block_diag(jnp.imag(b_t))], axis=1)
        c_re = _block_diag(jnp.swapaxes(w["ssm_c_re"][l, d], 1, 2))
        c_im = _block_diag(jnp.swapaxes(w["ssm_c_im"][l, d], 1, 2))
        bu_c = linear(u_ctx, b_mat)
        sc_re, sc_im = scan(_planes(bu_c[:, :SSM_LANES]), _planes(bu_c[:, SSM_LANES:]), lre, lim, zero, zero)
        last = 0 if d == 1 else u_ctx.shape[0] - 1
        bu_l = linear(u_lat, b_mat)
        sl_re, sl_im = scan(_planes(bu_l[:, :SSM_LANES]), _planes(bu_l[:, SSM_LANES:]), lre, lim,
                            sc_re[last], sc_im[last])
        flat = lambda s: s.reshape(s.shape[0], SSM_LANES)
        lat_terms += [linear(flat(sl_re), c_re), linear(flat(sl_im), c_im)]
        if with_ctx_out:
            ctx_terms += [linear(flat(sc_re), c_re), linear(flat(sc_im), c_im)]
    d_skip = w["ssm_d"][l][None, :]

    def out(terms, u):
        y, = s5_pre(*terms, u, d_skip)
        return glu(linear(y, w["glu_w"][l]))[0]

    return out(lat_terms, u_lat), (out(ctx_terms, u_ctx) if with_ctx_out else None)


def _pool_branch(xa, w, l):
    y = linear(pool_diff(xa), _block_diag(w["pool_w"][l]))
    return scale_rows(y, w["pool_scale"][l][None, :])[0]


def _merge_branches(branches, gate_logits, w, l):
    zs = [linear_b(y, w["branch_w"][l, k]) for k, y in enumerate(branches)]
    return linear(merge(gate_logits, *zs)[0], w["out_w"][l])


def _token_mixer(h, hc, cos, sin, w, l, with_ctx_out):
    w_in, w_gate = w["w_in"][l][:, :O_GATE], w["w_in"][l][:, O_GATE:]
    p = linear(h, w_in)
    p_gate = linear_b(h, w_gate)
    pc = linear(hc, w_in if with_ctx_out else w_in[:, :CTX_COLS])
    pc_gate = linear_b(hc, w_gate) if with_ctx_out else None
    sink = w["win_sink"][l]
    q_g = jnp.tile(w["qk_norm"][l, 0], KV_HEADS * Q_PER_KV)[None, :]
    k_g = jnp.tile(w["qk_norm"][l, 1], KV_HEADS)[None, :]
    k_win_c = _to_heads(pc[:, O_KB:O_VB], KV_HEADS)
    v_win_c = _to_heads(pc[:, O_VB:O_UC], KV_HEADS)
    k_glb_c = _to_heads(head_norm(pc[:, O_KD:O_VD], k_g)[0], KV_HEADS)
    v_glb_c = _to_heads(pc[:, O_VD:CTX_COLS], KV_HEADS)
    y_a = _pool_branch(p[:, O_XA:O_GATE], w, l)
    q_win = _q_heads(rope(p[:, O_QB:O_QD], cos, sin)[0])
    k_win = _to_heads(rope(p[:, O_KB:O_VB], cos, sin)[0], KV_HEADS)
    v_win = _to_heads(p[:, O_VB:O_UC], KV_HEADS)
    y_b = _from_q_heads(attn_window(q_win, k_win_c, v_win_c, k_win, v_win, sink))
    y_c, y_c_ctx = _s5_branch(p[:, O_UC:O_KD], pc[:, O_UC:O_KD], w, l, with_ctx_out)
    q_glb = _q_heads(norm_rope(p[:, O_QD:O_XA], cos, sin, q_g)[0])
    k_glb = _to_heads(norm_rope(p[:, O_KD:O_VD], cos, sin, k_g)[0], KV_HEADS)
    v_glb = _to_heads(p[:, O_VD:CTX_COLS], KV_HEADS)
    y_d = _from_q_heads(attn_global(q_glb, k_glb_c, v_glb_c, k_glb, v_glb))
    y = _merge_branches((y_a, y_b, y_c, y_d), p_gate, w, l)
    if not with_ctx_out:
        return y, None
    y_a_c = _pool_branch(pc[:, O_XA:O_GATE], w, l)
    y_b_c = _from_q_heads(attn_ctx_sink(_q_heads(pc[:, O_QB:O_QD]), k_win_c, v_win_c, sink))
    q_glb_c = _q_heads(head_norm(pc[:, O_QD:O_XA], q_g)[0])
    y_d_c = _from_q_heads(attn_ctx(q_glb_c, k_glb_c, v_glb_c))
    return y, _merge_branches((y_a_c, y_b_c, y_c_ctx, y_d_c), pc_gate, w, l)


def local_loss(w, x, c, ctx, target):
    depth = w["w_mod"].shape[0]
    cos, sin = _rope_tables(x.shape[0])
    cond = jnp.concatenate([c, w["c_ctx"][None, :], jnp.zeros((COND_ROWS - 2, D_MODEL), F32)], axis=0)
    s_all, = silu_rows(cond)
    for l in range(depth):
        last = l == depth - 1
        m_all = (linear(s_all, w["w_mod"][l]) + w["b_mod"][l][None, :]).reshape(COND_ROWS, N_SUB, 3, D_MODEL)
        m, mc = m_all[0], m_all[1]
        g = w["norm_g"][l][:, None, :]
        x = _ffn_half(x, m[0], g[0], w["ffn_in"][l, 0], w["ffn_out"][l, 0])
        ctx = _ffn_half(ctx, mc[0], g[0], w["ffn_in"][l, 0], w["ffn_out"][l, 0])
        h, = norm_mod(x, g[1], m[1, 0:1], m[1, 1:2])
        hc, = norm_mod(ctx, g[1], mc[1, 0:1], mc[1, 1:2])
        y, y_ctx = _token_mixer(h, hc, cos, sin, w, l, not last)
        x, = resid_full(x, y, m[1, 2:3])
        if not last:
            ctx, = resid_full(ctx, y_ctx, mc[1, 2:3])
        x = _ffn_half(x, m[2], g[2], w["ffn_in"][l, 1], w["ffn_out"][l, 1])
        if not last:
            ctx = _ffn_half(ctx, mc[2], g[2], w["ffn_in"][l, 1], w["ffn_out"][l, 1])
    return jnp.sum(loss_rows(x, target, w["final_g"][None, :])[0])


PACK_COLS = 1024


def _pack(arrays):
    flat = jnp.concatenate([a.reshape(-1) for a in arrays])
    pad = (-flat.shape[0]) % (PACK_COLS * 16)
    return jnp.pad(flat, (0, pad)).reshape(-1, PACK_COLS)


def _unpack(slab, shapes):
    flat = slab.reshape(-1)
    out, off = [], 0
    for s in shapes:
        n = math.prod(s)
        out.append(flat[off:off + n].reshape(s))
        off += n
    return out


def _full_from_shards(gathered, shard_shape, axis):
    z = jnp.moveaxis(gathered.reshape((N_DEV,) + tuple(shard_shape)), 0, axis)
    shape = list(shard_shape)
    shape[axis] *= N_DEV
    return z.reshape(shape)


def _shards_from_full(full, axis):
    shape = list(full.shape)
    shape[axis:axis + 1] = [N_DEV, shape[axis] // N_DEV]
    return jnp.moveaxis(full.reshape(shape), axis, 0)


def kernel(x, c, ctx, c_ctx, w_mod, b_mod, norm_g, ffn_in, ffn_out, w_in, win_sink, qk_norm, pool_w, pool_scale, ssm_a_re, ssm_a_im, ssm_log_dt, ssm_b_re, ssm_b_im, ssm_c_re, ssm_c_im, ssm_d, glu_w, branch_w, out_w, final_g, loss_target, m_c_ctx, m_w_mod, m_b_mod, m_norm_g, m_ffn_in, m_ffn_out, m_w_in, m_win_sink, m_qk_norm, m_pool_w, m_pool_scale, m_ssm_a_re, m_ssm_a_im, m_ssm_log_dt, m_ssm_b_re, m_ssm_b_im, m_ssm_c_re, m_ssm_c_im, m_ssm_d, m_glu_w, m_branch_w, m_out_w, m_final_g, v_c_ctx, v_w_mod, v_b_mod, v_norm_g, v_ffn_in, v_ffn_out, v_w_in, v_win_sink, v_qk_norm, v_pool_w, v_pool_scale, v_ssm_a_re, v_ssm_a_im, v_ssm_log_dt, v_ssm_b_re, v_ssm_b_im, v_ssm_c_re, v_ssm_c_im, v_ssm_d, v_glu_w, v_branch_w, v_out_w, v_final_g):
    given = dict(locals())
    wts = {n: given[n] for n in WEIGHTS}
    mom = {n: given["m_" + n] for n in WEIGHTS}
    var = {n: given["v_" + n] for n in WEIGHTS}
    me = _my_slot()

    shard_shapes = [wts[n].shape for n in SHARDED]
    w_slab = _pack([wts[n] for n in SHARDED])
    gathered = all_gather(w_slab.astype(BF16), "gather_weights")
    parts = [_unpack(gathered[s], shard_shapes) for s in range(N_DEV)]
    full = dict(wts)
    for i, n in enumerate(SHARDED):
        full[n] = _full_from_shards(jnp.stack([parts[s][i] for s in range(N_DEV)]), wts[n].shape, SHARD_AXIS[n])
    g_slab = _pack([norm_g])
    g_all = all_gather(g_slab, "gather_norm_g")
    full["norm_g"] = _full_from_shards(
        jnp.stack([_unpack(g_all[s], [norm_g.shape])[0] for s in range(N_DEV)]), norm_g.shape, 2)

    loss, (gw, gx) = jax.value_and_grad(local_loss, argnums=(0, 1))(full, x[0], c, ctx[0], loss_target[0])
    loss = lax.psum(loss, ("x", "y", "c"))

    dest = [_shards_from_full(gw[n], SHARD_AXIS[n]) for n in SHARDED]
    send = jnp.stack([_pack([d[s] for d in dest]) for s in range(N_DEV)])
    big_parts = all_to_all(send, "exchange_grads")
    small_names = SMALL + ("norm_g",)
    small_shapes = [gw[n].shape for n in small_names]
    small_parts = all_gather(_pack([gw[n] for n in small_names]), "gather_small_grads")

    big = _adamw_call(big_parts, w_slab, _pack([mom[n] for n in SHARDED]), _pack([var[n] for n in SHARDED]),
                      "adamw_sharded")
    big = [_unpack(b, shard_shapes) for b in big]
    col = me * norm_g.shape[2]

    def small_slab(src, shard_src):
        padded = jnp.zeros((norm_g.shape[0], norm_g.shape[1], norm_g.shape[2] * N_DEV), F32)
        padded = lax.dynamic_update_slice(padded, shard_src, (0, 0, col))
        return _pack([src[n] for n in SMALL] + [padded])

    small = _adamw_call(small_parts, small_slab(wts, norm_g), small_slab(mom, m_norm_g),
                        small_slab(var, v_norm_g), "adamw_small")
    small = [_unpack(s, small_shapes) for s in small]

    outs = {}
    for kind in range(4):
        for i, n in enumerate(SHARDED):
            outs[(kind, n)] = big[kind][i]
        for i, n in enumerate(small_names):
            val = small[kind][i]
            if n == "norm_g":
                val = lax.dynamic_slice(val, (0, 0, col), norm_g.shape)
            outs[(kind, n)] = val
    return (loss, gx[None], *[outs[(k, n)] for k in range(4) for n in WEIGHTS])
```

```python
import functools
import math

import jax
import jax.numpy as jnp
from jax import lax
from jax.experimental import pallas as pl
from jax.experimental.pallas import tpu as pltpu

F32 = jnp.float32
BF16 = jnp.bfloat16

D_MODEL = 1024
GRID_W = 64
HEAD_DIM = 64
N_BRANCH = 4
BRANCH_W = D_MODEL // N_BRANCH
WINDOW = 128
ROPE_THETA = 10000.0
EPS = 1e-6
D_FF = 2816
N_SUB = 3
POOL_WINDOWS = (2, 4, 8, 16)
POOL_GROUP = BRANCH_W // len(POOL_WINDOWS)
KV_HEADS = 2
Q_PER_KV = 2
SSM_GROUP = 16
SSM_GROUPS = BRANCH_W // SSM_GROUP
SSM_STATE = 64
SSM_LANES = SSM_GROUPS * SSM_STATE
O_KB, O_VB, O_UC, O_KD, O_VD, CTX_COLS = 0, 128, 256, 512, 640, 768
O_QB, O_QD, O_XA, O_GATE = 768, 1024, 1280, 1536
IN_W = O_GATE + N_BRANCH * D_MODEL

ADAM_LR, ADAM_B1, ADAM_B2, ADAM_EPS, ADAM_WD, ADAM_STEP = 0.001, 0.9, 0.999, 1e-08, 0.01, 10

N_DEV = 8
MESH = pl.DeviceIdType.MESH

V7X_VMEM_BYTES = 64 * 1024 * 1024
SUBLANES = 8
LANES = 128
NEG_BIG = -1e30
COND_ROWS = 128

SHARDED = ("w_mod", "ffn_in", "ffn_out", "w_in", "glu_w", "branch_w", "out_w")
SHARD_AXIS = {"w_mod": 2, "ffn_in": 3, "ffn_out": 2, "w_in": 2, "glu_w": 2, "branch_w": 3, "out_w": 1}
SMALL = ("c_ctx", "b_mod", "win_sink", "qk_norm", "pool_w", "pool_scale", "ssm_a_re", "ssm_a_im",
         "ssm_log_dt", "ssm_b_re", "ssm_b_im", "ssm_c_re", "ssm_c_im", "ssm_d", "final_g")
WEIGHTS = ("c_ctx", "w_mod", "b_mod", "norm_g", "ffn_in", "ffn_out", "w_in", "win_sink", "qk_norm",
           "pool_w", "pool_scale", "ssm_a_re", "ssm_a_im", "ssm_log_dt", "ssm_b_re", "ssm_b_im",
           "ssm_c_re", "ssm_c_im", "ssm_d", "glu_w", "branch_w", "out_w", "final_g")


def _tile(n, cap, mult):
    if n <= cap:
        return n
    t = (cap // mult) * mult
    while t >= mult:
        if n % t == 0:
            return t
        t -= mult
    return n


def _cparams(sem, tile_bytes, resident_bytes=0):
    limit = int(min(V7X_VMEM_BYTES - 8 * 2 ** 20,
                    max(32 * 2 ** 20, 3 * tile_bytes + resident_bytes + 8 * 2 ** 20)))
    return pltpu.CompilerParams(dimension_semantics=sem, vmem_limit_bytes=limit)


def _mm(a, b, ta=False, tb=False, out_dtype=F32, name="mm"):
    m, k = (a.shape[1], a.shape[0]) if ta else a.shape
    n = b.shape[0] if tb else b.shape[1]
    assert (b.shape[1] if tb else b.shape[0]) == k
    tm, tn, tk = _tile(m, 1024, LANES), _tile(n, 1536, LANES), _tile(k, 1536, LANES)
    nk = k // tk
    dims = (((0 if ta else 1,), (1 if tb else 0,)), ((), ()))

    def body(a_ref, b_ref, o_ref, acc_ref):
        kk = pl.program_id(2)

        @pl.when(kk == 0)
        def _():
            acc_ref[...] = jnp.zeros_like(acc_ref)

        acc_ref[...] += lax.dot_general(a_ref[...].astype(BF16), b_ref[...].astype(BF16), dims,
                                        preferred_element_type=F32)

        @pl.when(kk == nk - 1)
        def _():
            o_ref[...] = acc_ref[...].astype(o_ref.dtype)

    a_spec = (pl.BlockSpec((tk, tm), lambda i, j, kk: (kk, i)) if ta
              else pl.BlockSpec((tm, tk), lambda i, j, kk: (i, kk)))
    b_spec = (pl.BlockSpec((tn, tk), lambda i, j, kk: (j, kk)) if tb
              else pl.BlockSpec((tk, tn), lambda i, j, kk: (kk, j)))
    tile_bytes = (a.dtype.itemsize * tm * tk + b.dtype.itemsize * tk * tn
                  + jnp.dtype(out_dtype).itemsize * tm * tn + 2 * tm * tn)
    return pl.pallas_call(
        body, name=name, grid=(m // tm, n // tn, nk),
        in_specs=[a_spec, b_spec], out_specs=pl.BlockSpec((tm, tn), lambda i, j, kk: (i, j)),
        out_shape=jax.ShapeDtypeStruct((m, n), out_dtype),
        scratch_shapes=[pltpu.VMEM((tm, tn), F32)],
        compiler_params=_cparams(("parallel", "parallel", "arbitrary"), tile_bytes),
    )(a, b)


def make_linear(out_dtype):
    @jax.custom_vjp
    def op(x, w):
        return _mm(x, w, out_dtype=out_dtype, name="linear_fwd")

    def fwd(x, w):
        return op(x, w), (x, w)

    def bwd(res, dy):
        x, w = res
        return (_mm(dy, w, tb=True, out_dtype=x.dtype, name="linear_dx"),
                _mm(x, dy, ta=True, out_dtype=w.dtype, name="linear_dw"))

    op.defvjp(fwd, bwd)
    return op


linear = make_linear(F32)
linear_b = make_linear(BF16)


ROW_TILE_BYTES = 6 * 2 ** 20


def _row_tile(t, row_bytes):
    tm = 1024
    while tm > 2 * SUBLANES and tm * row_bytes > ROW_TILE_BYTES:
        tm //= 2
    return _tile(t, tm, 2 * SUBLANES)


def _row_call(fn, rows, params, n_reduce, name):
    t = rows[0].shape[0]
    out_avals = jax.eval_shape(fn, *rows, *params)
    n_out = len(out_avals) - n_reduce
    row_avals, red_avals = out_avals[:n_out], out_avals[n_out:]
    row_bytes = sum(r.shape[1] * r.dtype.itemsize for r in (*rows, *row_avals))
    tm = _row_tile(t, row_bytes)
    n_in = len(rows) + len(params)

    def body(*refs):
        outs = fn(*[r[...] for r in refs[:n_in]])
        o_refs = refs[n_in:]
        for o_ref, o in zip(o_refs[:n_out], outs[:n_out]):
            o_ref[...] = o.astype(o_ref.dtype)
        if n_reduce:
            @pl.when(pl.program_id(0) == 0)
            def _():
                for r in o_refs[n_out:]:
                    r[...] = jnp.zeros_like(r)

            for r, o in zip(o_refs[n_out:], outs[n_out:]):
                r[...] += o.astype(r.dtype)

    in_specs = ([pl.BlockSpec((tm, r.shape[1]), lambda i: (i, 0)) for r in rows]
                + [pl.BlockSpec(p.shape, lambda i: (0, 0)) for p in params])
    out_specs = ([pl.BlockSpec((tm, o.shape[1]), lambda i: (i, 0)) for o in row_avals]
                 + [pl.BlockSpec(o.shape, lambda i: (0, 0)) for o in red_avals])
    return pl.pallas_call(
        body, name=name, grid=(t // tm,), in_specs=in_specs, out_specs=out_specs,
        out_shape=[jax.ShapeDtypeStruct(o.shape, o.dtype) for o in out_avals],
        compiler_params=_cparams(("arbitrary",) if n_reduce else ("parallel",), tm * row_bytes),
    )(*rows, *params)


def rowwise(fn, n_rows, name):
    @jax.custom_vjp
    def op(*args):
        return tuple(_row_call(fn, args[:n_rows], args[n_rows:], 0, name + "_fwd"))

    def fwd(*args):
        return op(*args), args

    def bwd(args, cts):
        n_ct = len(cts)

        def bwd_fn(*a):
            r, ct, p = a[:n_rows], a[n_rows:n_rows + n_ct], a[n_rows + n_ct:]
            return jax.vjp(fn, *r, *p)[1](tuple(ct))

        return tuple(_row_call(bwd_fn, (*args[:n_rows], *cts), args[n_rows:], len(args) - n_rows,
                               name + "_bwd"))

    op.defvjp(fwd, bwd)
    return op


@functools.partial(jax.custom_vjp, nondiff_argnums=(1,))
def _swap_lanes(x, k):
    n = x.shape[-1]
    lane = lax.broadcasted_iota(jnp.int32, x.shape, x.ndim - 1)
    return jnp.where((lane & k) == 0, pltpu.roll(x, n - k, x.ndim - 1), pltpu.roll(x, k, x.ndim - 1))


def _swap_lanes_fwd(x, k):
    return _swap_lanes(x, k), None


def _swap_lanes_bwd(k, _, g):
    return (_swap_lanes(g, k),)


_swap_lanes.defvjp(_swap_lanes_fwd, _swap_lanes_bwd)


def _head_sum(x):
    s = x
    k = 1
    while k < HEAD_DIM:
        s = s + _swap_lanes(s, k)
        k *= 2
    return s


def _rms(x):
    return x * lax.rsqrt(jnp.mean(x * x, axis=-1, keepdims=True) + EPS)


def _norm_mod_fn(x, g, shift, scale):
    return ((_rms(x) * g) * (1.0 + scale) + shift,)


def _swiglu_fn(u):
    gate, up = u[:, :D_FF], u[:, D_FF:]
    return (jax.nn.silu(gate) * up,)


def _resid_fn(coef, x, y, gate):
    return (x + (coef * gate) * y,)


def _scale_fn(y, s):
    return (y * s,)


def _tile_lanes(tab, width):
    return tab if tab.shape[1] == width else jnp.concatenate([tab] * (width // tab.shape[1]), axis=1)


def _rope_fn(x, cos, sin):
    w = x.shape[1]
    return (x * _tile_lanes(cos, w) + _swap_lanes(x, 16) * _tile_lanes(sin, w),)


def _head_norm(x, g):
    ms = _head_sum(x * x) * (1.0 / HEAD_DIM)
    return x * lax.rsqrt(ms + EPS) * g


def _norm_rope_fn(x, cos, sin, g):
    return _rope_fn(_head_norm(x, g), cos, sin)


def _head_norm_fn(x, g):
    return (_head_norm(x, g),)


def _merge_fn(gl, z0, z1, z2, z3):
    zs = (z0, z1, z2, z3)
    terms = [jax.nn.sigmoid(gl[:, k * D_MODEL:(k + 1) * D_MODEL].astype(F32)) * zs[k].astype(F32)
             for k in range(N_BRANCH)]
    return (sum(terms[1:], terms[0]),)


def _s5_pre_fn(y0r, y0i, y1r, y1i, u, d):
    return (jax.nn.gelu(((y0r - y0i) + (y1r - y1i)) + d * u),)


def _glu_fn(z):
    return (z[:, :BRANCH_W] * jax.nn.sigmoid(z[:, BRANCH_W:]),)


def _silu_fn(x):
    return (jax.nn.silu(x),)


def _loss_fn(x, tgt, g):
    err = jnp.square(_rms(x) * g - tgt)
    return (0.5 * jnp.mean(err, axis=-1, keepdims=True),)


norm_mod = rowwise(_norm_mod_fn, 1, "norm_mod")
swiglu_act = rowwise(_swiglu_fn, 1, "swiglu")
resid_half = rowwise(functools.partial(_resid_fn, 0.5), 2, "resid_half")
resid_full = rowwise(functools.partial(_resid_fn, 1.0), 2, "resid_full")
scale_rows = rowwise(_scale_fn, 1, "pool_scale")
rope = rowwise(_rope_fn, 3, "rope")
norm_rope = rowwise(_norm_rope_fn, 3, "norm_rope")
head_norm = rowwise(_head_norm_fn, 1, "head_norm")
merge = rowwise(_merge_fn, 5, "merge")
s5_pre = rowwise(_s5_pre_fn, 5, "s5_pre")
glu = rowwise(_glu_fn, 1, "glu")
silu_rows = rowwise(_silu_fn, 1, "silu")
loss_rows = rowwise(_loss_fn, 2, "loss_head")


POOL_HALO = 16


def _pool_call(xa, adjoint, name):
    n, width = xa.shape
    tm = _tile(n, 512, POOL_HALO)
    halo_blocks = tm // POOL_HALO
    last_halo = n // POOL_HALO - 1
    ext_rows = tm + 2 * POOL_HALO

    def body(prev_ref, cur_ref, next_ref, o_ref, ext_ref):
        i = pl.program_id(0)
        ext_ref[0:POOL_HALO] = prev_ref[...]
        ext_ref[POOL_HALO:POOL_HALO + tm] = cur_ref[...]
        ext_ref[POOL_HALO + tm:ext_rows] = next_ref[...]
        e = ext_ref[...]
        row = lax.broadcasted_iota(jnp.int32, e.shape, 0) + (i * tm - POOL_HALO)
        grp = lax.broadcasted_iota(jnp.int32, e.shape, 1) // POOL_GROUP
        win = jnp.where(grp == 0, POOL_WINDOWS[0],
                        jnp.where(grp == 1, POOL_WINDOWS[1], jnp.where(grp == 2, POOL_WINDOWS[2], POOL_WINDOWS[3])))
        valid = (row >= 0) & (row < n)
        lo = jnp.clip(row - win // 2, 0, n)
        hi = jnp.clip(row - win // 2 + win, 0, n)
        cnt = jnp.maximum((hi - lo).astype(F32), 1.0)
        e0 = jnp.where(valid, e / cnt if adjoint else e, 0.0)

        def shift(z, s):
            return pltpu.roll(z, s % ext_rows, 0)

        s2 = e0 + shift(e0, -1 if adjoint else 1)
        s4 = shift(s2, 1) + shift(s2, -1)
        s8 = shift(s4, 2) + shift(s4, -2)
        s16 = shift(s8, 4) + shift(s8, -4)
        s = jnp.where(grp == 0, s2, jnp.where(grp == 1, s4, jnp.where(grp == 2, s8, s16)))
        out = (s - e) if adjoint else (s / cnt - e)
        o_ref[...] = out[POOL_HALO:POOL_HALO + tm]

    return pl.pallas_call(
        body, name=name, grid=(n // tm,),
        in_specs=[pl.BlockSpec((POOL_HALO, width), lambda i: (jnp.maximum(i * halo_blocks - 1, 0), 0)),
                  pl.BlockSpec((tm, width), lambda i: (i, 0)),
                  pl.BlockSpec((POOL_HALO, width), lambda i: (jnp.minimum((i + 1) * halo_blocks, last_halo), 0))],
        out_specs=pl.BlockSpec((tm, width), lambda i: (i, 0)),
        out_shape=jax.ShapeDtypeStruct((n, width), F32),
        scratch_shapes=[pltpu.VMEM((ext_rows, width), F32)],
        compiler_params=_cparams(("parallel",), 4 * 4 * ext_rows * width),
    )(xa, xa, xa)


@jax.custom_vjp
def pool_diff(xa):
    return _pool_call(xa, False, "pool_fwd")


pool_diff.defvjp(lambda xa: (pool_diff(xa), None), lambda _, g: (_pool_call(g, True, "pool_bwd"),))


ATT_SCALE = HEAD_DIM ** -0.5


def _attn_blocks(t, band):
    if band:
        b = _tile(t, 256, LANES)
        return b, b
    return _tile(t, 512, LANES), _tile(t, 1024, LANES)


def _qk_scores(q, k):
    return lax.dot_general((q * ATT_SCALE).astype(BF16), k.astype(BF16), (((1,), (1,)), ((), ())),
                           preferred_element_type=F32)


def _band_mask(qi, kblk, bq, bk, n_blocks, r0, rows):
    qpos = qi * bq + (r0 + lax.broadcasted_iota(jnp.int32, (rows, bk), 0)) % bq
    kpos = kblk * bk + lax.broadcasted_iota(jnp.int32, (rows, bk), 1)
    return (kblk >= 0) & (kblk < n_blocks) & (jnp.abs(kpos - qpos) <= WINDOW)


def _sink_rows(sink_ref, h, bq):
    r = lax.broadcasted_iota(jnp.int32, (Q_PER_KV * bq, 1), 0)
    return jnp.where(r < bq, sink_ref[h * Q_PER_KV], sink_ref[h * Q_PER_KV + 1])


def _lat_index(band, nb):
    if band:
        return lambda h, i, j: (h, jnp.clip(i - 1 + j, 0, nb - 1), 0)
    return lambda h, i, j: (h, j, 0)


ATT_SUB_ROWS = 256


def _flash_fwd(q, klt, vl, kct, vc, sink, band):
    kvh, g, t, dh = q.shape
    c = kct.shape[2]
    has_lat = klt is not None
    has_sink = sink is not None
    bq, bk = _attn_blocks(t, band)
    if not band:
        bk = _tile(t, 2048, LANES)
    nb = t // bk
    nkv = (3 if band else nb) if has_lat else 1
    rows = g * bq
    sub = min(ATT_SUB_ROWS, rows)

    def body(*refs):
        it = iter(refs)
        q_ref, kc_ref, vc_ref = next(it), next(it), next(it)
        kl_ref, vl_ref = (next(it), next(it)) if has_lat else (None, None)
        sink_ref = next(it) if has_sink else None
        o_ref, lse_ref, m_ref, l_ref, acc_ref = next(it), next(it), next(it), next(it), next(it)
        h, qi, kj = pl.program_id(0), pl.program_id(1), pl.program_id(2)
        qv = (q_ref[0].reshape(rows, dh) * ATT_SCALE).astype(BF16)

        def part(kt_ref, v_ref, masked):
            kt, v = kt_ref[0].astype(BF16), v_ref[0].astype(BF16)
            m_all, l_all, acc_all = m_ref[...], l_ref[...], acc_ref[...]
            m_out, l_out, acc_out = [], [], []
            for r0 in range(0, rows, sub):
                s = jnp.dot(qv[r0:r0 + sub], kt, preferred_element_type=F32)
                if masked:
                    s = jnp.where(_band_mask(qi, qi - 1 + kj, bq, bk, nb, r0, sub), s, NEG_BIG)
                m_old = m_all[r0:r0 + sub]
                m_new = jnp.maximum(m_old, jnp.max(s, axis=-1, keepdims=True))
                p = jnp.exp(s - m_new)
                alpha = jnp.exp(m_old - m_new)
                m_out.append(m_new)
                l_out.append(alpha * l_all[r0:r0 + sub] + jnp.sum(p, axis=-1, keepdims=True))
                acc_out.append(alpha * acc_all[r0:r0 + sub]
                               + jnp.dot(p.astype(BF16), v, preferred_element_type=F32))
            m_ref[...] = jnp.concatenate(m_out, axis=0)
            l_ref[...] = jnp.concatenate(l_out, axis=0)
            acc_ref[...] = jnp.concatenate(acc_out, axis=0)

        @pl.when(kj == 0)
        def _():
            if has_sink:
                m_ref[...] = _sink_rows(sink_ref, h, bq)
                l_ref[...] = jnp.ones_like(l_ref)
            else:
                m_ref[...] = jnp.full_like(m_ref, NEG_BIG)
                l_ref[...] = jnp.zeros_like(l_ref)
            acc_ref[...] = jnp.zeros_like(acc_ref)
            part(kc_ref, vc_ref, False)

        if has_lat:
            part(kl_ref, vl_ref, band)

        @pl.when(kj == nkv - 1)
        def _():
            o_ref[0] = (acc_ref[...] / l_ref[...]).reshape(g, bq, dh)
            lse_ref[0] = (m_ref[...] + jnp.log(l_ref[...])).reshape(g, bq, 1)

    q_spec = pl.BlockSpec((1, g, bq, dh), lambda h, i, j: (h, 0, i, 0))
    r_spec = pl.BlockSpec((1, g, bq, 1), lambda h, i, j: (h, 0, i, 0))
    c_spec = pl.BlockSpec((1, c, dh), lambda h, i, j: (h, 0, 0))
    ct_spec = pl.BlockSpec((1, dh, c), lambda h, i, j: (h, 0, 0))
    in_specs, args = [q_spec, ct_spec, c_spec], [q, kct, vc]
    if has_lat:
        lat = _lat_index(band, nb)
        in_specs += [pl.BlockSpec((1, dh, bk), lambda h, i, j: (h, 0, lat(h, i, j)[1])),
                     pl.BlockSpec((1, bk, dh), lat)]
        args += [klt, vl]
    if has_sink:
        in_specs.append(pl.BlockSpec(memory_space=pltpu.SMEM))
        args.append(sink)
    return pl.pallas_call(
        body, name="attn_fwd_band" if band else "attn_fwd", grid=(kvh, t // bq, nkv),
        in_specs=in_specs, out_specs=[q_spec, r_spec],
        out_shape=[jax.ShapeDtypeStruct(q.shape, F32), jax.ShapeDtypeStruct((kvh, g, t, 1), F32)],
        scratch_shapes=[pltpu.VMEM((rows, 1), F32), pltpu.VMEM((rows, 1), F32), pltpu.VMEM((rows, dh), F32)],
        compiler_params=_cparams(("parallel", "parallel", "arbitrary"), 4 * 4 * rows * max(bk, c)),
    )(*args)


def _flash_dq(q, kl, vl, kc, vc, sink, o, do, lse, band):
    kvh, g, t, dh = q.shape
    c = kc.shape[1]
    has_lat = kl is not None
    has_sink = sink is not None
    bq, bk = _attn_blocks(t, band)
    nb = t // bk
    nkv = (3 if band else nb) if has_lat else 1
    rows = g * bq

    def body(*refs):
        it = iter(refs)
        q_ref, o_ref, do_ref, lse_ref, kc_ref, vc_ref = (next(it) for _ in range(6))
        kl_ref, vl_ref = (next(it), next(it)) if has_lat else (None, None)
        sink_ref = next(it) if has_sink else None
        dq_ref, delta_ref, dsink_ref, acc_ref, dl_ref = (next(it) for _ in range(5))
        h, qi, kj = pl.program_id(0), pl.program_id(1), pl.program_id(2)
        qv = q_ref[0].reshape(rows, dh)
        dov = do_ref[0].reshape(rows, dh)
        lse_v = lse_ref[0].reshape(rows, 1)

        def update(s, k, v):
            p = jnp.exp(s - lse_v)
            dp = lax.dot_general(dov.astype(BF16), v.astype(BF16), (((1,), (1,)), ((), ())),
                                 preferred_element_type=F32)
            ds = p * (dp - dl_ref[...])
            acc_ref[...] += jnp.dot(ds.astype(BF16), k.astype(BF16), preferred_element_type=F32)

        @pl.when(kj == 0)
        def _():
            dl_ref[...] = jnp.sum(o_ref[0].reshape(rows, dh) * dov, axis=-1, keepdims=True)
            acc_ref[...] = jnp.zeros_like(acc_ref)
            update(_qk_scores(qv, kc_ref[0]), kc_ref[0], vc_ref[0])

        if has_lat:
            s = _qk_scores(qv, kl_ref[0])
            if band:
                s = jnp.where(_band_mask(qi, qi - 1 + kj, bq, bk, nb, 0, rows), s, NEG_BIG)
            update(s, kl_ref[0], vl_ref[0])

        @pl.when(kj == nkv - 1)
        def _():
            dq_ref[0] = (acc_ref[...] * ATT_SCALE).reshape(g, bq, dh)
            delta_ref[0] = dl_ref[...].reshape(g, bq, 1)
            if has_sink:
                p_sink = jnp.exp(_sink_rows(sink_ref, h, bq) - lse_v)
                dsink_ref[0] = (-p_sink * dl_ref[...]).reshape(g, bq, 1)
            else:
                dsink_ref[0] = jnp.zeros((g, bq, 1), F32)

    q_spec = pl.BlockSpec((1, g, bq, dh), lambda h, i, j: (h, 0, i, 0))
    r_spec = pl.BlockSpec((1, g, bq, 1), lambda h, i, j: (h, 0, i, 0))
    c_spec = pl.BlockSpec((1, c, dh), lambda h, i, j: (h, 0, 0))
    in_specs, args = [q_spec, q_spec, q_spec, r_spec, c_spec, c_spec], [q, o, do, lse, kc, vc]
    if has_lat:
        l_spec = pl.BlockSpec((1, bk, dh), _lat_index(band, nb))
        in_specs += [l_spec, l_spec]
        args += [kl, vl]
    if has_sink:
        in_specs.append(pl.BlockSpec(memory_space=pltpu.SMEM))
        args.append(sink)
    row_shape = jax.ShapeDtypeStruct((kvh, g, t, 1), F32)
    return pl.pallas_call(
        body, name="attn_dq_band" if band else "attn_dq", grid=(kvh, t // bq, nkv),
        in_specs=in_specs, out_specs=[q_spec, r_spec, r_spec],
        out_shape=[jax.ShapeDtypeStruct(q.shape, F32), row_shape, row_shape],
        scratch_shapes=[pltpu.VMEM((rows, dh), F32), pltpu.VMEM((rows, 1), F32)],
        compiler_params=_cparams(("parallel", "parallel", "arbitrary"), 4 * 6 * rows * max(bk, c)),
    )(*args)


def _flash_dkv(q, do, lse, delta, k, v, band):
    kvh, g, t, dh = q.shape
    nk_rows = k.shape[1]
    if band:
        bq, bk = _attn_blocks(t, True)
    else:
        bq = _tile(t, 512, LANES)
        bk = _tile(nk_rows, 1024, LANES)
    nbq = t // bq
    nq = 3 if band else nbq
    rows = g * bq

    def body(q_ref, do_ref, lse_ref, delta_ref, k_ref, v_ref, dk_ref, dv_ref, dk_acc, dv_acc):
        ki, qj = pl.program_id(1), pl.program_id(2)
        qv = q_ref[0].reshape(rows, dh)
        dov = do_ref[0].reshape(rows, dh)

        @pl.when(qj == 0)
        def _():
            dk_acc[...] = jnp.zeros_like(dk_acc)
            dv_acc[...] = jnp.zeros_like(dv_acc)

        s = _qk_scores(qv, k_ref[0])
        if band:
            qblk = ki - 1 + qj
            qpos = qblk * bq + lax.broadcasted_iota(jnp.int32, (rows, bk), 0) % bq
            kpos = ki * bk + lax.broadcasted_iota(jnp.int32, (rows, bk), 1)
            s = jnp.where((qblk >= 0) & (qblk < nbq) & (jnp.abs(kpos - qpos) <= WINDOW), s, NEG_BIG)
        p = jnp.exp(s - lse_ref[0].reshape(rows, 1))
        dp = lax.dot_general(dov.astype(BF16), v_ref[0].astype(BF16), (((1,), (1,)), ((), ())),
                             preferred_element_type=F32)
        ds = p * (dp - delta_ref[0].reshape(rows, 1))
        tn = (((0,), (0,)), ((), ()))
        dv_acc[...] += lax.dot_general(p.astype(BF16), dov.astype(BF16), tn, preferred_element_type=F32)
        dk_acc[...] += lax.dot_general(ds.astype(BF16), qv.astype(BF16), tn, preferred_element_type=F32)

        @pl.when(qj == nq - 1)
        def _():
            dk_ref[0] = dk_acc[...] * ATT_SCALE
            dv_ref[0] = dv_acc[...]

    if band:
        q_index = lambda h, i, j: (h, 0, jnp.clip(i - 1 + j, 0, nbq - 1), 0)
    else:
        q_index = lambda h, i, j: (h, 0, j, 0)
    q_spec = pl.BlockSpec((1, g, bq, dh), q_index)
    r_spec = pl.BlockSpec((1, g, bq, 1), q_index)
    k_spec = pl.BlockSpec((1, bk, dh), lambda h, i, j: (h, i, 0))
    return pl.pallas_call(
        body, name="attn_dkv_band" if band else "attn_dkv", grid=(kvh, nk_rows // bk, nq),
        in_specs=[q_spec, q_spec, r_spec, r_spec, k_spec, k_spec], out_specs=[k_spec, k_spec],
        out_shape=[jax.ShapeDtypeStruct(k.shape, F32), jax.ShapeDtypeStruct(k.shape, F32)],
        scratch_shapes=[pltpu.VMEM((bk, dh), F32), pltpu.VMEM((bk, dh), F32)],
        compiler_params=_cparams(("parallel", "parallel", "arbitrary"), 4 * 6 * rows * bk),
    )(q, do, lse, delta, k, v)


def _flash_bwd_full(q, kl, vl, kc, vc, o, do, lse):
    kvh, g, t, dh = q.shape
    c = kc.shape[1]
    bq, bk = _tile(t, 512, LANES), _tile(t, 1024, LANES)
    nq, nkv = t // bq, t // bk
    rows = g * bq
    klt, vlt = jnp.swapaxes(kl, 1, 2), jnp.swapaxes(vl, 1, 2)
    kct, vct = jnp.swapaxes(kc, 1, 2), jnp.swapaxes(vc, 1, 2)
    tn = (((0,), (0,)), ((), ()))

    def body(q_ref, o_ref, do_ref, lse_ref, kc_ref, kct_ref, vct_ref, kl_ref, klt_ref, vlt_ref,
             dq_ref, delta_ref, dk_hbm, dv_hbm, dq_acc, dl_ref, dk_acc, dv_acc):
        h, qi, kj = pl.program_id(0), pl.program_id(1), pl.program_id(2)
        q_raw = q_ref[0].reshape(rows, dh).astype(BF16)
        qv = (q_ref[0].reshape(rows, dh) * ATT_SCALE).astype(BF16)
        dov = do_ref[0].reshape(rows, dh).astype(BF16)
        lse_v = lse_ref[0].reshape(rows, 1)

        def tile(kt, vt):
            s = jnp.dot(qv, kt.astype(BF16), preferred_element_type=F32)
            p = jnp.exp(s - lse_v)
            dp = jnp.dot(dov, vt.astype(BF16), preferred_element_type=F32)
            return p, (p * (dp - dl_ref[...])).astype(BF16)

        @pl.when((qi == 0) & (kj == 0))
        def _():
            dk_acc[...] = jnp.zeros_like(dk_acc)
            dv_acc[...] = jnp.zeros_like(dv_acc)

        @pl.when(kj == 0)
        def _():
            dl_ref[...] = jnp.sum(o_ref[0].reshape(rows, dh) * do_ref[0].reshape(rows, dh),
                                  axis=-1, keepdims=True)
            _, ds = tile(kct_ref[0], vct_ref[0])
            dq_acc[...] = jnp.dot(ds, kc_ref[0].astype(BF16), preferred_element_type=F32)

        p, ds = tile(klt_ref[0], vlt_ref[0])
        dq_acc[...] += jnp.dot(ds, kl_ref[0].astype(BF16), preferred_element_type=F32)
        ks = pl.ds(pl.multiple_of(kj * bk, bk), bk)
        dv_acc[ks] += lax.dot_general(p.astype(BF16), dov, tn, preferred_element_type=F32)
        dk_acc[ks] += lax.dot_general(ds, q_raw, tn, preferred_element_type=F32)

        @pl.when(kj == nkv - 1)
        def _():
            dq_ref[0] = (dq_acc[...] * ATT_SCALE).reshape(g, bq, dh)
            delta_ref[0] = dl_ref[...].reshape(g, bq, 1)

        @pl.when((qi == nq - 1) & (kj == nkv - 1))
        def _():
            dk_acc[...] = dk_acc[...] * ATT_SCALE
            pltpu.sync_copy(dk_acc, dk_hbm.at[h])
            pltpu.sync_copy(dv_acc, dv_hbm.at[h])

    q_spec = pl.BlockSpec((1, g, bq, dh), lambda h, i, j: (h, 0, i, 0))
    r_spec = pl.BlockSpec((1, g, bq, 1), lambda h, i, j: (h, 0, i, 0))
    c_spec = pl.BlockSpec((1, c, dh), lambda h, i, j: (h, 0, 0))
    ct_spec = pl.BlockSpec((1, dh, c), lambda h, i, j: (h, 0, 0))
    l_spec = pl.BlockSpec((1, bk, dh), lambda h, i, j: (h, j, 0))
    lt_spec = pl.BlockSpec((1, dh, bk), lambda h, i, j: (h, 0, j))
    any_spec = pl.BlockSpec(memory_space=pl.ANY)
    kv_shape = jax.ShapeDtypeStruct((kvh, t, dh), F32)
    return pl.pallas_call(
        body, name="attn_bwd_full", grid=(kvh, nq, nkv),
        in_specs=[q_spec, q_spec, q_spec, r_spec, c_spec, ct_spec, ct_spec, l_spec, lt_spec, lt_spec],
        out_specs=[q_spec, r_spec, any_spec, any_spec],
        out_shape=[jax.ShapeDtypeStruct(q.shape, F32), jax.ShapeDtypeStruct((kvh, g, t, 1), F32),
                   kv_shape, kv_shape],
        scratch_shapes=[pltpu.VMEM((rows, dh), F32), pltpu.VMEM((rows, 1), F32),
                        pltpu.VMEM((t, dh), F32), pltpu.VMEM((t, dh), F32)],
        compiler_params=_cparams(("arbitrary", "arbitrary", "arbitrary"), 4 * 2 * rows * bk,
                                 2 * 4 * t * LANES),
    )(q, o, do, lse, kc, kct, vct, kl, klt, vlt)


def make_attention(band, has_lat, has_sink):
    def unpack(args):
        it = iter(args)
        q, kc, vc = next(it), next(it), next(it)
        kl, vl = (next(it), next(it)) if has_lat else (None, None)
        sink = next(it) if has_sink else None
        return q, kl, vl, kc, vc, sink

    def forward(args):
        q, kl, vl, kc, vc, sink = unpack(args)
        klt = jnp.swapaxes(kl, 1, 2) if has_lat else None
        return _flash_fwd(q, klt, vl, jnp.swapaxes(kc, 1, 2), vc, sink, band)

    @jax.custom_vjp
    def op(*args):
        return forward(args)[0]

    def fwd(*args):
        o, lse = forward(args)
        return o, (args, o, lse)

    def bwd(res, do):
        args, o, lse = res
        q, kl, vl, kc, vc, sink = unpack(args)
        if has_lat and not band and not has_sink:
            dq, delta, dkl, dvl = _flash_bwd_full(q, kl, vl, kc, vc, o, do, lse)
            dkc, dvc = _flash_dkv(q, do, lse, delta, kc, vc, False)
            return dq, dkc, dvc, dkl, dvl
        dq, delta, dsink_rows = _flash_dq(q, kl, vl, kc, vc, sink, o, do, lse, band)
        dkc, dvc = _flash_dkv(q, do, lse, delta, kc, vc, False)
        grads = [dq, dkc, dvc]
        if has_lat:
            grads += list(_flash_dkv(q, do, lse, delta, kl, vl, band))
        if has_sink:
            grads.append(jnp.sum(dsink_rows, axis=(2, 3)).reshape(-1))
        return tuple(grads)

    op.defvjp(fwd, bwd)
    return op


attn_window = make_attention(True, True, True)
attn_global = make_attention(False, True, False)
attn_ctx_sink = make_attention(False, False, True)
attn_ctx = make_attention(False, False, False)


def _to_heads(z, n_heads):
    return z.reshape(z.shape[0], n_heads, HEAD_DIM).transpose(1, 0, 2)


def _q_heads(z):
    return z.reshape(z.shape[0], KV_HEADS, Q_PER_KV, HEAD_DIM).transpose(1, 2, 0, 3)


def _from_q_heads(o):
    return o.transpose(2, 0, 1, 3).reshape(o.shape[2], KV_HEADS * Q_PER_KV * HEAD_DIM)


def _scan_tile(t):
    return _tile(t, 512, SUBLANES)


def _scan_fwd_call(bre, bim, lre, lim, h0re, h0im, rev):
    t = bre.shape[0]
    tt = _scan_tile(t)
    nb = t // tt
    plane = bre.shape[1:]

    def body(bre_ref, bim_ref, lre_ref, lim_ref, h0re_ref, h0im_ref, sre_ref, sim_ref, h_ref):
        @pl.when(pl.program_id(0) == 0)
        def _():
            h_ref[0] = h0re_ref[...]
            h_ref[1] = h0im_ref[...]

        ar, ai = lre_ref[...], lim_ref[...]

        def step(j, carry):
            hr, hi = carry
            tj = (tt - 1 - j) if rev else j
            nr = ar * hr - ai * hi + bre_ref[tj]
            ni = ar * hi + ai * hr + bim_ref[tj]
            sre_ref[tj] = nr
            sim_ref[tj] = ni
            return nr, ni

        hr, hi = lax.fori_loop(0, tt, step, (h_ref[0], h_ref[1]), unroll=8)
        h_ref[0] = hr
        h_ref[1] = hi

    blk = pl.BlockSpec((tt,) + plane, (lambda i: (nb - 1 - i, 0, 0)) if rev else (lambda i: (i, 0, 0)))
    par = pl.BlockSpec(plane, lambda i: (0, 0))
    return pl.pallas_call(
        body, name="s5_scan_fwd", grid=(nb,), in_specs=[blk, blk, par, par, par, par], out_specs=[blk, blk],
        out_shape=[jax.ShapeDtypeStruct(bre.shape, F32)] * 2,
        scratch_shapes=[pltpu.VMEM((2,) + plane, F32)],
        compiler_params=_cparams(("arbitrary",), 4 * 4 * tt * plane[0] * plane[1]),
    )(bre, bim, lre, lim, h0re, h0im)


def _scan_bwd_call(gre, gim, sre, sim, lre, lim, h0re, h0im, rev):
    t = gre.shape[0]
    tt = _scan_tile(t)
    nb = t // tt
    plane = gre.shape[1:]
    down = not rev

    def body(gre_ref, gim_ref, sre_ref, sim_ref, lre_ref, lim_ref, h0re_ref, h0im_ref,
             dbre_ref, dbim_ref, dare_ref, daim_ref, dh0re_ref, dh0im_ref, carry_ref):
        i = pl.program_id(0)

        @pl.when(i == 0)
        def _():
            carry_ref[...] = jnp.zeros_like(carry_ref)

        ar, ai = lre_ref[...], lim_ref[...]

        def step(j, carry):
            gr, gi, dar, dai = carry
            tj = (tt - 1 - j) if down else j
            hr, hi = sre_ref[tj], sim_ref[tj]
            dar = dar + hr * gr + hi * gi
            dai = dai + hr * gi - hi * gr
            ngr = gre_ref[tj] + ar * gr + ai * gi
            ngi = gim_ref[tj] + ar * gi - ai * gr
            dbre_ref[tj] = ngr
            dbim_ref[tj] = ngi
            return ngr, ngi, dar, dai

        gr, gi, dar, dai = lax.fori_loop(
            0, tt, step, (carry_ref[0], carry_ref[1], carry_ref[2], carry_ref[3]), unroll=8)
        carry_ref[0] = gr
        carry_ref[1] = gi
        carry_ref[2] = dar
        carry_ref[3] = dai

        @pl.when(i == nb - 1)
        def _():
            hr, hi = h0re_ref[...], h0im_ref[...]
            dare_ref[...] = dar + hr * gr + hi * gi
            daim_ref[...] = dai + hr * gi - hi * gr
            dh0re_ref[...] = ar * gr + ai * gi
            dh0im_ref[...] = ar * gi - ai * gr

    blk = pl.BlockSpec((tt,) + plane, (lambda i: (nb - 1 - i, 0, 0)) if down else (lambda i: (i, 0, 0)))
    par = pl.BlockSpec(plane, lambda i: (0, 0))
    return pl.pallas_call(
        body, name="s5_scan_bwd", grid=(nb,), in_specs=[blk, blk, blk, blk, par, par, par, par],
        out_specs=[blk, blk, par, par, par, par],
        out_shape=[jax.ShapeDtypeStruct(gre.shape, F32)] * 2 + [jax.ShapeDtypeStruct(plane, F32)] * 4,
        scratch_shapes=[pltpu.VMEM((4,) + plane, F32)],
        compiler_params=_cparams(("arbitrary",), 4 * 6 * tt * plane[0] * plane[1]),
    )(gre, gim, sre, sim, lre, lim, h0re, h0im)


def make_scan(rev):
    @jax.custom_vjp
    def op(bre, bim, lre, lim, h0re, h0im):
        return tuple(_scan_fwd_call(bre, bim, lre, lim, h0re, h0im, rev))

    def fwd(bre, bim, lre, lim, h0re, h0im):
        sre, sim = _scan_fwd_call(bre, bim, lre, lim, h0re, h0im, rev)
        return (sre, sim), (sre, sim, lre, lim, h0re, h0im)

    def bwd(res, cts):
        sre, sim, lre, lim, h0re, h0im = res
        return tuple(_scan_bwd_call(cts[0], cts[1], sre, sim, lre, lim, h0re, h0im, rev))

    op.defvjp(fwd, bwd)
    return op


scan_up = make_scan(False)
scan_down = make_scan(True)


def _adamw_call(parts, w, m, v, name):
    r, c = w.shape
    tr = _tile(r, 256, SUBLANES)
    nparts = parts.shape[0]
    c1 = 1.0 - ADAM_B1 ** ADAM_STEP
    c2 = 1.0 - ADAM_B2 ** ADAM_STEP

    def body(p_ref, w_ref, m_ref, v_ref, g_ref, d_ref, nm_ref, nv_ref):
        g = p_ref[0].astype(F32)
        for s in range(1, nparts):
            g = g + p_ref[s].astype(F32)
        m1 = ADAM_B1 * m_ref[...] + (1.0 - ADAM_B1) * g
        v1 = ADAM_B2 * v_ref[...] + (1.0 - ADAM_B2) * jnp.square(g)
        g_ref[...] = g
        nm_ref[...] = m1
        nv_ref[...] = v1
        d_ref[...] = -ADAM_LR * ((m1 / c1) / (jnp.sqrt(v1 / c2) + ADAM_EPS) + ADAM_WD * w_ref[...])

    blk = pl.BlockSpec((tr, c), lambda i: (i, 0))
    return pl.pallas_call(
        body, name=name, grid=(r // tr,),
        in_specs=[pl.BlockSpec((nparts, tr, c), lambda i: (0, i, 0)), blk, blk, blk], out_specs=[blk] * 4,
        out_shape=[jax.ShapeDtypeStruct((r, c), F32)] * 4,
        compiler_params=_cparams(("parallel",), 4 * tr * c * (nparts + 7)),
    )(parts, w, m, v)


def _peer(k):
    x, y, c = lax.axis_index("x"), lax.axis_index("y"), lax.axis_index("c")
    px = 1 - x if k & 4 else x
    py = 1 - y if k & 2 else y
    pc = 1 - c if k & 1 else c
    return (px, py, pc), 4 * px + 2 * py + pc


def _my_slot():
    return 4 * lax.axis_index("x") + 2 * lax.axis_index("y") + lax.axis_index("c")


def _exchange_call(x, gather, name):
    slab = x.shape if gather else x.shape[1:]

    def body(x_ref, out_ref, send_sems, recv_sems, local_sem):
        me = _my_slot()
        mine = pltpu.make_async_copy(x_ref if gather else x_ref.at[me], out_ref.at[me], local_sem)
        mine.start()
        sends = []
        for k in range(1, N_DEV):
            peer, slot = _peer(k)
            cp = pltpu.make_async_remote_copy(
                src_ref=x_ref if gather else x_ref.at[slot], dst_ref=out_ref.at[me],
                send_sem=send_sems.at[k - 1], recv_sem=recv_sems.at[k - 1],
                device_id=peer, device_id_type=MESH)
            cp.start()
            sends.append(cp)
        for k in range(1, N_DEV):
            peer, slot = _peer(k)
            pltpu.make_async_remote_copy(
                src_ref=x_ref if gather else x_ref.at[slot], dst_ref=out_ref.at[slot],
                send_sem=send_sems.at[k - 1], recv_sem=recv_sems.at[k - 1],
                device_id=peer, device_id_type=MESH).wait_recv()
        for cp in sends:
            cp.wait_send()
        mine.wait()

    return pl.pallas_call(
        body, name=name,
        in_specs=[pl.BlockSpec(memory_space=pl.ANY)], out_specs=pl.BlockSpec(memory_space=pl.ANY),
        out_shape=jax.ShapeDtypeStruct((N_DEV,) + tuple(slab), x.dtype),
        scratch_shapes=[pltpu.SemaphoreType.DMA((N_DEV - 1,)), pltpu.SemaphoreType.DMA((N_DEV - 1,)),
                        pltpu.SemaphoreType.DMA],
    )(x)


def all_gather(x, name):
    return _exchange_call(x, True, name)


def all_gather_two_level(x, name):
    def body(x_ref, out_ref, send_sems, recv_sems, local_sem):
        x_, y_, c_ = lax.axis_index("x"), lax.axis_index("y"), lax.axis_index("c")
        me, sibling = (x_, y_, c_), (x_, y_, 1 - c_)
        chips = [(1 - x_, y_), (x_, 1 - y_), (1 - x_, 1 - y_)]

        def slot(px, py, pc):
            return out_ref.at[4 * px + 2 * py + pc]

        def copy(k, block, to, src=None):
            return pltpu.make_async_remote_copy(
                src_ref=slot(*block) if src is None else src, dst_ref=slot(*block),
                send_sem=send_sems.at[k], recv_sem=recv_sems.at[k], device_id=to, device_id_type=MESH)

        mine = pltpu.make_async_copy(x_ref, slot(*me), local_sem)
        mine.start()
        first = [copy(0, me, sibling, src=x_ref)]
        first += [copy(1 + j, me, (*chip, c_), src=x_ref) for j, chip in enumerate(chips)]
        for cp in first:
            cp.start()
        passed = [copy(4 + j, (*chip, c_), sibling) for j, chip in enumerate(chips)]
        for j, chip in enumerate(chips):
            copy(1 + j, (*chip, c_), me).wait_recv()
            passed[j].start()
        copy(0, sibling, me).wait_recv()
        for j, chip in enumerate(chips):
            copy(4 + j, (*chip, 1 - c_), me).wait_recv()
        for cp in first + passed:
            cp.wait_send()
        mine.wait()

    return pl.pallas_call(
        body, name=name,
        in_specs=[pl.BlockSpec(memory_space=pl.ANY)], out_specs=pl.BlockSpec(memory_space=pl.ANY),
        out_shape=jax.ShapeDtypeStruct((N_DEV,) + tuple(x.shape), x.dtype),
        scratch_shapes=[pltpu.SemaphoreType.DMA((N_DEV - 1,)), pltpu.SemaphoreType.DMA((N_DEV - 1,)),
                        pltpu.SemaphoreType.DMA],
    )(x)


def all_to_all(x, name):
    return _exchange_call(x, False, name)


def _rope_tables(t):
    n_freq = HEAD_DIM // 4
    tok = jnp.arange(t)
    inv = ROPE_THETA ** (-jnp.arange(n_freq, dtype=F32) / n_freq)
    a_row = (tok // GRID_W).astype(F32)[:, None] * inv
    a_col = (tok % GRID_W).astype(F32)[:, None] * inv
    cos = jnp.concatenate([jnp.cos(a_row)] * 2 + [jnp.cos(a_col)] * 2, axis=1)
    sin = jnp.concatenate([-jnp.sin(a_row), jnp.sin(a_row), -jnp.sin(a_col), jnp.sin(a_col)], axis=1)
    return jnp.concatenate([cos, cos], axis=1), jnp.concatenate([sin, sin], axis=1)


def _block_diag(blocks):
    g, a, b = blocks.shape
    eye = jnp.eye(g, dtype=blocks.dtype)
    return jnp.einsum("gab,gk->gakb", blocks, eye).reshape(g * a, g * b)


def _ffn_fwd_calls(x, mod, g, w_in, w_out):
    shift, scale, gate = mod[0:1], mod[1:2], mod[2:3]
    h, = _row_call(lambda xt, gt, sh, sc: (_norm_mod_fn(xt, gt, sh, sc)[0].astype(BF16),),
                   [x], [g, shift, scale], 0, "ffn_norm")
    u = _mm(h, w_in, out_dtype=BF16, name="ffn_up")
    a, = _row_call(lambda ut: (_swiglu_fn(ut.astype(F32))[0].astype(BF16),), [u], [], 0, "ffn_act")
    y = _mm(a, w_out, name="ffn_down")
    out, = _row_call(functools.partial(_resid_fn, 0.5), [x, y], [gate], 0, "ffn_resid")
    return out, (h, u, a, y)


@jax.custom_vjp
def _ffn_half(x, mod, g, w_in, w_out):
    return _ffn_fwd_calls(x, mod, g, w_in, w_out)[0]


def _ffn_half_fwd(x, mod, g, w_in, w_out):
    out, saved = _ffn_fwd_calls(x, mod, g, w_in, w_out)
    return out, (x, mod, g, w_in, w_out, saved)


def _ffn_half_bwd(res, dxn):
    x, mod, g, w_in, w_out, (h, u, a, y) = res
    shift, scale, gate = mod[0:1], mod[1:2], mod[2:3]

    def resid_bwd(dt, yt, gt):
        return (0.5 * gt * dt).astype(BF16), 0.5 * jnp.sum(dt * yt, axis=0, keepdims=True)

    dy, dgate = _row_call(resid_bwd, [dxn, y], [gate], 1, "ffn_resid_bwd")
    da = _mm(dy, w_out, tb=True, out_dtype=BF16, name="ffn_down_dx")
    dw_out = _mm(a, dy, ta=True, out_dtype=w_out.dtype, name="ffn_down_dw")

    def act_bwd(ut, dat):
        return (jax.vjp(_swiglu_fn, ut.astype(F32))[1]((dat.astype(F32),))[0].astype(BF16),)

    du, = _row_call(act_bwd, [u, da], [], 0, "ffn_act_bwd")
    dh = _mm(du, w_in, tb=True, name="ffn_up_dx")
    dw_in = _mm(h, du, ta=True, out_dtype=w_in.dtype, name="ffn_up_dw")

    def norm_bwd(xt, dht, dt, gt, sh, sc):
        dx, dg, dsh, dsc = jax.vjp(_norm_mod_fn, xt, gt, sh, sc)[1]((dht,))
        return dx + dt, dg, dsh, dsc

    dx, dg, dshift, dscale = _row_call(norm_bwd, [x, dh, dxn], [g, shift, scale], 3, "ffn_norm_bwd")
    return dx, jnp.concatenate([dshift, dscale, dgate], axis=0), dg, dw_in, dw_out


_ffn_half.defvjp(_ffn_half_fwd, _ffn_half_bwd)


def _planes(z):
    return z.reshape(z.shape[0], SUBLANES, SSM_LANES // SUBLANES)


def _s5_discretize(a_re, a_im, log_dt, b_re, b_im):
    lam = lax.complex(a_re, a_im)
    dt = jnp.exp(log_dt)[:, None]
    lam_bar = jnp.exp(lam * dt)
    b_bar = ((lam_bar - 1.0) / lam)[..., None] * lax.complex(b_re, b_im)
    return lam_bar, b_bar


def _s5_branch(u_lat, u_ctx, w, l, with_ctx_out):
    zero = jnp.zeros((SUBLANES, SSM_LANES // SUBLANES), F32)
    lat_terms, ctx_terms = [], []
    for d, scan in enumerate((scan_up, scan_down)):
        lam_bar, b_bar = _s5_discretize(w["ssm_a_re"][l, d], w["ssm_a_im"][l, d], w["ssm_log_dt"][l, d],
                                        w["ssm_b_re"][l, d], w["ssm_b_im"][l, d])
        lre = jnp.real(lam_bar).reshape(zero.shape)
        lim = jnp.imag(lam_bar).reshape(zero.shape)
        b_t = jnp.swapaxes(b_bar, 1, 2)
        b_mat = jnp.concatenate([_block_diag(jnp.real(b_t)), _block_diag(jnp.imag(b_t))], axis=1)
        c_re = _block_diag(jnp.swapaxes(w["ssm_c_re"][l, d], 1, 2))
        c_im = _block_diag(jnp.swapaxes(w["ssm_c_im"][l, d], 1, 2))
        bu_c = linear(u_ctx, b_mat)
        sc_re, sc_im = scan(_planes(bu_c[:, :SSM_LANES]), _planes(bu_c[:, SSM_LANES:]), lre, lim, zero, zero)
        last = 0 if d == 1 else u_ctx.shape[0] - 1
        bu_l = linear(u_lat, b_mat)
        sl_re, sl_im = scan(_planes(bu_l[:, :SSM_LANES]), _planes(bu_l[:, SSM_LANES:]), lre, lim,
                            sc_re[last], sc_im[last])
        flat = lambda s: s.reshape(s.shape[0], SSM_LANES)
        lat_terms += [linear(flat(sl_re), c_re), linear(flat(sl_im), c_im)]
        if with_ctx_out:
            ctx_terms += [linear(flat(sc_re), c_re), linear(flat(sc_im), c_im)]
    d_skip = w["ssm_d"][l][None, :]

    def out(terms, u):
        y, = s5_pre(*terms, u, d_skip)
        return glu(linear(y, w["glu_w"][l]))[0]

    return out(lat_terms, u_lat), (out(ctx_terms, u_ctx) if with_ctx_out else None)


def _pool_branch(xa, w, l):
    y = linear(pool_diff(xa), _block_diag(w["pool_w"][l]))
    return scale_rows(y, w["pool_scale"][l][None, :])[0]


def _merge_branches(branches, gate_logits, w, l):
    zs = [linear_b(y, w["branch_w"][l, k]) for k, y in enumerate(branches)]
    return linear(merge(gate_logits, *zs)[0], w["out_w"][l])


def _token_mixer(h, hc, cos, sin, w, l, with_ctx_out):
    w_in, w_gate = w["w_in"][l][:, :O_GATE], w["w_in"][l][:, O_GATE:]
    p = linear(h, w_in)
    p_gate = linear_b(h, w_gate)
    pc = linear(hc, w_in if with_ctx_out else w_in[:, :CTX_COLS])
    pc_gate = linear_b(hc, w_gate) if with_ctx_out else None
    sink = w["win_sink"][l]
    q_g = jnp.tile(w["qk_norm"][l, 0], KV_HEADS * Q_PER_KV)[None, :]
    k_g = jnp.tile(w["qk_norm"][l, 1], KV_HEADS)[None, :]
    k_win_c = _to_heads(pc[:, O_KB:O_VB], KV_HEADS)
    v_win_c = _to_heads(pc[:, O_VB:O_UC], KV_HEADS)
    k_glb_c = _to_heads(head_norm(pc[:, O_KD:O_VD], k_g)[0], KV_HEADS)
    v_glb_c = _to_heads(pc[:, O_VD:CTX_COLS], KV_HEADS)
    y_a = _pool_branch(p[:, O_XA:O_GATE], w, l)
    q_win = _q_heads(rope(p[:, O_QB:O_QD], cos, sin)[0])
    k_win = _to_heads(rope(p[:, O_KB:O_VB], cos, sin)[0], KV_HEADS)
    v_win = _to_heads(p[:, O_VB:O_UC], KV_HEADS)
    y_b = _from_q_heads(attn_window(q_win, k_win_c, v_win_c, k_win, v_win, sink))
    y_c, y_c_ctx = _s5_branch(p[:, O_UC:O_KD], pc[:, O_UC:O_KD], w, l, with_ctx_out)
    q_glb = _q_heads(norm_rope(p[:, O_QD:O_XA], cos, sin, q_g)[0])
    k_glb = _to_heads(norm_rope(p[:, O_KD:O_VD], cos, sin, k_g)[0], KV_HEADS)
    v_glb = _to_heads(p[:, O_VD:CTX_COLS], KV_HEADS)
    y_d = _from_q_heads(attn_global(q_glb, k_glb_c, v_glb_c, k_glb, v_glb))
    y = _merge_branches((y_a, y_b, y_c, y_d), p_gate, w, l)
    if not with_ctx_out:
        return y, None
    y_a_c = _pool_branch(pc[:, O_XA:O_GATE], w, l)
    y_b_c = _from_q_heads(attn_ctx_sink(_q_heads(pc[:, O_QB:O_QD]), k_win_c, v_win_c, sink))
    q_glb_c = _q_heads(head_norm(pc[:, O_QD:O_XA], q_g)[0])
    y_d_c = _from_q_heads(attn_ctx(q_glb_c, k_glb_c, v_glb_c))
    return y, _merge_branches((y_a_c, y_b_c, y_c_ctx, y_d_c), pc_gate, w, l)


def local_loss(w, x, c, ctx, target):
    depth = w["w_mod"].shape[0]
    cos, sin = _rope_tables(x.shape[0])
    cond = jnp.concatenate([c, w["c_ctx"][None, :], jnp.zeros((COND_ROWS - 2, D_MODEL), F32)], axis=0)
    s_all, = silu_rows(cond)
    for l in range(depth):
        last = l == depth - 1
        m_all = (linear(s_all, w["w_mod"][l]) + w["b_mod"][l][None, :]).reshape(COND_ROWS, N_SUB, 3, D_MODEL)
        m, mc = m_all[0], m_all[1]
        g = w["norm_g"][l][:, None, :]
        x = _ffn_half(x, m[0], g[0], w["ffn_in"][l, 0], w["ffn_out"][l, 0])
        ctx = _ffn_half(ctx, mc[0], g[0], w["ffn_in"][l, 0], w["ffn_out"][l, 0])
        h, = norm_mod(x, g[1], m[1, 0:1], m[1, 1:2])
        hc, = norm_mod(ctx, g[1], mc[1, 0:1], mc[1, 1:2])
        y, y_ctx = _token_mixer(h, hc, cos, sin, w, l, not last)
        x, = resid_full(x, y, m[1, 2:3])
        if not last:
            ctx, = resid_full(ctx, y_ctx, mc[1, 2:3])
        x = _ffn_half(x, m[2], g[2], w["ffn_in"][l, 1], w["ffn_out"][l, 1])
        if not last:
            ctx = _ffn_half(ctx, mc[2], g[2], w["ffn_in"][l, 1], w["ffn_out"][l, 1])
    return jnp.sum(loss_rows(x, target, w["final_g"][None, :])[0])


PACK_COLS = 1024


def _pack(arrays):
    flat = jnp.concatenate([a.reshape(-1) for a in arrays])
    pad = (-flat.shape[0]) % (PACK_COLS * 16)
    return jnp.pad(flat, (0, pad)).reshape(-1, PACK_COLS)


def _unpack(slab, shapes):
    flat = slab.reshape(-1)
    out, off = [], 0
    for s in shapes:
        n = math.prod(s)
        out.append(flat[off:off + n].reshape(s))
        off += n
    return out


def _full_from_shards(gathered, shard_shape, axis):
    z = jnp.moveaxis(gathered.reshape((N_DEV,) + tuple(shard_shape)), 0, axis)
    shape = list(shard_shape)
    shape[axis] *= N_DEV
    return z.reshape(shape)


def _shards_from_full(full, axis):
    shape = list(full.shape)
    shape[axis:axis + 1] = [N_DEV, shape[axis] // N_DEV]
    return jnp.moveaxis(full.reshape(shape), axis, 0)


def kernel(x, c, ctx, c_ctx, w_mod, b_mod, norm_g, ffn_in, ffn_out, w_in, win_sink, qk_norm, pool_w, pool_scale, ssm_a_re, ssm_a_im, ssm_log_dt, ssm_b_re, ssm_b_im, ssm_c_re, ssm_c_im, ssm_d, glu_w, branch_w, out_w, final_g, loss_target, m_c_ctx, m_w_mod, m_b_mod, m_norm_g, m_ffn_in, m_ffn_out, m_w_in, m_win_sink, m_qk_norm, m_pool_w, m_pool_scale, m_ssm_a_re, m_ssm_a_im, m_ssm_log_dt, m_ssm_b_re, m_ssm_b_im, m_ssm_c_re, m_ssm_c_im, m_ssm_d, m_glu_w, m_branch_w, m_out_w, m_final_g, v_c_ctx, v_w_mod, v_b_mod, v_norm_g, v_ffn_in, v_ffn_out, v_w_in, v_win_sink, v_qk_norm, v_pool_w, v_pool_scale, v_ssm_a_re, v_ssm_a_im, v_ssm_log_dt, v_ssm_b_re, v_ssm_b_im, v_ssm_c_re, v_ssm_c_im, v_ssm_d, v_glu_w, v_branch_w, v_out_w, v_final_g):
    given = dict(locals())
    wts = {n: given[n] for n in WEIGHTS}
    mom = {n: given["m_" + n] for n in WEIGHTS}
    var = {n: given["v_" + n] for n in WEIGHTS}
    me = _my_slot()

    shard_shapes = [wts[n].shape for n in SHARDED]
    w_slab = _pack([wts[n] for n in SHARDED])
    gathered = all_gather_two_level(w_slab.astype(BF16), "gather_weights")
    parts = [_unpack(gathered[s], shard_shapes) for s in range(N_DEV)]
    full = dict(wts)
    for i, n in enumerate(SHARDED):
        full[n] = _full_from_shards(jnp.stack([parts[s][i] for s in range(N_DEV)]), wts[n].shape, SHARD_AXIS[n])
    g_slab = _pack([norm_g])
    g_all = all_gather(g_slab, "gather_norm_g")
    full["norm_g"] = _full_from_shards(
        jnp.stack([_unpack(g_all[s], [norm_g.shape])[0] for s in range(N_DEV)]), norm_g.shape, 2)

    loss, (gw, gx) = jax.value_and_grad(local_loss, argnums=(0, 1))(full, x[0], c, ctx[0], loss_target[0])
    loss = lax.psum(loss, ("x", "y", "c"))

    dest = [_shards_from_full(gw[n], SHARD_AXIS[n]) for n in SHARDED]
    send = jnp.stack([_pack([d[s] for d in dest]) for s in range(N_DEV)])
    big_parts = all_to_all(send, "exchange_grads")
    small_names = SMALL + ("norm_g",)
    small_shapes = [gw[n].shape for n in small_names]
    small_parts = all_gather(_pack([gw[n] for n in small_names]), "gather_small_grads")

    big = _adamw_call(big_parts, w_slab, _pack([mom[n] for n in SHARDED]), _pack([var[n] for n in SHARDED]),
                      "adamw_sharded")
    big = [_unpack(b, shard_shapes) for b in big]
    col = me * norm_g.shape[2]

    def small_slab(src, shard_src):
        padded = jnp.zeros((norm_g.shape[0], norm_g.shape[1], norm_g.shape[2] * N_DEV), F32)
        padded = lax.dynamic_update_slice(padded, shard_src, (0, 0, col))
        return _pack([src[n] for n in SMALL] + [padded])

    small = _adamw_call(small_parts, small_slab(wts, norm_g), small_slab(mom, m_norm_g),
                        small_slab(var, v_norm_g), "adamw_small")
    small = [_unpack(s, small_shapes) for s in small]

    outs = {}
    for kind in range(4):
        for i, n in enumerate(SHARDED):
            outs[(kind, n)] = big[kind][i]
        for i, n in enumerate(small_names):
            val = small[kind][i]
            if n == "norm_g":
                val = lax.dynamic_slice(val, (0, 0, col), norm_g.shape)
            outs[(kind, n)] = val
    return (loss, gx[None], *[outs[(k, n)] for k in range(4) for n in WEIGHTS])
```

```python
import functools
import math

import jax
import jax.numpy as jnp
from jax import lax
from jax.experimental import pallas as pl
from jax.experimental.pallas import tpu as pltpu

F32 = jnp.float32
BF16 = jnp.bfloat16

D_MODEL = 1024
GRID_W = 64
HEAD_DIM = 64
N_BRANCH = 4
BRANCH_W = D_MODEL // N_BRANCH
WINDOW = 128
ROPE_THETA = 10000.0
EPS = 1e-6
D_FF = 2816
N_SUB = 3
POOL_WINDOWS = (2, 4, 8, 16)
POOL_GROUP = BRANCH_W // len(POOL_WINDOWS)
KV_HEADS = 2
Q_PER_KV = 2
SSM_GROUP = 16
SSM_GROUPS = BRANCH_W // SSM_GROUP
SSM_STATE = 64
SSM_LANES = SSM_GROUPS * SSM_STATE
O_KB, O_VB, O_UC, O_KD, O_VD, CTX_COLS = 0, 128, 256, 512, 640, 768
O_QB, O_QD, O_XA, O_GATE = 768, 1024, 1280, 1536
IN_W = O_GATE + N_BRANCH * D_MODEL

ADAM_LR, ADAM_B1, ADAM_B2, ADAM_EPS, ADAM_WD, ADAM_STEP = 0.001, 0.9, 0.999, 1e-08, 0.01, 10

N_DEV = 8
MESH = pl.DeviceIdType.MESH

V7X_VMEM_BYTES = 64 * 1024 * 1024
SUBLANES = 8
LANES = 128
NEG_BIG = -1e30
COND_ROWS = 128

SHARDED = ("w_mod", "ffn_in", "ffn_out", "w_in", "glu_w", "branch_w", "out_w")
SHARD_AXIS = {"w_mod": 2, "ffn_in": 3, "ffn_out": 2, "w_in": 2, "glu_w": 2, "branch_w": 3, "out_w": 1}
SMALL = ("c_ctx", "b_mod", "win_sink", "qk_norm", "pool_w", "pool_scale", "ssm_a_re", "ssm_a_im",
         "ssm_log_dt", "ssm_b_re", "ssm_b_im", "ssm_c_re", "ssm_c_im", "ssm_d", "final_g")
WEIGHTS = ("c_ctx", "w_mod", "b_mod", "norm_g", "ffn_in", "ffn_out", "w_in", "win_sink", "qk_norm",
           "pool_w", "pool_scale", "ssm_a_re", "ssm_a_im", "ssm_log_dt", "ssm_b_re", "ssm_b_im",
           "ssm_c_re", "ssm_c_im", "ssm_d", "glu_w", "branch_w", "out_w", "final_g")


def _tile(n, cap, mult):
    if n <= cap:
        return n
    t = (cap // mult) * mult
    while t >= mult:
        if n % t == 0:
            return t
        t -= mult
    return n


def _cparams(sem, tile_bytes, resident_bytes=0):
    limit = int(min(V7X_VMEM_BYTES - 8 * 2 ** 20,
                    max(32 * 2 ** 20, 3 * tile_bytes + resident_bytes + 8 * 2 ** 20)))
    return pltpu.CompilerParams(dimension_semantics=sem, vmem_limit_bytes=limit)


def _mm(a, b, ta=False, tb=False, out_dtype=F32, name="mm"):
    m, k = (a.shape[1], a.shape[0]) if ta else a.shape
    n = b.shape[0] if tb else b.shape[1]
    assert (b.shape[1] if tb else b.shape[0]) == k
    tm, tn, tk = _tile(m, 1024, LANES), _tile(n, 1536, LANES), _tile(k, 1536, LANES)
    nk = k // tk
    dims = (((0 if ta else 1,), (1 if tb else 0,)), ((), ()))

    def body(a_ref, b_ref, o_ref, acc_ref):
        kk = pl.program_id(2)

        @pl.when(kk == 0)
        def _():
            acc_ref[...] = jnp.zeros_like(acc_ref)

        acc_ref[...] += lax.dot_general(a_ref[...].astype(BF16), b_ref[...].astype(BF16), dims,
                                        preferred_element_type=F32)

        @pl.when(kk == nk - 1)
        def _():
            o_ref[...] = acc_ref[...].astype(o_ref.dtype)

    a_spec = (pl.BlockSpec((tk, tm), lambda i, j, kk: (kk, i)) if ta
              else pl.BlockSpec((tm, tk), lambda i, j, kk: (i, kk)))
    b_spec = (pl.BlockSpec((tn, tk), lambda i, j, kk: (j, kk)) if tb
              else pl.BlockSpec((tk, tn), lambda i, j, kk: (kk, j)))
    tile_bytes = (a.dtype.itemsize * tm * tk + b.dtype.itemsize * tk * tn
                  + jnp.dtype(out_dtype).itemsize * tm * tn + 2 * tm * tn)
    return pl.pallas_call(
        body, name=name, grid=(m // tm, n // tn, nk),
        in_specs=[a_spec, b_spec], out_specs=pl.BlockSpec((tm, tn), lambda i, j, kk: (i, j)),
        out_shape=jax.ShapeDtypeStruct((m, n), out_dtype),
        scratch_shapes=[pltpu.VMEM((tm, tn), F32)],
        compiler_params=_cparams(("parallel", "parallel", "arbitrary"), tile_bytes),
    )(a, b)


def make_linear(out_dtype):
    @jax.custom_vjp
    def op(x, w):
        return _mm(x, w, out_dtype=out_dtype, name="linear_fwd")

    def fwd(x, w):
        return op(x, w), (x, w)

    def bwd(res, dy):
        x, w = res
        return (_mm(dy, w, tb=True, out_dtype=x.dtype, name="linear_dx"),
                _mm(x, dy, ta=True, out_dtype=w.dtype, name="linear_dw"))

    op.defvjp(fwd, bwd)
    return op


linear = make_linear(F32)
linear_b = make_linear(BF16)


ROW_TILE_BYTES = 6 * 2 ** 20


def _row_tile(t, row_bytes):
    tm = 1024
    while tm > 2 * SUBLANES and tm * row_bytes > ROW_TILE_BYTES:
        tm //= 2
    return _tile(t, tm, 2 * SUBLANES)


def _row_call(fn, rows, params, n_reduce, name):
    t = rows[0].shape[0]
    out_avals = jax.eval_shape(fn, *rows, *params)
    n_out = len(out_avals) - n_reduce
    row_avals, red_avals = out_avals[:n_out], out_avals[n_out:]
    row_bytes = sum(r.shape[1] * r.dtype.itemsize for r in (*rows, *row_avals))
    tm = _row_tile(t, row_bytes)
    n_in = len(rows) + len(params)

    def body(*refs):
        outs = fn(*[r[...] for r in refs[:n_in]])
        o_refs = refs[n_in:]
        for o_ref, o in zip(o_refs[:n_out], outs[:n_out]):
            o_ref[...] = o.astype(o_ref.dtype)
        if n_reduce:
            @pl.when(pl.program_id(0) == 0)
            def _():
                for r in o_refs[n_out:]:
                    r[...] = jnp.zeros_like(r)

            for r, o in zip(o_refs[n_out:], outs[n_out:]):
                r[...] += o.astype(r.dtype)

    in_specs = ([pl.BlockSpec((tm, r.shape[1]), lambda i: (i, 0)) for r in rows]
                + [pl.BlockSpec(p.shape, lambda i: (0, 0)) for p in params])
    out_specs = ([pl.BlockSpec((tm, o.shape[1]), lambda i: (i, 0)) for o in row_avals]
                 + [pl.BlockSpec(o.shape, lambda i: (0, 0)) for o in red_avals])
    return pl.pallas_call(
        body, name=name, grid=(t // tm,), in_specs=in_specs, out_specs=out_specs,
        out_shape=[jax.ShapeDtypeStruct(o.shape, o.dtype) for o in out_avals],
        compiler_params=_cparams(("arbitrary",) if n_reduce else ("parallel",), tm * row_bytes),
    )(*rows, *params)


def rowwise(fn, n_rows, name):
    @jax.custom_vjp
    def op(*args):
        return tuple(_row_call(fn, args[:n_rows], args[n_rows:], 0, name + "_fwd"))

    def fwd(*args):
        return op(*args), args

    def bwd(args, cts):
        n_ct = len(cts)

        def bwd_fn(*a):
            r, ct, p = a[:n_rows], a[n_rows:n_rows + n_ct], a[n_rows + n_ct:]
            return jax.vjp(fn, *r, *p)[1](tuple(ct))

        return tuple(_row_call(bwd_fn, (*args[:n_rows], *cts), args[n_rows:], len(args) - n_rows,
                               name + "_bwd"))

    op.defvjp(fwd, bwd)
    return op


@functools.partial(jax.custom_vjp, nondiff_argnums=(1,))
def _swap_lanes(x, k):
    n = x.shape[-1]
    lane = lax.broadcasted_iota(jnp.int32, x.shape, x.ndim - 1)
    return jnp.where((lane & k) == 0, pltpu.roll(x, n - k, x.ndim - 1), pltpu.roll(x, k, x.ndim - 1))


def _swap_lanes_fwd(x, k):
    return _swap_lanes(x, k), None


def _swap_lanes_bwd(k, _, g):
    return (_swap_lanes(g, k),)


_swap_lanes.defvjp(_swap_lanes_fwd, _swap_lanes_bwd)


def _head_sum(x):
    s = x
    k = 1
    while k < HEAD_DIM:
        s = s + _swap_lanes(s, k)
        k *= 2
    return s


def _rms(x):
    return x * lax.rsqrt(jnp.mean(x * x, axis=-1, keepdims=True) + EPS)


def _norm_mod_fn(x, g, shift, scale):
    return ((_rms(x) * g) * (1.0 + scale) + shift,)


def _swiglu_fn(u):
    gate, up = u[:, :D_FF], u[:, D_FF:]
    return (jax.nn.silu(gate) * up,)


def _resid_fn(coef, x, y, gate):
    return (x + (coef * gate) * y,)


def _scale_fn(y, s):
    return (y * s,)


def _tile_lanes(tab, width):
    return tab if tab.shape[1] == width else jnp.concatenate([tab] * (width // tab.shape[1]), axis=1)


def _rope_fn(x, cos, sin):
    w = x.shape[1]
    return (x * _tile_lanes(cos, w) + _swap_lanes(x, 16) * _tile_lanes(sin, w),)


def _head_norm(x, g):
    ms = _head_sum(x * x) * (1.0 / HEAD_DIM)
    return x * lax.rsqrt(ms + EPS) * g


def _norm_rope_fn(x, cos, sin, g):
    return _rope_fn(_head_norm(x, g), cos, sin)


def _head_norm_fn(x, g):
    return (_head_norm(x, g),)


def _merge_fn(gl, z0, z1, z2, z3):
    zs = (z0, z1, z2, z3)
    terms = [jax.nn.sigmoid(gl[:, k * D_MODEL:(k + 1) * D_MODEL].astype(F32)) * zs[k].astype(F32)
             for k in range(N_BRANCH)]
    return (sum(terms[1:], terms[0]),)


def _s5_pre_fn(y0r, y0i, y1r, y1i, u, d):
    return (jax.nn.gelu(((y0r - y0i) + (y1r - y1i)) + d * u),)


def _glu_fn(z):
    return (z[:, :BRANCH_W] * jax.nn.sigmoid(z[:, BRANCH_W:]),)


def _silu_fn(x):
    return (jax.nn.silu(x),)


def _loss_fn(x, tgt, g):
    err = jnp.square(_rms(x) * g - tgt)
    return (0.5 * jnp.mean(err, axis=-1, keepdims=True),)


norm_mod = rowwise(_norm_mod_fn, 1, "norm_mod")
swiglu_act = rowwise(_swiglu_fn, 1, "swiglu")
resid_half = rowwise(functools.partial(_resid_fn, 0.5), 2, "resid_half")
resid_full = rowwise(functools.partial(_resid_fn, 1.0), 2, "resid_full")
scale_rows = rowwise(_scale_fn, 1, "pool_scale")
rope = rowwise(_rope_fn, 3, "rope")
norm_rope = rowwise(_norm_rope_fn, 3, "norm_rope")
head_norm = rowwise(_head_norm_fn, 1, "head_norm")
merge = rowwise(_merge_fn, 5, "merge")
s5_pre = rowwise(_s5_pre_fn, 5, "s5_pre")
glu = rowwise(_glu_fn, 1, "glu")
silu_rows = rowwise(_silu_fn, 1, "silu")
loss_rows = rowwise(_loss_fn, 2, "loss_head")


POOL_HALO = 16


def _pool_call(xa, adjoint, name):
    n, width = xa.shape
    tm = _tile(n, 512, POOL_HALO)
    halo_blocks = tm // POOL_HALO
    last_halo = n // POOL_HALO - 1
    ext_rows = tm + 2 * POOL_HALO

    def body(prev_ref, cur_ref, next_ref, o_ref, ext_ref):
        i = pl.program_id(0)
        ext_ref[0:POOL_HALO] = prev_ref[...]
        ext_ref[POOL_HALO:POOL_HALO + tm] = cur_ref[...]
        ext_ref[POOL_HALO + tm:ext_rows] = next_ref[...]
        e = ext_ref[...]
        row = lax.broadcasted_iota(jnp.int32, e.shape, 0) + (i * tm - POOL_HALO)
        grp = lax.broadcasted_iota(jnp.int32, e.shape, 1) // POOL_GROUP
        win = jnp.where(grp == 0, POOL_WINDOWS[0],
                        jnp.where(grp == 1, POOL_WINDOWS[1], jnp.where(grp == 2, POOL_WINDOWS[2], POOL_WINDOWS[3])))
        valid = (row >= 0) & (row < n)
        lo = jnp.clip(row - win // 2, 0, n)
        hi = jnp.clip(row - win // 2 + win, 0, n)
        cnt = jnp.maximum((hi - lo).astype(F32), 1.0)
        e0 = jnp.where(valid, e / cnt if adjoint else e, 0.0)

        def shift(z, s):
            return pltpu.roll(z, s % ext_rows, 0)

        s2 = e0 + shift(e0, -1 if adjoint else 1)
        s4 = shift(s2, 1) + shift(s2, -1)
        s8 = shift(s4, 2) + shift(s4, -2)
        s16 = shift(s8, 4) + shift(s8, -4)
        s = jnp.where(grp == 0, s2, jnp.where(grp == 1, s4, jnp.where(grp == 2, s8, s16)))
        out = (s - e) if adjoint else (s / cnt - e)
        o_ref[...] = out[POOL_HALO:POOL_HALO + tm]

    return pl.pallas_call(
        body, name=name, grid=(n // tm,),
        in_specs=[pl.BlockSpec((POOL_HALO, width), lambda i: (jnp.maximum(i * halo_blocks - 1, 0), 0)),
                  pl.BlockSpec((tm, width), lambda i: (i, 0)),
                  pl.BlockSpec((POOL_HALO, width), lambda i: (jnp.minimum((i + 1) * halo_blocks, last_halo), 0))],
        out_specs=pl.BlockSpec((tm, width), lambda i: (i, 0)),
        out_shape=jax.ShapeDtypeStruct((n, width), F32),
        scratch_shapes=[pltpu.VMEM((ext_rows, width), F32)],
        compiler_params=_cparams(("parallel",), 4 * 4 * ext_rows * width),
    )(xa, xa, xa)


@jax.custom_vjp
def pool_diff(xa):
    return _pool_call(xa, False, "pool_fwd")


pool_diff.defvjp(lambda xa: (pool_diff(xa), None), lambda _, g: (_pool_call(g, True, "pool_bwd"),))


ATT_SCALE = HEAD_DIM ** -0.5


def _attn_blocks(t, band):
    if band:
        b = _tile(t, 256, LANES)
        return b, b
    return _tile(t, 512, LANES), _tile(t, 1024, LANES)


def _qk_scores(q, k):
    return lax.dot_general((q * ATT_SCALE).astype(BF16), k.astype(BF16), (((1,), (1,)), ((), ())),
                           preferred_element_type=F32)


def _band_mask(qi, kblk, bq, bk, n_blocks, r0, rows):
    qpos = qi * bq + (r0 + lax.broadcasted_iota(jnp.int32, (rows, bk), 0)) % bq
    kpos = kblk * bk + lax.broadcasted_iota(jnp.int32, (rows, bk), 1)
    return (kblk >= 0) & (kblk < n_blocks) & (jnp.abs(kpos - qpos) <= WINDOW)


def _sink_rows(sink_ref, h, bq):
    r = lax.broadcasted_iota(jnp.int32, (Q_PER_KV * bq, 1), 0)
    return jnp.where(r < bq, sink_ref[h * Q_PER_KV], sink_ref[h * Q_PER_KV + 1])


def _lat_index(band, nb):
    if band:
        return lambda h, i, j: (h, jnp.clip(i - 1 + j, 0, nb - 1), 0)
    return lambda h, i, j: (h, j, 0)


ATT_SUB_ROWS = 256


def _flash_fwd(q, klt, vl, kct, vc, sink, band):
    kvh, g, t, dh = q.shape
    c = kct.shape[2]
    has_lat = klt is not None
    has_sink = sink is not None
    bq, bk = _attn_blocks(t, band)
    if not band:
        bk = _tile(t, 4096, LANES)
    nb = t // bk
    nkv = (3 if band else nb) if has_lat else 1
    rows = g * bq
    sub = min(ATT_SUB_ROWS, rows)

    def body(*refs):
        it = iter(refs)
        q_ref, kc_ref, vc_ref = next(it), next(it), next(it)
        kl_ref, vl_ref = (next(it), next(it)) if has_lat else (None, None)
        sink_ref = next(it) if has_sink else None
        o_ref, lse_ref, m_ref, l_ref, acc_ref = next(it), next(it), next(it), next(it), next(it)
        h, qi, kj = pl.program_id(0), pl.program_id(1), pl.program_id(2)
        qv = (q_ref[0].reshape(rows, dh) * ATT_SCALE).astype(BF16)

        def part(kt_ref, v_ref, masked):
            kt, v = kt_ref[0].astype(BF16), v_ref[0].astype(BF16)
            m_all, l_all, acc_all = m_ref[...], l_ref[...], acc_ref[...]
            m_out, l_out, acc_out = [], [], []
            for r0 in range(0, rows, sub):
                s = jnp.dot(qv[r0:r0 + sub], kt, preferred_element_type=F32)
                if masked:
                    s = jnp.where(_band_mask(qi, qi - 1 + kj, bq, bk, nb, r0, sub), s, NEG_BIG)
                m_old = m_all[r0:r0 + sub]
                m_new = jnp.maximum(m_old, jnp.max(s, axis=-1, keepdims=True))
                p = jnp.exp(s - m_new)
                alpha = jnp.exp(m_old - m_new)
                m_out.append(m_new)
                l_out.append(alpha * l_all[r0:r0 + sub] + jnp.sum(p, axis=-1, keepdims=True))
                acc_out.append(alpha * acc_all[r0:r0 + sub]
                               + jnp.dot(p.astype(BF16), v, preferred_element_type=F32))
            m_ref[...] = jnp.concatenate(m_out, axis=0)
            l_ref[...] = jnp.concatenate(l_out, axis=0)
            acc_ref[...] = jnp.concatenate(acc_out, axis=0)

        @pl.when(kj == 0)
        def _():
            if has_sink:
                m_ref[...] = _sink_rows(sink_ref, h, bq)
                l_ref[...] = jnp.ones_like(l_ref)
            else:
                m_ref[...] = jnp.full_like(m_ref, NEG_BIG)
                l_ref[...] = jnp.zeros_like(l_ref)
            acc_ref[...] = jnp.zeros_like(acc_ref)
            part(kc_ref, vc_ref, False)

        if has_lat:
            part(kl_ref, vl_ref, band)

        @pl.when(kj == nkv - 1)
        def _():
            o_ref[0] = (acc_ref[...] / l_ref[...]).reshape(g, bq, dh)
            lse_ref[0] = (m_ref[...] + jnp.log(l_ref[...])).reshape(g, bq, 1)

    q_spec = pl.BlockSpec((1, g, bq, dh), lambda h, i, j: (h, 0, i, 0))
    r_spec = pl.BlockSpec((1, g, bq, 1), lambda h, i, j: (h, 0, i, 0))
    c_spec = pl.BlockSpec((1, c, dh), lambda h, i, j: (h, 0, 0))
    ct_spec = pl.BlockSpec((1, dh, c), lambda h, i, j: (h, 0, 0))
    in_specs, args = [q_spec, ct_spec, c_spec], [q, kct, vc]
    if has_lat:
        lat = _lat_index(band, nb)
        in_specs += [pl.BlockSpec((1, dh, bk), lambda h, i, j: (h, 0, lat(h, i, j)[1])),
                     pl.BlockSpec((1, bk, dh), lat)]
        args += [klt, vl]
    if has_sink:
        in_specs.append(pl.BlockSpec(memory_space=pltpu.SMEM))
        args.append(sink)
    return pl.pallas_call(
        body, name="attn_fwd_band" if band else "attn_fwd", grid=(kvh, t // bq, nkv),
        in_specs=in_specs, out_specs=[q_spec, r_spec],
        out_shape=[jax.ShapeDtypeStruct(q.shape, F32), jax.ShapeDtypeStruct((kvh, g, t, 1), F32)],
        scratch_shapes=[pltpu.VMEM((rows, 1), F32), pltpu.VMEM((rows, 1), F32), pltpu.VMEM((rows, dh), F32)],
        compiler_params=_cparams(("parallel", "parallel", "arbitrary"), 4 * 4 * rows * max(bk, c)),
    )(*args)


def _flash_dq(q, kl, vl, kc, vc, sink, o, do, lse, band):
    kvh, g, t, dh = q.shape
    c = kc.shape[1]
    has_lat = kl is not None
    has_sink = sink is not None
    bq, bk = _attn_blocks(t, band)
    nb = t // bk
    nkv = (3 if band else nb) if has_lat else 1
    rows = g * bq

    def body(*refs):
        it = iter(refs)
        q_ref, o_ref, do_ref, lse_ref, kc_ref, vc_ref = (next(it) for _ in range(6))
        kl_ref, vl_ref = (next(it), next(it)) if has_lat else (None, None)
        sink_ref = next(it) if has_sink else None
        dq_ref, delta_ref, dsink_ref, acc_ref, dl_ref = (next(it) for _ in range(5))
        h, qi, kj = pl.program_id(0), pl.program_id(1), pl.program_id(2)
        qv = q_ref[0].reshape(rows, dh)
        dov = do_ref[0].reshape(rows, dh)
        lse_v = lse_ref[0].reshape(rows, 1)

        def update(s, k, v):
            p = jnp.exp(s - lse_v)
            dp = lax.dot_general(dov.astype(BF16), v.astype(BF16), (((1,), (1,)), ((), ())),
                                 preferred_element_type=F32)
            ds = p * (dp - dl_ref[...])
            acc_ref[...] += jnp.dot(ds.astype(BF16), k.astype(BF16), preferred_element_type=F32)

        @pl.when(kj == 0)
        def _():
            dl_ref[...] = jnp.sum(o_ref[0].reshape(rows, dh) * dov, axis=-1, keepdims=True)
            acc_ref[...] = jnp.zeros_like(acc_ref)
            update(_qk_scores(qv, kc_ref[0]), kc_ref[0], vc_ref[0])

        if has_lat:
            s = _qk_scores(qv, kl_ref[0])
            if band:
                s = jnp.where(_band_mask(qi, qi - 1 + kj, bq, bk, nb, 0, rows), s, NEG_BIG)
            update(s, kl_ref[0], vl_ref[0])

        @pl.when(kj == nkv - 1)
        def _():
            dq_ref[0] = (acc_ref[...] * ATT_SCALE).reshape(g, bq, dh)
            delta_ref[0] = dl_ref[...].reshape(g, bq, 1)
            if has_sink:
                p_sink = jnp.exp(_sink_rows(sink_ref, h, bq) - lse_v)
                dsink_ref[0] = (-p_sink * dl_ref[...]).reshape(g, bq, 1)
            else:
                dsink_ref[0] = jnp.zeros((g, bq, 1), F32)

    q_spec = pl.BlockSpec((1, g, bq, dh), lambda h, i, j: (h, 0, i, 0))
    r_spec = pl.BlockSpec((1, g, bq, 1), lambda h, i, j: (h, 0, i, 0))
    c_spec = pl.BlockSpec((1, c, dh), lambda h, i, j: (h, 0, 0))
    in_specs, args = [q_spec, q_spec, q_spec, r_spec, c_spec, c_spec], [q, o, do, lse, kc, vc]
    if has_lat:
        l_spec = pl.BlockSpec((1, bk, dh), _lat_index(band, nb))
        in_specs += [l_spec, l_spec]
        args += [kl, vl]
    if has_sink:
        in_specs.append(pl.BlockSpec(memory_space=pltpu.SMEM))
        args.append(sink)
    row_shape = jax.ShapeDtypeStruct((kvh, g, t, 1), F32)
    return pl.pallas_call(
        body, name="attn_dq_band" if band else "attn_dq", grid=(kvh, t // bq, nkv),
        in_specs=in_specs, out_specs=[q_spec, r_spec, r_spec],
        out_shape=[jax.ShapeDtypeStruct(q.shape, F32), row_shape, row_shape],
        scratch_shapes=[pltpu.VMEM((rows, dh), F32), pltpu.VMEM((rows, 1), F32)],
        compiler_params=_cparams(("parallel", "parallel", "arbitrary"), 4 * 6 * rows * max(bk, c)),
    )(*args)


def _flash_dkv(q, do, lse, delta, k, v, band):
    kvh, g, t, dh = q.shape
    nk_rows = k.shape[1]
    if band:
        bq, bk = _attn_blocks(t, True)
    else:
        bq = _tile(t, 512, LANES)
        bk = _tile(nk_rows, 1024, LANES)
    nbq = t // bq
    nq = 3 if band else nbq
    rows = g * bq

    def body(q_ref, do_ref, lse_ref, delta_ref, k_ref, v_ref, dk_ref, dv_ref, dk_acc, dv_acc):
        ki, qj = pl.program_id(1), pl.program_id(2)
        qv = q_ref[0].reshape(rows, dh)
        dov = do_ref[0].reshape(rows, dh)

        @pl.when(qj == 0)
        def _():
            dk_acc[...] = jnp.zeros_like(dk_acc)
            dv_acc[...] = jnp.zeros_like(dv_acc)

        s = _qk_scores(qv, k_ref[0])
        if band:
            qblk = ki - 1 + qj
            qpos = qblk * bq + lax.broadcasted_iota(jnp.int32, (rows, bk), 0) % bq
            kpos = ki * bk + lax.broadcasted_iota(jnp.int32, (rows, bk), 1)
            s = jnp.where((qblk >= 0) & (qblk < nbq) & (jnp.abs(kpos - qpos) <= WINDOW), s, NEG_BIG)
        p = jnp.exp(s - lse_ref[0].reshape(rows, 1))
        dp = lax.dot_general(dov.astype(BF16), v_ref[0].astype(BF16), (((1,), (1,)), ((), ())),
                             preferred_element_type=F32)
        ds = p * (dp - delta_ref[0].reshape(rows, 1))
        tn = (((0,), (0,)), ((), ()))
        dv_acc[...] += lax.dot_general(p.astype(BF16), dov.astype(BF16), tn, preferred_element_type=F32)
        dk_acc[...] += lax.dot_general(ds.astype(BF16), qv.astype(BF16), tn, preferred_element_type=F32)

        @pl.when(qj == nq - 1)
        def _():
            dk_ref[0] = dk_acc[...] * ATT_SCALE
            dv_ref[0] = dv_acc[...]

    if band:
        q_index = lambda h, i, j: (h, 0, jnp.clip(i - 1 + j, 0, nbq - 1), 0)
    else:
        q_index = lambda h, i, j: (h, 0, j, 0)
    q_spec = pl.BlockSpec((1, g, bq, dh), q_index)
    r_spec = pl.BlockSpec((1, g, bq, 1), q_index)
    k_spec = pl.BlockSpec((1, bk, dh), lambda h, i, j: (h, i, 0))
    return pl.pallas_call(
        body, name="attn_dkv_band" if band else "attn_dkv", grid=(kvh, nk_rows // bk, nq),
        in_specs=[q_spec, q_spec, r_spec, r_spec, k_spec, k_spec], out_specs=[k_spec, k_spec],
        out_shape=[jax.ShapeDtypeStruct(k.shape, F32), jax.ShapeDtypeStruct(k.shape, F32)],
        scratch_shapes=[pltpu.VMEM((bk, dh), F32), pltpu.VMEM((bk, dh), F32)],
        compiler_params=_cparams(("parallel", "parallel", "arbitrary"), 4 * 6 * rows * bk),
    )(q, do, lse, delta, k, v)


def _flash_bwd_full(q, kl, vl, kc, vc, o, do, lse):
    kvh, g, t, dh = q.shape
    c = kc.shape[1]
    bq, bk = _tile(t, 512, LANES), _tile(t, 1024, LANES)
    nq, nkv = t // bq, t // bk
    rows = g * bq
    klt, vlt = jnp.swapaxes(kl, 1, 2), jnp.swapaxes(vl, 1, 2)
    kct, vct = jnp.swapaxes(kc, 1, 2), jnp.swapaxes(vc, 1, 2)
    tn = (((0,), (0,)), ((), ()))

    def body(q_ref, o_ref, do_ref, lse_ref, kc_ref, kct_ref, vct_ref, kl_ref, klt_ref, vlt_ref,
             dq_ref, delta_ref, dk_hbm, dv_hbm, dq_acc, dl_ref, dk_acc, dv_acc):
        h, qi, kj = pl.program_id(0), pl.program_id(1), pl.program_id(2)
        q_raw = q_ref[0].reshape(rows, dh).astype(BF16)
        qv = (q_ref[0].reshape(rows, dh) * ATT_SCALE).astype(BF16)
        dov = do_ref[0].reshape(rows, dh).astype(BF16)
        lse_v = lse_ref[0].reshape(rows, 1)

        def tile(kt, vt):
            s = jnp.dot(qv, kt.astype(BF16), preferred_element_type=F32)
            p = jnp.exp(s - lse_v)
            dp = jnp.dot(dov, vt.astype(BF16), preferred_element_type=F32)
            return p, (p * (dp - dl_ref[...])).astype(BF16)

        @pl.when((qi == 0) & (kj == 0))
        def _():
            dk_acc[...] = jnp.zeros_like(dk_acc)
            dv_acc[...] = jnp.zeros_like(dv_acc)

        @pl.when(kj == 0)
        def _():
            dl_ref[...] = jnp.sum(o_ref[0].reshape(rows, dh) * do_ref[0].reshape(rows, dh),
                                  axis=-1, keepdims=True)
            _, ds = tile(kct_ref[0], vct_ref[0])
            dq_acc[...] = jnp.dot(ds, kc_ref[0].astype(BF16), preferred_element_type=F32)

        p, ds = tile(klt_ref[0], vlt_ref[0])
        dq_acc[...] += jnp.dot(ds, kl_ref[0].astype(BF16), preferred_element_type=F32)
        ks = pl.ds(pl.multiple_of(kj * bk, bk), bk)
        dv_acc[ks] += lax.dot_general(p.astype(BF16), dov, tn, preferred_element_type=F32)
        dk_acc[ks] += lax.dot_general(ds, q_raw, tn, preferred_element_type=F32)

        @pl.when(kj == nkv - 1)
        def _():
            dq_ref[0] = (dq_acc[...] * ATT_SCALE).reshape(g, bq, dh)
            delta_ref[0] = dl_ref[...].reshape(g, bq, 1)

        @pl.when((qi == nq - 1) & (kj == nkv - 1))
        def _():
            dk_acc[...] = dk_acc[...] * ATT_SCALE
            pltpu.sync_copy(dk_acc, dk_hbm.at[h])
            pltpu.sync_copy(dv_acc, dv_hbm.at[h])

    q_spec = pl.BlockSpec((1, g, bq, dh), lambda h, i, j: (h, 0, i, 0))
    r_spec = pl.BlockSpec((1, g, bq, 1), lambda h, i, j: (h, 0, i, 0))
    c_spec = pl.BlockSpec((1, c, dh), lambda h, i, j: (h, 0, 0))
    ct_spec = pl.BlockSpec((1, dh, c), lambda h, i, j: (h, 0, 0))
    l_spec = pl.BlockSpec((1, bk, dh), lambda h, i, j: (h, j, 0))
    lt_spec = pl.BlockSpec((1, dh, bk), lambda h, i, j: (h, 0, j))
    any_spec = pl.BlockSpec(memory_space=pl.ANY)
    kv_shape = jax.ShapeDtypeStruct((kvh, t, dh), F32)
    return pl.pallas_call(
        body, name="attn_bwd_full", grid=(kvh, nq, nkv),
        in_specs=[q_spec, q_spec, q_spec, r_spec, c_spec, ct_spec, ct_spec, l_spec, lt_spec, lt_spec],
        out_specs=[q_spec, r_spec, any_spec, any_spec],
        out_shape=[jax.ShapeDtypeStruct(q.shape, F32), jax.ShapeDtypeStruct((kvh, g, t, 1), F32),
                   kv_shape, kv_shape],
        scratch_shapes=[pltpu.VMEM((rows, dh), F32), pltpu.VMEM((rows, 1), F32),
                        pltpu.VMEM((t, dh), F32), pltpu.VMEM((t, dh), F32)],
        compiler_params=_cparams(("arbitrary", "arbitrary", "arbitrary"), 4 * 2 * rows * bk,
                                 2 * 4 * t * LANES),
    )(q, o, do, lse, kc, kct, vct, kl, klt, vlt)


BAND_BQ = 512


def _band_specs(t, shape_of):
    per = BAND_BQ // WINDOW
    n_halo = t // WINDOW

    def spec(n, index):
        shape, axis = shape_of(n)

        def index_map(h, i):
            idx = [h] + [0] * (len(shape) - 1)
            idx[axis] = index(i)
            return tuple(idx)

        return pl.BlockSpec(shape, index_map)

    return [spec(WINDOW, lambda i: jnp.maximum(i * per - 1, 0)),
            spec(BAND_BQ, lambda i: i),
            spec(WINDOW, lambda i: jnp.minimum((i + 1) * per, n_halo - 1))]


def _band_visible(i, t, c, rows):
    cols = c + BAND_BQ + 2 * WINDOW
    col = lax.broadcasted_iota(jnp.int32, (rows, cols), 1)
    qpos = i * BAND_BQ + lax.broadcasted_iota(jnp.int32, (rows, cols), 0) % BAND_BQ
    kpos = i * BAND_BQ - WINDOW + (col - c)
    return (col < c) | ((kpos >= 0) & (kpos < t) & (jnp.abs(kpos - qpos) <= WINDOW))


def _band_fwd(q, klt, vl, kct, vc, sink):
    kvh, g, t, dh = q.shape
    c = kct.shape[2]
    rows = g * BAND_BQ

    def body(q_ref, kct_ref, vc_ref, ktp, ktc, ktn, vp, vcur, vn, sink_ref, o_ref, lse_ref):
        h, i = pl.program_id(0), pl.program_id(1)
        qv = (q_ref[0].reshape(rows, dh) * ATT_SCALE).astype(BF16)
        kt = jnp.concatenate([kct_ref[0], ktp[0], ktc[0], ktn[0]], axis=1).astype(BF16)
        v = jnp.concatenate([vc_ref[0], vp[0], vcur[0], vn[0]], axis=0).astype(BF16)
        s = jnp.where(_band_visible(i, t, c, rows), jnp.dot(qv, kt, preferred_element_type=F32), NEG_BIG)
        sink_r = _sink_rows(sink_ref, h, BAND_BQ)
        m = jnp.maximum(sink_r, jnp.max(s, axis=-1, keepdims=True))
        p = jnp.exp(s - m)
        l = jnp.exp(sink_r - m) + jnp.sum(p, axis=-1, keepdims=True)
        o_ref[0] = (jnp.dot(p.astype(BF16), v, preferred_element_type=F32) / l).reshape(g, BAND_BQ, dh)
        lse_ref[0] = (m + jnp.log(l)).reshape(g, BAND_BQ, 1)

    q_spec = pl.BlockSpec((1, g, BAND_BQ, dh), lambda h, i: (h, 0, i, 0))
    r_spec = pl.BlockSpec((1, g, BAND_BQ, 1), lambda h, i: (h, 0, i, 0))
    in_specs = ([q_spec, pl.BlockSpec((1, dh, c), lambda h, i: (h, 0, 0)),
                 pl.BlockSpec((1, c, dh), lambda h, i: (h, 0, 0))]
                + _band_specs(t,lambda n: ((1, dh, n), 2))
                + _band_specs(t,lambda n: ((1, n, dh), 1))
                + [pl.BlockSpec(memory_space=pltpu.SMEM)])
    return pl.pallas_call(
        body, name="attn_band_fwd", grid=(kvh, t // BAND_BQ), in_specs=in_specs, out_specs=[q_spec, r_spec],
        out_shape=[jax.ShapeDtypeStruct(q.shape, F32), jax.ShapeDtypeStruct((kvh, g, t, 1), F32)],
        compiler_params=_cparams(("parallel", "parallel"), 4 * 2 * rows * (c + BAND_BQ + 2 * WINDOW)),
    )(q, kct, vc, klt, klt, klt, vl, vl, vl, sink)


def _band_dq(q, kl, klt, vlt, kc, kct, vct, sink, o, do, lse):
    kvh, g, t, dh = q.shape
    c = kc.shape[1]
    rows = g * BAND_BQ

    def body(q_ref, o_ref, do_ref, lse_ref, kc_ref, kct_ref, vct_ref, kp, kcur, kn, ktp, ktc, ktn,
             vtp, vtc, vtn, sink_ref, dq_ref, delta_ref, dsink_ref):
        h, i = pl.program_id(0), pl.program_id(1)
        qv = (q_ref[0].reshape(rows, dh) * ATT_SCALE).astype(BF16)
        dov = do_ref[0].reshape(rows, dh)
        lse_v = lse_ref[0].reshape(rows, 1)
        kt = jnp.concatenate([kct_ref[0], ktp[0], ktc[0], ktn[0]], axis=1).astype(BF16)
        vt = jnp.concatenate([vct_ref[0], vtp[0], vtc[0], vtn[0]], axis=1).astype(BF16)
        k = jnp.concatenate([kc_ref[0], kp[0], kcur[0], kn[0]], axis=0).astype(BF16)
        s = jnp.where(_band_visible(i, t, c, rows), jnp.dot(qv, kt, preferred_element_type=F32), NEG_BIG)
        p = jnp.exp(s - lse_v)
        delta = jnp.sum(o_ref[0].reshape(rows, dh) * dov, axis=-1, keepdims=True)
        dp = jnp.dot(dov.astype(BF16), vt, preferred_element_type=F32)
        ds = (p * (dp - delta)).astype(BF16)
        dq_ref[0] = (jnp.dot(ds, k, preferred_element_type=F32) * ATT_SCALE).reshape(g, BAND_BQ, dh)
        delta_ref[0] = delta.reshape(g, BAND_BQ, 1)
        p_sink = jnp.exp(_sink_rows(sink_ref, h, BAND_BQ) - lse_v)
        dsink_ref[0] = (-p_sink * delta).reshape(g, BAND_BQ, 1)

    q_spec = pl.BlockSpec((1, g, BAND_BQ, dh), lambda h, i: (h, 0, i, 0))
    r_spec = pl.BlockSpec((1, g, BAND_BQ, 1), lambda h, i: (h, 0, i, 0))
    ct_spec = pl.BlockSpec((1, dh, c), lambda h, i: (h, 0, 0))
    rows_of = lambda n: ((1, n, dh), 1)
    lanes_of = lambda n: ((1, dh, n), 2)
    in_specs = ([q_spec, q_spec, q_spec, r_spec, pl.BlockSpec((1, c, dh), lambda h, i: (h, 0, 0)), ct_spec, ct_spec]
                + _band_specs(t,rows_of) + _band_specs(t,lanes_of) + _band_specs(t,lanes_of)
                + [pl.BlockSpec(memory_space=pltpu.SMEM)])
    row_shape = jax.ShapeDtypeStruct((kvh, g, t, 1), F32)
    return pl.pallas_call(
        body, name="attn_band_dq", grid=(kvh, t // BAND_BQ), in_specs=in_specs,
        out_specs=[q_spec, r_spec, r_spec], out_shape=[jax.ShapeDtypeStruct(q.shape, F32), row_shape, row_shape],
        compiler_params=_cparams(("parallel", "parallel"), 4 * 3 * rows * (c + BAND_BQ + 2 * WINDOW)),
    )(q, o, do, lse, kc, kct, vct, kl, kl, kl, klt, klt, klt, vlt, vlt, vlt, sink)


def _band_dkv(q, do, lse, delta, klt, vlt):
    kvh, g, t, dh = q.shape
    span = BAND_BQ + 2 * WINDOW
    rows = g * span
    tn = (((0,), (0,)), ((), ()))

    def body(qp, qc, qn, dop, doc, don, lp, lc, ln, dp_, dc_, dn_, kt_ref, vt_ref, dk_ref, dv_ref):
        j = pl.program_id(1)

        def stack(a, b, c_):
            return jnp.concatenate([jnp.concatenate([a[0, gi], b[0, gi], c_[0, gi]], axis=0)
                                    for gi in range(g)], axis=0)

        q_all, do_all = stack(qp, qc, qn), stack(dop, doc, don).astype(BF16)
        lse_all, delta_all = stack(lp, lc, ln), stack(dp_, dc_, dn_)
        s = jnp.dot((q_all * ATT_SCALE).astype(BF16), kt_ref[0].astype(BF16), preferred_element_type=F32)
        qpos = j * BAND_BQ - WINDOW + lax.broadcasted_iota(jnp.int32, (rows, BAND_BQ), 0) % span
        kpos = j * BAND_BQ + lax.broadcasted_iota(jnp.int32, (rows, BAND_BQ), 1)
        s = jnp.where((qpos >= 0) & (qpos < t) & (jnp.abs(kpos - qpos) <= WINDOW), s, NEG_BIG)
        p = jnp.exp(s - lse_all)
        dp = jnp.dot(do_all, vt_ref[0].astype(BF16), preferred_element_type=F32)
        ds = (p * (dp - delta_all)).astype(BF16)
        dv_ref[0] = lax.dot_general(p.astype(BF16), do_all, tn, preferred_element_type=F32)
        dk_ref[0] = lax.dot_general(ds, q_all.astype(BF16), tn, preferred_element_type=F32) * ATT_SCALE

    q_specs = _band_specs(t,lambda n: ((1, g, n, dh), 2))
    r_specs = _band_specs(t,lambda n: ((1, g, n, 1), 2))
    kt_spec = pl.BlockSpec((1, dh, BAND_BQ), lambda h, j: (h, 0, j))
    k_spec = pl.BlockSpec((1, BAND_BQ, dh), lambda h, j: (h, j, 0))
    kv_shape = jax.ShapeDtypeStruct((kvh, t, dh), F32)
    return pl.pallas_call(
        body, name="attn_band_dkv", grid=(kvh, t // BAND_BQ),
        in_specs=q_specs + q_specs + r_specs + r_specs + [kt_spec, kt_spec], out_specs=[k_spec, k_spec],
        out_shape=[kv_shape, kv_shape],
        compiler_params=_cparams(("parallel", "parallel"), 4 * 3 * rows * BAND_BQ),
    )(q, q, q, do, do, do, lse, lse, lse, delta, delta, delta, klt, vlt)


def make_attention(band, has_lat, has_sink):
    def unpack(args):
        it = iter(args)
        q, kc, vc = next(it), next(it), next(it)
        kl, vl = (next(it), next(it)) if has_lat else (None, None)
        sink = next(it) if has_sink else None
        return q, kl, vl, kc, vc, sink

    def forward(args):
        q, kl, vl, kc, vc, sink = unpack(args)
        klt = jnp.swapaxes(kl, 1, 2) if has_lat else None
        if band:
            return _band_fwd(q, klt, vl, jnp.swapaxes(kc, 1, 2), vc, sink)
        return _flash_fwd(q, klt, vl, jnp.swapaxes(kc, 1, 2), vc, sink, band)

    @jax.custom_vjp
    def op(*args):
        return forward(args)[0]

    def fwd(*args):
        o, lse = forward(args)
        return o, (args, o, lse)

    def bwd(res, do):
        args, o, lse = res
        q, kl, vl, kc, vc, sink = unpack(args)
        if has_lat and not band and not has_sink:
            dq, delta, dkl, dvl = _flash_bwd_full(q, kl, vl, kc, vc, o, do, lse)
            dkc, dvc = _flash_dkv(q, do, lse, delta, kc, vc, False)
            return dq, dkc, dvc, dkl, dvl
        if band:
            klt, vlt = jnp.swapaxes(kl, 1, 2), jnp.swapaxes(vl, 1, 2)
            dq, delta, dsink_rows = _band_dq(q, kl, klt, vlt, kc, jnp.swapaxes(kc, 1, 2),
                                             jnp.swapaxes(vc, 1, 2), sink, o, do, lse)
            dkc, dvc = _flash_dkv(q, do, lse, delta, kc, vc, False)
            dkl, dvl = _band_dkv(q, do, lse, delta, klt, vlt)
            return dq, dkc, dvc, dkl, dvl, jnp.sum(dsink_rows, axis=(2, 3)).reshape(-1)
        dq, delta, dsink_rows = _flash_dq(q, kl, vl, kc, vc, sink, o, do, lse, band)
        dkc, dvc = _flash_dkv(q, do, lse, delta, kc, vc, False)
        grads = [dq, dkc, dvc]
        if has_lat:
            grads += list(_flash_dkv(q, do, lse, delta, kl, vl, band))
        if has_sink:
            grads.append(jnp.sum(dsink_rows, axis=(2, 3)).reshape(-1))
        return tuple(grads)

    op.defvjp(fwd, bwd)
    return op


attn_window = make_attention(True, True, True)
attn_global = make_attention(False, True, False)
attn_ctx_sink = make_attention(False, False, True)
attn_ctx = make_attention(False, False, False)


def _to_heads(z, n_heads):
    return z.reshape(z.shape[0], n_heads, HEAD_DIM).transpose(1, 0, 2)


def _q_heads(z):
    return z.reshape(z.shape[0], KV_HEADS, Q_PER_KV, HEAD_DIM).transpose(1, 2, 0, 3)


def _from_q_heads(o):
    return o.transpose(2, 0, 1, 3).reshape(o.shape[2], KV_HEADS * Q_PER_KV * HEAD_DIM)


def _scan_tile(t):
    return _tile(t, 512, SUBLANES)


def _scan_fwd_call(bre, bim, lre, lim, h0re, h0im, rev):
    t = bre.shape[0]
    tt = _scan_tile(t)
    nb = t // tt
    plane = bre.shape[1:]

    def body(bre_ref, bim_ref, lre_ref, lim_ref, h0re_ref, h0im_ref, sre_ref, sim_ref, h_ref):
        @pl.when(pl.program_id(0) == 0)
        def _():
            h_ref[0] = h0re_ref[...]
            h_ref[1] = h0im_ref[...]

        ar, ai = lre_ref[...], lim_ref[...]

        def step(j, carry):
            hr, hi = carry
            tj = (tt - 1 - j) if rev else j
            nr = ar * hr - ai * hi + bre_ref[tj]
            ni = ar * hi + ai * hr + bim_ref[tj]
            sre_ref[tj] = nr
            sim_ref[tj] = ni
            return nr, ni

        hr, hi = lax.fori_loop(0, tt, step, (h_ref[0], h_ref[1]), unroll=8)
        h_ref[0] = hr
        h_ref[1] = hi

    blk = pl.BlockSpec((tt,) + plane, (lambda i: (nb - 1 - i, 0, 0)) if rev else (lambda i: (i, 0, 0)))
    par = pl.BlockSpec(plane, lambda i: (0, 0))
    return pl.pallas_call(
        body, name="s5_scan_fwd", grid=(nb,), in_specs=[blk, blk, par, par, par, par], out_specs=[blk, blk],
        out_shape=[jax.ShapeDtypeStruct(bre.shape, F32)] * 2,
        scratch_shapes=[pltpu.VMEM((2,) + plane, F32)],
        compiler_params=_cparams(("arbitrary",), 4 * 4 * tt * plane[0] * plane[1]),
    )(bre, bim, lre, lim, h0re, h0im)


def _scan_bwd_call(gre, gim, sre, sim, lre, lim, h0re, h0im, rev):
    t = gre.shape[0]
    tt = _scan_tile(t)
    nb = t // tt
    plane = gre.shape[1:]
    down = not rev

    def body(gre_ref, gim_ref, sre_ref, sim_ref, lre_ref, lim_ref, h0re_ref, h0im_ref,
             dbre_ref, dbim_ref, dare_ref, daim_ref, dh0re_ref, dh0im_ref, carry_ref):
        i = pl.program_id(0)

        @pl.when(i == 0)
        def _():
            carry_ref[...] = jnp.zeros_like(carry_ref)

        ar, ai = lre_ref[...], lim_ref[...]

        def step(j, carry):
            gr, gi, dar, dai = carry
            tj = (tt - 1 - j) if down else j
            hr, hi = sre_ref[tj], sim_ref[tj]
            dar = dar + hr * gr + hi * gi
            dai = dai + hr * gi - hi * gr
            ngr = gre_ref[tj] + ar * gr + ai * gi
            ngi = gim_ref[tj] + ar * gi - ai * gr
            dbre_ref[tj] = ngr
            dbim_ref[tj] = ngi
            return ngr, ngi, dar, dai

        gr, gi, dar, dai = lax.fori_loop(
            0, tt, step, (carry_ref[0], carry_ref[1], carry_ref[2], carry_ref[3]), unroll=8)
        carry_ref[0] = gr
        carry_ref[1] = gi
        carry_ref[2] = dar
        carry_ref[3] = dai

        @pl.when(i == nb - 1)
        def _():
            hr, hi = h0re_ref[...], h0im_ref[...]
            dare_ref[...] = dar + hr * gr + hi * gi
            daim_ref[...] = dai + hr * gi - hi * gr
            dh0re_ref[...] = ar * gr + ai * gi
            dh0im_ref[...] = ar * gi - ai * gr

    blk = pl.BlockSpec((tt,) + plane, (lambda i: (nb - 1 - i, 0, 0)) if down else (lambda i: (i, 0, 0)))
    par = pl.BlockSpec(plane, lambda i: (0, 0))
    return pl.pallas_call(
        body, name="s5_scan_bwd", grid=(nb,), in_specs=[blk, blk, blk, blk, par, par, par, par],
        out_specs=[blk, blk, par, par, par, par],
        out_shape=[jax.ShapeDtypeStruct(gre.shape, F32)] * 2 + [jax.ShapeDtypeStruct(plane, F32)] * 4,
        scratch_shapes=[pltpu.VMEM((4,) + plane, F32)],
        compiler_params=_cparams(("arbitrary",), 4 * 6 * tt * plane[0] * plane[1]),
    )(gre, gim, sre, sim, lre, lim, h0re, h0im)


def make_scan(rev):
    @jax.custom_vjp
    def op(bre, bim, lre, lim, h0re, h0im):
        return tuple(_scan_fwd_call(bre, bim, lre, lim, h0re, h0im, rev))

    def fwd(bre, bim, lre, lim, h0re, h0im):
        sre, sim = _scan_fwd_call(bre, bim, lre, lim, h0re, h0im, rev)
        return (sre, sim), (sre, sim, lre, lim, h0re, h0im)

    def bwd(res, cts):
        sre, sim, lre, lim, h0re, h0im = res
        return tuple(_scan_bwd_call(cts[0], cts[1], sre, sim, lre, lim, h0re, h0im, rev))

    op.defvjp(fwd, bwd)
    return op


scan_up = make_scan(False)
scan_down = make_scan(True)


def _adamw_call(parts, w, m, v, name):
    r, c = w.shape
    tr = _tile(r, 256, SUBLANES)
    nparts = parts.shape[0]
    c1 = 1.0 - ADAM_B1 ** ADAM_STEP
    c2 = 1.0 - ADAM_B2 ** ADAM_STEP

    def body(p_ref, w_ref, m_ref, v_ref, g_ref, d_ref, nm_ref, nv_ref):
        g = p_ref[0].astype(F32)
        for s in range(1, nparts):
            g = g + p_ref[s].astype(F32)
        m1 = ADAM_B1 * m_ref[...] + (1.0 - ADAM_B1) * g
        v1 = ADAM_B2 * v_ref[...] + (1.0 - ADAM_B2) * jnp.square(g)
        g_ref[...] = g
        nm_ref[...] = m1
        nv_ref[...] = v1
        d_ref[...] = -ADAM_LR * ((m1 / c1) / (jnp.sqrt(v1 / c2) + ADAM_EPS) + ADAM_WD * w_ref[...])

    blk = pl.BlockSpec((tr, c), lambda i: (i, 0))
    return pl.pallas_call(
        body, name=name, grid=(r // tr,),
        in_specs=[pl.BlockSpec((nparts, tr, c), lambda i: (0, i, 0)), blk, blk, blk], out_specs=[blk] * 4,
        out_shape=[jax.ShapeDtypeStruct((r, c), F32)] * 4,
        compiler_params=_cparams(("parallel",), 4 * tr * c * (nparts + 7)),
    )(parts, w, m, v)


def _peer(k):
    x, y, c = lax.axis_index("x"), lax.axis_index("y"), lax.axis_index("c")
    px = 1 - x if k & 4 else x
    py = 1 - y if k & 2 else y
    pc = 1 - c if k & 1 else c
    return (px, py, pc), 4 * px + 2 * py + pc


def _my_slot():
    return 4 * lax.axis_index("x") + 2 * lax.axis_index("y") + lax.axis_index("c")


def _exchange_call(x, gather, name):
    slab = x.shape if gather else x.shape[1:]

    def body(x_ref, out_ref, send_sems, recv_sems, local_sem):
        me = _my_slot()
        mine = pltpu.make_async_copy(x_ref if gather else x_ref.at[me], out_ref.at[me], local_sem)
        mine.start()
        sends = []
        for k in range(1, N_DEV):
            peer, slot = _peer(k)
            cp = pltpu.make_async_remote_copy(
                src_ref=x_ref if gather else x_ref.at[slot], dst_ref=out_ref.at[me],
                send_sem=send_sems.at[k - 1], recv_sem=recv_sems.at[k - 1],
                device_id=peer, device_id_type=MESH)
            cp.start()
            sends.append(cp)
        for k in range(1, N_DEV):
            peer, slot = _peer(k)
            pltpu.make_async_remote_copy(
                src_ref=x_ref if gather else x_ref.at[slot], dst_ref=out_ref.at[slot],
                send_sem=send_sems.at[k - 1], recv_sem=recv_sems.at[k - 1],
                device_id=peer, device_id_type=MESH).wait_recv()
        for cp in sends:
            cp.wait_send()
        mine.wait()

    return pl.pallas_call(
        body, name=name,
        in_specs=[pl.BlockSpec(memory_space=pl.ANY)], out_specs=pl.BlockSpec(memory_space=pl.ANY),
        out_shape=jax.ShapeDtypeStruct((N_DEV,) + tuple(slab), x.dtype),
        scratch_shapes=[pltpu.SemaphoreType.DMA((N_DEV - 1,)), pltpu.SemaphoreType.DMA((N_DEV - 1,)),
                        pltpu.SemaphoreType.DMA],
    )(x)


def all_gather(x, name):
    return _exchange_call(x, True, name)


def sibling_swap(x, name):
    def body(x_ref, out_ref, send_sem, recv_sem):
        x_, y_, c_ = lax.axis_index("x"), lax.axis_index("y"), lax.axis_index("c")
        cp = pltpu.make_async_remote_copy(
            src_ref=x_ref.at[1 - c_], dst_ref=out_ref, send_sem=send_sem, recv_sem=recv_sem,
            device_id=(x_, y_, 1 - c_), device_id_type=MESH)
        cp.start()
        cp.wait()

    return pl.pallas_call(
        body, name=name,
        in_specs=[pl.BlockSpec(memory_space=pl.ANY)], out_specs=pl.BlockSpec(memory_space=pl.ANY),
        out_shape=jax.ShapeDtypeStruct(x.shape[1:], x.dtype),
        scratch_shapes=[pltpu.SemaphoreType.DMA, pltpu.SemaphoreType.DMA],
    )(x)


def chip_exchange(x, name):
    def body(x_ref, out_ref, send_sems, recv_sems, local_sem):
        x_, y_, c_ = lax.axis_index("x"), lax.axis_index("y"), lax.axis_index("c")
        mine = 2 * x_ + y_
        local = pltpu.make_async_copy(x_ref.at[mine], out_ref.at[mine], local_sem)
        local.start()

        def copy(k):
            px = 1 - x_ if k & 2 else x_
            py = 1 - y_ if k & 1 else y_
            peer = 2 * px + py
            send = pltpu.make_async_remote_copy(
                src_ref=x_ref.at[peer], dst_ref=out_ref.at[mine], send_sem=send_sems.at[k - 1],
                recv_sem=recv_sems.at[k - 1], device_id=(px, py, c_), device_id_type=MESH)
            recv = pltpu.make_async_remote_copy(
                src_ref=x_ref.at[peer], dst_ref=out_ref.at[peer], send_sem=send_sems.at[k - 1],
                recv_sem=recv_sems.at[k - 1], device_id=(px, py, c_), device_id_type=MESH)
            return send, recv

        copies = [copy(k) for k in range(1, 4)]
        for send, _ in copies:
            send.start()
        for _, recv in copies:
            recv.wait_recv()
        for send, _ in copies:
            send.wait_send()
        local.wait()

    return pl.pallas_call(
        body, name=name,
        in_specs=[pl.BlockSpec(memory_space=pl.ANY)], out_specs=pl.BlockSpec(memory_space=pl.ANY),
        out_shape=jax.ShapeDtypeStruct(x.shape, x.dtype),
        scratch_shapes=[pltpu.SemaphoreType.DMA((3,)), pltpu.SemaphoreType.DMA((3,)),
                        pltpu.SemaphoreType.DMA],
    )(x)


def all_gather_two_level(x, name):
    def body(x_ref, out_ref, send_sems, recv_sems, local_sem):
        x_, y_, c_ = lax.axis_index("x"), lax.axis_index("y"), lax.axis_index("c")
        me, sibling = (x_, y_, c_), (x_, y_, 1 - c_)
        chips = [(1 - x_, y_), (x_, 1 - y_), (1 - x_, 1 - y_)]

        def slot(px, py, pc):
            return out_ref.at[4 * px + 2 * py + pc]

        def copy(k, block, to, src=None):
            return pltpu.make_async_remote_copy(
                src_ref=slot(*block) if src is None else src, dst_ref=slot(*block),
                send_sem=send_sems.at[k], recv_sem=recv_sems.at[k], device_id=to, device_id_type=MESH)

        mine = pltpu.make_async_copy(x_ref, slot(*me), local_sem)
        mine.start()
        first = [copy(0, me, sibling, src=x_ref)]
        first += [copy(1 + j, me, (*chip, c_), src=x_ref) for j, chip in enumerate(chips)]
        for cp in first:
            cp.start()
        passed = [copy(4 + j, (*chip, c_), sibling) for j, chip in enumerate(chips)]
        for j, chip in enumerate(chips):
            copy(1 + j, (*chip, c_), me).wait_recv()
            passed[j].start()
        copy(0, sibling, me).wait_recv()
        for j, chip in enumerate(chips):
            copy(4 + j, (*chip, 1 - c_), me).wait_recv()
        for cp in first + passed:
            cp.wait_send()
        mine.wait()

    return pl.pallas_call(
        body, name=name,
        in_specs=[pl.BlockSpec(memory_space=pl.ANY)], out_specs=pl.BlockSpec(memory_space=pl.ANY),
        out_shape=jax.ShapeDtypeStruct((N_DEV,) + tuple(x.shape), x.dtype),
        scratch_shapes=[pltpu.SemaphoreType.DMA((N_DEV - 1,)), pltpu.SemaphoreType.DMA((N_DEV - 1,)),
                        pltpu.SemaphoreType.DMA],
    )(x)


def all_to_all(x, name):
    return _exchange_call(x, False, name)


def _rope_tables(t):
    n_freq = HEAD_DIM // 4
    tok = jnp.arange(t)
    inv = ROPE_THETA ** (-jnp.arange(n_freq, dtype=F32) / n_freq)
    a_row = (tok // GRID_W).astype(F32)[:, None] * inv
    a_col = (tok % GRID_W).astype(F32)[:, None] * inv
    cos = jnp.concatenate([jnp.cos(a_row)] * 2 + [jnp.cos(a_col)] * 2, axis=1)
    sin = jnp.concatenate([-jnp.sin(a_row), jnp.sin(a_row), -jnp.sin(a_col), jnp.sin(a_col)], axis=1)
    return jnp.concatenate([cos, cos], axis=1), jnp.concatenate([sin, sin], axis=1)


def _block_diag(blocks):
    g, a, b = blocks.shape
    eye = jnp.eye(g, dtype=blocks.dtype)
    return jnp.einsum("gab,gk->gakb", blocks, eye).reshape(g * a, g * b)


def _ffn_fwd_calls(x, mod, g, w_in, w_out):
    shift, scale, gate = mod[0:1], mod[1:2], mod[2:3]
    h, = _row_call(lambda xt, gt, sh, sc: (_norm_mod_fn(xt, gt, sh, sc)[0].astype(BF16),),
                   [x], [g, shift, scale], 0, "ffn_norm")
    u = _mm(h, w_in, out_dtype=BF16, name="ffn_up")
    a, = _row_call(lambda ut: (_swiglu_fn(ut.astype(F32))[0].astype(BF16),), [u], [], 0, "ffn_act")
    y = _mm(a, w_out, name="ffn_down")
    out, = _row_call(functools.partial(_resid_fn, 0.5), [x, y], [gate], 0, "ffn_resid")
    return out, (h, u, a, y)


@jax.custom_vjp
def _ffn_half(x, mod, g, w_in, w_out):
    return _ffn_fwd_calls(x, mod, g, w_in, w_out)[0]


def _ffn_half_fwd(x, mod, g, w_in, w_out):
    out, saved = _ffn_fwd_calls(x, mod, g, w_in, w_out)
    return out, (x, mod, g, w_in, w_out, saved)


def _ffn_half_bwd(res, dxn):
    x, mod, g, w_in, w_out, (h, u, a, y) = res
    shift, scale, gate = mod[0:1], mod[1:2], mod[2:3]

    def resid_bwd(dt, yt, gt):
        return (0.5 * gt * dt).astype(BF16), 0.5 * jnp.sum(dt * yt, axis=0, keepdims=True)

    dy, dgate = _row_call(resid_bwd, [dxn, y], [gate], 1, "ffn_resid_bwd")
    da = _mm(dy, w_out, tb=True, out_dtype=BF16, name="ffn_down_dx")
    dw_out = _mm(a, dy, ta=True, out_dtype=w_out.dtype, name="ffn_down_dw")

    def act_bwd(ut, dat):
        return (jax.vjp(_swiglu_fn, ut.astype(F32))[1]((dat.astype(F32),))[0].astype(BF16),)

    du, = _row_call(act_bwd, [u, da], [], 0, "ffn_act_bwd")
    dh = _mm(du, w_in, tb=True, name="ffn_up_dx")
    dw_in = _mm(h, du, ta=True, out_dtype=w_in.dtype, name="ffn_up_dw")

    def norm_bwd(xt, dht, dt, gt, sh, sc):
        dx, dg, dsh, dsc = jax.vjp(_norm_mod_fn, xt, gt, sh, sc)[1]((dht,))
        return dx + dt, dg, dsh, dsc

    dx, dg, dshift, dscale = _row_call(norm_bwd, [x, dh, dxn], [g, shift, scale], 3, "ffn_norm_bwd")
    return dx, jnp.concatenate([dshift, dscale, dgate], axis=0), dg, dw_in, dw_out


_ffn_half.defvjp(_ffn_half_fwd, _ffn_half_bwd)


def _planes(z):
    return z.reshape(z.shape[0], SUBLANES, SSM_LANES // SUBLANES)


def _s5_discretize(a_re, a_im, log_dt, b_re, b_im):
    lam = lax.complex(a_re, a_im)
    dt = jnp.exp(log_dt)[:, None]
    lam_bar = jnp.exp(lam * dt)
    b_bar = ((lam_bar - 1.0) / lam)[..., None] * lax.complex(b_re, b_im)
    return lam_bar, b_bar


def _s5_branch(u_lat, u_ctx, w, l, with_ctx_out):
    zero = jnp.zeros((SUBLANES, SSM_LANES // SUBLANES), F32)
    lat_terms, ctx_terms = [], []
    for d, scan in enumerate((scan_up, scan_down)):
        lam_bar, b_bar = _s5_discretize(w["ssm_a_re"][l, d], w["ssm_a_im"][l, d], w["ssm_log_dt"][l, d],
                                        w["ssm_b_re"][l, d], w["ssm_b_im"][l, d])
        lre = jnp.real(lam_bar).reshape(zero.shape)
        lim = jnp.imag(lam_bar).reshape(zero.shape)
        b_t = jnp.swapaxes(b_bar, 1, 2)
        b_mat = jnp.concatenate([_block_diag(jnp.real(b_t)), _block_diag(jnp.imag(b_t))], axis=1)
        c_re = _block_diag(jnp.swapaxes(w["ssm_c_re"][l, d], 1, 2))
        c_im = _block_diag(jnp.swapaxes(w["ssm_c_im"][l, d], 1, 2))
        bu_c = linear(u_ctx, b_mat)
        sc_re, sc_im = scan(_planes(bu_c[:, :SSM_LANES]), _planes(bu_c[:, SSM_LANES:]), lre, lim, zero, zero)
        last = 0 if d == 1 else u_ctx.shape[0] - 1
        bu_l = linear(u_lat, b_mat)
        sl_re, sl_im = scan(_planes(bu_l[:, :SSM_LANES]), _planes(bu_l[:, SSM_LANES:]), lre, lim,
                            sc_re[last], sc_im[last])
        flat = lambda s: s.reshape(s.shape[0], SSM_LANES)
        lat_terms += [linear(flat(sl_re), c_re), linear(flat(sl_im), c_im)]
        if with_ctx_out:
            ctx_terms += [linear(flat(sc_re), c_re), linear(flat(sc_im), c_im)]
    d_skip = w["ssm_d"][l][None, :]

    def out(terms, u):
        y, = s5_pre(*terms, u, d_skip)
        return glu(linear(y, w["glu_w"][l]))[0]

    return out(lat_terms, u_lat), (out(ctx_terms, u_ctx) if with_ctx_out else None)


def _pool_branch(xa, w, l):
    y = linear(pool_diff(xa), _block_diag(w["pool_w"][l]))
    return scale_rows(y, w["pool_scale"][l][None, :])[0]


def _merge_branches(branches, gate_logits, w, l):
    zs = [linear_b(y, w["branch_w"][l, k]) for k, y in enumerate(branches)]
    return linear(merge(gate_logits, *zs)[0], w["out_w"][l])


def _token_mixer(h, hc, cos, sin, w, l, with_ctx_out):
    w_in, w_gate = w["w_in"][l][:, :O_GATE], w["w_in"][l][:, O_GATE:]
    p = linear(h, w_in)
    p_gate = linear_b(h, w_gate)
    pc = linear(hc, w_in if with_ctx_out else w_in[:, :CTX_COLS])
    pc_gate = linear_b(hc, w_gate) if with_ctx_out else None
    sink = w["win_sink"][l]
    q_g = jnp.tile(w["qk_norm"][l, 0], KV_HEADS * Q_PER_KV)[None, :]
    k_g = jnp.tile(w["qk_norm"][l, 1], KV_HEADS)[None, :]
    k_win_c = _to_heads(pc[:, O_KB:O_VB], KV_HEADS)
    v_win_c = _to_heads(pc[:, O_VB:O_UC], KV_HEADS)
    k_glb_c = _to_heads(head_norm(pc[:, O_KD:O_VD], k_g)[0], KV_HEADS)
    v_glb_c = _to_heads(pc[:, O_VD:CTX_COLS], KV_HEADS)
    y_a = _pool_branch(p[:, O_XA:O_GATE], w, l)
    q_win = _q_heads(rope(p[:, O_QB:O_QD], cos, sin)[0])
    k_win = _to_heads(rope(p[:, O_KB:O_VB], cos, sin)[0], KV_HEADS)
    v_win = _to_heads(p[:, O_VB:O_UC], KV_HEADS)
    y_b = _from_q_heads(attn_window(q_win, k_win_c, v_win_c, k_win, v_win, sink))
    y_c, y_c_ctx = _s5_branch(p[:, O_UC:O_KD], pc[:, O_UC:O_KD], w, l, with_ctx_out)
    q_glb = _q_heads(norm_rope(p[:, O_QD:O_XA], cos, sin, q_g)[0])
    k_glb = _to_heads(norm_rope(p[:, O_KD:O_VD], cos, sin, k_g)[0], KV_HEADS)
    v_glb = _to_heads(p[:, O_VD:CTX_COLS], KV_HEADS)
    y_d = _from_q_heads(attn_global(q_glb, k_glb_c, v_glb_c, k_glb, v_glb))
    y = _merge_branches((y_a, y_b, y_c, y_d), p_gate, w, l)
    if not with_ctx_out:
        return y, None
    y_a_c = _pool_branch(pc[:, O_XA:O_GATE], w, l)
    y_b_c = _from_q_heads(attn_ctx_sink(_q_heads(pc[:, O_QB:O_QD]), k_win_c, v_win_c, sink))
    q_glb_c = _q_heads(head_norm(pc[:, O_QD:O_XA], q_g)[0])
    y_d_c = _from_q_heads(attn_ctx(q_glb_c, k_glb_c, v_glb_c))
    return y, _merge_branches((y_a_c, y_b_c, y_c_ctx, y_d_c), pc_gate, w, l)


def local_loss(w, x, c, ctx, target):
    depth = w["w_mod"].shape[0]
    cos, sin = _rope_tables(x.shape[0])
    cond = jnp.concatenate([c, w["c_ctx"][None, :], jnp.zeros((COND_ROWS - 2, D_MODEL), F32)], axis=0)
    s_all, = silu_rows(cond)
    for l in range(depth):
        last = l == depth - 1
        m_all = (linear(s_all, w["w_mod"][l]) + w["b_mod"][l][None, :]).reshape(COND_ROWS, N_SUB, 3, D_MODEL)
        m, mc = m_all[0], m_all[1]
        g = w["norm_g"][l][:, None, :]
        x = _ffn_half(x, m[0], g[0], w["ffn_in"][l, 0], w["ffn_out"][l, 0])
        ctx = _ffn_half(ctx, mc[0], g[0], w["ffn_in"][l, 0], w["ffn_out"][l, 0])
        h, = norm_mod(x, g[1], m[1, 0:1], m[1, 1:2])
        hc, = norm_mod(ctx, g[1], mc[1, 0:1], mc[1, 1:2])
        y, y_ctx = _token_mixer(h, hc, cos, sin, w, l, not last)
        x, = resid_full(x, y, m[1, 2:3])
        if not last:
            ctx, = resid_full(ctx, y_ctx, mc[1, 2:3])
        x = _ffn_half(x, m[2], g[2], w["ffn_in"][l, 1], w["ffn_out"][l, 1])
        if not last:
            ctx = _ffn_half(ctx, mc[2], g[2], w["ffn_in"][l, 1], w["ffn_out"][l, 1])
    return jnp.sum(loss_rows(x, target, w["final_g"][None, :])[0])


PACK_COLS = 1024


def _pack(arrays):
    flat = jnp.concatenate([a.reshape(-1) for a in arrays])
    pad = (-flat.shape[0]) % (PACK_COLS * 16)
    return jnp.pad(flat, (0, pad)).reshape(-1, PACK_COLS)


def _unpack(slab, shapes):
    flat = slab.reshape(-1)
    out, off = [], 0
    for s in shapes:
        n = math.prod(s)
        out.append(flat[off:off + n].reshape(s))
        off += n
    return out


def _full_from_shards(gathered, shard_shape, axis):
    z = jnp.moveaxis(gathered.reshape((N_DEV,) + tuple(shard_shape)), 0, axis)
    shape = list(shard_shape)
    shape[axis] *= N_DEV
    return z.reshape(shape)


def _shards_from_full(full, axis):
    shape = list(full.shape)
    shape[axis:axis + 1] = [N_DEV, shape[axis] // N_DEV]
    return jnp.moveaxis(full.reshape(shape), axis, 0)


def kernel(x, c, ctx, c_ctx, w_mod, b_mod, norm_g, ffn_in, ffn_out, w_in, win_sink, qk_norm, pool_w, pool_scale, ssm_a_re, ssm_a_im, ssm_log_dt, ssm_b_re, ssm_b_im, ssm_c_re, ssm_c_im, ssm_d, glu_w, branch_w, out_w, final_g, loss_target, m_c_ctx, m_w_mod, m_b_mod, m_norm_g, m_ffn_in, m_ffn_out, m_w_in, m_win_sink, m_qk_norm, m_pool_w, m_pool_scale, m_ssm_a_re, m_ssm_a_im, m_ssm_log_dt, m_ssm_b_re, m_ssm_b_im, m_ssm_c_re, m_ssm_c_im, m_ssm_d, m_glu_w, m_branch_w, m_out_w, m_final_g, v_c_ctx, v_w_mod, v_b_mod, v_norm_g, v_ffn_in, v_ffn_out, v_w_in, v_win_sink, v_qk_norm, v_pool_w, v_pool_scale, v_ssm_a_re, v_ssm_a_im, v_ssm_log_dt, v_ssm_b_re, v_ssm_b_im, v_ssm_c_re, v_ssm_c_im, v_ssm_d, v_glu_w, v_branch_w, v_out_w, v_final_g):
    given = dict(locals())
    wts = {n: given[n] for n in WEIGHTS}
    mom = {n: given["m_" + n] for n in WEIGHTS}
    var = {n: given["v_" + n] for n in WEIGHTS}
    me = _my_slot()

    shard_shapes = [wts[n].shape for n in SHARDED]
    w_slab = _pack([wts[n] for n in SHARDED])
    gathered = all_gather_two_level(w_slab.astype(BF16), "gather_weights")
    parts = [_unpack(gathered[s], shard_shapes) for s in range(N_DEV)]
    full = dict(wts)
    for i, n in enumerate(SHARDED):
        full[n] = _full_from_shards(jnp.stack([parts[s][i] for s in range(N_DEV)]), wts[n].shape, SHARD_AXIS[n])
    g_slab = _pack([norm_g])
    g_all = all_gather(g_slab, "gather_norm_g")
    full["norm_g"] = _full_from_shards(
        jnp.stack([_unpack(g_all[s], [norm_g.shape])[0] for s in range(N_DEV)]), norm_g.shape, 2)

    loss, (gw, gx) = jax.value_and_grad(local_loss, argnums=(0, 1))(full, x[0], c, ctx[0], loss_target[0])
    loss = lax.psum(loss, ("x", "y", "c"))

    dest = [_shards_from_full(gw[n], SHARD_AXIS[n]) for n in SHARDED]
    slabs = [_pack([d[s] for d in dest]) for s in range(N_DEV)]
    send = jnp.stack([jnp.stack([slabs[2 * chip + core] for chip in range(N_DEV // 2)]) for core in range(2)])
    from_sibling = sibling_swap(send, "exchange_grads_sibling")
    own = lax.dynamic_index_in_dim(send, lax.axis_index("c"), axis=0, keepdims=False)
    pair, = _row_call(lambda a, b: ((a.astype(F32) + b.astype(F32)).astype(BF16),),
                      [own.reshape(-1, PACK_COLS), from_sibling.reshape(-1, PACK_COLS)], [], 0,
                      "exchange_pair_sum")
    big_parts = chip_exchange(pair.reshape(own.shape), "exchange_grads_chips")
    small_names = SMALL + ("norm_g",)
    small_shapes = [gw[n].shape for n in small_names]
    small_parts = all_gather(_pack([gw[n] for n in small_names]), "gather_small_grads")

    big = _adamw_call(big_parts, w_slab, _pack([mom[n] for n in SHARDED]), _pack([var[n] for n in SHARDED]),
                      "adamw_sharded")
    big = [_unpack(b, shard_shapes) for b in big]
    col = me * norm_g.shape[2]

    def small_slab(src, shard_src):
        padded = jnp.zeros((norm_g.shape[0], norm_g.shape[1], norm_g.shape[2] * N_DEV), F32)
        padded = lax.dynamic_update_slice(padded, shard_src, (0, 0, col))
        return _pack([src[n] for n in SMALL] + [padded])

    small = _adamw_call(small_parts, small_slab(wts, norm_g), small_slab(mom, m_norm_g),
                        small_slab(var, v_norm_g), "adamw_small")
    small = [_unpack(s, small_shapes) for s in small]

    outs = {}
    for kind in range(4):
        for i, n in enumerate(SHARDED):
            outs[(kind, n)] = big[kind][i]
        for i, n in enumerate(small_names):
            val = small[kind][i]
            if n == "norm_g":
                val = lax.dynamic_slice(val, (0, 0, col), norm_g.shape)
            outs[(kind, n)] = val
    return (loss, gx[None], *[outs[(k, n)] for k in range(4) for n in WEIGHTS])
```

```python
import functools
import math

import jax
import jax.numpy as jnp
from jax import lax
from jax.experimental import pallas as pl
from jax.experimental.pallas import tpu as pltpu

F32 = jnp.float32
BF16 = jnp.bfloat16

D_MODEL = 1024
GRID_W = 64
HEAD_DIM = 64
N_BRANCH = 4
BRANCH_W = D_MODEL // N_BRANCH
WINDOW = 128
ROPE_THETA = 10000.0
EPS = 1e-6
D_FF = 2816
N_SUB = 3
POOL_WINDOWS = (2, 4, 8, 16)
POOL_GROUP = BRANCH_W // len(POOL_WINDOWS)
KV_HEADS = 2
Q_PER_KV = 2
SSM_GROUP = 16
SSM_GROUPS = BRANCH_W // SSM_GROUP
SSM_STATE = 64
SSM_LANES = SSM_GROUPS * SSM_STATE
O_KB, O_VB, O_UC, O_KD, O_VD, CTX_COLS = 0, 128, 256, 512, 640, 768
O_QB, O_QD, O_XA, O_GATE = 768, 1024, 1280, 1536
IN_W = O_GATE + N_BRANCH * D_MODEL

ADAM_LR, ADAM_B1, ADAM_B2, ADAM_EPS, ADAM_WD, ADAM_STEP = 0.001, 0.9, 0.999, 1e-08, 0.01, 10

N_DEV = 8
MESH = pl.DeviceIdType.MESH

V7X_VMEM_BYTES = 64 * 1024 * 1024
SUBLANES = 8
LANES = 128
NEG_BIG = -1e30
COND_ROWS = 128

SHARDED = ("w_mod", "ffn_in", "ffn_out", "w_in", "glu_w", "branch_w", "out_w")
SHARD_AXIS = {"w_mod": 2, "ffn_in": 3, "ffn_out": 2, "w_in": 2, "glu_w": 2, "branch_w": 3, "out_w": 1}
SMALL = ("c_ctx", "b_mod", "win_sink", "qk_norm", "pool_w", "pool_scale", "ssm_a_re", "ssm_a_im",
         "ssm_log_dt", "ssm_b_re", "ssm_b_im", "ssm_c_re", "ssm_c_im", "ssm_d", "final_g")
WEIGHTS = ("c_ctx", "w_mod", "b_mod", "norm_g", "ffn_in", "ffn_out", "w_in", "win_sink", "qk_norm",
           "pool_w", "pool_scale", "ssm_a_re", "ssm_a_im", "ssm_log_dt", "ssm_b_re", "ssm_b_im",
           "ssm_c_re", "ssm_c_im", "ssm_d", "glu_w", "branch_w", "out_w", "final_g")


def _tile(n, cap, mult):
    if n <= cap:
        return n
    t = (cap // mult) * mult
    while t >= mult:
        if n % t == 0:
            return t
        t -= mult
    return n


def _cparams(sem, tile_bytes, resident_bytes=0):
    limit = int(min(V7X_VMEM_BYTES - 8 * 2 ** 20,
                    max(32 * 2 ** 20, 3 * tile_bytes + resident_bytes + 8 * 2 ** 20)))
    return pltpu.CompilerParams(dimension_semantics=sem, vmem_limit_bytes=limit)


def _mm(a, b, ta=False, tb=False, out_dtype=F32, name="mm"):
    m, k = (a.shape[1], a.shape[0]) if ta else a.shape
    n = b.shape[0] if tb else b.shape[1]
    assert (b.shape[1] if tb else b.shape[0]) == k
    tm, tn, tk = _tile(m, 1024, LANES), _tile(n, 1536, LANES), _tile(k, 1536, LANES)
    nk = k // tk
    dims = (((0 if ta else 1,), (1 if tb else 0,)), ((), ()))

    def body(a_ref, b_ref, o_ref, acc_ref):
        kk = pl.program_id(2)

        @pl.when(kk == 0)
        def _():
            acc_ref[...] = jnp.zeros_like(acc_ref)

        acc_ref[...] += lax.dot_general(a_ref[...].astype(BF16), b_ref[...].astype(BF16), dims,
                                        preferred_element_type=F32)

        @pl.when(kk == nk - 1)
        def _():
            o_ref[...] = acc_ref[...].astype(o_ref.dtype)

    a_spec = (pl.BlockSpec((tk, tm), lambda i, j, kk: (kk, i)) if ta
              else pl.BlockSpec((tm, tk), lambda i, j, kk: (i, kk)))
    b_spec = (pl.BlockSpec((tn, tk), lambda i, j, kk: (j, kk)) if tb
              else pl.BlockSpec((tk, tn), lambda i, j, kk: (kk, j)))
    tile_bytes = (a.dtype.itemsize * tm * tk + b.dtype.itemsize * tk * tn
                  + jnp.dtype(out_dtype).itemsize * tm * tn + 2 * tm * tn)
    return pl.pallas_call(
        body, name=name, grid=(m // tm, n // tn, nk),
        in_specs=[a_spec, b_spec], out_specs=pl.BlockSpec((tm, tn), lambda i, j, kk: (i, j)),
        out_shape=jax.ShapeDtypeStruct((m, n), out_dtype),
        scratch_shapes=[pltpu.VMEM((tm, tn), F32)],
        compiler_params=_cparams(("parallel", "parallel", "arbitrary"), tile_bytes),
    )(a, b)


def make_linear(out_dtype):
    @jax.custom_vjp
    def op(x, w):
        return _mm(x, w, out_dtype=out_dtype, name="linear_fwd")

    def fwd(x, w):
        return op(x, w), (x, w)

    def bwd(res, dy):
        x, w = res
        return (_mm(dy, w, tb=True, out_dtype=x.dtype, name="linear_dx"),
                _mm(x, dy, ta=True, out_dtype=w.dtype, name="linear_dw"))

    op.defvjp(fwd, bwd)
    return op


linear = make_linear(F32)
linear_b = make_linear(BF16)


def _split_rows(t):
    return _tile(t, 512, 2 * SUBLANES)


def _split_fwd(x, w, widths):
    t, k = x.shape
    n = w.shape[1]
    tm = _split_rows(t)

    def body(x_ref, w_ref, *o_refs):
        y = jnp.dot(x_ref[...].astype(BF16), w_ref[...].astype(BF16), preferred_element_type=F32)
        off = 0
        for o_ref, wd in zip(o_refs, widths):
            o_ref[...] = y[:, off:off + wd]
            off += wd

    return pl.pallas_call(
        body, name="split_linear_fwd", grid=(t // tm,),
        in_specs=[pl.BlockSpec((tm, k), lambda i: (i, 0)), pl.BlockSpec((k, n), lambda i: (0, 0))],
        out_specs=[pl.BlockSpec((tm, wd), lambda i: (i, 0)) for wd in widths],
        out_shape=[jax.ShapeDtypeStruct((t, wd), F32) for wd in widths],
        compiler_params=_cparams(("parallel",), 4 * tm * (k + 2 * n) + w.dtype.itemsize * k * n),
    )(x, w)


def _split_dx(cts, w):
    t = cts[0].shape[0]
    k, n = w.shape
    tm = _split_rows(t)

    def body(*refs):
        dy = jnp.concatenate([r[...].astype(BF16) for r in refs[:-2]], axis=1)
        refs[-1][...] = lax.dot_general(dy, refs[-2][...].astype(BF16), (((1,), (1,)), ((), ())),
                                        preferred_element_type=F32)

    return pl.pallas_call(
        body, name="split_linear_dx", grid=(t // tm,),
        in_specs=[pl.BlockSpec((tm, c.shape[1]), lambda i: (i, 0)) for c in cts]
        + [pl.BlockSpec((k, n), lambda i: (0, 0))],
        out_specs=pl.BlockSpec((tm, k), lambda i: (i, 0)), out_shape=jax.ShapeDtypeStruct((t, k), F32),
        compiler_params=_cparams(("parallel",), 4 * tm * (k + 2 * n) + w.dtype.itemsize * k * n),
    )(*cts, w)


def _split_dw(x, cts, out_dtype):
    t, k = x.shape
    n = sum(c.shape[1] for c in cts)
    tk = _split_rows(t)
    steps = t // tk

    def body(x_ref, *refs):
        o_ref, acc_ref = refs[-2], refs[-1]

        @pl.when(pl.program_id(0) == 0)
        def _():
            acc_ref[...] = jnp.zeros_like(acc_ref)

        dy = jnp.concatenate([r[...].astype(BF16) for r in refs[:-2]], axis=1)
        acc_ref[...] += lax.dot_general(x_ref[...].astype(BF16), dy, (((0,), (0,)), ((), ())),
                                        preferred_element_type=F32)

        @pl.when(pl.program_id(0) == steps - 1)
        def _():
            o_ref[...] = acc_ref[...].astype(o_ref.dtype)

    return pl.pallas_call(
        body, name="split_linear_dw", grid=(steps,),
        in_specs=[pl.BlockSpec((tk, k), lambda i: (i, 0))]
        + [pl.BlockSpec((tk, c.shape[1]), lambda i: (i, 0)) for c in cts],
        out_specs=pl.BlockSpec((k, n), lambda i: (0, 0)), out_shape=jax.ShapeDtypeStruct((k, n), out_dtype),
        scratch_shapes=[pltpu.VMEM((k, n), F32)],
        compiler_params=_cparams(("arbitrary",), 4 * tk * (k + n), 3 * 4 * k * n),
    )(x, *cts)


def make_split_linear(widths):
    @jax.custom_vjp
    def op(x, w):
        return tuple(_split_fwd(x, w, widths))

    def fwd(x, w):
        return op(x, w), (x, w)

    def bwd(res, cts):
        x, w = res
        return _split_dx(cts, w), _split_dw(x, cts, w.dtype)

    op.defvjp(fwd, bwd)
    return op


ROW_TILE_BYTES = 6 * 2 ** 20


def _row_tile(t, row_bytes):
    tm = 1024
    while tm > 2 * SUBLANES and tm * row_bytes > ROW_TILE_BYTES:
        tm //= 2
    return _tile(t, tm, 2 * SUBLANES)


def _row_call(fn, rows, params, n_reduce, name):
    t = rows[0].shape[0]
    out_avals = jax.eval_shape(fn, *rows, *params)
    n_out = len(out_avals) - n_reduce
    row_avals, red_avals = out_avals[:n_out], out_avals[n_out:]
    row_bytes = sum(r.shape[1] * r.dtype.itemsize for r in (*rows, *row_avals))
    tm = _row_tile(t, row_bytes)
    n_in = len(rows) + len(params)

    def body(*refs):
        outs = fn(*[r[...] for r in refs[:n_in]])
        o_refs = refs[n_in:]
        for o_ref, o in zip(o_refs[:n_out], outs[:n_out]):
            o_ref[...] = o.astype(o_ref.dtype)
        if n_reduce:
            @pl.when(pl.program_id(0) == 0)
            def _():
                for r in o_refs[n_out:]:
                    r[...] = jnp.zeros_like(r)

            for r, o in zip(o_refs[n_out:], outs[n_out:]):
                r[...] += o.astype(r.dtype)

    in_specs = ([pl.BlockSpec((tm, r.shape[1]), lambda i: (i, 0)) for r in rows]
                + [pl.BlockSpec(p.shape, lambda i: (0, 0)) for p in params])
    out_specs = ([pl.BlockSpec((tm, o.shape[1]), lambda i: (i, 0)) for o in row_avals]
                 + [pl.BlockSpec(o.shape, lambda i: (0, 0)) for o in red_avals])
    return pl.pallas_call(
        body, name=name, grid=(t // tm,), in_specs=in_specs, out_specs=out_specs,
        out_shape=[jax.ShapeDtypeStruct(o.shape, o.dtype) for o in out_avals],
        compiler_params=_cparams(("arbitrary",) if n_reduce else ("parallel",), tm * row_bytes),
    )(*rows, *params)


def rowwise(fn, n_rows, name):
    @jax.custom_vjp
    def op(*args):
        return tuple(_row_call(fn, args[:n_rows], args[n_rows:], 0, name + "_fwd"))

    def fwd(*args):
        return op(*args), args

    def bwd(args, cts):
        n_ct = len(cts)

        def bwd_fn(*a):
            r, ct, p = a[:n_rows], a[n_rows:n_rows + n_ct], a[n_rows + n_ct:]
            return jax.vjp(fn, *r, *p)[1](tuple(ct))

        return tuple(_row_call(bwd_fn, (*args[:n_rows], *cts), args[n_rows:], len(args) - n_rows,
                               name + "_bwd"))

    op.defvjp(fwd, bwd)
    return op


@functools.partial(jax.custom_vjp, nondiff_argnums=(1,))
def _swap_lanes(x, k):
    n = x.shape[-1]
    lane = lax.broadcasted_iota(jnp.int32, x.shape, x.ndim - 1)
    return jnp.where((lane & k) == 0, pltpu.roll(x, n - k, x.ndim - 1), pltpu.roll(x, k, x.ndim - 1))


def _swap_lanes_fwd(x, k):
    return _swap_lanes(x, k), None


def _swap_lanes_bwd(k, _, g):
    return (_swap_lanes(g, k),)


_swap_lanes.defvjp(_swap_lanes_fwd, _swap_lanes_bwd)


def _head_sum(x):
    s = x
    k = 1
    while k < HEAD_DIM:
        s = s + _swap_lanes(s, k)
        k *= 2
    return s


def _rms(x):
    return x * lax.rsqrt(jnp.mean(x * x, axis=-1, keepdims=True) + EPS)


def _norm_mod_fn(x, g, shift, scale):
    return ((_rms(x) * g) * (1.0 + scale) + shift,)


def _swiglu_fn(u):
    gate, up = u[:, :D_FF], u[:, D_FF:]
    return (jax.nn.silu(gate) * up,)


def _resid_fn(coef, x, y, gate):
    return (x + (coef * gate) * y,)


def _scale_fn(y, s):
    return (y * s,)


def _tile_lanes(tab, width):
    return tab if tab.shape[1] == width else jnp.concatenate([tab] * (width // tab.shape[1]), axis=1)


def _rope_fn(x, cos, sin):
    w = x.shape[1]
    return (x * _tile_lanes(cos, w) + _swap_lanes(x, 16) * _tile_lanes(sin, w),)


def _head_norm(x, g):
    ms = _head_sum(x * x) * (1.0 / HEAD_DIM)
    return x * lax.rsqrt(ms + EPS) * g


def _norm_rope_fn(x, cos, sin, g):
    return _rope_fn(_head_norm(x, g), cos, sin)


def _head_norm_fn(x, g):
    return (_head_norm(x, g),)


def _merge_fn(gl, z0, z1, z2, z3):
    zs = (z0, z1, z2, z3)
    terms = [jax.nn.sigmoid(gl[:, k * D_MODEL:(k + 1) * D_MODEL].astype(F32)) * zs[k].astype(F32)
             for k in range(N_BRANCH)]
    return (sum(terms[1:], terms[0]),)


def _s5_pre_fn(y0r, y0i, y1r, y1i, u, d):
    return (jax.nn.gelu(((y0r - y0i) + (y1r - y1i)) + d * u),)


def _glu_fn(z):
    return (z[:, :BRANCH_W] * jax.nn.sigmoid(z[:, BRANCH_W:]),)


def _silu_fn(x):
    return (jax.nn.silu(x),)


def _loss_fn(x, tgt, g):
    err = jnp.square(_rms(x) * g - tgt)
    return (0.5 * jnp.mean(err, axis=-1, keepdims=True),)


norm_mod = rowwise(_norm_mod_fn, 1, "norm_mod")
swiglu_act = rowwise(_swiglu_fn, 1, "swiglu")
resid_half = rowwise(functools.partial(_resid_fn, 0.5), 2, "resid_half")
resid_full = rowwise(functools.partial(_resid_fn, 1.0), 2, "resid_full")
scale_rows = rowwise(_scale_fn, 1, "pool_scale")
rope = rowwise(_rope_fn, 3, "rope")
norm_rope = rowwise(_norm_rope_fn, 3, "norm_rope")
head_norm = rowwise(_head_norm_fn, 1, "head_norm")
merge = rowwise(_merge_fn, 5, "merge")
s5_pre = rowwise(_s5_pre_fn, 5, "s5_pre")
glu = rowwise(_glu_fn, 1, "glu")
silu_rows = rowwise(_silu_fn, 1, "silu")
loss_rows = rowwise(_loss_fn, 2, "loss_head")


POOL_HALO = 16


def _pool_call(xa, adjoint, name):
    n, width = xa.shape
    tm = _tile(n, 512, POOL_HALO)
    halo_blocks = tm // POOL_HALO
    last_halo = n // POOL_HALO - 1
    ext_rows = tm + 2 * POOL_HALO

    def body(prev_ref, cur_ref, next_ref, o_ref, ext_ref):
        i = pl.program_id(0)
        ext_ref[0:POOL_HALO] = prev_ref[...]
        ext_ref[POOL_HALO:POOL_HALO + tm] = cur_ref[...]
        ext_ref[POOL_HALO + tm:ext_rows] = next_ref[...]
        e = ext_ref[...]
        row = lax.broadcasted_iota(jnp.int32, e.shape, 0) + (i * tm - POOL_HALO)
        grp = lax.broadcasted_iota(jnp.int32, e.shape, 1) // POOL_GROUP
        win = jnp.where(grp == 0, POOL_WINDOWS[0],
                        jnp.where(grp == 1, POOL_WINDOWS[1], jnp.where(grp == 2, POOL_WINDOWS[2], POOL_WINDOWS[3])))
        valid = (row >= 0) & (row < n)
        lo = jnp.clip(row - win // 2, 0, n)
        hi = jnp.clip(row - win // 2 + win, 0, n)
        cnt = jnp.maximum((hi - lo).astype(F32), 1.0)
        e0 = jnp.where(valid, e / cnt if adjoint else e, 0.0)

        def shift(z, s):
            return pltpu.roll(z, s % ext_rows, 0)

        s2 = e0 + shift(e0, -1 if adjoint else 1)
        s4 = shift(s2, 1) + shift(s2, -1)
        s8 = shift(s4, 2) + shift(s4, -2)
        s16 = shift(s8, 4) + shift(s8, -4)
        s = jnp.where(grp == 0, s2, jnp.where(grp == 1, s4, jnp.where(grp == 2, s8, s16)))
        out = (s - e) if adjoint else (s / cnt - e)
        o_ref[...] = out[POOL_HALO:POOL_HALO + tm]

    return pl.pallas_call(
        body, name=name, grid=(n // tm,),
        in_specs=[pl.BlockSpec((POOL_HALO, width), lambda i: (jnp.maximum(i * halo_blocks - 1, 0), 0)),
                  pl.BlockSpec((tm, width), lambda i: (i, 0)),
                  pl.BlockSpec((POOL_HALO, width), lambda i: (jnp.minimum((i + 1) * halo_blocks, last_halo), 0))],
        out_specs=pl.BlockSpec((tm, width), lambda i: (i, 0)),
        out_shape=jax.ShapeDtypeStruct((n, width), F32),
        scratch_shapes=[pltpu.VMEM((ext_rows, width), F32)],
        compiler_params=_cparams(("parallel",), 4 * 4 * ext_rows * width),
    )(xa, xa, xa)


@jax.custom_vjp
def pool_diff(xa):
    return _pool_call(xa, False, "pool_fwd")


pool_diff.defvjp(lambda xa: (pool_diff(xa), None), lambda _, g: (_pool_call(g, True, "pool_bwd"),))


ATT_SCALE = HEAD_DIM ** -0.5


def _attn_blocks(t, band):
    if band:
        b = _tile(t, 256, LANES)
        return b, b
    return _tile(t, 512, LANES), _tile(t, 1024, LANES)


def _qk_scores(q, k):
    return lax.dot_general((q * ATT_SCALE).astype(BF16), k.astype(BF16), (((1,), (1,)), ((), ())),
                           preferred_element_type=F32)


def _band_mask(qi, kblk, bq, bk, n_blocks, r0, rows):
    qpos = qi * bq + (r0 + lax.broadcasted_iota(jnp.int32, (rows, bk), 0)) % bq
    kpos = kblk * bk + lax.broadcasted_iota(jnp.int32, (rows, bk), 1)
    return (kblk >= 0) & (kblk < n_blocks) & (jnp.abs(kpos - qpos) <= WINDOW)


def _sink_rows(sink_ref, h, bq):
    r = lax.broadcasted_iota(jnp.int32, (Q_PER_KV * bq, 1), 0)
    return jnp.where(r < bq, sink_ref[h * Q_PER_KV], sink_ref[h * Q_PER_KV + 1])


def _lat_index(band, nb):
    if band:
        return lambda h, i, j: (h, jnp.clip(i - 1 + j, 0, nb - 1), 0)
    return lambda h, i, j: (h, j, 0)


ATT_SUB_ROWS = 256


def _flash_fwd(q, klt, vl, kct, vc, sink, band):
    kvh, g, t, dh = q.shape
    c = kct.shape[2]
    has_lat = klt is not None
    has_sink = sink is not None
    bq, bk = _attn_blocks(t, band)
    if not band:
        bk = _tile(t, 4096, LANES)
    nb = t // bk
    nkv = (3 if band else nb) if has_lat else 1
    rows = g * bq
    sub = min(ATT_SUB_ROWS, rows)

    def body(*refs):
        it = iter(refs)
        q_ref, kc_ref, vc_ref = next(it), next(it), next(it)
        kl_ref, vl_ref = (next(it), next(it)) if has_lat else (None, None)
        sink_ref = next(it) if has_sink else None
        o_ref, lse_ref, m_ref, l_ref, acc_ref = next(it), next(it), next(it), next(it), next(it)
        h, qi, kj = pl.program_id(0), pl.program_id(1), pl.program_id(2)
        qv = (q_ref[0].reshape(rows, dh) * ATT_SCALE).astype(BF16)

        def part(kt_ref, v_ref, masked):
            kt, v = kt_ref[0].astype(BF16), v_ref[0].astype(BF16)
            m_all, l_all, acc_all = m_ref[...], l_ref[...], acc_ref[...]
            m_out, l_out, acc_out = [], [], []
            for r0 in range(0, rows, sub):
                s = jnp.dot(qv[r0:r0 + sub], kt, preferred_element_type=F32)
                if masked:
                    s = jnp.where(_band_mask(qi, qi - 1 + kj, bq, bk, nb, r0, sub), s, NEG_BIG)
                m_old = m_all[r0:r0 + sub]
                m_new = jnp.maximum(m_old, jnp.max(s, axis=-1, keepdims=True))
                p = jnp.exp(s - m_new)
                alpha = jnp.exp(m_old - m_new)
                m_out.append(m_new)
                l_out.append(alpha * l_all[r0:r0 + sub] + jnp.sum(p, axis=-1, keepdims=True))
                acc_out.append(alpha * acc_all[r0:r0 + sub]
                               + jnp.dot(p.astype(BF16), v, preferred_element_type=F32))
            m_ref[...] = jnp.concatenate(m_out, axis=0)
            l_ref[...] = jnp.concatenate(l_out, axis=0)
            acc_ref[...] = jnp.concatenate(acc_out, axis=0)

        @pl.when(kj == 0)
        def _():
            if has_sink:
                m_ref[...] = _sink_rows(sink_ref, h, bq)
                l_ref[...] = jnp.ones_like(l_ref)
            else:
                m_ref[...] = jnp.full_like(m_ref, NEG_BIG)
                l_ref[...] = jnp.zeros_like(l_ref)
            acc_ref[...] = jnp.zeros_like(acc_ref)
            part(kc_ref, vc_ref, False)

        if has_lat:
            part(kl_ref, vl_ref, band)

        @pl.when(kj == nkv - 1)
        def _():
            o_ref[0] = (acc_ref[...] / l_ref[...]).reshape(g, bq, dh)
            lse_ref[0] = (m_ref[...] + jnp.log(l_ref[...])).reshape(g, bq, 1)

    q_spec = pl.BlockSpec((1, g, bq, dh), lambda h, i, j: (h, 0, i, 0))
    r_spec = pl.BlockSpec((1, g, bq, 1), lambda h, i, j: (h, 0, i, 0))
    c_spec = pl.BlockSpec((1, c, dh), lambda h, i, j: (h, 0, 0))
    ct_spec = pl.BlockSpec((1, dh, c), lambda h, i, j: (h, 0, 0))
    in_specs, args = [q_spec, ct_spec, c_spec], [q, kct, vc]
    if has_lat:
        lat = _lat_index(band, nb)
        in_specs += [pl.BlockSpec((1, dh, bk), lambda h, i, j: (h, 0, lat(h, i, j)[1])),
                     pl.BlockSpec((1, bk, dh), lat)]
        args += [klt, vl]
    if has_sink:
        in_specs.append(pl.BlockSpec(memory_space=pltpu.SMEM))
        args.append(sink)
    return pl.pallas_call(
        body, name="attn_fwd_band" if band else "attn_fwd", grid=(kvh, t // bq, nkv),
        in_specs=in_specs, out_specs=[q_spec, r_spec],
        out_shape=[jax.ShapeDtypeStruct(q.shape, F32), jax.ShapeDtypeStruct((kvh, g, t, 1), F32)],
        scratch_shapes=[pltpu.VMEM((rows, 1), F32), pltpu.VMEM((rows, 1), F32), pltpu.VMEM((rows, dh), F32)],
        compiler_params=_cparams(("parallel", "parallel", "arbitrary"), 4 * 4 * rows * max(bk, c)),
    )(*args)


def _flash_dq(q, kl, vl, kc, vc, sink, o, do, lse, band):
    kvh, g, t, dh = q.shape
    c = kc.shape[1]
    has_lat = kl is not None
    has_sink = sink is not None
    bq, bk = _attn_blocks(t, band)
    nb = t // bk
    nkv = (3 if band else nb) if has_lat else 1
    rows = g * bq

    def body(*refs):
        it = iter(refs)
        q_ref, o_ref, do_ref, lse_ref, kc_ref, vc_ref = (next(it) for _ in range(6))
        kl_ref, vl_ref = (next(it), next(it)) if has_lat else (None, None)
        sink_ref = next(it) if has_sink else None
        dq_ref, delta_ref, dsink_ref, acc_ref, dl_ref = (next(it) for _ in range(5))
        h, qi, kj = pl.program_id(0), pl.program_id(1), pl.program_id(2)
        qv = q_ref[0].reshape(rows, dh)
        dov = do_ref[0].reshape(rows, dh)
        lse_v = lse_ref[0].reshape(rows, 1)

        def update(s, k, v):
            p = jnp.exp(s - lse_v)
            dp = lax.dot_general(dov.astype(BF16), v.astype(BF16), (((1,), (1,)), ((), ())),
                                 preferred_element_type=F32)
            ds = p * (dp - dl_ref[...])
            acc_ref[...] += jnp.dot(ds.astype(BF16), k.astype(BF16), preferred_element_type=F32)

        @pl.when(kj == 0)
        def _():
            dl_ref[...] = jnp.sum(o_ref[0].reshape(rows, dh) * dov, axis=-1, keepdims=True)
            acc_ref[...] = jnp.zeros_like(acc_ref)
            update(_qk_scores(qv, kc_ref[0]), kc_ref[0], vc_ref[0])

        if has_lat:
            s = _qk_scores(qv, kl_ref[0])
            if band:
                s = jnp.where(_band_mask(qi, qi - 1 + kj, bq, bk, nb, 0, rows), s, NEG_BIG)
            update(s, kl_ref[0], vl_ref[0])

        @pl.when(kj == nkv - 1)
        def _():
            dq_ref[0] = (acc_ref[...] * ATT_SCALE).reshape(g, bq, dh)
            delta_ref[0] = dl_ref[...].reshape(g, bq, 1)
            if has_sink:
                p_sink = jnp.exp(_sink_rows(sink_ref, h, bq) - lse_v)
                dsink_ref[0] = (-p_sink * dl_ref[...]).reshape(g, bq, 1)
            else:
                dsink_ref[0] = jnp.zeros((g, bq, 1), F32)

    q_spec = pl.BlockSpec((1, g, bq, dh), lambda h, i, j: (h, 0, i, 0))
    r_spec = pl.BlockSpec((1, g, bq, 1), lambda h, i, j: (h, 0, i, 0))
    c_spec = pl.BlockSpec((1, c, dh), lambda h, i, j: (h, 0, 0))
    in_specs, args = [q_spec, q_spec, q_spec, r_spec, c_spec, c_spec], [q, o, do, lse, kc, vc]
    if has_lat:
        l_spec = pl.BlockSpec((1, bk, dh), _lat_index(band, nb))
        in_specs += [l_spec, l_spec]
        args += [kl, vl]
    if has_sink:
        in_specs.append(pl.BlockSpec(memory_space=pltpu.SMEM))
        args.append(sink)
    row_shape = jax.ShapeDtypeStruct((kvh, g, t, 1), F32)
    return pl.pallas_call(
        body, name="attn_dq_band" if band else "attn_dq", grid=(kvh, t // bq, nkv),
        in_specs=in_specs, out_specs=[q_spec, r_spec, r_spec],
        out_shape=[jax.ShapeDtypeStruct(q.shape, F32), row_shape, row_shape],
        scratch_shapes=[pltpu.VMEM((rows, dh), F32), pltpu.VMEM((rows, 1), F32)],
        compiler_params=_cparams(("parallel", "parallel", "arbitrary"), 4 * 6 * rows * max(bk, c)),
    )(*args)


def _flash_dkv(q, do, lse, delta, k, v, band):
    kvh, g, t, dh = q.shape
    nk_rows = k.shape[1]
    if band:
        bq, bk = _attn_blocks(t, True)
    else:
        bq = _tile(t, 512, LANES)
        bk = _tile(nk_rows, 1024, LANES)
    nbq = t // bq
    nq = 3 if band else nbq
    rows = g * bq

    def body(q_ref, do_ref, lse_ref, delta_ref, k_ref, v_ref, dk_ref, dv_ref, dk_acc, dv_acc):
        ki, qj = pl.program_id(1), pl.program_id(2)
        qv = q_ref[0].reshape(rows, dh)
        dov = do_ref[0].reshape(rows, dh)

        @pl.when(qj == 0)
        def _():
            dk_acc[...] = jnp.zeros_like(dk_acc)
            dv_acc[...] = jnp.zeros_like(dv_acc)

        s = _qk_scores(qv, k_ref[0])
        if band:
            qblk = ki - 1 + qj
            qpos = qblk * bq + lax.broadcasted_iota(jnp.int32, (rows, bk), 0) % bq
            kpos = ki * bk + lax.broadcasted_iota(jnp.int32, (rows, bk), 1)
            s = jnp.where((qblk >= 0) & (qblk < nbq) & (jnp.abs(kpos - qpos) <= WINDOW), s, NEG_BIG)
        p = jnp.exp(s - lse_ref[0].reshape(rows, 1))
        dp = lax.dot_general(dov.astype(BF16), v_ref[0].astype(BF16), (((1,), (1,)), ((), ())),
                             preferred_element_type=F32)
        ds = p * (dp - delta_ref[0].reshape(rows, 1))
        tn = (((0,), (0,)), ((), ()))
        dv_acc[...] += lax.dot_general(p.astype(BF16), dov.astype(BF16), tn, preferred_element_type=F32)
        dk_acc[...] += lax.dot_general(ds.astype(BF16), qv.astype(BF16), tn, preferred_element_type=F32)

        @pl.when(qj == nq - 1)
        def _():
            dk_ref[0] = dk_acc[...] * ATT_SCALE
            dv_ref[0] = dv_acc[...]

    if band:
        q_index = lambda h, i, j: (h, 0, jnp.clip(i - 1 + j, 0, nbq - 1), 0)
    else:
        q_index = lambda h, i, j: (h, 0, j, 0)
    q_spec = pl.BlockSpec((1, g, bq, dh), q_index)
    r_spec = pl.BlockSpec((1, g, bq, 1), q_index)
    k_spec = pl.BlockSpec((1, bk, dh), lambda h, i, j: (h, i, 0))
    return pl.pallas_call(
        body, name="attn_dkv_band" if band else "attn_dkv", grid=(kvh, nk_rows // bk, nq),
        in_specs=[q_spec, q_spec, r_spec, r_spec, k_spec, k_spec], out_specs=[k_spec, k_spec],
        out_shape=[jax.ShapeDtypeStruct(k.shape, F32), jax.ShapeDtypeStruct(k.shape, F32)],
        scratch_shapes=[pltpu.VMEM((bk, dh), F32), pltpu.VMEM((bk, dh), F32)],
        compiler_params=_cparams(("parallel", "parallel", "arbitrary"), 4 * 6 * rows * bk),
    )(q, do, lse, delta, k, v)


def _flash_bwd_full(q, kl, vl, kc, vc, o, do, lse):
    kvh, g, t, dh = q.shape
    c = kc.shape[1]
    bq, bk = _tile(t, 512, LANES), _tile(t, 1024, LANES)
    nq, nkv = t // bq, t // bk
    rows = g * bq
    klt, vlt = jnp.swapaxes(kl, 1, 2), jnp.swapaxes(vl, 1, 2)
    kct, vct = jnp.swapaxes(kc, 1, 2), jnp.swapaxes(vc, 1, 2)
    tn = (((0,), (0,)), ((), ()))

    def body(q_ref, o_ref, do_ref, lse_ref, kc_ref, kct_ref, vct_ref, kl_ref, klt_ref, vlt_ref,
             dq_ref, delta_ref, dk_hbm, dv_hbm, dq_acc, dl_ref, dk_acc, dv_acc):
        h, qi, kj = pl.program_id(0), pl.program_id(1), pl.program_id(2)
        q_raw = q_ref[0].reshape(rows, dh).astype(BF16)
        qv = (q_ref[0].reshape(rows, dh) * ATT_SCALE).astype(BF16)
        dov = do_ref[0].reshape(rows, dh).astype(BF16)
        lse_v = lse_ref[0].reshape(rows, 1)

        def tile(kt, vt):
            s = jnp.dot(qv, kt.astype(BF16), preferred_element_type=F32)
            p = jnp.exp(s - lse_v)
            dp = jnp.dot(dov, vt.astype(BF16), preferred_element_type=F32)
            return p, (p * (dp - dl_ref[...])).astype(BF16)

        @pl.when((qi == 0) & (kj == 0))
        def _():
            dk_acc[...] = jnp.zeros_like(dk_acc)
            dv_acc[...] = jnp.zeros_like(dv_acc)

        @pl.when(kj == 0)
        def _():
            dl_ref[...] = jnp.sum(o_ref[0].reshape(rows, dh) * do_ref[0].reshape(rows, dh),
                                  axis=-1, keepdims=True)
            _, ds = tile(kct_ref[0], vct_ref[0])
            dq_acc[...] = jnp.dot(ds, kc_ref[0].astype(BF16), preferred_element_type=F32)

        p, ds = tile(klt_ref[0], vlt_ref[0])
        dq_acc[...] += jnp.dot(ds, kl_ref[0].astype(BF16), preferred_element_type=F32)
        ks = pl.ds(pl.multiple_of(kj * bk, bk), bk)
        dv_acc[ks] += lax.dot_general(p.astype(BF16), dov, tn, preferred_element_type=F32)
        dk_acc[ks] += lax.dot_general(ds, q_raw, tn, preferred_element_type=F32)

        @pl.when(kj == nkv - 1)
        def _():
            dq_ref[0] = (dq_acc[...] * ATT_SCALE).reshape(g, bq, dh)
            delta_ref[0] = dl_ref[...].reshape(g, bq, 1)

        @pl.when((qi == nq - 1) & (kj == nkv - 1))
        def _():
            dk_acc[...] = dk_acc[...] * ATT_SCALE
            pltpu.sync_copy(dk_acc, dk_hbm.at[h])
            pltpu.sync_copy(dv_acc, dv_hbm.at[h])

    q_spec = pl.BlockSpec((1, g, bq, dh), lambda h, i, j: (h, 0, i, 0))
    r_spec = pl.BlockSpec((1, g, bq, 1), lambda h, i, j: (h, 0, i, 0))
    c_spec = pl.BlockSpec((1, c, dh), lambda h, i, j: (h, 0, 0))
    ct_spec = pl.BlockSpec((1, dh, c), lambda h, i, j: (h, 0, 0))
    l_spec = pl.BlockSpec((1, bk, dh), lambda h, i, j: (h, j, 0))
    lt_spec = pl.BlockSpec((1, dh, bk), lambda h, i, j: (h, 0, j))
    any_spec = pl.BlockSpec(memory_space=pl.ANY)
    kv_shape = jax.ShapeDtypeStruct((kvh, t, dh), F32)
    return pl.pallas_call(
        body, name="attn_bwd_full", grid=(kvh, nq, nkv),
        in_specs=[q_spec, q_spec, q_spec, r_spec, c_spec, ct_spec, ct_spec, l_spec, lt_spec, lt_spec],
        out_specs=[q_spec, r_spec, any_spec, any_spec],
        out_shape=[jax.ShapeDtypeStruct(q.shape, F32), jax.ShapeDtypeStruct((kvh, g, t, 1), F32),
                   kv_shape, kv_shape],
        scratch_shapes=[pltpu.VMEM((rows, dh), F32), pltpu.VMEM((rows, 1), F32),
                        pltpu.VMEM((t, dh), F32), pltpu.VMEM((t, dh), F32)],
        compiler_params=_cparams(("arbitrary", "arbitrary", "arbitrary"), 4 * 2 * rows * bk,
                                 2 * 4 * t * LANES),
    )(q, o, do, lse, kc, kct, vct, kl, klt, vlt)


BAND_BQ = 512


def _band_specs(t, shape_of):
    per = BAND_BQ // WINDOW
    n_halo = t // WINDOW

    def spec(n, index):
        shape, axis = shape_of(n)

        def index_map(h, i):
            idx = [h] + [0] * (len(shape) - 1)
            idx[axis] = index(i)
            return tuple(idx)

        return pl.BlockSpec(shape, index_map)

    return [spec(WINDOW, lambda i: jnp.maximum(i * per - 1, 0)),
            spec(BAND_BQ, lambda i: i),
            spec(WINDOW, lambda i: jnp.minimum((i + 1) * per, n_halo - 1))]


def _band_visible(i, t, c, rows):
    cols = c + BAND_BQ + 2 * WINDOW
    col = lax.broadcasted_iota(jnp.int32, (rows, cols), 1)
    qpos = i * BAND_BQ + lax.broadcasted_iota(jnp.int32, (rows, cols), 0) % BAND_BQ
    kpos = i * BAND_BQ - WINDOW + (col - c)
    return (col < c) | ((kpos >= 0) & (kpos < t) & (jnp.abs(kpos - qpos) <= WINDOW))


def _band_fwd(q, klt, vl, kct, vc, sink):
    kvh, g, t, dh = q.shape
    c = kct.shape[2]
    rows = g * BAND_BQ

    def body(q_ref, kct_ref, vc_ref, ktp, ktc, ktn, vp, vcur, vn, sink_ref, o_ref, lse_ref):
        h, i = pl.program_id(0), pl.program_id(1)
        qv = (q_ref[0].reshape(rows, dh) * ATT_SCALE).astype(BF16)
        kt = jnp.concatenate([kct_ref[0], ktp[0], ktc[0], ktn[0]], axis=1).astype(BF16)
        v = jnp.concatenate([vc_ref[0], vp[0], vcur[0], vn[0]], axis=0).astype(BF16)
        s = jnp.where(_band_visible(i, t, c, rows), jnp.dot(qv, kt, preferred_element_type=F32), NEG_BIG)
        sink_r = _sink_rows(sink_ref, h, BAND_BQ)
        m = jnp.maximum(sink_r, jnp.max(s, axis=-1, keepdims=True))
        p = jnp.exp(s - m)
        l = jnp.exp(sink_r - m) + jnp.sum(p, axis=-1, keepdims=True)
        o_ref[0] = (jnp.dot(p.astype(BF16), v, preferred_element_type=F32) / l).reshape(g, BAND_BQ, dh)
        lse_ref[0] = (m + jnp.log(l)).reshape(g, BAND_BQ, 1)

    q_spec = pl.BlockSpec((1, g, BAND_BQ, dh), lambda h, i: (h, 0, i, 0))
    r_spec = pl.BlockSpec((1, g, BAND_BQ, 1), lambda h, i: (h, 0, i, 0))
    in_specs = ([q_spec, pl.BlockSpec((1, dh, c), lambda h, i: (h, 0, 0)),
                 pl.BlockSpec((1, c, dh), lambda h, i: (h, 0, 0))]
                + _band_specs(t,lambda n: ((1, dh, n), 2))
                + _band_specs(t,lambda n: ((1, n, dh), 1))
                + [pl.BlockSpec(memory_space=pltpu.SMEM)])
    return pl.pallas_call(
        body, name="attn_band_fwd", grid=(kvh, t // BAND_BQ), in_specs=in_specs, out_specs=[q_spec, r_spec],
        out_shape=[jax.ShapeDtypeStruct(q.shape, F32), jax.ShapeDtypeStruct((kvh, g, t, 1), F32)],
        compiler_params=_cparams(("parallel", "parallel"), 4 * 2 * rows * (c + BAND_BQ + 2 * WINDOW)),
    )(q, kct, vc, klt, klt, klt, vl, vl, vl, sink)


def _band_dq(q, kl, klt, vlt, kc, kct, vct, sink, o, do, lse):
    kvh, g, t, dh = q.shape
    c = kc.shape[1]
    rows = g * BAND_BQ

    def body(q_ref, o_ref, do_ref, lse_ref, kc_ref, kct_ref, vct_ref, kp, kcur, kn, ktp, ktc, ktn,
             vtp, vtc, vtn, sink_ref, dq_ref, delta_ref, dsink_ref):
        h, i = pl.program_id(0), pl.program_id(1)
        qv = (q_ref[0].reshape(rows, dh) * ATT_SCALE).astype(BF16)
        dov = do_ref[0].reshape(rows, dh)
        lse_v = lse_ref[0].reshape(rows, 1)
        kt = jnp.concatenate([kct_ref[0], ktp[0], ktc[0], ktn[0]], axis=1).astype(BF16)
        vt = jnp.concatenate([vct_ref[0], vtp[0], vtc[0], vtn[0]], axis=1).astype(BF16)
        k = jnp.concatenate([kc_ref[0], kp[0], kcur[0], kn[0]], axis=0).astype(BF16)
        s = jnp.where(_band_visible(i, t, c, rows), jnp.dot(qv, kt, preferred_element_type=F32), NEG_BIG)
        p = jnp.exp(s - lse_v)
        delta = jnp.sum(o_ref[0].reshape(rows, dh) * dov, axis=-1, keepdims=True)
        dp = jnp.dot(dov.astype(BF16), vt, preferred_element_type=F32)
        ds = (p * (dp - delta)).astype(BF16)
        dq_ref[0] = (jnp.dot(ds, k, preferred_element_type=F32) * ATT_SCALE).reshape(g, BAND_BQ, dh)
        delta_ref[0] = delta.reshape(g, BAND_BQ, 1)
        p_sink = jnp.exp(_sink_rows(sink_ref, h, BAND_BQ) - lse_v)
        dsink_ref[0] = (-p_sink * delta).reshape(g, BAND_BQ, 1)

    q_spec = pl.BlockSpec((1, g, BAND_BQ, dh), lambda h, i: (h, 0, i, 0))
    r_spec = pl.BlockSpec((1, g, BAND_BQ, 1), lambda h, i: (h, 0, i, 0))
    ct_spec = pl.BlockSpec((1, dh, c), lambda h, i: (h, 0, 0))
    rows_of = lambda n: ((1, n, dh), 1)
    lanes_of = lambda n: ((1, dh, n), 2)
    in_specs = ([q_spec, q_spec, q_spec, r_spec, pl.BlockSpec((1, c, dh), lambda h, i: (h, 0, 0)), ct_spec, ct_spec]
                + _band_specs(t,rows_of) + _band_specs(t,lanes_of) + _band_specs(t,lanes_of)
                + [pl.BlockSpec(memory_space=pltpu.SMEM)])
    row_shape = jax.ShapeDtypeStruct((kvh, g, t, 1), F32)
    return pl.pallas_call(
        body, name="attn_band_dq", grid=(kvh, t // BAND_BQ), in_specs=in_specs,
        out_specs=[q_spec, r_spec, r_spec], out_shape=[jax.ShapeDtypeStruct(q.shape, F32), row_shape, row_shape],
        compiler_params=_cparams(("parallel", "parallel"), 4 * 3 * rows * (c + BAND_BQ + 2 * WINDOW)),
    )(q, o, do, lse, kc, kct, vct, kl, kl, kl, klt, klt, klt, vlt, vlt, vlt, sink)


def _band_dkv(q, do, lse, delta, klt, vlt):
    kvh, g, t, dh = q.shape
    span = BAND_BQ + 2 * WINDOW
    rows = g * span
    tn = (((0,), (0,)), ((), ()))

    def body(qp, qc, qn, dop, doc, don, lp, lc, ln, dp_, dc_, dn_, kt_ref, vt_ref, dk_ref, dv_ref):
        j = pl.program_id(1)

        def stack(a, b, c_):
            return jnp.concatenate([jnp.concatenate([a[0, gi], b[0, gi], c_[0, gi]], axis=0)
                                    for gi in range(g)], axis=0)

        q_all, do_all = stack(qp, qc, qn), stack(dop, doc, don).astype(BF16)
        lse_all, delta_all = stack(lp, lc, ln), stack(dp_, dc_, dn_)
        s = jnp.dot((q_all * ATT_SCALE).astype(BF16), kt_ref[0].astype(BF16), preferred_element_type=F32)
        qpos = j * BAND_BQ - WINDOW + lax.broadcasted_iota(jnp.int32, (rows, BAND_BQ), 0) % span
        kpos = j * BAND_BQ + lax.broadcasted_iota(jnp.int32, (rows, BAND_BQ), 1)
        s = jnp.where((qpos >= 0) & (qpos < t) & (jnp.abs(kpos - qpos) <= WINDOW), s, NEG_BIG)
        p = jnp.exp(s - lse_all)
        dp = jnp.dot(do_all, vt_ref[0].astype(BF16), preferred_element_type=F32)
        ds = (p * (dp - delta_all)).astype(BF16)
        dv_ref[0] = lax.dot_general(p.astype(BF16), do_all, tn, preferred_element_type=F32)
        dk_ref[0] = lax.dot_general(ds, q_all.astype(BF16), tn, preferred_element_type=F32) * ATT_SCALE

    q_specs = _band_specs(t,lambda n: ((1, g, n, dh), 2))
    r_specs = _band_specs(t,lambda n: ((1, g, n, 1), 2))
    kt_spec = pl.BlockSpec((1, dh, BAND_BQ), lambda h, j: (h, 0, j))
    k_spec = pl.BlockSpec((1, BAND_BQ, dh), lambda h, j: (h, j, 0))
    kv_shape = jax.ShapeDtypeStruct((kvh, t, dh), F32)
    return pl.pallas_call(
        body, name="attn_band_dkv", grid=(kvh, t // BAND_BQ),
        in_specs=q_specs + q_specs + r_specs + r_specs + [kt_spec, kt_spec], out_specs=[k_spec, k_spec],
        out_shape=[kv_shape, kv_shape],
        compiler_params=_cparams(("parallel", "parallel"), 4 * 3 * rows * BAND_BQ),
    )(q, q, q, do, do, do, lse, lse, lse, delta, delta, delta, klt, vlt)


def make_attention(band, has_lat, has_sink):
    def unpack(args):
        it = iter(args)
        q, kc, vc = next(it), next(it), next(it)
        kl, vl = (next(it), next(it)) if has_lat else (None, None)
        sink = next(it) if has_sink else None
        return q, kl, vl, kc, vc, sink

    def forward(args):
        q, kl, vl, kc, vc, sink = unpack(args)
        klt = jnp.swapaxes(kl, 1, 2) if has_lat else None
        if band:
            return _band_fwd(q, klt, vl, jnp.swapaxes(kc, 1, 2), vc, sink)
        return _flash_fwd(q, klt, vl, jnp.swapaxes(kc, 1, 2), vc, sink, band)

    @jax.custom_vjp
    def op(*args):
        return forward(args)[0]

    def fwd(*args):
        o, lse = forward(args)
        return o, (args, o, lse)

    def bwd(res, do):
        args, o, lse = res
        q, kl, vl, kc, vc, sink = unpack(args)
        if has_lat and not band and not has_sink:
            dq, delta, dkl, dvl = _flash_bwd_full(q, kl, vl, kc, vc, o, do, lse)
            dkc, dvc = _flash_dkv(q, do, lse, delta, kc, vc, False)
            return dq, dkc, dvc, dkl, dvl
        if band:
            klt, vlt = jnp.swapaxes(kl, 1, 2), jnp.swapaxes(vl, 1, 2)
            dq, delta, dsink_rows = _band_dq(q, kl, klt, vlt, kc, jnp.swapaxes(kc, 1, 2),
                                             jnp.swapaxes(vc, 1, 2), sink, o, do, lse)
            dkc, dvc = _flash_dkv(q, do, lse, delta, kc, vc, False)
            dkl, dvl = _band_dkv(q, do, lse, delta, klt, vlt)
            return dq, dkc, dvc, dkl, dvl, jnp.sum(dsink_rows, axis=(2, 3)).reshape(-1)
        dq, delta, dsink_rows = _flash_dq(q, kl, vl, kc, vc, sink, o, do, lse, band)
        dkc, dvc = _flash_dkv(q, do, lse, delta, kc, vc, False)
        grads = [dq, dkc, dvc]
        if has_lat:
            grads += list(_flash_dkv(q, do, lse, delta, kl, vl, band))
        if has_sink:
            grads.append(jnp.sum(dsink_rows, axis=(2, 3)).reshape(-1))
        return tuple(grads)

    op.defvjp(fwd, bwd)
    return op


attn_window = make_attention(True, True, True)
attn_global = make_attention(False, True, False)
attn_ctx_sink = make_attention(False, False, True)
attn_ctx = make_attention(False, False, False)


def _to_heads(z, n_heads):
    return z.reshape(z.shape[0], n_heads, HEAD_DIM).transpose(1, 0, 2)


def _q_heads(z):
    return z.reshape(z.shape[0], KV_HEADS, Q_PER_KV, HEAD_DIM).transpose(1, 2, 0, 3)


def _from_q_heads(o):
    return o.transpose(2, 0, 1, 3).reshape(o.shape[2], KV_HEADS * Q_PER_KV * HEAD_DIM)


def _scan_tile(t):
    return _tile(t, 512, SUBLANES)


def _scan_fwd_call(bre, bim, lre, lim, h0re, h0im, rev):
    t = bre.shape[0]
    tt = _scan_tile(t)
    nb = t // tt
    plane = bre.shape[1:]

    def body(bre_ref, bim_ref, lre_ref, lim_ref, h0re_ref, h0im_ref, sre_ref, sim_ref, h_ref):
        @pl.when(pl.program_id(0) == 0)
        def _():
            h_ref[0] = h0re_ref[...]
            h_ref[1] = h0im_ref[...]

        ar, ai = lre_ref[...], lim_ref[...]

        def step(j, carry):
            hr, hi = carry
            tj = (tt - 1 - j) if rev else j
            nr = ar * hr - ai * hi + bre_ref[tj]
            ni = ar * hi + ai * hr + bim_ref[tj]
            sre_ref[tj] = nr
            sim_ref[tj] = ni
            return nr, ni

        hr, hi = lax.fori_loop(0, tt, step, (h_ref[0], h_ref[1]), unroll=8)
        h_ref[0] = hr
        h_ref[1] = hi

    blk = pl.BlockSpec((tt,) + plane, (lambda i: (nb - 1 - i, 0, 0)) if rev else (lambda i: (i, 0, 0)))
    par = pl.BlockSpec(plane, lambda i: (0, 0))
    return pl.pallas_call(
        body, name="s5_scan_fwd", grid=(nb,), in_specs=[blk, blk, par, par, par, par], out_specs=[blk, blk],
        out_shape=[jax.ShapeDtypeStruct(bre.shape, F32)] * 2,
        scratch_shapes=[pltpu.VMEM((2,) + plane, F32)],
        compiler_params=_cparams(("arbitrary",), 4 * 4 * tt * plane[0] * plane[1]),
    )(bre, bim, lre, lim, h0re, h0im)


def _scan_bwd_call(gre, gim, sre, sim, lre, lim, h0re, h0im, rev):
    t = gre.shape[0]
    tt = _scan_tile(t)
    nb = t // tt
    plane = gre.shape[1:]
    down = not rev

    def body(gre_ref, gim_ref, sre_ref, sim_ref, lre_ref, lim_ref, h0re_ref, h0im_ref,
             dbre_ref, dbim_ref, dare_ref, daim_ref, dh0re_ref, dh0im_ref, carry_ref):
        i = pl.program_id(0)

        @pl.when(i == 0)
        def _():
            carry_ref[...] = jnp.zeros_like(carry_ref)

        ar, ai = lre_ref[...], lim_ref[...]

        def step(j, carry):
            gr, gi, dar, dai = carry
            tj = (tt - 1 - j) if down else j
            hr, hi = sre_ref[tj], sim_ref[tj]
            dar = dar + hr * gr + hi * gi
            dai = dai + hr * gi - hi * gr
            ngr = gre_ref[tj] + ar * gr + ai * gi
            ngi = gim_ref[tj] + ar * gi - ai * gr
            dbre_ref[tj] = ngr
            dbim_ref[tj] = ngi
            return ngr, ngi, dar, dai

        gr, gi, dar, dai = lax.fori_loop(
            0, tt, step, (carry_ref[0], carry_ref[1], carry_ref[2], carry_ref[3]), unroll=8)
        carry_ref[0] = gr
        carry_ref[1] = gi
        carry_ref[2] = dar
        carry_ref[3] = dai

        @pl.when(i == nb - 1)
        def _():
            hr, hi = h0re_ref[...], h0im_ref[...]
            dare_ref[...] = dar + hr * gr + hi * gi
            daim_ref[...] = dai + hr * gi - hi * gr
            dh0re_ref[...] = ar * gr + ai * gi
            dh0im_ref[...] = ar * gi - ai * gr

    blk = pl.BlockSpec((tt,) + plane, (lambda i: (nb - 1 - i, 0, 0)) if down else (lambda i: (i, 0, 0)))
    par = pl.BlockSpec(plane, lambda i: (0, 0))
    return pl.pallas_call(
        body, name="s5_scan_bwd", grid=(nb,), in_specs=[blk, blk, blk, blk, par, par, par, par],
        out_specs=[blk, blk, par, par, par, par],
        out_shape=[jax.ShapeDtypeStruct(gre.shape, F32)] * 2 + [jax.ShapeDtypeStruct(plane, F32)] * 4,
        scratch_shapes=[pltpu.VMEM((4,) + plane, F32)],
        compiler_params=_cparams(("arbitrary",), 4 * 6 * tt * plane[0] * plane[1]),
    )(gre, gim, sre, sim, lre, lim, h0re, h0im)


def make_scan(rev):
    @jax.custom_vjp
    def op(bre, bim, lre, lim, h0re, h0im):
        return tuple(_scan_fwd_call(bre, bim, lre, lim, h0re, h0im, rev))

    def fwd(bre, bim, lre, lim, h0re, h0im):
        sre, sim = _scan_fwd_call(bre, bim, lre, lim, h0re, h0im, rev)
        return (sre, sim), (sre, sim, lre, lim, h0re, h0im)

    def bwd(res, cts):
        sre, sim, lre, lim, h0re, h0im = res
        return tuple(_scan_bwd_call(cts[0], cts[1], sre, sim, lre, lim, h0re, h0im, rev))

    op.defvjp(fwd, bwd)
    return op


scan_up = make_scan(False)
scan_down = make_scan(True)


def _adamw_call(parts, w, m, v, name):
    r, c = w.shape
    tr = _tile(r, 256, SUBLANES)
    nparts = parts.shape[0]
    c1 = 1.0 - ADAM_B1 ** ADAM_STEP
    c2 = 1.0 - ADAM_B2 ** ADAM_STEP

    def body(p_ref, w_ref, m_ref, v_ref, g_ref, d_ref, nm_ref, nv_ref):
        g = p_ref[0].astype(F32)
        for s in range(1, nparts):
            g = g + p_ref[s].astype(F32)
        m1 = ADAM_B1 * m_ref[...] + (1.0 - ADAM_B1) * g
        v1 = ADAM_B2 * v_ref[...] + (1.0 - ADAM_B2) * jnp.square(g)
        g_ref[...] = g
        nm_ref[...] = m1
        nv_ref[...] = v1
        d_ref[...] = -ADAM_LR * ((m1 / c1) / (jnp.sqrt(v1 / c2) + ADAM_EPS) + ADAM_WD * w_ref[...])

    blk = pl.BlockSpec((tr, c), lambda i: (i, 0))
    return pl.pallas_call(
        body, name=name, grid=(r // tr,),
        in_specs=[pl.BlockSpec((nparts, tr, c), lambda i: (0, i, 0)), blk, blk, blk], out_specs=[blk] * 4,
        out_shape=[jax.ShapeDtypeStruct((r, c), F32)] * 4,
        compiler_params=_cparams(("parallel",), 4 * tr * c * (nparts + 7)),
    )(parts, w, m, v)


def _peer(k):
    x, y, c = lax.axis_index("x"), lax.axis_index("y"), lax.axis_index("c")
    px = 1 - x if k & 4 else x
    py = 1 - y if k & 2 else y
    pc = 1 - c if k & 1 else c
    return (px, py, pc), 4 * px + 2 * py + pc


def _my_slot():
    return 4 * lax.axis_index("x") + 2 * lax.axis_index("y") + lax.axis_index("c")


def _exchange_call(x, gather, name):
    slab = x.shape if gather else x.shape[1:]

    def body(x_ref, out_ref, send_sems, recv_sems, local_sem):
        me = _my_slot()
        mine = pltpu.make_async_copy(x_ref if gather else x_ref.at[me], out_ref.at[me], local_sem)
        mine.start()
        sends = []
        for k in range(1, N_DEV):
            peer, slot = _peer(k)
            cp = pltpu.make_async_remote_copy(
                src_ref=x_ref if gather else x_ref.at[slot], dst_ref=out_ref.at[me],
                send_sem=send_sems.at[k - 1], recv_sem=recv_sems.at[k - 1],
                device_id=peer, device_id_type=MESH)
            cp.start()
            sends.append(cp)
        for k in range(1, N_DEV):
            peer, slot = _peer(k)
            pltpu.make_async_remote_copy(
                src_ref=x_ref if gather else x_ref.at[slot], dst_ref=out_ref.at[slot],
                send_sem=send_sems.at[k - 1], recv_sem=recv_sems.at[k - 1],
                device_id=peer, device_id_type=MESH).wait_recv()
        for cp in sends:
            cp.wait_send()
        mine.wait()

    return pl.pallas_call(
        body, name=name,
        in_specs=[pl.BlockSpec(memory_space=pl.ANY)], out_specs=pl.BlockSpec(memory_space=pl.ANY),
        out_shape=jax.ShapeDtypeStruct((N_DEV,) + tuple(slab), x.dtype),
        scratch_shapes=[pltpu.SemaphoreType.DMA((N_DEV - 1,)), pltpu.SemaphoreType.DMA((N_DEV - 1,)),
                        pltpu.SemaphoreType.DMA],
    )(x)


def all_gather(x, name):
    return _exchange_call(x, True, name)


def sibling_swap(x, name):
    def body(x_ref, out_ref, send_sem, recv_sem):
        x_, y_, c_ = lax.axis_index("x"), lax.axis_index("y"), lax.axis_index("c")
        cp = pltpu.make_async_remote_copy(
            src_ref=x_ref.at[1 - c_], dst_ref=out_ref, send_sem=send_sem, recv_sem=recv_sem,
            device_id=(x_, y_, 1 - c_), device_id_type=MESH)
        cp.start()
        cp.wait()

    return pl.pallas_call(
        body, name=name,
        in_specs=[pl.BlockSpec(memory_space=pl.ANY)], out_specs=pl.BlockSpec(memory_space=pl.ANY),
        out_shape=jax.ShapeDtypeStruct(x.shape[1:], x.dtype),
        scratch_shapes=[pltpu.SemaphoreType.DMA, pltpu.SemaphoreType.DMA],
    )(x)


def chip_exchange(x, name):
    def body(x_ref, out_ref, send_sems, recv_sems, local_sem):
        x_, y_, c_ = lax.axis_index("x"), lax.axis_index("y"), lax.axis_index("c")
        mine = 2 * x_ + y_
        local = pltpu.make_async_copy(x_ref.at[mine], out_ref.at[mine], local_sem)
        local.start()

        def copy(k):
            px = 1 - x_ if k & 2 else x_
            py = 1 - y_ if k & 1 else y_
            peer = 2 * px + py
            send = pltpu.make_async_remote_copy(
                src_ref=x_ref.at[peer], dst_ref=out_ref.at[mine], send_sem=send_sems.at[k - 1],
                recv_sem=recv_sems.at[k - 1], device_id=(px, py, c_), device_id_type=MESH)
            recv = pltpu.make_async_remote_copy(
                src_ref=x_ref.at[peer], dst_ref=out_ref.at[peer], send_sem=send_sems.at[k - 1],
                recv_sem=recv_sems.at[k - 1], device_id=(px, py, c_), device_id_type=MESH)
            return send, recv

        copies = [copy(k) for k in range(1, 4)]
        for send, _ in copies:
            send.start()
        for _, recv in copies:
            recv.wait_recv()
        for send, _ in copies:
            send.wait_send()
        local.wait()

    return pl.pallas_call(
        body, name=name,
        in_specs=[pl.BlockSpec(memory_space=pl.ANY)], out_specs=pl.BlockSpec(memory_space=pl.ANY),
        out_shape=jax.ShapeDtypeStruct(x.shape, x.dtype),
        scratch_shapes=[pltpu.SemaphoreType.DMA((3,)), pltpu.SemaphoreType.DMA((3,)),
                        pltpu.SemaphoreType.DMA],
    )(x)


def all_gather_two_level(x, name):
    def body(x_ref, out_ref, send_sems, recv_sems, local_sem):
        x_, y_, c_ = lax.axis_index("x"), lax.axis_index("y"), lax.axis_index("c")
        me, sibling = (x_, y_, c_), (x_, y_, 1 - c_)
        chips = [(1 - x_, y_), (x_, 1 - y_), (1 - x_, 1 - y_)]

        def slot(px, py, pc):
            return out_ref.at[4 * px + 2 * py + pc]

        def copy(k, block, to, src=None):
            return pltpu.make_async_remote_copy(
                src_ref=slot(*block) if src is None else src, dst_ref=slot(*block),
                send_sem=send_sems.at[k], recv_sem=recv_sems.at[k], device_id=to, device_id_type=MESH)

        mine = pltpu.make_async_copy(x_ref, slot(*me), local_sem)
        mine.start()
        first = [copy(0, me, sibling, src=x_ref)]
        first += [copy(1 + j, me, (*chip, c_), src=x_ref) for j, chip in enumerate(chips)]
        for cp in first:
            cp.start()
        passed = [copy(4 + j, (*chip, c_), sibling) for j, chip in enumerate(chips)]
        for j, chip in enumerate(chips):
            copy(1 + j, (*chip, c_), me).wait_recv()
            passed[j].start()
        copy(0, sibling, me).wait_recv()
        for j, chip in enumerate(chips):
            copy(4 + j, (*chip, 1 - c_), me).wait_recv()
        for cp in first + passed:
            cp.wait_send()
        mine.wait()

    return pl.pallas_call(
        body, name=name,
        in_specs=[pl.BlockSpec(memory_space=pl.ANY)], out_specs=pl.BlockSpec(memory_space=pl.ANY),
        out_shape=jax.ShapeDtypeStruct((N_DEV,) + tuple(x.shape), x.dtype),
        scratch_shapes=[pltpu.SemaphoreType.DMA((N_DEV - 1,)), pltpu.SemaphoreType.DMA((N_DEV - 1,)),
                        pltpu.SemaphoreType.DMA],
    )(x)


def all_to_all(x, name):
    return _exchange_call(x, False, name)


def _rope_tables(t):
    n_freq = HEAD_DIM // 4
    tok = jnp.arange(t)
    inv = ROPE_THETA ** (-jnp.arange(n_freq, dtype=F32) / n_freq)
    a_row = (tok // GRID_W).astype(F32)[:, None] * inv
    a_col = (tok % GRID_W).astype(F32)[:, None] * inv
    cos = jnp.concatenate([jnp.cos(a_row)] * 2 + [jnp.cos(a_col)] * 2, axis=1)
    sin = jnp.concatenate([-jnp.sin(a_row), jnp.sin(a_row), -jnp.sin(a_col), jnp.sin(a_col)], axis=1)
    return jnp.concatenate([cos, cos], axis=1), jnp.concatenate([sin, sin], axis=1)


def _block_diag(blocks):
    g, a, b = blocks.shape
    eye = jnp.eye(g, dtype=blocks.dtype)
    return jnp.einsum("gab,gk->gakb", blocks, eye).reshape(g * a, g * b)


def _ffn_fwd_calls(x, mod, g, w_in, w_out):
    shift, scale, gate = mod[0:1], mod[1:2], mod[2:3]
    h, = _row_call(lambda xt, gt, sh, sc: (_norm_mod_fn(xt, gt, sh, sc)[0].astype(BF16),),
                   [x], [g, shift, scale], 0, "ffn_norm")
    u = _mm(h, w_in, out_dtype=BF16, name="ffn_up")
    a, = _row_call(lambda ut: (_swiglu_fn(ut.astype(F32))[0].astype(BF16),), [u], [], 0, "ffn_act")
    y = _mm(a, w_out, name="ffn_down")
    out, = _row_call(functools.partial(_resid_fn, 0.5), [x, y], [gate], 0, "ffn_resid")
    return out, (h, u, a, y)


@jax.custom_vjp
def _ffn_half(x, mod, g, w_in, w_out):
    return _ffn_fwd_calls(x, mod, g, w_in, w_out)[0]


def _ffn_half_fwd(x, mod, g, w_in, w_out):
    out, saved = _ffn_fwd_calls(x, mod, g, w_in, w_out)
    return out, (x, mod, g, w_in, w_out, saved)


def _ffn_half_bwd(res, dxn):
    x, mod, g, w_in, w_out, (h, u, a, y) = res
    shift, scale, gate = mod[0:1], mod[1:2], mod[2:3]

    def resid_bwd(dt, yt, gt):
        return (0.5 * gt * dt).astype(BF16), 0.5 * jnp.sum(dt * yt, axis=0, keepdims=True)

    dy, dgate = _row_call(resid_bwd, [dxn, y], [gate], 1, "ffn_resid_bwd")
    da = _mm(dy, w_out, tb=True, out_dtype=BF16, name="ffn_down_dx")
    dw_out = _mm(a, dy, ta=True, out_dtype=w_out.dtype, name="ffn_down_dw")

    def act_bwd(ut, dat):
        return (jax.vjp(_swiglu_fn, ut.astype(F32))[1]((dat.astype(F32),))[0].astype(BF16),)

    du, = _row_call(act_bwd, [u, da], [], 0, "ffn_act_bwd")
    dh = _mm(du, w_in, tb=True, name="ffn_up_dx")
    dw_in = _mm(h, du, ta=True, out_dtype=w_in.dtype, name="ffn_up_dw")

    def norm_bwd(xt, dht, dt, gt, sh, sc):
        dx, dg, dsh, dsc = jax.vjp(_norm_mod_fn, xt, gt, sh, sc)[1]((dht,))
        return dx + dt, dg, dsh, dsc

    dx, dg, dshift, dscale = _row_call(norm_bwd, [x, dh, dxn], [g, shift, scale], 3, "ffn_norm_bwd")
    return dx, jnp.concatenate([dshift, dscale, dgate], axis=0), dg, dw_in, dw_out


_ffn_half.defvjp(_ffn_half_fwd, _ffn_half_bwd)


def _planes(z):
    return z.reshape(z.shape[0], SUBLANES, SSM_LANES // SUBLANES)


def _s5_discretize(a_re, a_im, log_dt, b_re, b_im):
    lam = lax.complex(a_re, a_im)
    dt = jnp.exp(log_dt)[:, None]
    lam_bar = jnp.exp(lam * dt)
    b_bar = ((lam_bar - 1.0) / lam)[..., None] * lax.complex(b_re, b_im)
    return lam_bar, b_bar


def _s5_branch(u_lat, u_ctx, w, l, with_ctx_out):
    zero = jnp.zeros((SUBLANES, SSM_LANES // SUBLANES), F32)
    lat_terms, ctx_terms = [], []
    for d, scan in enumerate((scan_up, scan_down)):
        lam_bar, b_bar = _s5_discretize(w["ssm_a_re"][l, d], w["ssm_a_im"][l, d], w["ssm_log_dt"][l, d],
                                        w["ssm_b_re"][l, d], w["ssm_b_im"][l, d])
        lre = jnp.real(lam_bar).reshape(zero.shape)
        lim = jnp.imag(lam_bar).reshape(zero.shape)
        b_t = jnp.swapaxes(b_bar, 1, 2)
        b_mat = jnp.concatenate([_block_diag(jnp.real(b_t)), _block_diag(jnp.imag(b_t))], axis=1)
        c_re = _block_diag(jnp.swapaxes(w["ssm_c_re"][l, d], 1, 2))
        c_im = _block_diag(jnp.swapaxes(w["ssm_c_im"][l, d], 1, 2))
        bu_c = linear(u_ctx, b_mat)
        sc_re, sc_im = scan(_planes(bu_c[:, :SSM_LANES]), _planes(bu_c[:, SSM_LANES:]), lre, lim, zero, zero)
        last = 0 if d == 1 else u_ctx.shape[0] - 1
        bu_l = linear(u_lat, b_mat)
        sl_re, sl_im = scan(_planes(bu_l[:, :SSM_LANES]), _planes(bu_l[:, SSM_LANES:]), lre, lim,
                            sc_re[last], sc_im[last])
        flat = lambda s: s.reshape(s.shape[0], SSM_LANES)
        lat_terms += [linear(flat(sl_re), c_re), linear(flat(sl_im), c_im)]
        if with_ctx_out:
            ctx_terms += [linear(flat(sc_re), c_re), linear(flat(sc_im), c_im)]
    d_skip = w["ssm_d"][l][None, :]

    def out(terms, u):
        y, = s5_pre(*terms, u, d_skip)
        return glu(linear(y, w["glu_w"][l]))[0]

    return out(lat_terms, u_lat), (out(ctx_terms, u_ctx) if with_ctx_out else None)


def _pool_branch(xa, w, l):
    y = linear(pool_diff(xa), _block_diag(w["pool_w"][l]))
    return scale_rows(y, w["pool_scale"][l][None, :])[0]


_CTX_GROUPS = (O_VB - O_KB, O_UC - O_VB, O_KD - O_UC, O_VD - O_KD, CTX_COLS - O_VD)
_ALL_GROUPS = _CTX_GROUPS + (O_QD - O_QB, O_XA - O_QD, O_GATE - O_XA)
project_ctx = make_split_linear(_CTX_GROUPS)
project_all = make_split_linear(_ALL_GROUPS)


def _merge_branches(branches, gate_logits, w, l):
    zs = [linear_b(y, w["branch_w"][l, k]) for k, y in enumerate(branches)]
    return linear(merge(gate_logits, *zs)[0], w["out_w"][l])


def _token_mixer(h, hc, cos, sin, w, l, with_ctx_out):
    w_in, w_gate = w["w_in"][l][:, :O_GATE], w["w_in"][l][:, O_GATE:]
    kb, vb, uc, kd, vd, qb, qd, xa = project_all(h, w_in)
    p_gate = linear_b(h, w_gate)
    if with_ctx_out:
        kb_c, vb_c, uc_c, kd_c, vd_c, qb_c, qd_c, xa_c = project_all(hc, w_in)
        pc_gate = linear_b(hc, w_gate)
    else:
        kb_c, vb_c, uc_c, kd_c, vd_c = project_ctx(hc, w_in[:, :CTX_COLS])
    sink = w["win_sink"][l]
    q_g = jnp.tile(w["qk_norm"][l, 0], KV_HEADS * Q_PER_KV)[None, :]
    k_g = jnp.tile(w["qk_norm"][l, 1], KV_HEADS)[None, :]
    k_win_c = _to_heads(kb_c, KV_HEADS)
    v_win_c = _to_heads(vb_c, KV_HEADS)
    k_glb_c = _to_heads(head_norm(kd_c, k_g)[0], KV_HEADS)
    v_glb_c = _to_heads(vd_c, KV_HEADS)
    y_a = _pool_branch(xa, w, l)
    q_win = _q_heads(rope(qb, cos, sin)[0])
    k_win = _to_heads(rope(kb, cos, sin)[0], KV_HEADS)
    v_win = _to_heads(vb, KV_HEADS)
    y_b = _from_q_heads(attn_window(q_win, k_win_c, v_win_c, k_win, v_win, sink))
    y_c, y_c_ctx = _s5_branch(uc, uc_c, w, l, with_ctx_out)
    q_glb = _q_heads(norm_rope(qd, cos, sin, q_g)[0])
    k_glb = _to_heads(norm_rope(kd, cos, sin, k_g)[0], KV_HEADS)
    v_glb = _to_heads(vd, KV_HEADS)
    y_d = _from_q_heads(attn_global(q_glb, k_glb_c, v_glb_c, k_glb, v_glb))
    y = _merge_branches((y_a, y_b, y_c, y_d), p_gate, w, l)
    if not with_ctx_out:
        return y, None
    y_a_c = _pool_branch(xa_c, w, l)
    y_b_c = _from_q_heads(attn_ctx_sink(_q_heads(qb_c), k_win_c, v_win_c, sink))
    q_glb_c = _q_heads(head_norm(qd_c, q_g)[0])
    y_d_c = _from_q_heads(attn_ctx(q_glb_c, k_glb_c, v_glb_c))
    return y, _merge_branches((y_a_c, y_b_c, y_c_ctx, y_d_c), pc_gate, w, l)


def local_loss(w, x, c, ctx, target):
    depth = w["w_mod"].shape[0]
    cos, sin = _rope_tables(x.shape[0])
    cond = jnp.concatenate([c, w["c_ctx"][None, :], jnp.zeros((COND_ROWS - 2, D_MODEL), F32)], axis=0)
    s_all, = silu_rows(cond)
    for l in range(depth):
        last = l == depth - 1
        m_all = (linear(s_all, w["w_mod"][l]) + w["b_mod"][l][None, :]).reshape(COND_ROWS, N_SUB, 3, D_MODEL)
        m, mc = m_all[0], m_all[1]
        g = w["norm_g"][l][:, None, :]
        x = _ffn_half(x, m[0], g[0], w["ffn_in"][l, 0], w["ffn_out"][l, 0])
        ctx = _ffn_half(ctx, mc[0], g[0], w["ffn_in"][l, 0], w["ffn_out"][l, 0])
        h, = norm_mod(x, g[1], m[1, 0:1], m[1, 1:2])
        hc, = norm_mod(ctx, g[1], mc[1, 0:1], mc[1, 1:2])
        y, y_ctx = _token_mixer(h, hc, cos, sin, w, l, not last)
        x, = resid_full(x, y, m[1, 2:3])
        if not last:
            ctx, = resid_full(ctx, y_ctx, mc[1, 2:3])
        x = _ffn_half(x, m[2], g[2], w["ffn_in"][l, 1], w["ffn_out"][l, 1])
        if not last:
            ctx = _ffn_half(ctx, mc[2], g[2], w["ffn_in"][l, 1], w["ffn_out"][l, 1])
    return jnp.sum(loss_rows(x, target, w["final_g"][None, :])[0])


PACK_COLS = 1024


def _pack(arrays):
    flat = jnp.concatenate([a.reshape(-1) for a in arrays])
    pad = (-flat.shape[0]) % (PACK_COLS * 16)
    return jnp.pad(flat, (0, pad)).reshape(-1, PACK_COLS)


def _unpack(slab, shapes):
    flat = slab.reshape(-1)
    out, off = [], 0
    for s in shapes:
        n = math.prod(s)
        out.append(flat[off:off + n].reshape(s))
        off += n
    return out


def _full_from_shards(gathered, shard_shape, axis):
    z = jnp.moveaxis(gathered.reshape((N_DEV,) + tuple(shard_shape)), 0, axis)
    shape = list(shard_shape)
    shape[axis] *= N_DEV
    return z.reshape(shape)


def _shards_from_full(full, axis):
    shape = list(full.shape)
    shape[axis:axis + 1] = [N_DEV, shape[axis] // N_DEV]
    return jnp.moveaxis(full.reshape(shape), axis, 0)


def kernel(x, c, ctx, c_ctx, w_mod, b_mod, norm_g, ffn_in, ffn_out, w_in, win_sink, qk_norm, pool_w, pool_scale, ssm_a_re, ssm_a_im, ssm_log_dt, ssm_b_re, ssm_b_im, ssm_c_re, ssm_c_im, ssm_d, glu_w, branch_w, out_w, final_g, loss_target, m_c_ctx, m_w_mod, m_b_mod, m_norm_g, m_ffn_in, m_ffn_out, m_w_in, m_win_sink, m_qk_norm, m_pool_w, m_pool_scale, m_ssm_a_re, m_ssm_a_im, m_ssm_log_dt, m_ssm_b_re, m_ssm_b_im, m_ssm_c_re, m_ssm_c_im, m_ssm_d, m_glu_w, m_branch_w, m_out_w, m_final_g, v_c_ctx, v_w_mod, v_b_mod, v_norm_g, v_ffn_in, v_ffn_out, v_w_in, v_win_sink, v_qk_norm, v_pool_w, v_pool_scale, v_ssm_a_re, v_ssm_a_im, v_ssm_log_dt, v_ssm_b_re, v_ssm_b_im, v_ssm_c_re, v_ssm_c_im, v_ssm_d, v_glu_w, v_branch_w, v_out_w, v_final_g):
    given = dict(locals())
    wts = {n: given[n] for n in WEIGHTS}
    mom = {n: given["m_" + n] for n in WEIGHTS}
    var = {n: given["v_" + n] for n in WEIGHTS}
    me = _my_slot()

    shard_shapes = [wts[n].shape for n in SHARDED]
    w_slab = _pack([wts[n] for n in SHARDED])
    gathered = all_gather_two_level(w_slab.astype(BF16), "gather_weights")
    full = dict(wts)
    row = 0
    for n in SHARDED:
        n_rows = math.prod(wts[n].shape) // PACK_COLS
        full[n] = _full_from_shards(gathered[:, row:row + n_rows], wts[n].shape, SHARD_AXIS[n])
        row += n_rows
    g_slab = _pack([norm_g])
    g_all = all_gather(g_slab, "gather_norm_g")
    full["norm_g"] = _full_from_shards(
        jnp.stack([_unpack(g_all[s], [norm_g.shape])[0] for s in range(N_DEV)]), norm_g.shape, 2)

    loss, (gw, gx) = jax.value_and_grad(local_loss, argnums=(0, 1))(full, x[0], c, ctx[0], loss_target[0])
    loss = lax.psum(loss, ("x", "y", "c"))

    dest = [_shards_from_full(gw[n], SHARD_AXIS[n]).reshape(N_DEV // 2, 2, -1, PACK_COLS) for n in SHARDED]
    send = jnp.swapaxes(jnp.concatenate(dest, axis=2), 0, 1)
    from_sibling = sibling_swap(send, "exchange_grads_sibling")
    own = lax.dynamic_index_in_dim(send, lax.axis_index("c"), axis=0, keepdims=False)
    pair, = _row_call(lambda a, b: ((a.astype(F32) + b.astype(F32)).astype(BF16),),
                      [own.reshape(-1, PACK_COLS), from_sibling.reshape(-1, PACK_COLS)], [], 0,
                      "exchange_pair_sum")
    big_parts = chip_exchange(pair.reshape(own.shape), "exchange_grads_chips")
    small_names = SMALL + ("norm_g",)
    small_shapes = [gw[n].shape for n in small_names]
    small_parts = all_gather(_pack([gw[n] for n in small_names]), "gather_small_grads")

    big = _adamw_call(big_parts, w_slab, _pack([mom[n] for n in SHARDED]), _pack([var[n] for n in SHARDED]),
                      "adamw_sharded")
    big = [_unpack(b, shard_shapes) for b in big]
    col = me * norm_g.shape[2]

    def small_slab(src, shard_src):
        padded = jnp.zeros((norm_g.shape[0], norm_g.shape[1], norm_g.shape[2] * N_DEV), F32)
        padded = lax.dynamic_update_slice(padded, shard_src, (0, 0, col))
        return _pack([src[n] for n in SMALL] + [padded])

    small = _adamw_call(small_parts, small_slab(wts, norm_g), small_slab(mom, m_norm_g),
                        small_slab(var, v_norm_g), "adamw_small")
    small = [_unpack(s, small_shapes) for s in small]

    outs = {}
    for kind in range(4):
        for i, n in enumerate(SHARDED):
            outs[(kind, n)] = big[kind][i]
        for i, n in enumerate(small_names):
            val = small[kind][i]
            if n == "norm_g":
                val = lax.dynamic_slice(val, (0, 0, col), norm_g.shape)
            outs[(kind, n)] = val
    return (loss, gx[None], *[outs[(k, n)] for k in range(4) for n in WEIGHTS])
```

```python
import functools
import math

import jax
import jax.numpy as jnp
from jax import lax
from jax.experimental import pallas as pl
from jax.experimental.pallas import tpu as pltpu

F32 = jnp.float32
BF16 = jnp.bfloat16

D_MODEL = 1024
GRID_W = 64
HEAD_DIM = 64
N_BRANCH = 4
BRANCH_W = D_MODEL // N_BRANCH
WINDOW = 128
ROPE_THETA = 10000.0
EPS = 1e-6
D_FF = 2816
N_SUB = 3
POOL_WINDOWS = (2, 4, 8, 16)
POOL_GROUP = BRANCH_W // len(POOL_WINDOWS)
KV_HEADS = 2
Q_PER_KV = 2
SSM_GROUP = 16
SSM_GROUPS = BRANCH_W // SSM_GROUP
SSM_STATE = 64
SSM_LANES = SSM_GROUPS * SSM_STATE
O_KB, O_VB, O_UC, O_KD, O_VD, CTX_COLS = 0, 128, 256, 512, 640, 768
O_QB, O_QD, O_XA, O_GATE = 768, 1024, 1280, 1536
IN_W = O_GATE + N_BRANCH * D_MODEL

ADAM_LR, ADAM_B1, ADAM_B2, ADAM_EPS, ADAM_WD, ADAM_STEP = 0.001, 0.9, 0.999, 1e-08, 0.01, 10

N_DEV = 8
MESH = pl.DeviceIdType.MESH

V7X_VMEM_BYTES = 64 * 1024 * 1024
SUBLANES = 8
LANES = 128
NEG_BIG = -1e30
COND_ROWS = 128

SHARDED = ("w_mod", "ffn_in", "ffn_out", "w_in", "glu_w", "branch_w", "out_w")
SHARD_AXIS = {"w_mod": 2, "ffn_in": 3, "ffn_out": 2, "w_in": 2, "glu_w": 2, "branch_w": 3, "out_w": 1}
SMALL = ("c_ctx", "b_mod", "win_sink", "qk_norm", "pool_w", "pool_scale", "ssm_a_re", "ssm_a_im",
         "ssm_log_dt", "ssm_b_re", "ssm_b_im", "ssm_c_re", "ssm_c_im", "ssm_d", "final_g")
WEIGHTS = ("c_ctx", "w_mod", "b_mod", "norm_g", "ffn_in", "ffn_out", "w_in", "win_sink", "qk_norm",
           "pool_w", "pool_scale", "ssm_a_re", "ssm_a_im", "ssm_log_dt", "ssm_b_re", "ssm_b_im",
           "ssm_c_re", "ssm_c_im", "ssm_d", "glu_w", "branch_w", "out_w", "final_g")


def _tile(n, cap, mult):
    if n <= cap:
        return n
    t = (cap // mult) * mult
    while t >= mult:
        if n % t == 0:
            return t
        t -= mult
    return n


def _cparams(sem, tile_bytes, resident_bytes=0):
    limit = int(min(V7X_VMEM_BYTES - 8 * 2 ** 20,
                    max(32 * 2 ** 20, 3 * tile_bytes + resident_bytes + 8 * 2 ** 20)))
    return pltpu.CompilerParams(dimension_semantics=sem, vmem_limit_bytes=limit)


PLANE = (SUBLANES, 128)
PLANE_COLS = PLANE[0] * PLANE[1]


def _planes_to_rows(ref):
    return jnp.concatenate([ref[:, j, :] for j in range(PLANE[0])], axis=1)


def _mm(a, b, ta=False, tb=False, out_dtype=F32, a_planes=False, b_planes=False, out_planes=False,
        name="mm"):
    a_shape = (a.shape[0], PLANE_COLS) if a_planes else a.shape
    b_shape = (b.shape[0], PLANE_COLS) if b_planes else b.shape
    m, k = (a_shape[1], a_shape[0]) if ta else a_shape
    n = b_shape[0] if tb else b_shape[1]
    assert (b_shape[1] if tb else b_shape[0]) == k and not (b_planes and tb)
    tm, tn, tk = _tile(m, 1024, LANES), _tile(n, 1536, LANES), _tile(k, 1536, LANES)
    nk = k // tk
    dims = (((0 if ta else 1,), (1 if tb else 0,)), ((), ()))

    def body(a_ref, b_ref, o_ref, acc_ref):
        kk = pl.program_id(2)

        @pl.when(kk == 0)
        def _():
            acc_ref[...] = jnp.zeros_like(acc_ref)

        av = _planes_to_rows(a_ref) if a_planes else a_ref[...]
        bv = _planes_to_rows(b_ref) if b_planes else b_ref[...]
        acc_ref[...] += lax.dot_general(av.astype(BF16), bv.astype(BF16), dims, preferred_element_type=F32)

        @pl.when(kk == nk - 1)
        def _():
            if out_planes:
                for j in range(PLANE[0]):
                    o_ref[:, j, :] = acc_ref[:, j * PLANE[1]:(j + 1) * PLANE[1]].astype(o_ref.dtype)
            else:
                o_ref[...] = acc_ref[...].astype(o_ref.dtype)

    if a_planes:
        assert (tm if ta else tk) == PLANE_COLS
        a_spec = pl.BlockSpec(((tk if ta else tm),) + PLANE, (lambda i, j, kk: (kk, 0, 0)) if ta
                              else (lambda i, j, kk: (i, 0, 0)))
    else:
        a_spec = (pl.BlockSpec((tk, tm), lambda i, j, kk: (kk, i)) if ta
                  else pl.BlockSpec((tm, tk), lambda i, j, kk: (i, kk)))
    if b_planes:
        assert tn == PLANE_COLS
        b_spec = pl.BlockSpec((tk,) + PLANE, lambda i, j, kk: (kk, 0, 0))
    else:
        b_spec = (pl.BlockSpec((tn, tk), lambda i, j, kk: (j, kk)) if tb
                  else pl.BlockSpec((tk, tn), lambda i, j, kk: (kk, j)))
    if out_planes:
        assert tn == PLANE_COLS
        o_spec = pl.BlockSpec((tm,) + PLANE, lambda i, j, kk: (i, 0, 0))
        o_shape = jax.ShapeDtypeStruct((m,) + PLANE, out_dtype)
    else:
        o_spec = pl.BlockSpec((tm, tn), lambda i, j, kk: (i, j))
        o_shape = jax.ShapeDtypeStruct((m, n), out_dtype)
    tile_bytes = (a.dtype.itemsize * tm * tk + b.dtype.itemsize * tk * tn
                  + jnp.dtype(out_dtype).itemsize * tm * tn + 2 * tm * tn)
    return pl.pallas_call(
        body, name=name, grid=(m // tm, n // tn, nk),
        in_specs=[a_spec, b_spec], out_specs=o_spec, out_shape=o_shape,
        scratch_shapes=[pltpu.VMEM((tm, tn), F32)],
        compiler_params=_cparams(("parallel", "parallel", "arbitrary"), tile_bytes),
    )(a, b)


def make_linear(out_dtype, x_planes=False, out_planes=False):
    @jax.custom_vjp
    def op(x, w):
        return _mm(x, w, out_dtype=out_dtype, a_planes=x_planes, out_planes=out_planes, name="linear_fwd")

    def fwd(x, w):
        return op(x, w), (x, w)

    def bwd(res, dy):
        x, w = res
        return (_mm(dy, w, tb=True, out_dtype=x.dtype, a_planes=out_planes, out_planes=x_planes,
                    name="linear_dx"),
                _mm(x, dy, ta=True, out_dtype=w.dtype, a_planes=x_planes, b_planes=out_planes,
                    name="linear_dw"))

    op.defvjp(fwd, bwd)
    return op


linear = make_linear(F32)
linear_b = make_linear(BF16)
linear_to_planes = make_linear(F32, out_planes=True)
linear_from_planes = make_linear(F32, x_planes=True)


def _split_rows(t):
    return _tile(t, 512, 2 * SUBLANES)


def _split_fwd(x, w, widths):
    t, k = x.shape
    n = w.shape[1]
    tm = _split_rows(t)

    def body(x_ref, w_ref, *o_refs):
        y = jnp.dot(x_ref[...].astype(BF16), w_ref[...].astype(BF16), preferred_element_type=F32)
        off = 0
        for o_ref, wd in zip(o_refs, widths):
            o_ref[...] = y[:, off:off + wd]
            off += wd

    return pl.pallas_call(
        body, name="split_linear_fwd", grid=(t // tm,),
        in_specs=[pl.BlockSpec((tm, k), lambda i: (i, 0)), pl.BlockSpec((k, n), lambda i: (0, 0))],
        out_specs=[pl.BlockSpec((tm, wd), lambda i: (i, 0)) for wd in widths],
        out_shape=[jax.ShapeDtypeStruct((t, wd), F32) for wd in widths],
        compiler_params=_cparams(("parallel",), 4 * tm * (k + 2 * n) + w.dtype.itemsize * k * n),
    )(x, w)


def _split_dx(cts, w):
    t = cts[0].shape[0]
    k, n = w.shape
    tm = _split_rows(t)

    def body(*refs):
        dy = jnp.concatenate([r[...].astype(BF16) for r in refs[:-2]], axis=1)
        refs[-1][...] = lax.dot_general(dy, refs[-2][...].astype(BF16), (((1,), (1,)), ((), ())),
                                        preferred_element_type=F32)

    return pl.pallas_call(
        body, name="split_linear_dx", grid=(t // tm,),
        in_specs=[pl.BlockSpec((tm, c.shape[1]), lambda i: (i, 0)) for c in cts]
        + [pl.BlockSpec((k, n), lambda i: (0, 0))],
        out_specs=pl.BlockSpec((tm, k), lambda i: (i, 0)), out_shape=jax.ShapeDtypeStruct((t, k), F32),
        compiler_params=_cparams(("parallel",), 4 * tm * (k + 2 * n) + w.dtype.itemsize * k * n),
    )(*cts, w)


def _split_dw(x, cts, out_dtype):
    t, k = x.shape
    n = sum(c.shape[1] for c in cts)
    tk = _split_rows(t)
    steps = t // tk

    def body(x_ref, *refs):
        o_ref, acc_ref = refs[-2], refs[-1]

        @pl.when(pl.program_id(0) == 0)
        def _():
            acc_ref[...] = jnp.zeros_like(acc_ref)

        dy = jnp.concatenate([r[...].astype(BF16) for r in refs[:-2]], axis=1)
        acc_ref[...] += lax.dot_general(x_ref[...].astype(BF16), dy, (((0,), (0,)), ((), ())),
                                        preferred_element_type=F32)

        @pl.when(pl.program_id(0) == steps - 1)
        def _():
            o_ref[...] = acc_ref[...].astype(o_ref.dtype)

    return pl.pallas_call(
        body, name="split_linear_dw", grid=(steps,),
        in_specs=[pl.BlockSpec((tk, k), lambda i: (i, 0))]
        + [pl.BlockSpec((tk, c.shape[1]), lambda i: (i, 0)) for c in cts],
        out_specs=pl.BlockSpec((k, n), lambda i: (0, 0)), out_shape=jax.ShapeDtypeStruct((k, n), out_dtype),
        scratch_shapes=[pltpu.VMEM((k, n), F32)],
        compiler_params=_cparams(("arbitrary",), 4 * tk * (k + n), 3 * 4 * k * n),
    )(x, *cts)


def make_split_linear(widths):
    @jax.custom_vjp
    def op(x, w):
        return tuple(_split_fwd(x, w, widths))

    def fwd(x, w):
        return op(x, w), (x, w)

    def bwd(res, cts):
        x, w = res
        return _split_dx(cts, w), _split_dw(x, cts, w.dtype)

    op.defvjp(fwd, bwd)
    return op


ROW_TILE_BYTES = 6 * 2 ** 20


def _row_tile(t, row_bytes):
    tm = 1024
    while tm > 2 * SUBLANES and tm * row_bytes > ROW_TILE_BYTES:
        tm //= 2
    return _tile(t, tm, 2 * SUBLANES)


def _row_call(fn, rows, params, n_reduce, name):
    t = rows[0].shape[0]
    out_avals = jax.eval_shape(fn, *rows, *params)
    n_out = len(out_avals) - n_reduce
    row_avals, red_avals = out_avals[:n_out], out_avals[n_out:]
    row_bytes = sum(r.shape[1] * r.dtype.itemsize for r in (*rows, *row_avals))
    tm = _row_tile(t, row_bytes)
    n_in = len(rows) + len(params)

    def body(*refs):
        outs = fn(*[r[...] for r in refs[:n_in]])
        o_refs = refs[n_in:]
        for o_ref, o in zip(o_refs[:n_out], outs[:n_out]):
            o_ref[...] = o.astype(o_ref.dtype)
        if n_reduce:
            @pl.when(pl.program_id(0) == 0)
            def _():
                for r in o_refs[n_out:]:
                    r[...] = jnp.zeros_like(r)

            for r, o in zip(o_refs[n_out:], outs[n_out:]):
                r[...] += o.astype(r.dtype)

    in_specs = ([pl.BlockSpec((tm, r.shape[1]), lambda i: (i, 0)) for r in rows]
                + [pl.BlockSpec(p.shape, lambda i: (0, 0)) for p in params])
    out_specs = ([pl.BlockSpec((tm, o.shape[1]), lambda i: (i, 0)) for o in row_avals]
                 + [pl.BlockSpec(o.shape, lambda i: (0, 0)) for o in red_avals])
    return pl.pallas_call(
        body, name=name, grid=(t // tm,), in_specs=in_specs, out_specs=out_specs,
        out_shape=[jax.ShapeDtypeStruct(o.shape, o.dtype) for o in out_avals],
        compiler_params=_cparams(("arbitrary",) if n_reduce else ("parallel",), tm * row_bytes),
    )(*rows, *params)


def rowwise(fn, n_rows, name):
    @jax.custom_vjp
    def op(*args):
        return tuple(_row_call(fn, args[:n_rows], args[n_rows:], 0, name + "_fwd"))

    def fwd(*args):
        return op(*args), args

    def bwd(args, cts):
        n_ct = len(cts)

        def bwd_fn(*a):
            r, ct, p = a[:n_rows], a[n_rows:n_rows + n_ct], a[n_rows + n_ct:]
            return jax.vjp(fn, *r, *p)[1](tuple(ct))

        return tuple(_row_call(bwd_fn, (*args[:n_rows], *cts), args[n_rows:], len(args) - n_rows,
                               name + "_bwd"))

    op.defvjp(fwd, bwd)
    return op


@functools.partial(jax.custom_vjp, nondiff_argnums=(1,))
def _swap_lanes(x, k):
    n = x.shape[-1]
    lane = lax.broadcasted_iota(jnp.int32, x.shape, x.ndim - 1)
    return jnp.where((lane & k) == 0, pltpu.roll(x, n - k, x.ndim - 1), pltpu.roll(x, k, x.ndim - 1))


def _swap_lanes_fwd(x, k):
    return _swap_lanes(x, k), None


def _swap_lanes_bwd(k, _, g):
    return (_swap_lanes(g, k),)


_swap_lanes.defvjp(_swap_lanes_fwd, _swap_lanes_bwd)


def _head_sum(x):
    s = x
    k = 1
    while k < HEAD_DIM:
        s = s + _swap_lanes(s, k)
        k *= 2
    return s


def _rms(x):
    return x * lax.rsqrt(jnp.mean(x * x, axis=-1, keepdims=True) + EPS)


def _norm_mod_fn(x, g, shift, scale):
    return ((_rms(x) * g) * (1.0 + scale) + shift,)


def _swiglu_fn(u):
    gate, up = u[:, :D_FF], u[:, D_FF:]
    return (jax.nn.silu(gate) * up,)


def _resid_fn(coef, x, y, gate):
    return (x + (coef * gate) * y,)


def _scale_fn(y, s):
    return (y * s,)


def _tile_lanes(tab, width):
    return tab if tab.shape[1] == width else jnp.concatenate([tab] * (width // tab.shape[1]), axis=1)


def _rope_fn(x, cos, sin):
    w = x.shape[1]
    return (x * _tile_lanes(cos, w) + _swap_lanes(x, 16) * _tile_lanes(sin, w),)


def _head_norm(x, g):
    ms = _head_sum(x * x) * (1.0 / HEAD_DIM)
    return x * lax.rsqrt(ms + EPS) * g


def _norm_rope_fn(x, cos, sin, g):
    return _rope_fn(_head_norm(x, g), cos, sin)


def _head_norm_fn(x, g):
    return (_head_norm(x, g),)


def _merge_fn(gl, z0, z1, z2, z3):
    zs = (z0, z1, z2, z3)
    terms = [jax.nn.sigmoid(gl[:, k * D_MODEL:(k + 1) * D_MODEL].astype(F32)) * zs[k].astype(F32)
             for k in range(N_BRANCH)]
    return (sum(terms[1:], terms[0]),)


def _s5_pre_fn(y0r, y0i, y1r, y1i, u, d):
    return (jax.nn.gelu(((y0r - y0i) + (y1r - y1i)) + d * u),)


def _glu_fn(z):
    return (z[:, :BRANCH_W] * jax.nn.sigmoid(z[:, BRANCH_W:]),)


def _silu_fn(x):
    return (jax.nn.silu(x),)


def _loss_fn(x, tgt, g):
    err = jnp.square(_rms(x) * g - tgt)
    return (0.5 * jnp.mean(err, axis=-1, keepdims=True),)


norm_mod = rowwise(_norm_mod_fn, 1, "norm_mod")
resid_full = rowwise(functools.partial(_resid_fn, 1.0), 2, "resid_full")
scale_rows = rowwise(_scale_fn, 1, "pool_scale")
rope = rowwise(_rope_fn, 3, "rope")
norm_rope = rowwise(_norm_rope_fn, 3, "norm_rope")
head_norm = rowwise(_head_norm_fn, 1, "head_norm")
merge = rowwise(_merge_fn, 5, "merge")
s5_pre = rowwise(_s5_pre_fn, 5, "s5_pre")
glu = rowwise(_glu_fn, 1, "glu")
silu_rows = rowwise(_silu_fn, 1, "silu")
loss_rows = rowwise(_loss_fn, 2, "loss_head")


POOL_HALO = 16


def _pool_call(xa, adjoint, name):
    n, width = xa.shape
    tm = _tile(n, 512, POOL_HALO)
    halo_blocks = tm // POOL_HALO
    last_halo = n // POOL_HALO - 1
    ext_rows = tm + 2 * POOL_HALO

    def body(prev_ref, cur_ref, next_ref, o_ref, ext_ref):
        i = pl.program_id(0)
        ext_ref[0:POOL_HALO] = prev_ref[...]
        ext_ref[POOL_HALO:POOL_HALO + tm] = cur_ref[...]
        ext_ref[POOL_HALO + tm:ext_rows] = next_ref[...]
        e = ext_ref[...]
        row = lax.broadcasted_iota(jnp.int32, e.shape, 0) + (i * tm - POOL_HALO)
        grp = lax.broadcasted_iota(jnp.int32, e.shape, 1) // POOL_GROUP
        win = jnp.where(grp == 0, POOL_WINDOWS[0],
                        jnp.where(grp == 1, POOL_WINDOWS[1], jnp.where(grp == 2, POOL_WINDOWS[2], POOL_WINDOWS[3])))
        valid = (row >= 0) & (row < n)
        lo = jnp.clip(row - win // 2, 0, n)
        hi = jnp.clip(row - win // 2 + win, 0, n)
        cnt = jnp.maximum((hi - lo).astype(F32), 1.0)
        e0 = jnp.where(valid, e / cnt if adjoint else e, 0.0)

        def shift(z, s):
            return pltpu.roll(z, s % ext_rows, 0)

        s2 = e0 + shift(e0, -1 if adjoint else 1)
        s4 = shift(s2, 1) + shift(s2, -1)
        s8 = shift(s4, 2) + shift(s4, -2)
        s16 = shift(s8, 4) + shift(s8, -4)
        s = jnp.where(grp == 0, s2, jnp.where(grp == 1, s4, jnp.where(grp == 2, s8, s16)))
        out = (s - e) if adjoint else (s / cnt - e)
        o_ref[...] = out[POOL_HALO:POOL_HALO + tm]

    return pl.pallas_call(
        body, name=name, grid=(n // tm,),
        in_specs=[pl.BlockSpec((POOL_HALO, width), lambda i: (jnp.maximum(i * halo_blocks - 1, 0), 0)),
                  pl.BlockSpec((tm, width), lambda i: (i, 0)),
                  pl.BlockSpec((POOL_HALO, width), lambda i: (jnp.minimum((i + 1) * halo_blocks, last_halo), 0))],
        out_specs=pl.BlockSpec((tm, width), lambda i: (i, 0)),
        out_shape=jax.ShapeDtypeStruct((n, width), F32),
        scratch_shapes=[pltpu.VMEM((ext_rows, width), F32)],
        compiler_params=_cparams(("parallel",), 4 * 4 * ext_rows * width),
    )(xa, xa, xa)


@jax.custom_vjp
def pool_diff(xa):
    return _pool_call(xa, False, "pool_fwd")


pool_diff.defvjp(lambda xa: (pool_diff(xa), None), lambda _, g: (_pool_call(g, True, "pool_bwd"),))


ATT_SCALE = HEAD_DIM ** -0.5
ATT_BQ = 512


def _qk_scores(q, k):
    return lax.dot_general((q * ATT_SCALE).astype(BF16), k.astype(BF16), (((1,), (1,)), ((), ())),
                           preferred_element_type=F32)


def _sink_rows(sink_ref, h, bq):
    r = lax.broadcasted_iota(jnp.int32, (Q_PER_KV * bq, 1), 0)
    return jnp.where(r < bq, sink_ref[h * Q_PER_KV], sink_ref[h * Q_PER_KV + 1])


ATT_SUB_ROWS = 256


def _flash_fwd(q, klt, vl, kct, vc, sink):
    kvh, g, t, dh = q.shape
    c = kct.shape[2]
    has_lat = klt is not None
    has_sink = sink is not None
    bq = _tile(t, ATT_BQ, LANES)
    bk = _tile(t, 4096, LANES)
    nkv = t // bk if has_lat else 1
    rows = g * bq
    sub = min(ATT_SUB_ROWS, rows)

    def body(*refs):
        it = iter(refs)
        q_ref, kc_ref, vc_ref = next(it), next(it), next(it)
        kl_ref, vl_ref = (next(it), next(it)) if has_lat else (None, None)
        sink_ref = next(it) if has_sink else None
        o_ref, lse_ref, m_ref, l_ref, acc_ref = next(it), next(it), next(it), next(it), next(it)
        h, kj = pl.program_id(0), pl.program_id(2)
        qv = (q_ref[0].reshape(rows, dh) * ATT_SCALE).astype(BF16)

        def part(kt_ref, v_ref):
            kt, v = kt_ref[0].astype(BF16), v_ref[0].astype(BF16)
            m_all, l_all, acc_all = m_ref[...], l_ref[...], acc_ref[...]
            m_out, l_out, acc_out = [], [], []
            for r0 in range(0, rows, sub):
                s = jnp.dot(qv[r0:r0 + sub], kt, preferred_element_type=F32)
                m_old = m_all[r0:r0 + sub]
                m_new = jnp.maximum(m_old, jnp.max(s, axis=-1, keepdims=True))
                p = jnp.exp(s - m_new)
                alpha = jnp.exp(m_old - m_new)
                m_out.append(m_new)
                l_out.append(alpha * l_all[r0:r0 + sub] + jnp.sum(p, axis=-1, keepdims=True))
                acc_out.append(alpha * acc_all[r0:r0 + sub]
                               + jnp.dot(p.astype(BF16), v, preferred_element_type=F32))
            m_ref[...] = jnp.concatenate(m_out, axis=0)
            l_ref[...] = jnp.concatenate(l_out, axis=0)
            acc_ref[...] = jnp.concatenate(acc_out, axis=0)

        @pl.when(kj == 0)
        def _():
            if has_sink:
                m_ref[...] = _sink_rows(sink_ref, h, bq)
                l_ref[...] = jnp.ones_like(l_ref)
            else:
                m_ref[...] = jnp.full_like(m_ref, NEG_BIG)
                l_ref[...] = jnp.zeros_like(l_ref)
            acc_ref[...] = jnp.zeros_like(acc_ref)
            part(kc_ref, vc_ref)

        if has_lat:
            part(kl_ref, vl_ref)

        @pl.when(kj == nkv - 1)
        def _():
            o_ref[0] = (acc_ref[...] / l_ref[...]).reshape(g, bq, dh)
            lse_ref[0] = (m_ref[...] + jnp.log(l_ref[...])).reshape(g, bq, 1)

    q_spec = pl.BlockSpec((1, g, bq, dh), lambda h, i, j: (h, 0, i, 0))
    r_spec = pl.BlockSpec((1, g, bq, 1), lambda h, i, j: (h, 0, i, 0))
    c_spec = pl.BlockSpec((1, c, dh), lambda h, i, j: (h, 0, 0))
    ct_spec = pl.BlockSpec((1, dh, c), lambda h, i, j: (h, 0, 0))
    in_specs, args = [q_spec, ct_spec, c_spec], [q, kct, vc]
    if has_lat:
        in_specs += [pl.BlockSpec((1, dh, bk), lambda h, i, j: (h, 0, j)),
                     pl.BlockSpec((1, bk, dh), lambda h, i, j: (h, j, 0))]
        args += [klt, vl]
    if has_sink:
        in_specs.append(pl.BlockSpec(memory_space=pltpu.SMEM))
        args.append(sink)
    return pl.pallas_call(
        body, name="attn_fwd", grid=(kvh, t // bq, nkv),
        in_specs=in_specs, out_specs=[q_spec, r_spec],
        out_shape=[jax.ShapeDtypeStruct(q.shape, F32), jax.ShapeDtypeStruct((kvh, g, t, 1), F32)],
        scratch_shapes=[pltpu.VMEM((rows, 1), F32), pltpu.VMEM((rows, 1), F32), pltpu.VMEM((rows, dh), F32)],
        compiler_params=_cparams(("parallel", "parallel", "arbitrary"), 4 * 4 * rows * max(bk, c)),
    )(*args)


def _ctx_dq(q, kc, vc, sink, o, do, lse):
    kvh, g, t, dh = q.shape
    c = kc.shape[1]
    has_sink = sink is not None
    bq = _tile(t, ATT_BQ, LANES)
    rows = g * bq

    def body(*refs):
        q_ref, o_ref, do_ref, lse_ref, kc_ref, vc_ref = refs[:6]
        sink_ref = refs[6] if has_sink else None
        dq_ref, delta_ref, dsink_ref = refs[-3:]
        dov = do_ref[0].reshape(rows, dh)
        lse_v = lse_ref[0].reshape(rows, 1)
        delta = jnp.sum(o_ref[0].reshape(rows, dh) * dov, axis=-1, keepdims=True)
        p = jnp.exp(_qk_scores(q_ref[0].reshape(rows, dh), kc_ref[0]) - lse_v)
        dp = lax.dot_general(dov.astype(BF16), vc_ref[0].astype(BF16), (((1,), (1,)), ((), ())),
                             preferred_element_type=F32)
        ds = (p * (dp - delta)).astype(BF16)
        dq = jnp.dot(ds, kc_ref[0].astype(BF16), preferred_element_type=F32) * ATT_SCALE
        dq_ref[0] = dq.reshape(g, bq, dh)
        delta_ref[0] = delta.reshape(g, bq, 1)
        if has_sink:
            p_sink = jnp.exp(_sink_rows(sink_ref, pl.program_id(0), bq) - lse_v)
            dsink_ref[0] = (-p_sink * delta).reshape(g, bq, 1)
        else:
            dsink_ref[0] = jnp.zeros((g, bq, 1), F32)

    q_spec = pl.BlockSpec((1, g, bq, dh), lambda h, i: (h, 0, i, 0))
    r_spec = pl.BlockSpec((1, g, bq, 1), lambda h, i: (h, 0, i, 0))
    c_spec = pl.BlockSpec((1, c, dh), lambda h, i: (h, 0, 0))
    in_specs, args = [q_spec, q_spec, q_spec, r_spec, c_spec, c_spec], [q, o, do, lse, kc, vc]
    if has_sink:
        in_specs.append(pl.BlockSpec(memory_space=pltpu.SMEM))
        args.append(sink)
    row_shape = jax.ShapeDtypeStruct((kvh, g, t, 1), F32)
    return pl.pallas_call(
        body, name="attn_ctx_dq", grid=(kvh, t // bq),
        in_specs=in_specs, out_specs=[q_spec, r_spec, r_spec],
        out_shape=[jax.ShapeDtypeStruct(q.shape, F32), row_shape, row_shape],
        compiler_params=_cparams(("parallel", "parallel"), 4 * 6 * rows * c),
    )(*args)


def _flash_dkv(q, do, lse, delta, k, v):
    kvh, g, t, dh = q.shape
    nk_rows = k.shape[1]
    bq = _tile(t, ATT_BQ, LANES)
    bk = _tile(nk_rows, 1024, LANES)
    nq = t // bq
    rows = g * bq

    def body(q_ref, do_ref, lse_ref, delta_ref, k_ref, v_ref, dk_ref, dv_ref, dk_acc, dv_acc):
        qj = pl.program_id(2)
        qv = q_ref[0].reshape(rows, dh)
        dov = do_ref[0].reshape(rows, dh)

        @pl.when(qj == 0)
        def _():
            dk_acc[...] = jnp.zeros_like(dk_acc)
            dv_acc[...] = jnp.zeros_like(dv_acc)

        s = _qk_scores(qv, k_ref[0])
        p = jnp.exp(s - lse_ref[0].reshape(rows, 1))
        dp = lax.dot_general(dov.astype(BF16), v_ref[0].astype(BF16), (((1,), (1,)), ((), ())),
                             preferred_element_type=F32)
        ds = p * (dp - delta_ref[0].reshape(rows, 1))
        tn = (((0,), (0,)), ((), ()))
        dv_acc[...] += lax.dot_general(p.astype(BF16), dov.astype(BF16), tn, preferred_element_type=F32)
        dk_acc[...] += lax.dot_general(ds.astype(BF16), qv.astype(BF16), tn, preferred_element_type=F32)

        @pl.when(qj == nq - 1)
        def _():
            dk_ref[0] = dk_acc[...] * ATT_SCALE
            dv_ref[0] = dv_acc[...]

    q_spec = pl.BlockSpec((1, g, bq, dh), lambda h, i, j: (h, 0, j, 0))
    r_spec = pl.BlockSpec((1, g, bq, 1), lambda h, i, j: (h, 0, j, 0))
    k_spec = pl.BlockSpec((1, bk, dh), lambda h, i, j: (h, i, 0))
    return pl.pallas_call(
        body, name="attn_dkv", grid=(kvh, nk_rows // bk, nq),
        in_specs=[q_spec, q_spec, r_spec, r_spec, k_spec, k_spec], out_specs=[k_spec, k_spec],
        out_shape=[jax.ShapeDtypeStruct(k.shape, F32), jax.ShapeDtypeStruct(k.shape, F32)],
        scratch_shapes=[pltpu.VMEM((bk, dh), F32), pltpu.VMEM((bk, dh), F32)],
        compiler_params=_cparams(("parallel", "parallel", "arbitrary"), 4 * 6 * rows * bk),
    )(q, do, lse, delta, k, v)


def _flash_bwd_full(q, kl, vl, kc, vc, o, do, lse):
    kvh, g, t, dh = q.shape
    c = kc.shape[1]
    bq, bk = _tile(t, 512, LANES), _tile(t, 1024, LANES)
    nq, nkv = t // bq, t // bk
    rows = g * bq
    klt, vlt = jnp.swapaxes(kl, 1, 2), jnp.swapaxes(vl, 1, 2)
    kct, vct = jnp.swapaxes(kc, 1, 2), jnp.swapaxes(vc, 1, 2)
    tn = (((0,), (0,)), ((), ()))

    def body(q_ref, o_ref, do_ref, lse_ref, kc_ref, kct_ref, vct_ref, kl_ref, klt_ref, vlt_ref,
             dq_ref, delta_ref, dk_hbm, dv_hbm, dq_acc, dl_ref, dk_acc, dv_acc):
        h, qi, kj = pl.program_id(0), pl.program_id(1), pl.program_id(2)
        q_raw = q_ref[0].reshape(rows, dh).astype(BF16)
        qv = (q_ref[0].reshape(rows, dh) * ATT_SCALE).astype(BF16)
        dov = do_ref[0].reshape(rows, dh).astype(BF16)
        lse_v = lse_ref[0].reshape(rows, 1)

        def tile(kt, vt):
            s = jnp.dot(qv, kt.astype(BF16), preferred_element_type=F32)
            p = jnp.exp(s - lse_v)
            dp = jnp.dot(dov, vt.astype(BF16), preferred_element_type=F32)
            return p, (p * (dp - dl_ref[...])).astype(BF16)

        @pl.when((qi == 0) & (kj == 0))
        def _():
            dk_acc[...] = jnp.zeros_like(dk_acc)
            dv_acc[...] = jnp.zeros_like(dv_acc)

        @pl.when(kj == 0)
        def _():
            dl_ref[...] = jnp.sum(o_ref[0].reshape(rows, dh) * do_ref[0].reshape(rows, dh),
                                  axis=-1, keepdims=True)
            _, ds = tile(kct_ref[0], vct_ref[0])
            dq_acc[...] = jnp.dot(ds, kc_ref[0].astype(BF16), preferred_element_type=F32)

        p, ds = tile(klt_ref[0], vlt_ref[0])
        dq_acc[...] += jnp.dot(ds, kl_ref[0].astype(BF16), preferred_element_type=F32)
        ks = pl.ds(pl.multiple_of(kj * bk, bk), bk)
        dv_acc[ks] += lax.dot_general(p.astype(BF16), dov, tn, preferred_element_type=F32)
        dk_acc[ks] += lax.dot_general(ds, q_raw, tn, preferred_element_type=F32)

        @pl.when(kj == nkv - 1)
        def _():
            dq_ref[0] = (dq_acc[...] * ATT_SCALE).reshape(g, bq, dh)
            delta_ref[0] = dl_ref[...].reshape(g, bq, 1)

        @pl.when((qi == nq - 1) & (kj == nkv - 1))
        def _():
            dk_acc[...] = dk_acc[...] * ATT_SCALE
            pltpu.sync_copy(dk_acc, dk_hbm.at[h])
            pltpu.sync_copy(dv_acc, dv_hbm.at[h])

    q_spec = pl.BlockSpec((1, g, bq, dh), lambda h, i, j: (h, 0, i, 0))
    r_spec = pl.BlockSpec((1, g, bq, 1), lambda h, i, j: (h, 0, i, 0))
    c_spec = pl.BlockSpec((1, c, dh), lambda h, i, j: (h, 0, 0))
    ct_spec = pl.BlockSpec((1, dh, c), lambda h, i, j: (h, 0, 0))
    l_spec = pl.BlockSpec((1, bk, dh), lambda h, i, j: (h, j, 0))
    lt_spec = pl.BlockSpec((1, dh, bk), lambda h, i, j: (h, 0, j))
    any_spec = pl.BlockSpec(memory_space=pl.ANY)
    kv_shape = jax.ShapeDtypeStruct((kvh, t, dh), F32)
    return pl.pallas_call(
        body, name="attn_bwd_full", grid=(kvh, nq, nkv),
        in_specs=[q_spec, q_spec, q_spec, r_spec, c_spec, ct_spec, ct_spec, l_spec, lt_spec, lt_spec],
        out_specs=[q_spec, r_spec, any_spec, any_spec],
        out_shape=[jax.ShapeDtypeStruct(q.shape, F32), jax.ShapeDtypeStruct((kvh, g, t, 1), F32),
                   kv_shape, kv_shape],
        scratch_shapes=[pltpu.VMEM((rows, dh), F32), pltpu.VMEM((rows, 1), F32),
                        pltpu.VMEM((t, dh), F32), pltpu.VMEM((t, dh), F32)],
        compiler_params=_cparams(("arbitrary", "arbitrary", "arbitrary"), 4 * 2 * rows * bk,
                                 2 * 4 * t * LANES),
    )(q, o, do, lse, kc, kct, vct, kl, klt, vlt)


BAND_BQ = 512


def _band_specs(t, shape_of):
    per = BAND_BQ // WINDOW
    n_halo = t // WINDOW

    def spec(n, index):
        shape, axis = shape_of(n)

        def index_map(h, i):
            idx = [h] + [0] * (len(shape) - 1)
            idx[axis] = index(i)
            return tuple(idx)

        return pl.BlockSpec(shape, index_map)

    return [spec(WINDOW, lambda i: jnp.maximum(i * per - 1, 0)),
            spec(BAND_BQ, lambda i: i),
            spec(WINDOW, lambda i: jnp.minimum((i + 1) * per, n_halo - 1))]


def _band_visible(i, t, c, rows):
    cols = c + BAND_BQ + 2 * WINDOW
    col = lax.broadcasted_iota(jnp.int32, (rows, cols), 1)
    qpos = i * BAND_BQ + lax.broadcasted_iota(jnp.int32, (rows, cols), 0) % BAND_BQ
    kpos = i * BAND_BQ - WINDOW + (col - c)
    return (col < c) | ((kpos >= 0) & (kpos < t) & (jnp.abs(kpos - qpos) <= WINDOW))


def _band_fwd(q, klt, vl, kct, vc, sink):
    kvh, g, t, dh = q.shape
    c = kct.shape[2]
    rows = g * BAND_BQ

    def body(q_ref, kct_ref, vc_ref, ktp, ktc, ktn, vp, vcur, vn, sink_ref, o_ref, lse_ref):
        h, i = pl.program_id(0), pl.program_id(1)
        qv = (q_ref[0].reshape(rows, dh) * ATT_SCALE).astype(BF16)
        kt = jnp.concatenate([kct_ref[0], ktp[0], ktc[0], ktn[0]], axis=1).astype(BF16)
        v = jnp.concatenate([vc_ref[0], vp[0], vcur[0], vn[0]], axis=0).astype(BF16)
        s = jnp.where(_band_visible(i, t, c, rows), jnp.dot(qv, kt, preferred_element_type=F32), NEG_BIG)
        sink_r = _sink_rows(sink_ref, h, BAND_BQ)
        m = jnp.maximum(sink_r, jnp.max(s, axis=-1, keepdims=True))
        p = jnp.exp(s - m)
        l = jnp.exp(sink_r - m) + jnp.sum(p, axis=-1, keepdims=True)
        o_ref[0] = (jnp.dot(p.astype(BF16), v, preferred_element_type=F32) / l).reshape(g, BAND_BQ, dh)
        lse_ref[0] = (m + jnp.log(l)).reshape(g, BAND_BQ, 1)

    q_spec = pl.BlockSpec((1, g, BAND_BQ, dh), lambda h, i: (h, 0, i, 0))
    r_spec = pl.BlockSpec((1, g, BAND_BQ, 1), lambda h, i: (h, 0, i, 0))
    in_specs = ([q_spec, pl.BlockSpec((1, dh, c), lambda h, i: (h, 0, 0)),
                 pl.BlockSpec((1, c, dh), lambda h, i: (h, 0, 0))]
                + _band_specs(t,lambda n: ((1, dh, n), 2))
                + _band_specs(t,lambda n: ((1, n, dh), 1))
                + [pl.BlockSpec(memory_space=pltpu.SMEM)])
    return pl.pallas_call(
        body, name="attn_band_fwd", grid=(kvh, t // BAND_BQ), in_specs=in_specs, out_specs=[q_spec, r_spec],
        out_shape=[jax.ShapeDtypeStruct(q.shape, F32), jax.ShapeDtypeStruct((kvh, g, t, 1), F32)],
        compiler_params=_cparams(("parallel", "parallel"), 4 * 2 * rows * (c + BAND_BQ + 2 * WINDOW)),
    )(q, kct, vc, klt, klt, klt, vl, vl, vl, sink)


def _band_dq(q, kl, klt, vlt, kc, kct, vct, sink, o, do, lse):
    kvh, g, t, dh = q.shape
    c = kc.shape[1]
    rows = g * BAND_BQ

    def body(q_ref, o_ref, do_ref, lse_ref, kc_ref, kct_ref, vct_ref, kp, kcur, kn, ktp, ktc, ktn,
             vtp, vtc, vtn, sink_ref, dq_ref, delta_ref, dsink_ref):
        h, i = pl.program_id(0), pl.program_id(1)
        qv = (q_ref[0].reshape(rows, dh) * ATT_SCALE).astype(BF16)
        dov = do_ref[0].reshape(rows, dh)
        lse_v = lse_ref[0].reshape(rows, 1)
        kt = jnp.concatenate([kct_ref[0], ktp[0], ktc[0], ktn[0]], axis=1).astype(BF16)
        vt = jnp.concatenate([vct_ref[0], vtp[0], vtc[0], vtn[0]], axis=1).astype(BF16)
        k = jnp.concatenate([kc_ref[0], kp[0], kcur[0], kn[0]], axis=0).astype(BF16)
        s = jnp.where(_band_visible(i, t, c, rows), jnp.dot(qv, kt, preferred_element_type=F32), NEG_BIG)
        p = jnp.exp(s - lse_v)
        delta = jnp.sum(o_ref[0].reshape(rows, dh) * dov, axis=-1, keepdims=True)
        dp = jnp.dot(dov.astype(BF16), vt, preferred_element_type=F32)
        ds = (p * (dp - delta)).astype(BF16)
        dq_ref[0] = (jnp.dot(ds, k, preferred_element_type=F32) * ATT_SCALE).reshape(g, BAND_BQ, dh)
        delta_ref[0] = delta.reshape(g, BAND_BQ, 1)
        p_sink = jnp.exp(_sink_rows(sink_ref, h, BAND_BQ) - lse_v)
        dsink_ref[0] = (-p_sink * delta).reshape(g, BAND_BQ, 1)

    q_spec = pl.BlockSpec((1, g, BAND_BQ, dh), lambda h, i: (h, 0, i, 0))
    r_spec = pl.BlockSpec((1, g, BAND_BQ, 1), lambda h, i: (h, 0, i, 0))
    ct_spec = pl.BlockSpec((1, dh, c), lambda h, i: (h, 0, 0))
    rows_of = lambda n: ((1, n, dh), 1)
    lanes_of = lambda n: ((1, dh, n), 2)
    in_specs = ([q_spec, q_spec, q_spec, r_spec, pl.BlockSpec((1, c, dh), lambda h, i: (h, 0, 0)), ct_spec, ct_spec]
                + _band_specs(t,rows_of) + _band_specs(t,lanes_of) + _band_specs(t,lanes_of)
                + [pl.BlockSpec(memory_space=pltpu.SMEM)])
    row_shape = jax.ShapeDtypeStruct((kvh, g, t, 1), F32)
    return pl.pallas_call(
        body, name="attn_band_dq", grid=(kvh, t // BAND_BQ), in_specs=in_specs,
        out_specs=[q_spec, r_spec, r_spec], out_shape=[jax.ShapeDtypeStruct(q.shape, F32), row_shape, row_shape],
        compiler_params=_cparams(("parallel", "parallel"), 4 * 3 * rows * (c + BAND_BQ + 2 * WINDOW)),
    )(q, o, do, lse, kc, kct, vct, kl, kl, kl, klt, klt, klt, vlt, vlt, vlt, sink)


def _band_dkv(q, do, lse, delta, klt, vlt):
    kvh, g, t, dh = q.shape
    span = BAND_BQ + 2 * WINDOW
    rows = g * span
    tn = (((0,), (0,)), ((), ()))

    def body(qp, qc, qn, dop, doc, don, lp, lc, ln, dp_, dc_, dn_, kt_ref, vt_ref, dk_ref, dv_ref):
        j = pl.program_id(1)

        def stack(a, b, c_):
            return jnp.concatenate([jnp.concatenate([a[0, gi], b[0, gi], c_[0, gi]], axis=0)
                                    for gi in range(g)], axis=0)

        q_all, do_all = stack(qp, qc, qn), stack(dop, doc, don).astype(BF16)
        lse_all, delta_all = stack(lp, lc, ln), stack(dp_, dc_, dn_)
        s = jnp.dot((q_all * ATT_SCALE).astype(BF16), kt_ref[0].astype(BF16), preferred_element_type=F32)
        qpos = j * BAND_BQ - WINDOW + lax.broadcasted_iota(jnp.int32, (rows, BAND_BQ), 0) % span
        kpos = j * BAND_BQ + lax.broadcasted_iota(jnp.int32, (rows, BAND_BQ), 1)
        s = jnp.where((qpos >= 0) & (qpos < t) & (jnp.abs(kpos - qpos) <= WINDOW), s, NEG_BIG)
        p = jnp.exp(s - lse_all)
        dp = jnp.dot(do_all, vt_ref[0].astype(BF16), preferred_element_type=F32)
        ds = (p * (dp - delta_all)).astype(BF16)
        dv_ref[0] = lax.dot_general(p.astype(BF16), do_all, tn, preferred_element_type=F32)
        dk_ref[0] = lax.dot_general(ds, q_all.astype(BF16), tn, preferred_element_type=F32) * ATT_SCALE

    q_specs = _band_specs(t,lambda n: ((1, g, n, dh), 2))
    r_specs = _band_specs(t,lambda n: ((1, g, n, 1), 2))
    kt_spec = pl.BlockSpec((1, dh, BAND_BQ), lambda h, j: (h, 0, j))
    k_spec = pl.BlockSpec((1, BAND_BQ, dh), lambda h, j: (h, j, 0))
    kv_shape = jax.ShapeDtypeStruct((kvh, t, dh), F32)
    return pl.pallas_call(
        body, name="attn_band_dkv", grid=(kvh, t // BAND_BQ),
        in_specs=q_specs + q_specs + r_specs + r_specs + [kt_spec, kt_spec], out_specs=[k_spec, k_spec],
        out_shape=[kv_shape, kv_shape],
        compiler_params=_cparams(("parallel", "parallel"), 4 * 3 * rows * BAND_BQ),
    )(q, q, q, do, do, do, lse, lse, lse, delta, delta, delta, klt, vlt)


def make_attention(kind, has_sink):
    has_lat = kind != "ctx"

    def unpack(args):
        it = iter(args)
        q, kc, vc = next(it), next(it), next(it)
        kl, vl = (next(it), next(it)) if has_lat else (None, None)
        sink = next(it) if has_sink else None
        return q, kl, vl, kc, vc, sink

    def forward(args):
        q, kl, vl, kc, vc, sink = unpack(args)
        klt = jnp.swapaxes(kl, 1, 2) if has_lat else None
        if kind == "window":
            return _band_fwd(q, klt, vl, jnp.swapaxes(kc, 1, 2), vc, sink)
        return _flash_fwd(q, klt, vl, jnp.swapaxes(kc, 1, 2), vc, sink)

    @jax.custom_vjp
    def op(*args):
        return forward(args)[0]

    def fwd(*args):
        o, lse = forward(args)
        return o, (args, o, lse)

    def bwd(res, do):
        args, o, lse = res
        q, kl, vl, kc, vc, sink = unpack(args)
        if kind == "global":
            dq, delta, dkl, dvl = _flash_bwd_full(q, kl, vl, kc, vc, o, do, lse)
            grads = [dq, *_flash_dkv(q, do, lse, delta, kc, vc), dkl, dvl]
        elif kind == "window":
            klt, vlt = jnp.swapaxes(kl, 1, 2), jnp.swapaxes(vl, 1, 2)
            dq, delta, dsink_rows = _band_dq(q, kl, klt, vlt, kc, jnp.swapaxes(kc, 1, 2),
                                             jnp.swapaxes(vc, 1, 2), sink, o, do, lse)
            grads = [dq, *_flash_dkv(q, do, lse, delta, kc, vc), *_band_dkv(q, do, lse, delta, klt, vlt)]
        else:
            dq, delta, dsink_rows = _ctx_dq(q, kc, vc, sink, o, do, lse)
            grads = [dq, *_flash_dkv(q, do, lse, delta, kc, vc)]
        if has_sink:
            grads.append(jnp.sum(dsink_rows, axis=(2, 3)).reshape(-1))
        return tuple(grads)

    op.defvjp(fwd, bwd)
    return op


attn_window = make_attention("window", True)
attn_global = make_attention("global", False)
attn_ctx_sink = make_attention("ctx", True)
attn_ctx = make_attention("ctx", False)


def _to_heads(z, n_heads):
    return z.reshape(z.shape[0], n_heads, HEAD_DIM).transpose(1, 0, 2)


def _q_heads(z):
    return z.reshape(z.shape[0], KV_HEADS, Q_PER_KV, HEAD_DIM).transpose(1, 2, 0, 3)


def _from_q_heads(o):
    return o.transpose(2, 0, 1, 3).reshape(o.shape[2], KV_HEADS * Q_PER_KV * HEAD_DIM)


def _scan_tile(t):
    return _tile(t, 512, SUBLANES)


def _scan_fwd_call(bre, bim, lre, lim, h0re, h0im, rev):
    t = bre.shape[0]
    tt = _scan_tile(t)
    nb = t // tt
    plane = bre.shape[1:]

    def body(bre_ref, bim_ref, lre_ref, lim_ref, h0re_ref, h0im_ref, sre_ref, sim_ref, h_ref):
        @pl.when(pl.program_id(0) == 0)
        def _():
            h_ref[0] = h0re_ref[...]
            h_ref[1] = h0im_ref[...]

        ar, ai = lre_ref[...], lim_ref[...]

        def step(j, carry):
            hr, hi = carry
            tj = (tt - 1 - j) if rev else j
            nr = ar * hr - ai * hi + bre_ref[tj]
            ni = ar * hi + ai * hr + bim_ref[tj]
            sre_ref[tj] = nr
            sim_ref[tj] = ni
            return nr, ni

        hr, hi = lax.fori_loop(0, tt, step, (h_ref[0], h_ref[1]), unroll=8)
        h_ref[0] = hr
        h_ref[1] = hi

    blk = pl.BlockSpec((tt,) + plane, (lambda i: (nb - 1 - i, 0, 0)) if rev else (lambda i: (i, 0, 0)))
    par = pl.BlockSpec(plane, lambda i: (0, 0))
    return pl.pallas_call(
        body, name="s5_scan_fwd", grid=(nb,), in_specs=[blk, blk, par, par, par, par], out_specs=[blk, blk],
        out_shape=[jax.ShapeDtypeStruct(bre.shape, F32)] * 2,
        scratch_shapes=[pltpu.VMEM((2,) + plane, F32)],
        compiler_params=_cparams(("arbitrary",), 4 * 4 * tt * plane[0] * plane[1]),
    )(bre, bim, lre, lim, h0re, h0im)


def _scan_bwd_call(gre, gim, sre, sim, lre, lim, h0re, h0im, rev):
    t = gre.shape[0]
    tt = _scan_tile(t)
    nb = t // tt
    plane = gre.shape[1:]
    down = not rev

    def body(gre_ref, gim_ref, sre_ref, sim_ref, lre_ref, lim_ref, h0re_ref, h0im_ref,
             dbre_ref, dbim_ref, dare_ref, daim_ref, dh0re_ref, dh0im_ref, carry_ref):
        i = pl.program_id(0)

        @pl.when(i == 0)
        def _():
            carry_ref[...] = jnp.zeros_like(carry_ref)

        ar, ai = lre_ref[...], lim_ref[...]

        def step(j, carry):
            gr, gi, dar, dai = carry
            tj = (tt - 1 - j) if down else j
            hr, hi = sre_ref[tj], sim_ref[tj]
            dar = dar + hr * gr + hi * gi
            dai = dai + hr * gi - hi * gr
            ngr = gre_ref[tj] + ar * gr + ai * gi
            ngi = gim_ref[tj] + ar * gi - ai * gr
            dbre_ref[tj] = ngr
            dbim_ref[tj] = ngi
            return ngr, ngi, dar, dai

        gr, gi, dar, dai = lax.fori_loop(
            0, tt, step, (carry_ref[0], carry_ref[1], carry_ref[2], carry_ref[3]), unroll=8)
        carry_ref[0] = gr
        carry_ref[1] = gi
        carry_ref[2] = dar
        carry_ref[3] = dai

        @pl.when(i == nb - 1)
        def _():
            hr, hi = h0re_ref[...], h0im_ref[...]
            dare_ref[...] = dar + hr * gr + hi * gi
            daim_ref[...] = dai + hr * gi - hi * gr
            dh0re_ref[...] = ar * gr + ai * gi
            dh0im_ref[...] = ar * gi - ai * gr

    blk = pl.BlockSpec((tt,) + plane, (lambda i: (nb - 1 - i, 0, 0)) if down else (lambda i: (i, 0, 0)))
    par = pl.BlockSpec(plane, lambda i: (0, 0))
    return pl.pallas_call(
        body, name="s5_scan_bwd", grid=(nb,), in_specs=[blk, blk, blk, blk, par, par, par, par],
        out_specs=[blk, blk, par, par, par, par],
        out_shape=[jax.ShapeDtypeStruct(gre.shape, F32)] * 2 + [jax.ShapeDtypeStruct(plane, F32)] * 4,
        scratch_shapes=[pltpu.VMEM((4,) + plane, F32)],
        compiler_params=_cparams(("arbitrary",), 4 * 6 * tt * plane[0] * plane[1]),
    )(gre, gim, sre, sim, lre, lim, h0re, h0im)


def make_scan(rev):
    @jax.custom_vjp
    def op(bre, bim, lre, lim, h0re, h0im):
        return tuple(_scan_fwd_call(bre, bim, lre, lim, h0re, h0im, rev))

    def fwd(bre, bim, lre, lim, h0re, h0im):
        sre, sim = _scan_fwd_call(bre, bim, lre, lim, h0re, h0im, rev)
        return (sre, sim), (sre, sim, lre, lim, h0re, h0im)

    def bwd(res, cts):
        sre, sim, lre, lim, h0re, h0im = res
        return tuple(_scan_bwd_call(cts[0], cts[1], sre, sim, lre, lim, h0re, h0im, rev))

    op.defvjp(fwd, bwd)
    return op


scan_up = make_scan(False)
scan_down = make_scan(True)


def _adamw_call(parts, w, m, v, name):
    r, c = w.shape
    tr = _tile(r, 256, SUBLANES)
    nparts = parts.shape[0]
    c1 = 1.0 - ADAM_B1 ** ADAM_STEP
    c2 = 1.0 - ADAM_B2 ** ADAM_STEP

    def body(p_ref, w_ref, m_ref, v_ref, g_ref, d_ref, nm_ref, nv_ref):
        g = p_ref[0].astype(F32)
        for s in range(1, nparts):
            g = g + p_ref[s].astype(F32)
        m1 = ADAM_B1 * m_ref[...] + (1.0 - ADAM_B1) * g
        v1 = ADAM_B2 * v_ref[...] + (1.0 - ADAM_B2) * jnp.square(g)
        g_ref[...] = g
        nm_ref[...] = m1
        nv_ref[...] = v1
        d_ref[...] = -ADAM_LR * ((m1 / c1) / (jnp.sqrt(v1 / c2) + ADAM_EPS) + ADAM_WD * w_ref[...])

    blk = pl.BlockSpec((tr, c), lambda i: (i, 0))
    return pl.pallas_call(
        body, name=name, grid=(r // tr,),
        in_specs=[pl.BlockSpec((nparts, tr, c), lambda i: (0, i, 0)), blk, blk, blk], out_specs=[blk] * 4,
        out_shape=[jax.ShapeDtypeStruct((r, c), F32)] * 4,
        compiler_params=_cparams(("parallel",), 4 * tr * c * (nparts + 7)),
    )(parts, w, m, v)


def _peer(k):
    x, y, c = lax.axis_index("x"), lax.axis_index("y"), lax.axis_index("c")
    px = 1 - x if k & 4 else x
    py = 1 - y if k & 2 else y
    pc = 1 - c if k & 1 else c
    return (px, py, pc), 4 * px + 2 * py + pc


def _my_slot():
    return 4 * lax.axis_index("x") + 2 * lax.axis_index("y") + lax.axis_index("c")


def all_gather(x, name):
    def body(x_ref, out_ref, send_sems, recv_sems, local_sem):
        me = _my_slot()
        mine = pltpu.make_async_copy(x_ref, out_ref.at[me], local_sem)
        mine.start()
        sends = []
        for k in range(1, N_DEV):
            peer, _ = _peer(k)
            cp = pltpu.make_async_remote_copy(
                src_ref=x_ref, dst_ref=out_ref.at[me], send_sem=send_sems.at[k - 1],
                recv_sem=recv_sems.at[k - 1], device_id=peer, device_id_type=MESH)
            cp.start()
            sends.append(cp)
        for k in range(1, N_DEV):
            peer, slot = _peer(k)
            pltpu.make_async_remote_copy(
                src_ref=x_ref, dst_ref=out_ref.at[slot], send_sem=send_sems.at[k - 1],
                recv_sem=recv_sems.at[k - 1], device_id=peer, device_id_type=MESH).wait_recv()
        for cp in sends:
            cp.wait_send()
        mine.wait()

    return pl.pallas_call(
        body, name=name,
        in_specs=[pl.BlockSpec(memory_space=pl.ANY)], out_specs=pl.BlockSpec(memory_space=pl.ANY),
        out_shape=jax.ShapeDtypeStruct((N_DEV,) + tuple(x.shape), x.dtype),
        scratch_shapes=[pltpu.SemaphoreType.DMA((N_DEV - 1,)), pltpu.SemaphoreType.DMA((N_DEV - 1,)),
                        pltpu.SemaphoreType.DMA],
    )(x)


def sibling_swap(x, name):
    def body(x_ref, out_ref, send_sem, recv_sem):
        x_, y_, c_ = lax.axis_index("x"), lax.axis_index("y"), lax.axis_index("c")
        cp = pltpu.make_async_remote_copy(
            src_ref=x_ref.at[1 - c_], dst_ref=out_ref, send_sem=send_sem, recv_sem=recv_sem,
            device_id=(x_, y_, 1 - c_), device_id_type=MESH)
        cp.start()
        cp.wait()

    return pl.pallas_call(
        body, name=name,
        in_specs=[pl.BlockSpec(memory_space=pl.ANY)], out_specs=pl.BlockSpec(memory_space=pl.ANY),
        out_shape=jax.ShapeDtypeStruct(x.shape[1:], x.dtype),
        scratch_shapes=[pltpu.SemaphoreType.DMA, pltpu.SemaphoreType.DMA],
    )(x)


def chip_exchange(x, name):
    def body(x_ref, out_ref, send_sems, recv_sems, local_sem):
        x_, y_, c_ = lax.axis_index("x"), lax.axis_index("y"), lax.axis_index("c")
        mine = 2 * x_ + y_
        local = pltpu.make_async_copy(x_ref.at[mine], out_ref.at[mine], local_sem)
        local.start()

        def copy(k):
            px = 1 - x_ if k & 2 else x_
            py = 1 - y_ if k & 1 else y_
            peer = 2 * px + py
            send = pltpu.make_async_remote_copy(
                src_ref=x_ref.at[peer], dst_ref=out_ref.at[mine], send_sem=send_sems.at[k - 1],
                recv_sem=recv_sems.at[k - 1], device_id=(px, py, c_), device_id_type=MESH)
            recv = pltpu.make_async_remote_copy(
                src_ref=x_ref.at[peer], dst_ref=out_ref.at[peer], send_sem=send_sems.at[k - 1],
                recv_sem=recv_sems.at[k - 1], device_id=(px, py, c_), device_id_type=MESH)
            return send, recv

        copies = [copy(k) for k in range(1, 4)]
        for send, _ in copies:
            send.start()
        for _, recv in copies:
            recv.wait_recv()
        for send, _ in copies:
            send.wait_send()
        local.wait()

    return pl.pallas_call(
        body, name=name,
        in_specs=[pl.BlockSpec(memory_space=pl.ANY)], out_specs=pl.BlockSpec(memory_space=pl.ANY),
        out_shape=jax.ShapeDtypeStruct(x.shape, x.dtype),
        scratch_shapes=[pltpu.SemaphoreType.DMA((3,)), pltpu.SemaphoreType.DMA((3,)),
                        pltpu.SemaphoreType.DMA],
    )(x)


def all_gather_two_level(x, name):
    def body(x_ref, out_ref, send_sems, recv_sems, local_sem):
        x_, y_, c_ = lax.axis_index("x"), lax.axis_index("y"), lax.axis_index("c")
        me, sibling = (x_, y_, c_), (x_, y_, 1 - c_)
        chips = [(1 - x_, y_), (x_, 1 - y_), (1 - x_, 1 - y_)]

        def slot(px, py, pc):
            return out_ref.at[4 * px + 2 * py + pc]

        def copy(k, block, to, src=None):
            return pltpu.make_async_remote_copy(
                src_ref=slot(*block) if src is None else src, dst_ref=slot(*block),
                send_sem=send_sems.at[k], recv_sem=recv_sems.at[k], device_id=to, device_id_type=MESH)

        mine = pltpu.make_async_copy(x_ref, slot(*me), local_sem)
        mine.start()
        first = [copy(0, me, sibling, src=x_ref)]
        first += [copy(1 + j, me, (*chip, c_), src=x_ref) for j, chip in enumerate(chips)]
        for cp in first:
            cp.start()
        passed = [copy(4 + j, (*chip, c_), sibling) for j, chip in enumerate(chips)]
        for j, chip in enumerate(chips):
            copy(1 + j, (*chip, c_), me).wait_recv()
            passed[j].start()
        copy(0, sibling, me).wait_recv()
        for j, chip in enumerate(chips):
            copy(4 + j, (*chip, 1 - c_), me).wait_recv()
        for cp in first + passed:
            cp.wait_send()
        mine.wait()

    return pl.pallas_call(
        body, name=name,
        in_specs=[pl.BlockSpec(memory_space=pl.ANY)], out_specs=pl.BlockSpec(memory_space=pl.ANY),
        out_shape=jax.ShapeDtypeStruct((N_DEV,) + tuple(x.shape), x.dtype),
        scratch_shapes=[pltpu.SemaphoreType.DMA((N_DEV - 1,)), pltpu.SemaphoreType.DMA((N_DEV - 1,)),
                        pltpu.SemaphoreType.DMA],
    )(x)


def _rope_tables(t):
    n_freq = HEAD_DIM // 4
    tok = jnp.arange(t)
    inv = ROPE_THETA ** (-jnp.arange(n_freq, dtype=F32) / n_freq)
    a_row = (tok // GRID_W).astype(F32)[:, None] * inv
    a_col = (tok % GRID_W).astype(F32)[:, None] * inv
    cos = jnp.concatenate([jnp.cos(a_row)] * 2 + [jnp.cos(a_col)] * 2, axis=1)
    sin = jnp.concatenate([-jnp.sin(a_row), jnp.sin(a_row), -jnp.sin(a_col), jnp.sin(a_col)], axis=1)
    return jnp.concatenate([cos, cos], axis=1), jnp.concatenate([sin, sin], axis=1)


def _block_diag(blocks):
    g, a, b = blocks.shape
    eye = jnp.eye(g, dtype=blocks.dtype)
    return jnp.einsum("gab,gk->gakb", blocks, eye).reshape(g * a, g * b)


def _ffn_fwd_calls(x, mod, g, w_in, w_out):
    shift, scale, gate = mod[0:1], mod[1:2], mod[2:3]
    h, = _row_call(lambda xt, gt, sh, sc: (_norm_mod_fn(xt, gt, sh, sc)[0].astype(BF16),),
                   [x], [g, shift, scale], 0, "ffn_norm")
    u = _mm(h, w_in, out_dtype=BF16, name="ffn_up")
    a, = _row_call(lambda ut: (_swiglu_fn(ut.astype(F32))[0].astype(BF16),), [u], [], 0, "ffn_act")
    y = _mm(a, w_out, name="ffn_down")
    out, = _row_call(functools.partial(_resid_fn, 0.5), [x, y], [gate], 0, "ffn_resid")
    return out, (h, u, a, y)


@jax.custom_vjp
def _ffn_half(x, mod, g, w_in, w_out):
    return _ffn_fwd_calls(x, mod, g, w_in, w_out)[0]


def _ffn_half_fwd(x, mod, g, w_in, w_out):
    out, saved = _ffn_fwd_calls(x, mod, g, w_in, w_out)
    return out, (x, mod, g, w_in, w_out, saved)


def _ffn_half_bwd(res, dxn):
    x, mod, g, w_in, w_out, (h, u, a, y) = res
    shift, scale, gate = mod[0:1], mod[1:2], mod[2:3]

    def resid_bwd(dt, yt, gt):
        return (0.5 * gt * dt).astype(BF16), 0.5 * jnp.sum(dt * yt, axis=0, keepdims=True)

    dy, dgate = _row_call(resid_bwd, [dxn, y], [gate], 1, "ffn_resid_bwd")
    da = _mm(dy, w_out, tb=True, out_dtype=BF16, name="ffn_down_dx")
    dw_out = _mm(a, dy, ta=True, out_dtype=w_out.dtype, name="ffn_down_dw")

    def act_bwd(ut, dat):
        return (jax.vjp(_swiglu_fn, ut.astype(F32))[1]((dat.astype(F32),))[0].astype(BF16),)

    du, = _row_call(act_bwd, [u, da], [], 0, "ffn_act_bwd")
    dh = _mm(du, w_in, tb=True, name="ffn_up_dx")
    dw_in = _mm(h, du, ta=True, out_dtype=w_in.dtype, name="ffn_up_dw")

    def norm_bwd(xt, dht, dt, gt, sh, sc):
        dx, dg, dsh, dsc = jax.vjp(_norm_mod_fn, xt, gt, sh, sc)[1]((dht,))
        return dx + dt, dg, dsh, dsc

    dx, dg, dshift, dscale = _row_call(norm_bwd, [x, dh, dxn], [g, shift, scale], 3, "ffn_norm_bwd")
    return dx, jnp.concatenate([dshift, dscale, dgate], axis=0), dg, dw_in, dw_out


_ffn_half.defvjp(_ffn_half_fwd, _ffn_half_bwd)


def _s5_discretize(a_re, a_im, log_dt, b_re, b_im):
    lam = lax.complex(a_re, a_im)
    dt = jnp.exp(log_dt)[:, None]
    lam_bar = jnp.exp(lam * dt)
    b_bar = ((lam_bar - 1.0) / lam)[..., None] * lax.complex(b_re, b_im)
    return lam_bar, b_bar


def _s5_branch(u_lat, u_ctx, w, l, with_ctx_out):
    zero = jnp.zeros((SUBLANES, SSM_LANES // SUBLANES), F32)
    lat_terms, ctx_terms = [], []
    for d, scan in enumerate((scan_up, scan_down)):
        lam_bar, b_bar = _s5_discretize(w["ssm_a_re"][l, d], w["ssm_a_im"][l, d], w["ssm_log_dt"][l, d],
                                        w["ssm_b_re"][l, d], w["ssm_b_im"][l, d])
        lre = jnp.real(lam_bar).reshape(zero.shape)
        lim = jnp.imag(lam_bar).reshape(zero.shape)
        b_t = jnp.swapaxes(b_bar, 1, 2)
        b_re, b_im = _block_diag(jnp.real(b_t)), _block_diag(jnp.imag(b_t))
        c_re = _block_diag(jnp.swapaxes(w["ssm_c_re"][l, d], 1, 2))
        c_im = _block_diag(jnp.swapaxes(w["ssm_c_im"][l, d], 1, 2))
        sc_re, sc_im = scan(linear_to_planes(u_ctx, b_re), linear_to_planes(u_ctx, b_im), lre, lim, zero, zero)
        last = 0 if d == 1 else u_ctx.shape[0] - 1
        sl_re, sl_im = scan(linear_to_planes(u_lat, b_re), linear_to_planes(u_lat, b_im), lre, lim,
                            sc_re[last], sc_im[last])
        lat_terms += [linear_from_planes(sl_re, c_re), linear_from_planes(sl_im, c_im)]
        if with_ctx_out:
            ctx_terms += [linear_from_planes(sc_re, c_re), linear_from_planes(sc_im, c_im)]
    d_skip = w["ssm_d"][l][None, :]

    def out(terms, u):
        y, = s5_pre(*terms, u, d_skip)
        return glu(linear(y, w["glu_w"][l]))[0]

    return out(lat_terms, u_lat), (out(ctx_terms, u_ctx) if with_ctx_out else None)


def _pool_branch(xa, w, l):
    y = linear(pool_diff(xa), _block_diag(w["pool_w"][l]))
    return scale_rows(y, w["pool_scale"][l][None, :])[0]


_CTX_GROUPS = (O_VB - O_KB, O_UC - O_VB, O_KD - O_UC, O_VD - O_KD, CTX_COLS - O_VD)
_ALL_GROUPS = _CTX_GROUPS + (O_QD - O_QB, O_XA - O_QD, O_GATE - O_XA)
project_ctx = make_split_linear(_CTX_GROUPS)
project_all = make_split_linear(_ALL_GROUPS)


def _merge_branches(branches, gate_logits, w, l):
    zs = [linear_b(y, w["branch_w"][l, k]) for k, y in enumerate(branches)]
    return linear(merge(gate_logits, *zs)[0], w["out_w"][l])


def _token_mixer(h, hc, cos, sin, w, l, with_ctx_out):
    w_in, w_gate = w["w_in"][l][:, :O_GATE], w["w_in"][l][:, O_GATE:]
    kb, vb, uc, kd, vd, qb, qd, xa = project_all(h, w_in)
    p_gate = linear_b(h, w_gate)
    if with_ctx_out:
        kb_c, vb_c, uc_c, kd_c, vd_c, qb_c, qd_c, xa_c = project_all(hc, w_in)
        pc_gate = linear_b(hc, w_gate)
    else:
        kb_c, vb_c, uc_c, kd_c, vd_c = project_ctx(hc, w_in[:, :CTX_COLS])
    sink = w["win_sink"][l]
    q_g = jnp.tile(w["qk_norm"][l, 0], KV_HEADS * Q_PER_KV)[None, :]
    k_g = jnp.tile(w["qk_norm"][l, 1], KV_HEADS)[None, :]
    k_win_c = _to_heads(kb_c, KV_HEADS)
    v_win_c = _to_heads(vb_c, KV_HEADS)
    k_glb_c = _to_heads(head_norm(kd_c, k_g)[0], KV_HEADS)
    v_glb_c = _to_heads(vd_c, KV_HEADS)
    y_a = _pool_branch(xa, w, l)
    q_win = _q_heads(rope(qb, cos, sin)[0])
    k_win = _to_heads(rope(kb, cos, sin)[0], KV_HEADS)
    v_win = _to_heads(vb, KV_HEADS)
    y_b = _from_q_heads(attn_window(q_win, k_win_c, v_win_c, k_win, v_win, sink))
    y_c, y_c_ctx = _s5_branch(uc, uc_c, w, l, with_ctx_out)
    q_glb = _q_heads(norm_rope(qd, cos, sin, q_g)[0])
    k_glb = _to_heads(norm_rope(kd, cos, sin, k_g)[0], KV_HEADS)
    v_glb = _to_heads(vd, KV_HEADS)
    y_d = _from_q_heads(attn_global(q_glb, k_glb_c, v_glb_c, k_glb, v_glb))
    y = _merge_branches((y_a, y_b, y_c, y_d), p_gate, w, l)
    if not with_ctx_out:
        return y, None
    y_a_c = _pool_branch(xa_c, w, l)
    y_b_c = _from_q_heads(attn_ctx_sink(_q_heads(qb_c), k_win_c, v_win_c, sink))
    q_glb_c = _q_heads(head_norm(qd_c, q_g)[0])
    y_d_c = _from_q_heads(attn_ctx(q_glb_c, k_glb_c, v_glb_c))
    return y, _merge_branches((y_a_c, y_b_c, y_c_ctx, y_d_c), pc_gate, w, l)


def local_loss(w, x, c, ctx, target):
    depth = w["w_mod"].shape[0]
    cos, sin = _rope_tables(x.shape[0])
    cond = jnp.concatenate([c, w["c_ctx"][None, :], jnp.zeros((COND_ROWS - 2, D_MODEL), F32)], axis=0)
    s_all, = silu_rows(cond)
    for l in range(depth):
        last = l == depth - 1
        m_all = (linear(s_all, w["w_mod"][l]) + w["b_mod"][l][None, :]).reshape(COND_ROWS, N_SUB, 3, D_MODEL)
        m, mc = m_all[0], m_all[1]
        g = w["norm_g"][l][:, None, :]
        x = _ffn_half(x, m[0], g[0], w["ffn_in"][l, 0], w["ffn_out"][l, 0])
        ctx = _ffn_half(ctx, mc[0], g[0], w["ffn_in"][l, 0], w["ffn_out"][l, 0])
        h, = norm_mod(x, g[1], m[1, 0:1], m[1, 1:2])
        hc, = norm_mod(ctx, g[1], mc[1, 0:1], mc[1, 1:2])
        y, y_ctx = _token_mixer(h, hc, cos, sin, w, l, not last)
        x, = resid_full(x, y, m[1, 2:3])
        if not last:
            ctx, = resid_full(ctx, y_ctx, mc[1, 2:3])
        x = _ffn_half(x, m[2], g[2], w["ffn_in"][l, 1], w["ffn_out"][l, 1])
        if not last:
            ctx = _ffn_half(ctx, mc[2], g[2], w["ffn_in"][l, 1], w["ffn_out"][l, 1])
    return jnp.sum(loss_rows(x, target, w["final_g"][None, :])[0])


PACK_COLS = 1024


def _pack(arrays):
    flat = jnp.concatenate([a.reshape(-1) for a in arrays])
    pad = (-flat.shape[0]) % (PACK_COLS * 16)
    return jnp.pad(flat, (0, pad)).reshape(-1, PACK_COLS)


def _unpack(slab, shapes):
    flat = slab.reshape(-1)
    out, off = [], 0
    for s in shapes:
        n = math.prod(s)
        out.append(flat[off:off + n].reshape(s))
        off += n
    return out


def _full_from_shards(gathered, shard_shape, axis):
    z = jnp.moveaxis(gathered.reshape((N_DEV,) + tuple(shard_shape)), 0, axis)
    shape = list(shard_shape)
    shape[axis] *= N_DEV
    return z.reshape(shape)


def _shards_from_full(full, axis):
    shape = list(full.shape)
    shape[axis:axis + 1] = [N_DEV, shape[axis] // N_DEV]
    return jnp.moveaxis(full.reshape(shape), axis, 0)


def kernel(x, c, ctx, c_ctx, w_mod, b_mod, norm_g, ffn_in, ffn_out, w_in, win_sink, qk_norm, pool_w, pool_scale, ssm_a_re, ssm_a_im, ssm_log_dt, ssm_b_re, ssm_b_im, ssm_c_re, ssm_c_im, ssm_d, glu_w, branch_w, out_w, final_g, loss_target, m_c_ctx, m_w_mod, m_b_mod, m_norm_g, m_ffn_in, m_ffn_out, m_w_in, m_win_sink, m_qk_norm, m_pool_w, m_pool_scale, m_ssm_a_re, m_ssm_a_im, m_ssm_log_dt, m_ssm_b_re, m_ssm_b_im, m_ssm_c_re, m_ssm_c_im, m_ssm_d, m_glu_w, m_branch_w, m_out_w, m_final_g, v_c_ctx, v_w_mod, v_b_mod, v_norm_g, v_ffn_in, v_ffn_out, v_w_in, v_win_sink, v_qk_norm, v_pool_w, v_pool_scale, v_ssm_a_re, v_ssm_a_im, v_ssm_log_dt, v_ssm_b_re, v_ssm_b_im, v_ssm_c_re, v_ssm_c_im, v_ssm_d, v_glu_w, v_branch_w, v_out_w, v_final_g):
    given = dict(locals())
    wts = {n: given[n] for n in WEIGHTS}
    mom = {n: given["m_" + n] for n in WEIGHTS}
    var = {n: given["v_" + n] for n in WEIGHTS}
    me = _my_slot()

    shard_shapes = [wts[n].shape for n in SHARDED]
    w_slab = _pack([wts[n] for n in SHARDED])
    gathered = all_gather_two_level(w_slab.astype(BF16), "gather_weights")
    full = dict(wts)
    row = 0
    for n in SHARDED:
        n_rows = math.prod(wts[n].shape) // PACK_COLS
        full[n] = _full_from_shards(gathered[:, row:row + n_rows], wts[n].shape, SHARD_AXIS[n])
        row += n_rows
    g_slab = _pack([norm_g])
    g_all = all_gather(g_slab, "gather_norm_g")
    full["norm_g"] = _full_from_shards(
        jnp.stack([_unpack(g_all[s], [norm_g.shape])[0] for s in range(N_DEV)]), norm_g.shape, 2)

    loss, (gw, gx) = jax.value_and_grad(local_loss, argnums=(0, 1))(full, x[0], c, ctx[0], loss_target[0])
    loss = lax.psum(loss, ("x", "y", "c"))

    dest = [_shards_from_full(gw[n], SHARD_AXIS[n]).reshape(N_DEV // 2, 2, -1, PACK_COLS) for n in SHARDED]
    send = jnp.swapaxes(jnp.concatenate(dest, axis=2), 0, 1)
    from_sibling = sibling_swap(send, "exchange_grads_sibling")
    own = lax.dynamic_index_in_dim(send, lax.axis_index("c"), axis=0, keepdims=False)
    pair, = _row_call(lambda a, b: ((a.astype(F32) + b.astype(F32)).astype(BF16),),
                      [own.reshape(-1, PACK_COLS), from_sibling.reshape(-1, PACK_COLS)], [], 0,
                      "exchange_pair_sum")
    big_parts = chip_exchange(pair.reshape(own.shape), "exchange_grads_chips")
    small_names = SMALL + ("norm_g",)
    small_shapes = [gw[n].shape for n in small_names]
    small_parts = all_gather(_pack([gw[n] for n in small_names]), "gather_small_grads")

    big = _adamw_call(big_parts, w_slab, _pack([mom[n] for n in SHARDED]), _pack([var[n] for n in SHARDED]),
                      "adamw_sharded")
    big = [_unpack(b, shard_shapes) for b in big]
    col = me * norm_g.shape[2]

    def small_slab(src, shard_src):
        padded = jnp.zeros((norm_g.shape[0], norm_g.shape[1], norm_g.shape[2] * N_DEV), F32)
        padded = lax.dynamic_update_slice(padded, shard_src, (0, 0, col))
        return _pack([src[n] for n in SMALL] + [padded])

    small = _adamw_call(small_parts, small_slab(wts, norm_g), small_slab(mom, m_norm_g),
                        small_slab(var, v_norm_g), "adamw_small")
    small = [_unpack(s, small_shapes) for s in small]

    outs = {}
    for kind in range(4):
        for i, n in enumerate(SHARDED):
            outs[(kind, n)] = big[kind][i]
        for i, n in enumerate(small_names):
            val = small[kind][i]
            if n == "norm_g":
                val = lax.dynamic_slice(val, (0, 0, col), norm_g.shape)
            outs[(kind, n)] = val
    return (loss, gx[None], *[outs[(k, n)] for k in range(4) for n in WEIGHTS])
```

```python
import functools
import math

import jax
import jax.numpy as jnp
from jax import lax
from jax.experimental import pallas as pl
from jax.experimental.pallas import tpu as pltpu

F32 = jnp.float32
BF16 = jnp.bfloat16

D_MODEL = 1024
GRID_W = 64
HEAD_DIM = 64
N_BRANCH = 4
BRANCH_W = D_MODEL // N_BRANCH
WINDOW = 128
ROPE_THETA = 10000.0
EPS = 1e-6
D_FF = 2816
N_SUB = 3
POOL_WINDOWS = (2, 4, 8, 16)
POOL_GROUP = BRANCH_W // len(POOL_WINDOWS)
KV_HEADS = 2
Q_PER_KV = 2
SSM_GROUP = 16
SSM_GROUPS = BRANCH_W // SSM_GROUP
SSM_STATE = 64
SSM_LANES = SSM_GROUPS * SSM_STATE
O_KB, O_VB, O_UC, O_KD, O_VD, CTX_COLS = 0, 128, 256, 512, 640, 768
O_QB, O_QD, O_XA, O_GATE = 768, 1024, 1280, 1536
IN_W = O_GATE + N_BRANCH * D_MODEL

ADAM_LR, ADAM_B1, ADAM_B2, ADAM_EPS, ADAM_WD, ADAM_STEP = 0.001, 0.9, 0.999, 1e-08, 0.01, 10

N_DEV = 8
MESH = pl.DeviceIdType.MESH

V7X_VMEM_BYTES = 64 * 1024 * 1024
SUBLANES = 8
LANES = 128
NEG_BIG = -1e30
COND_ROWS = 128

SHARDED = ("w_mod", "ffn_in", "ffn_out", "w_in", "glu_w", "branch_w", "out_w")
SHARD_AXIS = {"w_mod": 2, "ffn_in": 3, "ffn_out": 2, "w_in": 2, "glu_w": 2, "branch_w": 3, "out_w": 1}
SMALL = ("c_ctx", "b_mod", "win_sink", "qk_norm", "pool_w", "pool_scale", "ssm_a_re", "ssm_a_im",
         "ssm_log_dt", "ssm_b_re", "ssm_b_im", "ssm_c_re", "ssm_c_im", "ssm_d", "final_g")
WEIGHTS = ("c_ctx", "w_mod", "b_mod", "norm_g", "ffn_in", "ffn_out", "w_in", "win_sink", "qk_norm",
           "pool_w", "pool_scale", "ssm_a_re", "ssm_a_im", "ssm_log_dt", "ssm_b_re", "ssm_b_im",
           "ssm_c_re", "ssm_c_im", "ssm_d", "glu_w", "branch_w", "out_w", "final_g")


def _tile(n, cap, mult):
    if n <= cap:
        return n
    t = (cap // mult) * mult
    while t >= mult:
        if n % t == 0:
            return t
        t -= mult
    return n


def _cparams(sem, tile_bytes, resident_bytes=0):
    limit = int(min(V7X_VMEM_BYTES - 8 * 2 ** 20,
                    max(32 * 2 ** 20, 3 * tile_bytes + resident_bytes + 8 * 2 ** 20)))
    return pltpu.CompilerParams(dimension_semantics=sem, vmem_limit_bytes=limit)


PLANE = (SUBLANES, 128)
PLANE_COLS = PLANE[0] * PLANE[1]


def _planes_to_rows(ref):
    return jnp.concatenate([ref[:, j, :] for j in range(PLANE[0])], axis=1)


def _mm(a, b, ta=False, tb=False, out_dtype=F32, a_planes=False, b_planes=False, out_planes=False,
        name="mm"):
    a_shape = (a.shape[0], PLANE_COLS) if a_planes else a.shape
    b_shape = (b.shape[0], PLANE_COLS) if b_planes else b.shape
    m, k = (a_shape[1], a_shape[0]) if ta else a_shape
    n = b_shape[0] if tb else b_shape[1]
    assert (b_shape[1] if tb else b_shape[0]) == k and not (b_planes and tb)
    tm, tn, tk = _tile(m, 1024, LANES), _tile(n, 1536, LANES), _tile(k, 1536, LANES)
    nk = k // tk
    dims = (((0 if ta else 1,), (1 if tb else 0,)), ((), ()))

    def body(a_ref, b_ref, o_ref, acc_ref):
        kk = pl.program_id(2)

        @pl.when(kk == 0)
        def _():
            acc_ref[...] = jnp.zeros_like(acc_ref)

        av = _planes_to_rows(a_ref) if a_planes else a_ref[...]
        bv = _planes_to_rows(b_ref) if b_planes else b_ref[...]
        acc_ref[...] += lax.dot_general(av.astype(BF16), bv.astype(BF16), dims, preferred_element_type=F32)

        @pl.when(kk == nk - 1)
        def _():
            if out_planes:
                for j in range(PLANE[0]):
                    o_ref[:, j, :] = acc_ref[:, j * PLANE[1]:(j + 1) * PLANE[1]].astype(o_ref.dtype)
            else:
                o_ref[...] = acc_ref[...].astype(o_ref.dtype)

    if a_planes:
        assert (tm if ta else tk) == PLANE_COLS
        a_spec = pl.BlockSpec(((tk if ta else tm),) + PLANE, (lambda i, j, kk: (kk, 0, 0)) if ta
                              else (lambda i, j, kk: (i, 0, 0)))
    else:
        a_spec = (pl.BlockSpec((tk, tm), lambda i, j, kk: (kk, i)) if ta
                  else pl.BlockSpec((tm, tk), lambda i, j, kk: (i, kk)))
    if b_planes:
        assert tn == PLANE_COLS
        b_spec = pl.BlockSpec((tk,) + PLANE, lambda i, j, kk: (kk, 0, 0))
    else:
        b_spec = (pl.BlockSpec((tn, tk), lambda i, j, kk: (j, kk)) if tb
                  else pl.BlockSpec((tk, tn), lambda i, j, kk: (kk, j)))
    if out_planes:
        assert tn == PLANE_COLS
        o_spec = pl.BlockSpec((tm,) + PLANE, lambda i, j, kk: (i, 0, 0))
        o_shape = jax.ShapeDtypeStruct((m,) + PLANE, out_dtype)
    else:
        o_spec = pl.BlockSpec((tm, tn), lambda i, j, kk: (i, j))
        o_shape = jax.ShapeDtypeStruct((m, n), out_dtype)
    tile_bytes = (a.dtype.itemsize * tm * tk + b.dtype.itemsize * tk * tn
                  + jnp.dtype(out_dtype).itemsize * tm * tn + 2 * tm * tn)
    return pl.pallas_call(
        body, name=name, grid=(m // tm, n // tn, nk),
        in_specs=[a_spec, b_spec], out_specs=o_spec, out_shape=o_shape,
        scratch_shapes=[pltpu.VMEM((tm, tn), F32)],
        compiler_params=_cparams(("parallel", "parallel", "arbitrary"), tile_bytes),
    )(a, b)


def make_linear(out_dtype, x_planes=False, out_planes=False):
    @jax.custom_vjp
    def op(x, w):
        return _mm(x, w, out_dtype=out_dtype, a_planes=x_planes, out_planes=out_planes, name="linear_fwd")

    def fwd(x, w):
        return op(x, w), (x, w)

    def bwd(res, dy):
        x, w = res
        return (_mm(dy, w, tb=True, out_dtype=x.dtype, a_planes=out_planes, out_planes=x_planes,
                    name="linear_dx"),
                _mm(x, dy, ta=True, out_dtype=w.dtype, a_planes=x_planes, b_planes=out_planes,
                    name="linear_dw"))

    op.defvjp(fwd, bwd)
    return op


linear = make_linear(F32)
linear_b = make_linear(BF16)
linear_to_planes = make_linear(F32, out_planes=True)
linear_from_planes = make_linear(F32, x_planes=True)


def _split_rows(t):
    return _tile(t, 512, 2 * SUBLANES)


def _split_fwd(x, w, widths):
    t, k = x.shape
    n = w.shape[1]
    tm = _split_rows(t)

    def body(x_ref, w_ref, *o_refs):
        y = jnp.dot(x_ref[...].astype(BF16), w_ref[...].astype(BF16), preferred_element_type=F32)
        off = 0
        for o_ref, wd in zip(o_refs, widths):
            o_ref[...] = y[:, off:off + wd]
            off += wd

    return pl.pallas_call(
        body, name="split_linear_fwd", grid=(t // tm,),
        in_specs=[pl.BlockSpec((tm, k), lambda i: (i, 0)), pl.BlockSpec((k, n), lambda i: (0, 0))],
        out_specs=[pl.BlockSpec((tm, wd), lambda i: (i, 0)) for wd in widths],
        out_shape=[jax.ShapeDtypeStruct((t, wd), F32) for wd in widths],
        compiler_params=_cparams(("parallel",), 4 * tm * (k + 2 * n) + w.dtype.itemsize * k * n),
    )(x, w)


def _split_dx(cts, w):
    t = cts[0].shape[0]
    k, n = w.shape
    tm = _split_rows(t)

    def body(*refs):
        dy = jnp.concatenate([r[...].astype(BF16) for r in refs[:-2]], axis=1)
        refs[-1][...] = lax.dot_general(dy, refs[-2][...].astype(BF16), (((1,), (1,)), ((), ())),
                                        preferred_element_type=F32)

    return pl.pallas_call(
        body, name="split_linear_dx", grid=(t // tm,),
        in_specs=[pl.BlockSpec((tm, c.shape[1]), lambda i: (i, 0)) for c in cts]
        + [pl.BlockSpec((k, n), lambda i: (0, 0))],
        out_specs=pl.BlockSpec((tm, k), lambda i: (i, 0)), out_shape=jax.ShapeDtypeStruct((t, k), F32),
        compiler_params=_cparams(("parallel",), 4 * tm * (k + 2 * n) + w.dtype.itemsize * k * n),
    )(*cts, w)


def _split_dw(x, cts, out_dtype):
    t, k = x.shape
    n = sum(c.shape[1] for c in cts)
    tk = _split_rows(t)
    steps = t // tk

    def body(x_ref, *refs):
        o_ref, acc_ref = refs[-2], refs[-1]

        @pl.when(pl.program_id(0) == 0)
        def _():
            acc_ref[...] = jnp.zeros_like(acc_ref)

        dy = jnp.concatenate([r[...].astype(BF16) for r in refs[:-2]], axis=1)
        acc_ref[...] += lax.dot_general(x_ref[...].astype(BF16), dy, (((0,), (0,)), ((), ())),
                                        preferred_element_type=F32)

        @pl.when(pl.program_id(0) == steps - 1)
        def _():
            o_ref[...] = acc_ref[...].astype(o_ref.dtype)

    return pl.pallas_call(
        body, name="split_linear_dw", grid=(steps,),
        in_specs=[pl.BlockSpec((tk, k), lambda i: (i, 0))]
        + [pl.BlockSpec((tk, c.shape[1]), lambda i: (i, 0)) for c in cts],
        out_specs=pl.BlockSpec((k, n), lambda i: (0, 0)), out_shape=jax.ShapeDtypeStruct((k, n), out_dtype),
        scratch_shapes=[pltpu.VMEM((k, n), F32)],
        compiler_params=_cparams(("arbitrary",), 4 * tk * (k + n), 3 * 4 * k * n),
    )(x, *cts)


def make_split_linear(widths):
    @jax.custom_vjp
    def op(x, w):
        return tuple(_split_fwd(x, w, widths))

    def fwd(x, w):
        return op(x, w), (x, w)

    def bwd(res, cts):
        x, w = res
        return _split_dx(cts, w), _split_dw(x, cts, w.dtype)

    op.defvjp(fwd, bwd)
    return op


ROW_TILE_BYTES = 6 * 2 ** 20


def _row_tile(t, row_bytes):
    tm = 1024
    while tm > 2 * SUBLANES and tm * row_bytes > ROW_TILE_BYTES:
        tm //= 2
    return _tile(t, tm, 2 * SUBLANES)


def _row_call(fn, rows, params, n_reduce, name):
    t = rows[0].shape[0]
    out_avals = jax.eval_shape(fn, *rows, *params)
    n_out = len(out_avals) - n_reduce
    row_avals, red_avals = out_avals[:n_out], out_avals[n_out:]
    row_bytes = sum(r.shape[1] * r.dtype.itemsize for r in (*rows, *row_avals))
    tm = _row_tile(t, row_bytes)
    n_in = len(rows) + len(params)

    def body(*refs):
        outs = fn(*[r[...] for r in refs[:n_in]])
        o_refs = refs[n_in:]
        for o_ref, o in zip(o_refs[:n_out], outs[:n_out]):
            o_ref[...] = o.astype(o_ref.dtype)
        if n_reduce:
            @pl.when(pl.program_id(0) == 0)
            def _():
                for r in o_refs[n_out:]:
                    r[...] = jnp.zeros_like(r)

            for r, o in zip(o_refs[n_out:], outs[n_out:]):
                r[...] += o.astype(r.dtype)

    in_specs = ([pl.BlockSpec((tm, r.shape[1]), lambda i: (i, 0)) for r in rows]
                + [pl.BlockSpec(p.shape, lambda i: (0, 0)) for p in params])
    out_specs = ([pl.BlockSpec((tm, o.shape[1]), lambda i: (i, 0)) for o in row_avals]
                 + [pl.BlockSpec(o.shape, lambda i: (0, 0)) for o in red_avals])
    return pl.pallas_call(
        body, name=name, grid=(t // tm,), in_specs=in_specs, out_specs=out_specs,
        out_shape=[jax.ShapeDtypeStruct(o.shape, o.dtype) for o in out_avals],
        compiler_params=_cparams(("arbitrary",) if n_reduce else ("parallel",), tm * row_bytes),
    )(*rows, *params)


def rowwise(fn, n_rows, name):
    @jax.custom_vjp
    def op(*args):
        return tuple(_row_call(fn, args[:n_rows], args[n_rows:], 0, name + "_fwd"))

    def fwd(*args):
        return op(*args), args

    def bwd(args, cts):
        n_ct = len(cts)

        def bwd_fn(*a):
            r, ct, p = a[:n_rows], a[n_rows:n_rows + n_ct], a[n_rows + n_ct:]
            return jax.vjp(fn, *r, *p)[1](tuple(ct))

        return tuple(_row_call(bwd_fn, (*args[:n_rows], *cts), args[n_rows:], len(args) - n_rows,
                               name + "_bwd"))

    op.defvjp(fwd, bwd)
    return op


@functools.partial(jax.custom_vjp, nondiff_argnums=(1,))
def _swap_lanes(x, k):
    n = x.shape[-1]
    lane = lax.broadcasted_iota(jnp.int32, x.shape, x.ndim - 1)
    return jnp.where((lane & k) == 0, pltpu.roll(x, n - k, x.ndim - 1), pltpu.roll(x, k, x.ndim - 1))


def _swap_lanes_fwd(x, k):
    return _swap_lanes(x, k), None


def _swap_lanes_bwd(k, _, g):
    return (_swap_lanes(g, k),)


_swap_lanes.defvjp(_swap_lanes_fwd, _swap_lanes_bwd)


def _head_sum(x):
    s = x
    k = 1
    while k < HEAD_DIM:
        s = s + _swap_lanes(s, k)
        k *= 2
    return s


def _rms(x):
    return x * lax.rsqrt(jnp.mean(x * x, axis=-1, keepdims=True) + EPS)


def _norm_mod_fn(x, g, shift, scale):
    return ((_rms(x) * g) * (1.0 + scale) + shift,)


def _swiglu_fn(u):
    gate, up = u[:, :D_FF], u[:, D_FF:]
    return (jax.nn.silu(gate) * up,)


def _resid_fn(coef, x, y, gate):
    return (x + (coef * gate) * y,)


def _scale_fn(y, s):
    return (y * s,)


def _tile_lanes(tab, width):
    return tab if tab.shape[1] == width else jnp.concatenate([tab] * (width // tab.shape[1]), axis=1)


def _rope_fn(x, cos, sin):
    w = x.shape[1]
    return (x * _tile_lanes(cos, w) + _swap_lanes(x, 16) * _tile_lanes(sin, w),)


def _head_norm(x, g):
    ms = _head_sum(x * x) * (1.0 / HEAD_DIM)
    return x * lax.rsqrt(ms + EPS) * g


def _norm_rope_fn(x, cos, sin, g):
    return _rope_fn(_head_norm(x, g), cos, sin)


def _head_norm_fn(x, g):
    return (_head_norm(x, g),)


def _merge_fn(gl, z0, z1, z2, z3):
    zs = (z0, z1, z2, z3)
    terms = [jax.nn.sigmoid(gl[:, k * D_MODEL:(k + 1) * D_MODEL].astype(F32)) * zs[k].astype(F32)
             for k in range(N_BRANCH)]
    return (sum(terms[1:], terms[0]),)


def _s5_pre_fn(y0r, y0i, y1r, y1i, u, d):
    return (jax.nn.gelu(((y0r - y0i) + (y1r - y1i)) + d * u),)


def _glu_fn(z):
    return (z[:, :BRANCH_W] * jax.nn.sigmoid(z[:, BRANCH_W:]),)


def _silu_fn(x):
    return (jax.nn.silu(x),)


def _loss_fn(x, tgt, g):
    err = jnp.square(_rms(x) * g - tgt)
    return (0.5 * jnp.mean(err, axis=-1, keepdims=True),)


norm_mod = rowwise(_norm_mod_fn, 1, "norm_mod")
resid_full = rowwise(functools.partial(_resid_fn, 1.0), 2, "resid_full")
scale_rows = rowwise(_scale_fn, 1, "pool_scale")
rope = rowwise(_rope_fn, 3, "rope")
norm_rope = rowwise(_norm_rope_fn, 3, "norm_rope")
head_norm = rowwise(_head_norm_fn, 1, "head_norm")
merge = rowwise(_merge_fn, 5, "merge")
s5_pre = rowwise(_s5_pre_fn, 5, "s5_pre")
glu = rowwise(_glu_fn, 1, "glu")
silu_rows = rowwise(_silu_fn, 1, "silu")
loss_rows = rowwise(_loss_fn, 2, "loss_head")


POOL_HALO = 16


def _pool_call(xa, adjoint, name):
    n, width = xa.shape
    tm = _tile(n, 512, POOL_HALO)
    halo_blocks = tm // POOL_HALO
    last_halo = n // POOL_HALO - 1
    ext_rows = tm + 2 * POOL_HALO

    def body(prev_ref, cur_ref, next_ref, o_ref, ext_ref):
        i = pl.program_id(0)
        ext_ref[0:POOL_HALO] = prev_ref[...]
        ext_ref[POOL_HALO:POOL_HALO + tm] = cur_ref[...]
        ext_ref[POOL_HALO + tm:ext_rows] = next_ref[...]
        e = ext_ref[...]
        row = lax.broadcasted_iota(jnp.int32, e.shape, 0) + (i * tm - POOL_HALO)
        grp = lax.broadcasted_iota(jnp.int32, e.shape, 1) // POOL_GROUP
        win = jnp.where(grp == 0, POOL_WINDOWS[0],
                        jnp.where(grp == 1, POOL_WINDOWS[1], jnp.where(grp == 2, POOL_WINDOWS[2], POOL_WINDOWS[3])))
        valid = (row >= 0) & (row < n)
        lo = jnp.clip(row - win // 2, 0, n)
        hi = jnp.clip(row - win // 2 + win, 0, n)
        cnt = jnp.maximum((hi - lo).astype(F32), 1.0)
        e0 = jnp.where(valid, e / cnt if adjoint else e, 0.0)

        def shift(z, s):
            return pltpu.roll(z, s % ext_rows, 0)

        s2 = e0 + shift(e0, -1 if adjoint else 1)
        s4 = shift(s2, 1) + shift(s2, -1)
        s8 = shift(s4, 2) + shift(s4, -2)
        s16 = shift(s8, 4) + shift(s8, -4)
        s = jnp.where(grp == 0, s2, jnp.where(grp == 1, s4, jnp.where(grp == 2, s8, s16)))
        out = (s - e) if adjoint else (s / cnt - e)
        o_ref[...] = out[POOL_HALO:POOL_HALO + tm]

    return pl.pallas_call(
        body, name=name, grid=(n // tm,),
        in_specs=[pl.BlockSpec((POOL_HALO, width), lambda i: (jnp.maximum(i * halo_blocks - 1, 0), 0)),
                  pl.BlockSpec((tm, width), lambda i: (i, 0)),
                  pl.BlockSpec((POOL_HALO, width), lambda i: (jnp.minimum((i + 1) * halo_blocks, last_halo), 0))],
        out_specs=pl.BlockSpec((tm, width), lambda i: (i, 0)),
        out_shape=jax.ShapeDtypeStruct((n, width), F32),
        scratch_shapes=[pltpu.VMEM((ext_rows, width), F32)],
        compiler_params=_cparams(("parallel",), 4 * 4 * ext_rows * width),
    )(xa, xa, xa)


@jax.custom_vjp
def pool_diff(xa):
    return _pool_call(xa, False, "pool_fwd")


pool_diff.defvjp(lambda xa: (pool_diff(xa), None), lambda _, g: (_pool_call(g, True, "pool_bwd"),))


ATT_SCALE = HEAD_DIM ** -0.5
ATT_BQ = 512


def _qk_scores(q, k):
    return lax.dot_general((q * ATT_SCALE).astype(BF16), k.astype(BF16), (((1,), (1,)), ((), ())),
                           preferred_element_type=F32)


def _sink_rows(sink_ref, h, bq):
    r = lax.broadcasted_iota(jnp.int32, (Q_PER_KV * bq, 1), 0)
    return jnp.where(r < bq, sink_ref[h * Q_PER_KV], sink_ref[h * Q_PER_KV + 1])


ATT_SUB_ROWS = 256


def _flash_fwd(q, klt, vl, kct, vc, sink):
    kvh, g, t, dh = q.shape
    c = kct.shape[2]
    has_lat = klt is not None
    has_sink = sink is not None
    bq = _tile(t, ATT_BQ, LANES)
    bk = _tile(t, 4096, LANES)
    nkv = t // bk if has_lat else 1
    rows = g * bq
    sub = min(ATT_SUB_ROWS, rows)

    def body(*refs):
        it = iter(refs)
        q_ref, kc_ref, vc_ref = next(it), next(it), next(it)
        kl_ref, vl_ref = (next(it), next(it)) if has_lat else (None, None)
        sink_ref = next(it) if has_sink else None
        o_ref, lse_ref, m_ref, l_ref, acc_ref = next(it), next(it), next(it), next(it), next(it)
        h, kj = pl.program_id(0), pl.program_id(2)
        qv = (q_ref[0].reshape(rows, dh) * ATT_SCALE).astype(BF16)

        def part(kt_ref, v_ref):
            kt, v = kt_ref[0].astype(BF16), v_ref[0].astype(BF16)
            m_all, l_all, acc_all = m_ref[...], l_ref[...], acc_ref[...]
            m_out, l_out, acc_out = [], [], []
            for r0 in range(0, rows, sub):
                s = jnp.dot(qv[r0:r0 + sub], kt, preferred_element_type=F32)
                m_old = m_all[r0:r0 + sub]
                m_new = jnp.maximum(m_old, jnp.max(s, axis=-1, keepdims=True))
                p = jnp.exp(s - m_new)
                alpha = jnp.exp(m_old - m_new)
                m_out.append(m_new)
                l_out.append(alpha * l_all[r0:r0 + sub] + jnp.sum(p, axis=-1, keepdims=True))
                acc_out.append(alpha * acc_all[r0:r0 + sub]
                               + jnp.dot(p.astype(BF16), v, preferred_element_type=F32))
            m_ref[...] = jnp.concatenate(m_out, axis=0)
            l_ref[...] = jnp.concatenate(l_out, axis=0)
            acc_ref[...] = jnp.concatenate(acc_out, axis=0)

        @pl.when(kj == 0)
        def _():
            if has_sink:
                m_ref[...] = _sink_rows(sink_ref, h, bq)
                l_ref[...] = jnp.ones_like(l_ref)
            else:
                m_ref[...] = jnp.full_like(m_ref, NEG_BIG)
                l_ref[...] = jnp.zeros_like(l_ref)
            acc_ref[...] = jnp.zeros_like(acc_ref)
            part(kc_ref, vc_ref)

        if has_lat:
            part(kl_ref, vl_ref)

        @pl.when(kj == nkv - 1)
        def _():
            o_ref[0] = (acc_ref[...] / l_ref[...]).reshape(g, bq, dh)
            lse_ref[0] = (m_ref[...] + jnp.log(l_ref[...])).reshape(g, bq, 1)

    q_spec = pl.BlockSpec((1, g, bq, dh), lambda h, i, j: (h, 0, i, 0))
    r_spec = pl.BlockSpec((1, g, bq, 1), lambda h, i, j: (h, 0, i, 0))
    c_spec = pl.BlockSpec((1, c, dh), lambda h, i, j: (h, 0, 0))
    ct_spec = pl.BlockSpec((1, dh, c), lambda h, i, j: (h, 0, 0))
    in_specs, args = [q_spec, ct_spec, c_spec], [q, kct, vc]
    if has_lat:
        in_specs += [pl.BlockSpec((1, dh, bk), lambda h, i, j: (h, 0, j)),
                     pl.BlockSpec((1, bk, dh), lambda h, i, j: (h, j, 0))]
        args += [klt, vl]
    if has_sink:
        in_specs.append(pl.BlockSpec(memory_space=pltpu.SMEM))
        args.append(sink)
    return pl.pallas_call(
        body, name="attn_fwd", grid=(kvh, t // bq, nkv),
        in_specs=in_specs, out_specs=[q_spec, r_spec],
        out_shape=[jax.ShapeDtypeStruct(q.shape, F32), jax.ShapeDtypeStruct((kvh, g, t, 1), F32)],
        scratch_shapes=[pltpu.VMEM((rows, 1), F32), pltpu.VMEM((rows, 1), F32), pltpu.VMEM((rows, dh), F32)],
        compiler_params=_cparams(("parallel", "parallel", "arbitrary"), 4 * 4 * rows * max(bk, c)),
    )(*args)


def _ctx_dq(q, kc, vc, sink, o, do, lse):
    kvh, g, t, dh = q.shape
    c = kc.shape[1]
    has_sink = sink is not None
    bq = _tile(t, ATT_BQ, LANES)
    rows = g * bq

    def body(*refs):
        q_ref, o_ref, do_ref, lse_ref, kc_ref, vc_ref = refs[:6]
        sink_ref = refs[6] if has_sink else None
        dq_ref, delta_ref, dsink_ref = refs[-3:]
        dov = do_ref[0].reshape(rows, dh)
        lse_v = lse_ref[0].reshape(rows, 1)
        delta = jnp.sum(o_ref[0].reshape(rows, dh) * dov, axis=-1, keepdims=True)
        p = jnp.exp(_qk_scores(q_ref[0].reshape(rows, dh), kc_ref[0]) - lse_v)
        dp = lax.dot_general(dov.astype(BF16), vc_ref[0].astype(BF16), (((1,), (1,)), ((), ())),
                             preferred_element_type=F32)
        ds = (p * (dp - delta)).astype(BF16)
        dq = jnp.dot(ds, kc_ref[0].astype(BF16), preferred_element_type=F32) * ATT_SCALE
        dq_ref[0] = dq.reshape(g, bq, dh)
        delta_ref[0] = delta.reshape(g, bq, 1)
        if has_sink:
            p_sink = jnp.exp(_sink_rows(sink_ref, pl.program_id(0), bq) - lse_v)
            dsink_ref[0] = (-p_sink * delta).reshape(g, bq, 1)
        else:
            dsink_ref[0] = jnp.zeros((g, bq, 1), F32)

    q_spec = pl.BlockSpec((1, g, bq, dh), lambda h, i: (h, 0, i, 0))
    r_spec = pl.BlockSpec((1, g, bq, 1), lambda h, i: (h, 0, i, 0))
    c_spec = pl.BlockSpec((1, c, dh), lambda h, i: (h, 0, 0))
    in_specs, args = [q_spec, q_spec, q_spec, r_spec, c_spec, c_spec], [q, o, do, lse, kc, vc]
    if has_sink:
        in_specs.append(pl.BlockSpec(memory_space=pltpu.SMEM))
        args.append(sink)
    row_shape = jax.ShapeDtypeStruct((kvh, g, t, 1), F32)
    return pl.pallas_call(
        body, name="attn_ctx_dq", grid=(kvh, t // bq),
        in_specs=in_specs, out_specs=[q_spec, r_spec, r_spec],
        out_shape=[jax.ShapeDtypeStruct(q.shape, F32), row_shape, row_shape],
        compiler_params=_cparams(("parallel", "parallel"), 4 * 6 * rows * c),
    )(*args)


def _flash_dkv(q, do, lse, delta, k, v):
    kvh, g, t, dh = q.shape
    nk_rows = k.shape[1]
    bq = _tile(t, ATT_BQ, LANES)
    bk = _tile(nk_rows, 1024, LANES)
    nq = t // bq
    rows = g * bq

    def body(q_ref, do_ref, lse_ref, delta_ref, k_ref, v_ref, dk_ref, dv_ref, dk_acc, dv_acc):
        qj = pl.program_id(2)
        qv = q_ref[0].reshape(rows, dh)
        dov = do_ref[0].reshape(rows, dh)

        @pl.when(qj == 0)
        def _():
            dk_acc[...] = jnp.zeros_like(dk_acc)
            dv_acc[...] = jnp.zeros_like(dv_acc)

        s = _qk_scores(qv, k_ref[0])
        p = jnp.exp(s - lse_ref[0].reshape(rows, 1))
        dp = lax.dot_general(dov.astype(BF16), v_ref[0].astype(BF16), (((1,), (1,)), ((), ())),
                             preferred_element_type=F32)
        ds = p * (dp - delta_ref[0].reshape(rows, 1))
        tn = (((0,), (0,)), ((), ()))
        dv_acc[...] += lax.dot_general(p.astype(BF16), dov.astype(BF16), tn, preferred_element_type=F32)
        dk_acc[...] += lax.dot_general(ds.astype(BF16), qv.astype(BF16), tn, preferred_element_type=F32)

        @pl.when(qj == nq - 1)
        def _():
            dk_ref[0] = dk_acc[...] * ATT_SCALE
            dv_ref[0] = dv_acc[...]

    q_spec = pl.BlockSpec((1, g, bq, dh), lambda h, i, j: (h, 0, j, 0))
    r_spec = pl.BlockSpec((1, g, bq, 1), lambda h, i, j: (h, 0, j, 0))
    k_spec = pl.BlockSpec((1, bk, dh), lambda h, i, j: (h, i, 0))
    return pl.pallas_call(
        body, name="attn_dkv", grid=(kvh, nk_rows // bk, nq),
        in_specs=[q_spec, q_spec, r_spec, r_spec, k_spec, k_spec], out_specs=[k_spec, k_spec],
        out_shape=[jax.ShapeDtypeStruct(k.shape, F32), jax.ShapeDtypeStruct(k.shape, F32)],
        scratch_shapes=[pltpu.VMEM((bk, dh), F32), pltpu.VMEM((bk, dh), F32)],
        compiler_params=_cparams(("parallel", "parallel", "arbitrary"), 4 * 6 * rows * bk),
    )(q, do, lse, delta, k, v)


def _flash_bwd_full(q, kl, vl, kc, vc, o, do, lse):
    kvh, g, t, dh = q.shape
    c = kc.shape[1]
    bq, bk = _tile(t, 512, LANES), _tile(t, 1024, LANES)
    nq, nkv = t // bq, t // bk
    rows = g * bq
    klt, vlt = jnp.swapaxes(kl, 1, 2), jnp.swapaxes(vl, 1, 2)
    kct, vct = jnp.swapaxes(kc, 1, 2), jnp.swapaxes(vc, 1, 2)
    tn = (((0,), (0,)), ((), ()))

    def body(q_ref, o_ref, do_ref, lse_ref, kc_ref, kct_ref, vct_ref, kl_ref, klt_ref, vlt_ref,
             dq_ref, delta_ref, dk_hbm, dv_hbm, dq_acc, dl_ref, dk_acc, dv_acc):
        h, qi, kj = pl.program_id(0), pl.program_id(1), pl.program_id(2)
        q_raw = q_ref[0].reshape(rows, dh).astype(BF16)
        qv = (q_ref[0].reshape(rows, dh) * ATT_SCALE).astype(BF16)
        dov = do_ref[0].reshape(rows, dh).astype(BF16)
        lse_v = lse_ref[0].reshape(rows, 1)

        def tile(kt, vt):
            s = jnp.dot(qv, kt.astype(BF16), preferred_element_type=F32)
            p = jnp.exp(s - lse_v)
            dp = jnp.dot(dov, vt.astype(BF16), preferred_element_type=F32)
            return p, (p * (dp - dl_ref[...])).astype(BF16)

        @pl.when((qi == 0) & (kj == 0))
        def _():
            dk_acc[...] = jnp.zeros_like(dk_acc)
            dv_acc[...] = jnp.zeros_like(dv_acc)

        @pl.when(kj == 0)
        def _():
            dl_ref[...] = jnp.sum(o_ref[0].reshape(rows, dh) * do_ref[0].reshape(rows, dh),
                                  axis=-1, keepdims=True)
            _, ds = tile(kct_ref[0], vct_ref[0])
            dq_acc[...] = jnp.dot(ds, kc_ref[0].astype(BF16), preferred_element_type=F32)

        p, ds = tile(klt_ref[0], vlt_ref[0])
        dq_acc[...] += jnp.dot(ds, kl_ref[0].astype(BF16), preferred_element_type=F32)
        ks = pl.ds(pl.multiple_of(kj * bk, bk), bk)
        dv_acc[ks] += lax.dot_general(p.astype(BF16), dov, tn, preferred_element_type=F32)
        dk_acc[ks] += lax.dot_general(ds, q_raw, tn, preferred_element_type=F32)

        @pl.when(kj == nkv - 1)
        def _():
            dq_ref[0] = (dq_acc[...] * ATT_SCALE).reshape(g, bq, dh)
            delta_ref[0] = dl_ref[...].reshape(g, bq, 1)

        @pl.when((qi == nq - 1) & (kj == nkv - 1))
        def _():
            dk_acc[...] = dk_acc[...] * ATT_SCALE
            pltpu.sync_copy(dk_acc, dk_hbm.at[h])
            pltpu.sync_copy(dv_acc, dv_hbm.at[h])

    q_spec = pl.BlockSpec((1, g, bq, dh), lambda h, i, j: (h, 0, i, 0))
    r_spec = pl.BlockSpec((1, g, bq, 1), lambda h, i, j: (h, 0, i, 0))
    c_spec = pl.BlockSpec((1, c, dh), lambda h, i, j: (h, 0, 0))
    ct_spec = pl.BlockSpec((1, dh, c), lambda h, i, j: (h, 0, 0))
    l_spec = pl.BlockSpec((1, bk, dh), lambda h, i, j: (h, j, 0))
    lt_spec = pl.BlockSpec((1, dh, bk), lambda h, i, j: (h, 0, j))
    any_spec = pl.BlockSpec(memory_space=pl.ANY)
    kv_shape = jax.ShapeDtypeStruct((kvh, t, dh), F32)
    return pl.pallas_call(
        body, name="attn_bwd_full", grid=(kvh, nq, nkv),
        in_specs=[q_spec, q_spec, q_spec, r_spec, c_spec, ct_spec, ct_spec, l_spec, lt_spec, lt_spec],
        out_specs=[q_spec, r_spec, any_spec, any_spec],
        out_shape=[jax.ShapeDtypeStruct(q.shape, F32), jax.ShapeDtypeStruct((kvh, g, t, 1), F32),
                   kv_shape, kv_shape],
        scratch_shapes=[pltpu.VMEM((rows, dh), F32), pltpu.VMEM((rows, 1), F32),
                        pltpu.VMEM((t, dh), F32), pltpu.VMEM((t, dh), F32)],
        compiler_params=_cparams(("arbitrary", "arbitrary", "arbitrary"), 4 * 2 * rows * bk,
                                 2 * 4 * t * LANES),
    )(q, o, do, lse, kc, kct, vct, kl, klt, vlt)


BAND_BQ = 512


def _band_specs(t, shape_of):
    per = BAND_BQ // WINDOW
    n_halo = t // WINDOW

    def spec(n, index):
        shape, axis = shape_of(n)

        def index_map(h, i):
            idx = [h] + [0] * (len(shape) - 1)
            idx[axis] = index(i)
            return tuple(idx)

        return pl.BlockSpec(shape, index_map)

    return [spec(WINDOW, lambda i: jnp.maximum(i * per - 1, 0)),
            spec(BAND_BQ, lambda i: i),
            spec(WINDOW, lambda i: jnp.minimum((i + 1) * per, n_halo - 1))]


def _band_visible(i, t, c, rows):
    cols = c + BAND_BQ + 2 * WINDOW
    col = lax.broadcasted_iota(jnp.int32, (rows, cols), 1)
    qpos = i * BAND_BQ + lax.broadcasted_iota(jnp.int32, (rows, cols), 0) % BAND_BQ
    kpos = i * BAND_BQ - WINDOW + (col - c)
    return (col < c) | ((kpos >= 0) & (kpos < t) & (jnp.abs(kpos - qpos) <= WINDOW))


def _band_fwd(q, klt, vl, kct, vc, sink):
    kvh, g, t, dh = q.shape
    c = kct.shape[2]
    rows = g * BAND_BQ

    def body(q_ref, kct_ref, vc_ref, ktp, ktc, ktn, vp, vcur, vn, sink_ref, o_ref, lse_ref):
        h, i = pl.program_id(0), pl.program_id(1)
        qv = (q_ref[0].reshape(rows, dh) * ATT_SCALE).astype(BF16)
        kt = jnp.concatenate([kct_ref[0], ktp[0], ktc[0], ktn[0]], axis=1).astype(BF16)
        v = jnp.concatenate([vc_ref[0], vp[0], vcur[0], vn[0]], axis=0).astype(BF16)
        s = jnp.where(_band_visible(i, t, c, rows), jnp.dot(qv, kt, preferred_element_type=F32), NEG_BIG)
        sink_r = _sink_rows(sink_ref, h, BAND_BQ)
        m = jnp.maximum(sink_r, jnp.max(s, axis=-1, keepdims=True))
        p = jnp.exp(s - m)
        l = jnp.exp(sink_r - m) + jnp.sum(p, axis=-1, keepdims=True)
        o_ref[0] = (jnp.dot(p.astype(BF16), v, preferred_element_type=F32) / l).reshape(g, BAND_BQ, dh)
        lse_ref[0] = (m + jnp.log(l)).reshape(g, BAND_BQ, 1)

    q_spec = pl.BlockSpec((1, g, BAND_BQ, dh), lambda h, i: (h, 0, i, 0))
    r_spec = pl.BlockSpec((1, g, BAND_BQ, 1), lambda h, i: (h, 0, i, 0))
    in_specs = ([q_spec, pl.BlockSpec((1, dh, c), lambda h, i: (h, 0, 0)),
                 pl.BlockSpec((1, c, dh), lambda h, i: (h, 0, 0))]
                + _band_specs(t,lambda n: ((1, dh, n), 2))
                + _band_specs(t,lambda n: ((1, n, dh), 1))
                + [pl.BlockSpec(memory_space=pltpu.SMEM)])
    return pl.pallas_call(
        body, name="attn_band_fwd", grid=(kvh, t // BAND_BQ), in_specs=in_specs, out_specs=[q_spec, r_spec],
        out_shape=[jax.ShapeDtypeStruct(q.shape, F32), jax.ShapeDtypeStruct((kvh, g, t, 1), F32)],
        compiler_params=_cparams(("parallel", "parallel"), 4 * 2 * rows * (c + BAND_BQ + 2 * WINDOW)),
    )(q, kct, vc, klt, klt, klt, vl, vl, vl, sink)


def _band_dq(q, kl, klt, vlt, kc, kct, vct, sink, o, do, lse):
    kvh, g, t, dh = q.shape
    c = kc.shape[1]
    rows = g * BAND_BQ

    def body(q_ref, o_ref, do_ref, lse_ref, kc_ref, kct_ref, vct_ref, kp, kcur, kn, ktp, ktc, ktn,
             vtp, vtc, vtn, sink_ref, dq_ref, delta_ref, dsink_ref):
        h, i = pl.program_id(0), pl.program_id(1)
        qv = (q_ref[0].reshape(rows, dh) * ATT_SCALE).astype(BF16)
        dov = do_ref[0].reshape(rows, dh)
        lse_v = lse_ref[0].reshape(rows, 1)
        kt = jnp.concatenate([kct_ref[0], ktp[0], ktc[0], ktn[0]], axis=1).astype(BF16)
        vt = jnp.concatenate([vct_ref[0], vtp[0], vtc[0], vtn[0]], axis=1).astype(BF16)
        k = jnp.concatenate([kc_ref[0], kp[0], kcur[0], kn[0]], axis=0).astype(BF16)
        s = jnp.where(_band_visible(i, t, c, rows), jnp.dot(qv, kt, preferred_element_type=F32), NEG_BIG)
        p = jnp.exp(s - lse_v)
        delta = jnp.sum(o_ref[0].reshape(rows, dh) * dov, axis=-1, keepdims=True)
        dp = jnp.dot(dov.astype(BF16), vt, preferred_element_type=F32)
        ds = (p * (dp - delta)).astype(BF16)
        dq_ref[0] = (jnp.dot(ds, k, preferred_element_type=F32) * ATT_SCALE).reshape(g, BAND_BQ, dh)
        delta_ref[0] = delta.reshape(g, BAND_BQ, 1)
        p_sink = jnp.exp(_sink_rows(sink_ref, h, BAND_BQ) - lse_v)
        dsink_ref[0] = (-p_sink * delta).reshape(g, BAND_BQ, 1)

    q_spec = pl.BlockSpec((1, g, BAND_BQ, dh), lambda h, i: (h, 0, i, 0))
    r_spec = pl.BlockSpec((1, g, BAND_BQ, 1), lambda h, i: (h, 0, i, 0))
    ct_spec = pl.BlockSpec((1, dh, c), lambda h, i: (h, 0, 0))
    rows_of = lambda n: ((1, n, dh), 1)
    lanes_of = lambda n: ((1, dh, n), 2)
    in_specs = ([q_spec, q_spec, q_spec, r_spec, pl.BlockSpec((1, c, dh), lambda h, i: (h, 0, 0)), ct_spec, ct_spec]
                + _band_specs(t,rows_of) + _band_specs(t,lanes_of) + _band_specs(t,lanes_of)
                + [pl.BlockSpec(memory_space=pltpu.SMEM)])
    row_shape = jax.ShapeDtypeStruct((kvh, g, t, 1), F32)
    return pl.pallas_call(
        body, name="attn_band_dq", grid=(kvh, t // BAND_BQ), in_specs=in_specs,
        out_specs=[q_spec, r_spec, r_spec], out_shape=[jax.ShapeDtypeStruct(q.shape, F32), row_shape, row_shape],
        compiler_params=_cparams(("parallel", "parallel"), 4 * 3 * rows * (c + BAND_BQ + 2 * WINDOW)),
    )(q, o, do, lse, kc, kct, vct, kl, kl, kl, klt, klt, klt, vlt, vlt, vlt, sink)


def _band_dkv(q, do, lse, delta, klt, vlt):
    kvh, g, t, dh = q.shape
    span = BAND_BQ + 2 * WINDOW
    rows = g * span
    tn = (((0,), (0,)), ((), ()))

    def body(qp, qc, qn, dop, doc, don, lp, lc, ln, dp_, dc_, dn_, kt_ref, vt_ref, dk_ref, dv_ref):
        j = pl.program_id(1)

        def stack(a, b, c_):
            return jnp.concatenate([jnp.concatenate([a[0, gi], b[0, gi], c_[0, gi]], axis=0)
                                    for gi in range(g)], axis=0)

        q_all, do_all = stack(qp, qc, qn), stack(dop, doc, don).astype(BF16)
        lse_all, delta_all = stack(lp, lc, ln), stack(dp_, dc_, dn_)
        s = jnp.dot((q_all * ATT_SCALE).astype(BF16), kt_ref[0].astype(BF16), preferred_element_type=F32)
        qpos = j * BAND_BQ - WINDOW + lax.broadcasted_iota(jnp.int32, (rows, BAND_BQ), 0) % span
        kpos = j * BAND_BQ + lax.broadcasted_iota(jnp.int32, (rows, BAND_BQ), 1)
        s = jnp.where((qpos >= 0) & (qpos < t) & (jnp.abs(kpos - qpos) <= WINDOW), s, NEG_BIG)
        p = jnp.exp(s - lse_all)
        dp = jnp.dot(do_all, vt_ref[0].astype(BF16), preferred_element_type=F32)
        ds = (p * (dp - delta_all)).astype(BF16)
        dv_ref[0] = lax.dot_general(p.astype(BF16), do_all, tn, preferred_element_type=F32)
        dk_ref[0] = lax.dot_general(ds, q_all.astype(BF16), tn, preferred_element_type=F32) * ATT_SCALE

    q_specs = _band_specs(t,lambda n: ((1, g, n, dh), 2))
    r_specs = _band_specs(t,lambda n: ((1, g, n, 1), 2))
    kt_spec = pl.BlockSpec((1, dh, BAND_BQ), lambda h, j: (h, 0, j))
    k_spec = pl.BlockSpec((1, BAND_BQ, dh), lambda h, j: (h, j, 0))
    kv_shape = jax.ShapeDtypeStruct((kvh, t, dh), F32)
    return pl.pallas_call(
        body, name="attn_band_dkv", grid=(kvh, t // BAND_BQ),
        in_specs=q_specs + q_specs + r_specs + r_specs + [kt_spec, kt_spec], out_specs=[k_spec, k_spec],
        out_shape=[kv_shape, kv_shape],
        compiler_params=_cparams(("parallel", "parallel"), 4 * 3 * rows * BAND_BQ),
    )(q, q, q, do, do, do, lse, lse, lse, delta, delta, delta, klt, vlt)


def make_attention(kind, has_sink):
    has_lat = kind != "ctx"

    def unpack(args):
        it = iter(args)
        q, kc, vc = next(it), next(it), next(it)
        kl, vl = (next(it), next(it)) if has_lat else (None, None)
        sink = next(it) if has_sink else None
        return q, kl, vl, kc, vc, sink

    def forward(args):
        q, kl, vl, kc, vc, sink = unpack(args)
        klt = jnp.swapaxes(kl, 1, 2) if has_lat else None
        if kind == "window":
            return _band_fwd(q, klt, vl, jnp.swapaxes(kc, 1, 2), vc, sink)
        return _flash_fwd(q, klt, vl, jnp.swapaxes(kc, 1, 2), vc, sink)

    @jax.custom_vjp
    def op(*args):
        return forward(args)[0]

    def fwd(*args):
        o, lse = forward(args)
        return o, (args, o, lse)

    def bwd(res, do):
        args, o, lse = res
        q, kl, vl, kc, vc, sink = unpack(args)
        if kind == "global":
            dq, delta, dkl, dvl = _flash_bwd_full(q, kl, vl, kc, vc, o, do, lse)
            grads = [dq, *_flash_dkv(q, do, lse, delta, kc, vc), dkl, dvl]
        elif kind == "window":
            klt, vlt = jnp.swapaxes(kl, 1, 2), jnp.swapaxes(vl, 1, 2)
            dq, delta, dsink_rows = _band_dq(q, kl, klt, vlt, kc, jnp.swapaxes(kc, 1, 2),
                                             jnp.swapaxes(vc, 1, 2), sink, o, do, lse)
            grads = [dq, *_flash_dkv(q, do, lse, delta, kc, vc), *_band_dkv(q, do, lse, delta, klt, vlt)]
        else:
            dq, delta, dsink_rows = _ctx_dq(q, kc, vc, sink, o, do, lse)
            grads = [dq, *_flash_dkv(q, do, lse, delta, kc, vc)]
        if has_sink:
            grads.append(jnp.sum(dsink_rows, axis=(2, 3)).reshape(-1))
        return tuple(grads)

    op.defvjp(fwd, bwd)
    return op


attn_window = make_attention("window", True)
attn_global = make_attention("global", False)
attn_ctx_sink = make_attention("ctx", True)
attn_ctx = make_attention("ctx", False)


def _to_heads(z, n_heads):
    return z.reshape(z.shape[0], n_heads, HEAD_DIM).transpose(1, 0, 2)


def _q_heads(z):
    return z.reshape(z.shape[0], KV_HEADS, Q_PER_KV, HEAD_DIM).transpose(1, 2, 0, 3)


def _from_q_heads(o):
    return o.transpose(2, 0, 1, 3).reshape(o.shape[2], KV_HEADS * Q_PER_KV * HEAD_DIM)


SCAN_PAIRS = 4


def _scan_tile(t):
    return _tile(t, 512, 2 * SCAN_PAIRS)


def _scan_fwd_call(bre, bim, lre, lim, h0re, h0im, rev):
    t = bre.shape[0]
    tt = _scan_tile(t)
    nb = t // tt
    plane = bre.shape[1:]

    def body(bre_ref, bim_ref, lre_ref, lim_ref, h0re_ref, h0im_ref, sre_ref, sim_ref, h_ref):
        @pl.when(pl.program_id(0) == 0)
        def _():
            h_ref[0] = h0re_ref[...]
            h_ref[1] = h0im_ref[...]

        ar, ai = lre_ref[...], lim_ref[...]
        a2r, a2i = ar * ar - ai * ai, 2.0 * ar * ai

        def step(j, carry):
            hr, hi = carry
            for u in range(SCAN_PAIRS):
                t1 = (tt - 1 - 2 * (SCAN_PAIRS * j + u)) if rev else 2 * (SCAN_PAIRS * j + u)
                t2 = (t1 - 1) if rev else (t1 + 1)
                b1r, b1i, b2r, b2i = bre_ref[t1], bim_ref[t1], bre_ref[t2], bim_ref[t2]
                er = ar * b1r - ai * b1i + b2r
                ei = ar * b1i + ai * b1r + b2i
                sre_ref[t1] = ar * hr - ai * hi + b1r
                sim_ref[t1] = ar * hi + ai * hr + b1i
                hr, hi = a2r * hr - a2i * hi + er, a2r * hi + a2i * hr + ei
                sre_ref[t2] = hr
                sim_ref[t2] = hi
            return hr, hi

        hr, hi = lax.fori_loop(0, tt // (2 * SCAN_PAIRS), step, (h_ref[0], h_ref[1]))
        h_ref[0] = hr
        h_ref[1] = hi

    blk = pl.BlockSpec((tt,) + plane, (lambda i: (nb - 1 - i, 0, 0)) if rev else (lambda i: (i, 0, 0)))
    par = pl.BlockSpec(plane, lambda i: (0, 0))
    return pl.pallas_call(
        body, name="s5_scan_fwd", grid=(nb,), in_specs=[blk, blk, par, par, par, par], out_specs=[blk, blk],
        out_shape=[jax.ShapeDtypeStruct(bre.shape, F32)] * 2,
        scratch_shapes=[pltpu.VMEM((2,) + plane, F32)],
        compiler_params=_cparams(("arbitrary",), 4 * 4 * tt * plane[0] * plane[1]),
    )(bre, bim, lre, lim, h0re, h0im)


def _scan_bwd_call(gre, gim, sre, sim, lre, lim, h0re, h0im, rev):
    t = gre.shape[0]
    tt = _scan_tile(t)
    nb = t // tt
    plane = gre.shape[1:]
    down = not rev

    def body(gre_ref, gim_ref, sre_ref, sim_ref, lre_ref, lim_ref, h0re_ref, h0im_ref,
             dbre_ref, dbim_ref, dare_ref, daim_ref, dh0re_ref, dh0im_ref, carry_ref):
        i = pl.program_id(0)

        @pl.when(i == 0)
        def _():
            carry_ref[...] = jnp.zeros_like(carry_ref)

        ar, ai = lre_ref[...], lim_ref[...]
        a2r, a2i = ar * ar - ai * ai, 2.0 * ar * ai

        def step(j, carry):
            gr, gi, dar, dai = carry
            for u in range(SCAN_PAIRS):
                t1 = (tt - 1 - 2 * (SCAN_PAIRS * j + u)) if down else 2 * (SCAN_PAIRS * j + u)
                t2 = (t1 - 1) if down else (t1 + 1)
                h1r, h1i, h2r, h2i = sre_ref[t1], sim_ref[t1], sre_ref[t2], sim_ref[t2]
                c1r, c1i, c2r, c2i = gre_ref[t1], gim_ref[t1], gre_ref[t2], gim_ref[t2]
                g1r = c1r + ar * gr + ai * gi
                g1i = c1i + ar * gi - ai * gr
                er = c2r + ar * c1r + ai * c1i
                ei = c2i + ar * c1i - ai * c1r
                dar = dar + ((h1r * gr + h1i * gi) + (h2r * g1r + h2i * g1i))
                dai = dai + ((h1r * gi - h1i * gr) + (h2r * g1i - h2i * g1r))
                gr, gi = er + a2r * gr + a2i * gi, ei + a2r * gi - a2i * gr
                dbre_ref[t1] = g1r
                dbim_ref[t1] = g1i
                dbre_ref[t2] = gr
                dbim_ref[t2] = gi
            return gr, gi, dar, dai

        gr, gi, dar, dai = lax.fori_loop(
            0, tt // (2 * SCAN_PAIRS), step, (carry_ref[0], carry_ref[1], carry_ref[2], carry_ref[3]))
        carry_ref[0] = gr
        carry_ref[1] = gi
        carry_ref[2] = dar
        carry_ref[3] = dai

        @pl.when(i == nb - 1)
        def _():
            hr, hi = h0re_ref[...], h0im_ref[...]
            dare_ref[...] = dar + hr * gr + hi * gi
            daim_ref[...] = dai + hr * gi - hi * gr
            dh0re_ref[...] = ar * gr + ai * gi
            dh0im_ref[...] = ar * gi - ai * gr

    blk = pl.BlockSpec((tt,) + plane, (lambda i: (nb - 1 - i, 0, 0)) if down else (lambda i: (i, 0, 0)))
    par = pl.BlockSpec(plane, lambda i: (0, 0))
    return pl.pallas_call(
        body, name="s5_scan_bwd", grid=(nb,), in_specs=[blk, blk, blk, blk, par, par, par, par],
        out_specs=[blk, blk, par, par, par, par],
        out_shape=[jax.ShapeDtypeStruct(gre.shape, F32)] * 2 + [jax.ShapeDtypeStruct(plane, F32)] * 4,
        scratch_shapes=[pltpu.VMEM((4,) + plane, F32)],
        compiler_params=_cparams(("arbitrary",), 4 * 6 * tt * plane[0] * plane[1]),
    )(gre, gim, sre, sim, lre, lim, h0re, h0im)


def make_scan(rev):
    @jax.custom_vjp
    def op(bre, bim, lre, lim, h0re, h0im):
        return tuple(_scan_fwd_call(bre, bim, lre, lim, h0re, h0im, rev))

    def fwd(bre, bim, lre, lim, h0re, h0im):
        sre, sim = _scan_fwd_call(bre, bim, lre, lim, h0re, h0im, rev)
        return (sre, sim), (sre, sim, lre, lim, h0re, h0im)

    def bwd(res, cts):
        sre, sim, lre, lim, h0re, h0im = res
        return tuple(_scan_bwd_call(cts[0], cts[1], sre, sim, lre, lim, h0re, h0im, rev))

    op.defvjp(fwd, bwd)
    return op


scan_up = make_scan(False)
scan_down = make_scan(True)


def _adamw_call(parts, w, m, v, name):
    r, c = w.shape
    tr = _tile(r, 256, SUBLANES)
    nparts = parts.shape[0]
    c1 = 1.0 - ADAM_B1 ** ADAM_STEP
    c2 = 1.0 - ADAM_B2 ** ADAM_STEP

    def body(p_ref, w_ref, m_ref, v_ref, g_ref, d_ref, nm_ref, nv_ref):
        g = p_ref[0].astype(F32)
        for s in range(1, nparts):
            g = g + p_ref[s].astype(F32)
        m1 = ADAM_B1 * m_ref[...] + (1.0 - ADAM_B1) * g
        v1 = ADAM_B2 * v_ref[...] + (1.0 - ADAM_B2) * jnp.square(g)
        g_ref[...] = g
        nm_ref[...] = m1
        nv_ref[...] = v1
        d_ref[...] = -ADAM_LR * ((m1 / c1) / (jnp.sqrt(v1 / c2) + ADAM_EPS) + ADAM_WD * w_ref[...])

    blk = pl.BlockSpec((tr, c), lambda i: (i, 0))
    return pl.pallas_call(
        body, name=name, grid=(r // tr,),
        in_specs=[pl.BlockSpec((nparts, tr, c), lambda i: (0, i, 0)), blk, blk, blk], out_specs=[blk] * 4,
        out_shape=[jax.ShapeDtypeStruct((r, c), F32)] * 4,
        compiler_params=_cparams(("parallel",), 4 * tr * c * (nparts + 7)),
    )(parts, w, m, v)


def _peer(k):
    x, y, c = lax.axis_index("x"), lax.axis_index("y"), lax.axis_index("c")
    px = 1 - x if k & 4 else x
    py = 1 - y if k & 2 else y
    pc = 1 - c if k & 1 else c
    return (px, py, pc), 4 * px + 2 * py + pc


def _my_slot():
    return 4 * lax.axis_index("x") + 2 * lax.axis_index("y") + lax.axis_index("c")


def all_gather(x, name):
    def body(x_ref, out_ref, send_sems, recv_sems, local_sem):
        me = _my_slot()
        mine = pltpu.make_async_copy(x_ref, out_ref.at[me], local_sem)
        mine.start()
        sends = []
        for k in range(1, N_DEV):
            peer, _ = _peer(k)
            cp = pltpu.make_async_remote_copy(
                src_ref=x_ref, dst_ref=out_ref.at[me], send_sem=send_sems.at[k - 1],
                recv_sem=recv_sems.at[k - 1], device_id=peer, device_id_type=MESH)
            cp.start()
            sends.append(cp)
        for k in range(1, N_DEV):
            peer, slot = _peer(k)
            pltpu.make_async_remote_copy(
                src_ref=x_ref, dst_ref=out_ref.at[slot], send_sem=send_sems.at[k - 1],
                recv_sem=recv_sems.at[k - 1], device_id=peer, device_id_type=MESH).wait_recv()
        for cp in sends:
            cp.wait_send()
        mine.wait()

    return pl.pallas_call(
        body, name=name,
        in_specs=[pl.BlockSpec(memory_space=pl.ANY)], out_specs=pl.BlockSpec(memory_space=pl.ANY),
        out_shape=jax.ShapeDtypeStruct((N_DEV,) + tuple(x.shape), x.dtype),
        scratch_shapes=[pltpu.SemaphoreType.DMA((N_DEV - 1,)), pltpu.SemaphoreType.DMA((N_DEV - 1,)),
                        pltpu.SemaphoreType.DMA],
    )(x)


def sibling_swap(x, name):
    def body(x_ref, out_ref, send_sem, recv_sem):
        x_, y_, c_ = lax.axis_index("x"), lax.axis_index("y"), lax.axis_index("c")
        cp = pltpu.make_async_remote_copy(
            src_ref=x_ref.at[1 - c_], dst_ref=out_ref, send_sem=send_sem, recv_sem=recv_sem,
            device_id=(x_, y_, 1 - c_), device_id_type=MESH)
        cp.start()
        cp.wait()

    return pl.pallas_call(
        body, name=name,
        in_specs=[pl.BlockSpec(memory_space=pl.ANY)], out_specs=pl.BlockSpec(memory_space=pl.ANY),
        out_shape=jax.ShapeDtypeStruct(x.shape[1:], x.dtype),
        scratch_shapes=[pltpu.SemaphoreType.DMA, pltpu.SemaphoreType.DMA],
    )(x)


def chip_exchange(x, name):
    def body(x_ref, out_ref, send_sems, recv_sems, local_sem):
        x_, y_, c_ = lax.axis_index("x"), lax.axis_index("y"), lax.axis_index("c")
        mine = 2 * x_ + y_
        local = pltpu.make_async_copy(x_ref.at[mine], out_ref.at[mine], local_sem)
        local.start()

        def copy(k):
            px = 1 - x_ if k & 2 else x_
            py = 1 - y_ if k & 1 else y_
            peer = 2 * px + py
            send = pltpu.make_async_remote_copy(
                src_ref=x_ref.at[peer], dst_ref=out_ref.at[mine], send_sem=send_sems.at[k - 1],
                recv_sem=recv_sems.at[k - 1], device_id=(px, py, c_), device_id_type=MESH)
            recv = pltpu.make_async_remote_copy(
                src_ref=x_ref.at[peer], dst_ref=out_ref.at[peer], send_sem=send_sems.at[k - 1],
                recv_sem=recv_sems.at[k - 1], device_id=(px, py, c_), device_id_type=MESH)
            return send, recv

        copies = [copy(k) for k in range(1, 4)]
        for send, _ in copies:
            send.start()
        for _, recv in copies:
            recv.wait_recv()
        for send, _ in copies:
            send.wait_send()
        local.wait()

    return pl.pallas_call(
        body, name=name,
        in_specs=[pl.BlockSpec(memory_space=pl.ANY)], out_specs=pl.BlockSpec(memory_space=pl.ANY),
        out_shape=jax.ShapeDtypeStruct(x.shape, x.dtype),
        scratch_shapes=[pltpu.SemaphoreType.DMA((3,)), pltpu.SemaphoreType.DMA((3,)),
                        pltpu.SemaphoreType.DMA],
    )(x)


def all_gather_two_level(x, name):
    def body(x_ref, out_ref, send_sems, recv_sems, local_sem):
        x_, y_, c_ = lax.axis_index("x"), lax.axis_index("y"), lax.axis_index("c")
        me, sibling = (x_, y_, c_), (x_, y_, 1 - c_)
        chips = [(1 - x_, y_), (x_, 1 - y_), (1 - x_, 1 - y_)]

        def slot(px, py, pc):
            return out_ref.at[4 * px + 2 * py + pc]

        def copy(k, block, to, src=None):
            return pltpu.make_async_remote_copy(
                src_ref=slot(*block) if src is None else src, dst_ref=slot(*block),
                send_sem=send_sems.at[k], recv_sem=recv_sems.at[k], device_id=to, device_id_type=MESH)

        mine = pltpu.make_async_copy(x_ref, slot(*me), local_sem)
        mine.start()
        first = [copy(0, me, sibling, src=x_ref)]
        first += [copy(1 + j, me, (*chip, c_), src=x_ref) for j, chip in enumerate(chips)]
        for cp in first:
            cp.start()
        passed = [copy(4 + j, (*chip, c_), sibling) for j, chip in enumerate(chips)]
        for j, chip in enumerate(chips):
            copy(1 + j, (*chip, c_), me).wait_recv()
            passed[j].start()
        copy(0, sibling, me).wait_recv()
        for j, chip in enumerate(chips):
            copy(4 + j, (*chip, 1 - c_), me).wait_recv()
        for cp in first + passed:
            cp.wait_send()
        mine.wait()

    return pl.pallas_call(
        body, name=name,
        in_specs=[pl.BlockSpec(memory_space=pl.ANY)], out_specs=pl.BlockSpec(memory_space=pl.ANY),
        out_shape=jax.ShapeDtypeStruct((N_DEV,) + tuple(x.shape), x.dtype),
        scratch_shapes=[pltpu.SemaphoreType.DMA((N_DEV - 1,)), pltpu.SemaphoreType.DMA((N_DEV - 1,)),
                        pltpu.SemaphoreType.DMA],
    )(x)


def _rope_tables(t):
    n_freq = HEAD_DIM // 4
    tok = jnp.arange(t)
    inv = ROPE_THETA ** (-jnp.arange(n_freq, dtype=F32) / n_freq)
    a_row = (tok // GRID_W).astype(F32)[:, None] * inv
    a_col = (tok % GRID_W).astype(F32)[:, None] * inv
    cos = jnp.concatenate([jnp.cos(a_row)] * 2 + [jnp.cos(a_col)] * 2, axis=1)
    sin = jnp.concatenate([-jnp.sin(a_row), jnp.sin(a_row), -jnp.sin(a_col), jnp.sin(a_col)], axis=1)
    return jnp.concatenate([cos, cos], axis=1), jnp.concatenate([sin, sin], axis=1)


def _block_diag(blocks):
    g, a, b = blocks.shape
    eye = jnp.eye(g, dtype=blocks.dtype)
    return jnp.einsum("gab,gk->gakb", blocks, eye).reshape(g * a, g * b)


def _ffn_fwd_calls(x, mod, g, w_in, w_out):
    shift, scale, gate = mod[0:1], mod[1:2], mod[2:3]
    h, = _row_call(lambda xt, gt, sh, sc: (_norm_mod_fn(xt, gt, sh, sc)[0].astype(BF16),),
                   [x], [g, shift, scale], 0, "ffn_norm")
    u = _mm(h, w_in, out_dtype=BF16, name="ffn_up")
    a, = _row_call(lambda ut: (_swiglu_fn(ut.astype(F32))[0].astype(BF16),), [u], [], 0, "ffn_act")
    y = _mm(a, w_out, name="ffn_down")
    out, = _row_call(functools.partial(_resid_fn, 0.5), [x, y], [gate], 0, "ffn_resid")
    return out, (h, u, a, y)


@jax.custom_vjp
def _ffn_half(x, mod, g, w_in, w_out):
    return _ffn_fwd_calls(x, mod, g, w_in, w_out)[0]


def _ffn_half_fwd(x, mod, g, w_in, w_out):
    out, saved = _ffn_fwd_calls(x, mod, g, w_in, w_out)
    return out, (x, mod, g, w_in, w_out, saved)


def _ffn_half_bwd(res, dxn):
    x, mod, g, w_in, w_out, (h, u, a, y) = res
    shift, scale, gate = mod[0:1], mod[1:2], mod[2:3]

    def resid_bwd(dt, yt, gt):
        return (0.5 * gt * dt).astype(BF16), 0.5 * jnp.sum(dt * yt, axis=0, keepdims=True)

    dy, dgate = _row_call(resid_bwd, [dxn, y], [gate], 1, "ffn_resid_bwd")
    da = _mm(dy, w_out, tb=True, out_dtype=BF16, name="ffn_down_dx")
    dw_out = _mm(a, dy, ta=True, out_dtype=w_out.dtype, name="ffn_down_dw")

    def act_bwd(ut, dat):
        return (jax.vjp(_swiglu_fn, ut.astype(F32))[1]((dat.astype(F32),))[0].astype(BF16),)

    du, = _row_call(act_bwd, [u, da], [], 0, "ffn_act_bwd")
    dh = _mm(du, w_in, tb=True, name="ffn_up_dx")
    dw_in = _mm(h, du, ta=True, out_dtype=w_in.dtype, name="ffn_up_dw")

    def norm_bwd(xt, dht, dt, gt, sh, sc):
        dx, dg, dsh, dsc = jax.vjp(_norm_mod_fn, xt, gt, sh, sc)[1]((dht,))
        return dx + dt, dg, dsh, dsc

    dx, dg, dshift, dscale = _row_call(norm_bwd, [x, dh, dxn], [g, shift, scale], 3, "ffn_norm_bwd")
    return dx, jnp.concatenate([dshift, dscale, dgate], axis=0), dg, dw_in, dw_out


_ffn_half.defvjp(_ffn_half_fwd, _ffn_half_bwd)


def _s5_discretize(a_re, a_im, log_dt, b_re, b_im):
    lam = lax.complex(a_re, a_im)
    dt = jnp.exp(log_dt)[:, None]
    lam_bar = jnp.exp(lam * dt)
    b_bar = ((lam_bar - 1.0) / lam)[..., None] * lax.complex(b_re, b_im)
    return lam_bar, b_bar


def _s5_branch(u_lat, u_ctx, w, l, with_ctx_out):
    zero = jnp.zeros((SUBLANES, SSM_LANES // SUBLANES), F32)
    lat_terms, ctx_terms = [], []
    for d, scan in enumerate((scan_up, scan_down)):
        lam_bar, b_bar = _s5_discretize(w["ssm_a_re"][l, d], w["ssm_a_im"][l, d], w["ssm_log_dt"][l, d],
                                        w["ssm_b_re"][l, d], w["ssm_b_im"][l, d])
        lre = jnp.real(lam_bar).reshape(zero.shape)
        lim = jnp.imag(lam_bar).reshape(zero.shape)
        b_t = jnp.swapaxes(b_bar, 1, 2)
        b_re, b_im = _block_diag(jnp.real(b_t)), _block_diag(jnp.imag(b_t))
        c_re = _block_diag(jnp.swapaxes(w["ssm_c_re"][l, d], 1, 2))
        c_im = _block_diag(jnp.swapaxes(w["ssm_c_im"][l, d], 1, 2))
        sc_re, sc_im = scan(linear_to_planes(u_ctx, b_re), linear_to_planes(u_ctx, b_im), lre, lim, zero, zero)
        last = 0 if d == 1 else u_ctx.shape[0] - 1
        sl_re, sl_im = scan(linear_to_planes(u_lat, b_re), linear_to_planes(u_lat, b_im), lre, lim,
                            sc_re[last], sc_im[last])
        lat_terms += [linear_from_planes(sl_re, c_re), linear_from_planes(sl_im, c_im)]
        if with_ctx_out:
            ctx_terms += [linear_from_planes(sc_re, c_re), linear_from_planes(sc_im, c_im)]
    d_skip = w["ssm_d"][l][None, :]

    def out(terms, u):
        y, = s5_pre(*terms, u, d_skip)
        return glu(linear(y, w["glu_w"][l]))[0]

    return out(lat_terms, u_lat), (out(ctx_terms, u_ctx) if with_ctx_out else None)


def _pool_branch(xa, w, l):
    y = linear(pool_diff(xa), _block_diag(w["pool_w"][l]))
    return scale_rows(y, w["pool_scale"][l][None, :])[0]


_CTX_GROUPS = (O_VB - O_KB, O_UC - O_VB, O_KD - O_UC, O_VD - O_KD, CTX_COLS - O_VD)
_ALL_GROUPS = _CTX_GROUPS + (O_QD - O_QB, O_XA - O_QD, O_GATE - O_XA)
project_ctx = make_split_linear(_CTX_GROUPS)
project_all = make_split_linear(_ALL_GROUPS)


def _merge_branches(branches, gate_logits, w, l):
    zs = [linear_b(y, w["branch_w"][l, k]) for k, y in enumerate(branches)]
    return linear(merge(gate_logits, *zs)[0], w["out_w"][l])


def _token_mixer(h, hc, cos, sin, w, l, with_ctx_out):
    w_in, w_gate = w["w_in"][l][:, :O_GATE], w["w_in"][l][:, O_GATE:]
    kb, vb, uc, kd, vd, qb, qd, xa = project_all(h, w_in)
    p_gate = linear_b(h, w_gate)
    if with_ctx_out:
        kb_c, vb_c, uc_c, kd_c, vd_c, qb_c, qd_c, xa_c = project_all(hc, w_in)
        pc_gate = linear_b(hc, w_gate)
    else:
        kb_c, vb_c, uc_c, kd_c, vd_c = project_ctx(hc, w_in[:, :CTX_COLS])
    sink = w["win_sink"][l]
    q_g = jnp.tile(w["qk_norm"][l, 0], KV_HEADS * Q_PER_KV)[None, :]
    k_g = jnp.tile(w["qk_norm"][l, 1], KV_HEADS)[None, :]
    k_win_c = _to_heads(kb_c, KV_HEADS)
    v_win_c = _to_heads(vb_c, KV_HEADS)
    k_glb_c = _to_heads(head_norm(kd_c, k_g)[0], KV_HEADS)
    v_glb_c = _to_heads(vd_c, KV_HEADS)
    y_a = _pool_branch(xa, w, l)
    q_win = _q_heads(rope(qb, cos, sin)[0])
    k_win = _to_heads(rope(kb, cos, sin)[0], KV_HEADS)
    v_win = _to_heads(vb, KV_HEADS)
    y_b = _from_q_heads(attn_window(q_win, k_win_c, v_win_c, k_win, v_win, sink))
    y_c, y_c_ctx = _s5_branch(uc, uc_c, w, l, with_ctx_out)
    q_glb = _q_heads(norm_rope(qd, cos, sin, q_g)[0])
    k_glb = _to_heads(norm_rope(kd, cos, sin, k_g)[0], KV_HEADS)
    v_glb = _to_heads(vd, KV_HEADS)
    y_d = _from_q_heads(attn_global(q_glb, k_glb_c, v_glb_c, k_glb, v_glb))
    y = _merge_branches((y_a, y_b, y_c, y_d), p_gate, w, l)
    if not with_ctx_out:
        return y, None
    y_a_c = _pool_branch(xa_c, w, l)
    y_b_c = _from_q_heads(attn_ctx_sink(_q_heads(qb_c), k_win_c, v_win_c, sink))
    q_glb_c = _q_heads(head_norm(qd_c, q_g)[0])
    y_d_c = _from_q_heads(attn_ctx(q_glb_c, k_glb_c, v_glb_c))
    return y, _merge_branches((y_a_c, y_b_c, y_c_ctx, y_d_c), pc_gate, w, l)


def local_loss(w, x, c, ctx, target):
    depth = w["w_mod"].shape[0]
    cos, sin = _rope_tables(x.shape[0])
    cond = jnp.concatenate([c, w["c_ctx"][None, :], jnp.zeros((COND_ROWS - 2, D_MODEL), F32)], axis=0)
    s_all, = silu_rows(cond)
    for l in range(depth):
        last = l == depth - 1
        m_all = (linear(s_all, w["w_mod"][l]) + w["b_mod"][l][None, :]).reshape(COND_ROWS, N_SUB, 3, D_MODEL)
        m, mc = m_all[0], m_all[1]
        g = w["norm_g"][l][:, None, :]
        x = _ffn_half(x, m[0], g[0], w["ffn_in"][l, 0], w["ffn_out"][l, 0])
        ctx = _ffn_half(ctx, mc[0], g[0], w["ffn_in"][l, 0], w["ffn_out"][l, 0])
        h, = norm_mod(x, g[1], m[1, 0:1], m[1, 1:2])
        hc, = norm_mod(ctx, g[1], mc[1, 0:1], mc[1, 1:2])
        y, y_ctx = _token_mixer(h, hc, cos, sin, w, l, not last)
        x, = resid_full(x, y, m[1, 2:3])
        if not last:
            ctx, = resid_full(ctx, y_ctx, mc[1, 2:3])
        x = _ffn_half(x, m[2], g[2], w["ffn_in"][l, 1], w["ffn_out"][l, 1])
        if not last:
            ctx = _ffn_half(ctx, mc[2], g[2], w["ffn_in"][l, 1], w["ffn_out"][l, 1])
    return jnp.sum(loss_rows(x, target, w["final_g"][None, :])[0])


PACK_COLS = 1024


def _pack(arrays):
    flat = jnp.concatenate([a.reshape(-1) for a in arrays])
    pad = (-flat.shape[0]) % (PACK_COLS * 16)
    return jnp.pad(flat, (0, pad)).reshape(-1, PACK_COLS)


def _unpack(slab, shapes):
    out, off = [], 0
    for s in shapes:
        n = math.prod(s)
        r0, r1 = off // PACK_COLS, -(-(off + n) // PACK_COLS)
        rows, start = slab[r0:r1], off - r0 * PACK_COLS
        if start == 0 and n == (r1 - r0) * PACK_COLS:
            out.append(rows.reshape(s))
        else:
            out.append(rows.reshape(-1)[start:start + n].reshape(s))
        off += n
    return out


def _full_from_shards(gathered, shard_shape, axis):
    z = jnp.moveaxis(gathered.reshape((N_DEV,) + tuple(shard_shape)), 0, axis)
    shape = list(shard_shape)
    shape[axis] *= N_DEV
    return z.reshape(shape)


def _shards_from_full(full, axis):
    shape = list(full.shape)
    shape[axis:axis + 1] = [N_DEV, shape[axis] // N_DEV]
    return jnp.moveaxis(full.reshape(shape), axis, 0)


def kernel(x, c, ctx, c_ctx, w_mod, b_mod, norm_g, ffn_in, ffn_out, w_in, win_sink, qk_norm, pool_w, pool_scale, ssm_a_re, ssm_a_im, ssm_log_dt, ssm_b_re, ssm_b_im, ssm_c_re, ssm_c_im, ssm_d, glu_w, branch_w, out_w, final_g, loss_target, m_c_ctx, m_w_mod, m_b_mod, m_norm_g, m_ffn_in, m_ffn_out, m_w_in, m_win_sink, m_qk_norm, m_pool_w, m_pool_scale, m_ssm_a_re, m_ssm_a_im, m_ssm_log_dt, m_ssm_b_re, m_ssm_b_im, m_ssm_c_re, m_ssm_c_im, m_ssm_d, m_glu_w, m_branch_w, m_out_w, m_final_g, v_c_ctx, v_w_mod, v_b_mod, v_norm_g, v_ffn_in, v_ffn_out, v_w_in, v_win_sink, v_qk_norm, v_pool_w, v_pool_scale, v_ssm_a_re, v_ssm_a_im, v_ssm_log_dt, v_ssm_b_re, v_ssm_b_im, v_ssm_c_re, v_ssm_c_im, v_ssm_d, v_glu_w, v_branch_w, v_out_w, v_final_g):
    given = dict(locals())
    wts = {n: given[n] for n in WEIGHTS}
    mom = {n: given["m_" + n] for n in WEIGHTS}
    var = {n: given["v_" + n] for n in WEIGHTS}
    me = _my_slot()

    shard_shapes = [wts[n].shape for n in SHARDED]
    w_slab = _pack([wts[n] for n in SHARDED])
    gathered = all_gather_two_level(w_slab.astype(BF16), "gather_weights")
    full = dict(wts)
    row = 0
    for n in SHARDED:
        n_rows = math.prod(wts[n].shape) // PACK_COLS
        full[n] = _full_from_shards(gathered[:, row:row + n_rows], wts[n].shape, SHARD_AXIS[n])
        row += n_rows
    g_slab = _pack([norm_g])
    g_all = all_gather(g_slab, "gather_norm_g")
    full["norm_g"] = _full_from_shards(
        jnp.stack([_unpack(g_all[s], [norm_g.shape])[0] for s in range(N_DEV)]), norm_g.shape, 2)

    loss, (gw, gx) = jax.value_and_grad(local_loss, argnums=(0, 1))(full, x[0], c, ctx[0], loss_target[0])
    loss = lax.psum(loss, ("x", "y", "c"))

    dest = [_shards_from_full(gw[n], SHARD_AXIS[n]).reshape(N_DEV // 2, 2, -1, PACK_COLS) for n in SHARDED]
    send = jnp.swapaxes(jnp.concatenate(dest, axis=2), 0, 1)
    from_sibling = sibling_swap(send, "exchange_grads_sibling")
    own = lax.dynamic_index_in_dim(send, lax.axis_index("c"), axis=0, keepdims=False)
    pair, = _row_call(lambda a, b: ((a.astype(F32) + b.astype(F32)).astype(BF16),),
                      [own.reshape(-1, PACK_COLS), from_sibling.reshape(-1, PACK_COLS)], [], 0,
                      "exchange_pair_sum")
    big_parts = chip_exchange(pair.reshape(own.shape), "exchange_grads_chips")
    small_names = SMALL + ("norm_g",)
    small_shapes = [gw[n].shape for n in small_names]
    small_parts = all_gather(_pack([gw[n] for n in small_names]), "gather_small_grads")

    big = _adamw_call(big_parts, w_slab, _pack([mom[n] for n in SHARDED]), _pack([var[n] for n in SHARDED]),
                      "adamw_sharded")
    big = [_unpack(b, shard_shapes) for b in big]
    col = me * norm_g.shape[2]

    def small_slab(src, shard_src):
        padded = jnp.zeros((norm_g.shape[0], norm_g.shape[1], norm_g.shape[2] * N_DEV), F32)
        padded = lax.dynamic_update_slice(padded, shard_src, (0, 0, col))
        return _pack([src[n] for n in SMALL] + [padded])

    small = _adamw_call(small_parts, small_slab(wts, norm_g), small_slab(mom, m_norm_g),
                        small_slab(var, v_norm_g), "adamw_small")
    small = [_unpack(s, small_shapes) for s in small]

    outs = {}
    for kind in range(4):
        for i, n in enumerate(SHARDED):
            outs[(kind, n)] = big[kind][i]
        for i, n in enumerate(small_names):
            val = small[kind][i]
            if n == "norm_g":
                val = lax.dynamic_slice(val, (0, 0, col), norm_g.shape)
            outs[(kind, n)] = val
    return (loss, gx[None], *[outs[(k, n)] for k in range(4) for n in WEIGHTS])
```

```python
import functools
import math

import jax
import jax.numpy as jnp
from jax import lax
from jax.experimental import pallas as pl
from jax.experimental.pallas import tpu as pltpu

F32 = jnp.float32
BF16 = jnp.bfloat16

D_MODEL = 1024
GRID_W = 64
HEAD_DIM = 64
N_BRANCH = 4
BRANCH_W = D_MODEL // N_BRANCH
WINDOW = 128
ROPE_THETA = 10000.0
EPS = 1e-6
D_FF = 2816
N_SUB = 3
POOL_WINDOWS = (2, 4, 8, 16)
POOL_GROUP = BRANCH_W // len(POOL_WINDOWS)
KV_HEADS = 2
Q_PER_KV = 2
SSM_GROUP = 16
SSM_GROUPS = BRANCH_W // SSM_GROUP
SSM_STATE = 64
SSM_LANES = SSM_GROUPS * SSM_STATE
O_KB, O_VB, O_UC, O_KD, O_VD, CTX_COLS = 0, 128, 256, 512, 640, 768
O_QB, O_QD, O_XA, O_GATE = 768, 1024, 1280, 1536
IN_W = O_GATE + N_BRANCH * D_MODEL

ADAM_LR, ADAM_B1, ADAM_B2, ADAM_EPS, ADAM_WD, ADAM_STEP = 0.001, 0.9, 0.999, 1e-08, 0.01, 10

N_DEV = 8
MESH = pl.DeviceIdType.MESH

V7X_VMEM_BYTES = 64 * 1024 * 1024
SUBLANES = 8
LANES = 128
NEG_BIG = -1e30
COND_ROWS = 128

SHARDED = ("w_mod", "ffn_in", "ffn_out", "w_in", "glu_w", "branch_w", "out_w")
SHARD_AXIS = {"w_mod": 2, "ffn_in": 3, "ffn_out": 2, "w_in": 2, "glu_w": 2, "branch_w": 3, "out_w": 1}
SMALL = ("c_ctx", "b_mod", "win_sink", "qk_norm", "pool_w", "pool_scale", "ssm_a_re", "ssm_a_im",
         "ssm_log_dt", "ssm_b_re", "ssm_b_im", "ssm_c_re", "ssm_c_im", "ssm_d", "final_g")
WEIGHTS = ("c_ctx", "w_mod", "b_mod", "norm_g", "ffn_in", "ffn_out", "w_in", "win_sink", "qk_norm",
           "pool_w", "pool_scale", "ssm_a_re", "ssm_a_im", "ssm_log_dt", "ssm_b_re", "ssm_b_im",
           "ssm_c_re", "ssm_c_im", "ssm_d", "glu_w", "branch_w", "out_w", "final_g")


def _tile(n, cap, mult):
    if n <= cap:
        return n
    t = (cap // mult) * mult
    while t >= mult:
        if n % t == 0:
            return t
        t -= mult
    return n


def _cparams(sem, tile_bytes, resident_bytes=0):
    limit = int(min(V7X_VMEM_BYTES - 8 * 2 ** 20,
                    max(32 * 2 ** 20, 3 * tile_bytes + resident_bytes + 8 * 2 ** 20)))
    return pltpu.CompilerParams(dimension_semantics=sem, vmem_limit_bytes=limit)


PLANE = (SUBLANES, 128)
PLANE_COLS = PLANE[0] * PLANE[1]


def _planes_to_rows(ref):
    return jnp.concatenate([ref[:, j, :] for j in range(PLANE[0])], axis=1)


def _mm(a, b, ta=False, tb=False, out_dtype=F32, a_planes=False, b_planes=False, out_planes=False,
        name="mm"):
    a_shape = (a.shape[0], PLANE_COLS) if a_planes else a.shape
    b_shape = (b.shape[0], PLANE_COLS) if b_planes else b.shape
    m, k = (a_shape[1], a_shape[0]) if ta else a_shape
    n = b_shape[0] if tb else b_shape[1]
    assert (b_shape[1] if tb else b_shape[0]) == k and not (b_planes and tb)
    tm, tn, tk = _tile(m, 1024, LANES), _tile(n, 1536, LANES), _tile(k, 1536, LANES)
    nk = k // tk
    dims = (((0 if ta else 1,), (1 if tb else 0,)), ((), ()))

    def body(a_ref, b_ref, o_ref, acc_ref):
        kk = pl.program_id(2)

        @pl.when(kk == 0)
        def _():
            acc_ref[...] = jnp.zeros_like(acc_ref)

        av = _planes_to_rows(a_ref) if a_planes else a_ref[...]
        bv = _planes_to_rows(b_ref) if b_planes else b_ref[...]
        acc_ref[...] += lax.dot_general(av.astype(BF16), bv.astype(BF16), dims, preferred_element_type=F32)

        @pl.when(kk == nk - 1)
        def _():
            if out_planes:
                for j in range(PLANE[0]):
                    o_ref[:, j, :] = acc_ref[:, j * PLANE[1]:(j + 1) * PLANE[1]].astype(o_ref.dtype)
            else:
                o_ref[...] = acc_ref[...].astype(o_ref.dtype)

    if a_planes:
        assert (tm if ta else tk) == PLANE_COLS
        a_spec = pl.BlockSpec(((tk if ta else tm),) + PLANE, (lambda i, j, kk: (kk, 0, 0)) if ta
                              else (lambda i, j, kk: (i, 0, 0)))
    else:
        a_spec = (pl.BlockSpec((tk, tm), lambda i, j, kk: (kk, i)) if ta
                  else pl.BlockSpec((tm, tk), lambda i, j, kk: (i, kk)))
    if b_planes:
        assert tn == PLANE_COLS
        b_spec = pl.BlockSpec((tk,) + PLANE, lambda i, j, kk: (kk, 0, 0))
    else:
        b_spec = (pl.BlockSpec((tn, tk), lambda i, j, kk: (j, kk)) if tb
                  else pl.BlockSpec((tk, tn), lambda i, j, kk: (kk, j)))
    if out_planes:
        assert tn == PLANE_COLS
        o_spec = pl.BlockSpec((tm,) + PLANE, lambda i, j, kk: (i, 0, 0))
        o_shape = jax.ShapeDtypeStruct((m,) + PLANE, out_dtype)
    else:
        o_spec = pl.BlockSpec((tm, tn), lambda i, j, kk: (i, j))
        o_shape = jax.ShapeDtypeStruct((m, n), out_dtype)
    tile_bytes = (a.dtype.itemsize * tm * tk + b.dtype.itemsize * tk * tn
                  + jnp.dtype(out_dtype).itemsize * tm * tn + 2 * tm * tn)
    return pl.pallas_call(
        body, name=name, grid=(m // tm, n // tn, nk),
        in_specs=[a_spec, b_spec], out_specs=o_spec, out_shape=o_shape,
        scratch_shapes=[pltpu.VMEM((tm, tn), F32)],
        compiler_params=_cparams(("parallel", "parallel", "arbitrary"), tile_bytes),
    )(a, b)


def make_linear(out_dtype, x_planes=False, out_planes=False):
    @jax.custom_vjp
    def op(x, w):
        return _mm(x, w, out_dtype=out_dtype, a_planes=x_planes, out_planes=out_planes, name="linear_fwd")

    def fwd(x, w):
        return op(x, w), (x, w)

    def bwd(res, dy):
        x, w = res
        return (_mm(dy, w, tb=True, out_dtype=x.dtype, a_planes=out_planes, out_planes=x_planes,
                    name="linear_dx"),
                _mm(x, dy, ta=True, out_dtype=w.dtype, a_planes=x_planes, b_planes=out_planes,
                    name="linear_dw"))

    op.defvjp(fwd, bwd)
    return op


linear = make_linear(F32)
linear_b = make_linear(BF16)
linear_to_planes = make_linear(F32, out_planes=True)
linear_from_planes = make_linear(F32, x_planes=True)


def _split_rows(t):
    return _tile(t, 512, 2 * SUBLANES)


def _split_fwd(x, w, widths):
    t, k = x.shape
    n = w.shape[1]
    tm = _split_rows(t)

    def body(x_ref, w_ref, *o_refs):
        y = jnp.dot(x_ref[...].astype(BF16), w_ref[...].astype(BF16), preferred_element_type=F32)
        off = 0
        for o_ref, wd in zip(o_refs, widths):
            o_ref[...] = y[:, off:off + wd]
            off += wd

    return pl.pallas_call(
        body, name="split_linear_fwd", grid=(t // tm,),
        in_specs=[pl.BlockSpec((tm, k), lambda i: (i, 0)), pl.BlockSpec((k, n), lambda i: (0, 0))],
        out_specs=[pl.BlockSpec((tm, wd), lambda i: (i, 0)) for wd in widths],
        out_shape=[jax.ShapeDtypeStruct((t, wd), F32) for wd in widths],
        compiler_params=_cparams(("parallel",), 4 * tm * (k + 2 * n) + w.dtype.itemsize * k * n),
    )(x, w)


def _split_dx(cts, w):
    t = cts[0].shape[0]
    k, n = w.shape
    tm = _split_rows(t)

    def body(*refs):
        dy = jnp.concatenate([r[...].astype(BF16) for r in refs[:-2]], axis=1)
        refs[-1][...] = lax.dot_general(dy, refs[-2][...].astype(BF16), (((1,), (1,)), ((), ())),
                                        preferred_element_type=F32)

    return pl.pallas_call(
        body, name="split_linear_dx", grid=(t // tm,),
        in_specs=[pl.BlockSpec((tm, c.shape[1]), lambda i: (i, 0)) for c in cts]
        + [pl.BlockSpec((k, n), lambda i: (0, 0))],
        out_specs=pl.BlockSpec((tm, k), lambda i: (i, 0)), out_shape=jax.ShapeDtypeStruct((t, k), F32),
        compiler_params=_cparams(("parallel",), 4 * tm * (k + 2 * n) + w.dtype.itemsize * k * n),
    )(*cts, w)


def _split_dw(x, cts, out_dtype):
    t, k = x.shape
    n = sum(c.shape[1] for c in cts)
    tk = _split_rows(t)
    steps = t // tk

    def body(x_ref, *refs):
        o_ref, acc_ref = refs[-2], refs[-1]

        @pl.when(pl.program_id(0) == 0)
        def _():
            acc_ref[...] = jnp.zeros_like(acc_ref)

        dy = jnp.concatenate([r[...].astype(BF16) for r in refs[:-2]], axis=1)
        acc_ref[...] += lax.dot_general(x_ref[...].astype(BF16), dy, (((0,), (0,)), ((), ())),
                                        preferred_element_type=F32)

        @pl.when(pl.program_id(0) == steps - 1)
        def _():
            o_ref[...] = acc_ref[...].astype(o_ref.dtype)

    return pl.pallas_call(
        body, name="split_linear_dw", grid=(steps,),
        in_specs=[pl.BlockSpec((tk, k), lambda i: (i, 0))]
        + [pl.BlockSpec((tk, c.shape[1]), lambda i: (i, 0)) for c in cts],
        out_specs=pl.BlockSpec((k, n), lambda i: (0, 0)), out_shape=jax.ShapeDtypeStruct((k, n), out_dtype),
        scratch_shapes=[pltpu.VMEM((k, n), F32)],
        compiler_params=_cparams(("arbitrary",), 4 * tk * (k + n), 3 * 4 * k * n),
    )(x, *cts)


def make_split_linear(widths):
    @jax.custom_vjp
    def op(x, w):
        return tuple(_split_fwd(x, w, widths))

    def fwd(x, w):
        return op(x, w), (x, w)

    def bwd(res, cts):
        x, w = res
        return _split_dx(cts, w), _split_dw(x, cts, w.dtype)

    op.defvjp(fwd, bwd)
    return op


ROW_TILE_BYTES = 6 * 2 ** 20


def _row_tile(t, row_bytes):
    tm = 1024
    while tm > 2 * SUBLANES and tm * row_bytes > ROW_TILE_BYTES:
        tm //= 2
    return _tile(t, tm, 2 * SUBLANES)


def _row_call(fn, rows, params, n_reduce, name):
    t = rows[0].shape[0]
    out_avals = jax.eval_shape(fn, *rows, *params)
    n_out = len(out_avals) - n_reduce
    row_avals, red_avals = out_avals[:n_out], out_avals[n_out:]
    row_bytes = sum(r.shape[1] * r.dtype.itemsize for r in (*rows, *row_avals))
    tm = _row_tile(t, row_bytes)
    n_in = len(rows) + len(params)

    def body(*refs):
        outs = fn(*[r[...] for r in refs[:n_in]])
        o_refs = refs[n_in:]
        for o_ref, o in zip(o_refs[:n_out], outs[:n_out]):
            o_ref[...] = o.astype(o_ref.dtype)
        if n_reduce:
            @pl.when(pl.program_id(0) == 0)
            def _():
                for r in o_refs[n_out:]:
                    r[...] = jnp.zeros_like(r)

            for r, o in zip(o_refs[n_out:], outs[n_out:]):
                r[...] += o.astype(r.dtype)

    in_specs = ([pl.BlockSpec((tm, r.shape[1]), lambda i: (i, 0)) for r in rows]
                + [pl.BlockSpec(p.shape, lambda i: (0, 0)) for p in params])
    out_specs = ([pl.BlockSpec((tm, o.shape[1]), lambda i: (i, 0)) for o in row_avals]
                 + [pl.BlockSpec(o.shape, lambda i: (0, 0)) for o in red_avals])
    return pl.pallas_call(
        body, name=name, grid=(t // tm,), in_specs=in_specs, out_specs=out_specs,
        out_shape=[jax.ShapeDtypeStruct(o.shape, o.dtype) for o in out_avals],
        compiler_params=_cparams(("arbitrary",) if n_reduce else ("parallel",), tm * row_bytes),
    )(*rows, *params)


def rowwise(fn, n_rows, name):
    @jax.custom_vjp
    def op(*args):
        return tuple(_row_call(fn, args[:n_rows], args[n_rows:], 0, name + "_fwd"))

    def fwd(*args):
        return op(*args), args

    def bwd(args, cts):
        n_ct = len(cts)

        def bwd_fn(*a):
            r, ct, p = a[:n_rows], a[n_rows:n_rows + n_ct], a[n_rows + n_ct:]
            return jax.vjp(fn, *r, *p)[1](tuple(ct))

        return tuple(_row_call(bwd_fn, (*args[:n_rows], *cts), args[n_rows:], len(args) - n_rows,
                               name + "_bwd"))

    op.defvjp(fwd, bwd)
    return op


@functools.partial(jax.custom_vjp, nondiff_argnums=(1,))
def _swap_lanes(x, k):
    n = x.shape[-1]
    lane = lax.broadcasted_iota(jnp.int32, x.shape, x.ndim - 1)
    return jnp.where((lane & k) == 0, pltpu.roll(x, n - k, x.ndim - 1), pltpu.roll(x, k, x.ndim - 1))


def _swap_lanes_fwd(x, k):
    return _swap_lanes(x, k), None


def _swap_lanes_bwd(k, _, g):
    return (_swap_lanes(g, k),)


_swap_lanes.defvjp(_swap_lanes_fwd, _swap_lanes_bwd)


def _head_sum(x):
    s = x
    k = 1
    while k < HEAD_DIM:
        s = s + _swap_lanes(s, k)
        k *= 2
    return s


def _rms(x):
    return x * lax.rsqrt(jnp.mean(x * x, axis=-1, keepdims=True) + EPS)


def _norm_mod_fn(x, g, shift, scale):
    return ((_rms(x) * g) * (1.0 + scale) + shift,)


def _swiglu_fn(u):
    gate, up = u[:, :D_FF], u[:, D_FF:]
    return (jax.nn.silu(gate) * up,)


def _resid_fn(coef, x, y, gate):
    return (x + (coef * gate) * y,)


def _scale_fn(y, s):
    return (y * s,)


def _tile_lanes(tab, width):
    return tab if tab.shape[1] == width else jnp.concatenate([tab] * (width // tab.shape[1]), axis=1)


def _rope_fn(x, cos, sin):
    w = x.shape[1]
    return (x * _tile_lanes(cos, w) + _swap_lanes(x, 16) * _tile_lanes(sin, w),)


def _head_norm(x, g):
    ms = _head_sum(x * x) * (1.0 / HEAD_DIM)
    return x * lax.rsqrt(ms + EPS) * g


def _norm_rope_fn(x, cos, sin, g):
    return _rope_fn(_head_norm(x, g), cos, sin)


def _head_norm_fn(x, g):
    return (_head_norm(x, g),)


def _merge_fn(gl, z0, z1, z2, z3):
    zs = (z0, z1, z2, z3)
    terms = [jax.nn.sigmoid(gl[:, k * D_MODEL:(k + 1) * D_MODEL].astype(F32)) * zs[k].astype(F32)
             for k in range(N_BRANCH)]
    return (sum(terms[1:], terms[0]),)


def _s5_pre_fn(y0r, y0i, y1r, y1i, u, d):
    return (jax.nn.gelu(((y0r - y0i) + (y1r - y1i)) + d * u),)


def _glu_fn(z):
    return (z[:, :BRANCH_W] * jax.nn.sigmoid(z[:, BRANCH_W:]),)


def _silu_fn(x):
    return (jax.nn.silu(x),)


def _loss_fn(x, tgt, g):
    err = jnp.square(_rms(x) * g - tgt)
    return (0.5 * jnp.mean(err, axis=-1, keepdims=True),)


norm_mod = rowwise(_norm_mod_fn, 1, "norm_mod")
resid_full = rowwise(functools.partial(_resid_fn, 1.0), 2, "resid_full")
scale_rows = rowwise(_scale_fn, 1, "pool_scale")
rope = rowwise(_rope_fn, 3, "rope")
norm_rope = rowwise(_norm_rope_fn, 3, "norm_rope")
head_norm = rowwise(_head_norm_fn, 1, "head_norm")
merge = rowwise(_merge_fn, 5, "merge")
s5_pre = rowwise(_s5_pre_fn, 5, "s5_pre")
glu = rowwise(_glu_fn, 1, "glu")
silu_rows = rowwise(_silu_fn, 1, "silu")
loss_rows = rowwise(_loss_fn, 2, "loss_head")


POOL_HALO = 16


def _pool_call(xa, adjoint, name):
    n, width = xa.shape
    tm = _tile(n, 512, POOL_HALO)
    halo_blocks = tm // POOL_HALO
    last_halo = n // POOL_HALO - 1
    ext_rows = tm + 2 * POOL_HALO

    def body(prev_ref, cur_ref, next_ref, o_ref, ext_ref):
        i = pl.program_id(0)
        ext_ref[0:POOL_HALO] = prev_ref[...]
        ext_ref[POOL_HALO:POOL_HALO + tm] = cur_ref[...]
        ext_ref[POOL_HALO + tm:ext_rows] = next_ref[...]
        e = ext_ref[...]
        row = lax.broadcasted_iota(jnp.int32, e.shape, 0) + (i * tm - POOL_HALO)
        grp = lax.broadcasted_iota(jnp.int32, e.shape, 1) // POOL_GROUP
        win = jnp.where(grp == 0, POOL_WINDOWS[0],
                        jnp.where(grp == 1, POOL_WINDOWS[1], jnp.where(grp == 2, POOL_WINDOWS[2], POOL_WINDOWS[3])))
        valid = (row >= 0) & (row < n)
        lo = jnp.clip(row - win // 2, 0, n)
        hi = jnp.clip(row - win // 2 + win, 0, n)
        cnt = jnp.maximum((hi - lo).astype(F32), 1.0)
        e0 = jnp.where(valid, e / cnt if adjoint else e, 0.0)

        def shift(z, s):
            return pltpu.roll(z, s % ext_rows, 0)

        s2 = e0 + shift(e0, -1 if adjoint else 1)
        s4 = shift(s2, 1) + shift(s2, -1)
        s8 = shift(s4, 2) + shift(s4, -2)
        s16 = shift(s8, 4) + shift(s8, -4)
        s = jnp.where(grp == 0, s2, jnp.where(grp == 1, s4, jnp.where(grp == 2, s8, s16)))
        out = (s - e) if adjoint else (s / cnt - e)
        o_ref[...] = out[POOL_HALO:POOL_HALO + tm]

    return pl.pallas_call(
        body, name=name, grid=(n // tm,),
        in_specs=[pl.BlockSpec((POOL_HALO, width), lambda i: (jnp.maximum(i * halo_blocks - 1, 0), 0)),
                  pl.BlockSpec((tm, width), lambda i: (i, 0)),
                  pl.BlockSpec((POOL_HALO, width), lambda i: (jnp.minimum((i + 1) * halo_blocks, last_halo), 0))],
        out_specs=pl.BlockSpec((tm, width), lambda i: (i, 0)),
        out_shape=jax.ShapeDtypeStruct((n, width), F32),
        scratch_shapes=[pltpu.VMEM((ext_rows, width), F32)],
        compiler_params=_cparams(("parallel",), 4 * 4 * ext_rows * width),
    )(xa, xa, xa)


@jax.custom_vjp
def pool_diff(xa):
    return _pool_call(xa, False, "pool_fwd")


pool_diff.defvjp(lambda xa: (pool_diff(xa), None), lambda _, g: (_pool_call(g, True, "pool_bwd"),))


ATT_SCALE = HEAD_DIM ** -0.5
ATT_BQ = 512


def _qk_scores(q, k):
    return lax.dot_general((q * ATT_SCALE).astype(BF16), k.astype(BF16), (((1,), (1,)), ((), ())),
                           preferred_element_type=F32)


def _sink_rows(sink_ref, h, bq):
    r = lax.broadcasted_iota(jnp.int32, (Q_PER_KV * bq, 1), 0)
    return jnp.where(r < bq, sink_ref[h * Q_PER_KV], sink_ref[h * Q_PER_KV + 1])


ATT_SUB_ROWS = 256


def _flash_fwd(q, klt, vl, kct, vc, sink):
    kvh, g, t, dh = q.shape
    c = kct.shape[2]
    has_lat = klt is not None
    has_sink = sink is not None
    bq = _tile(t, ATT_BQ, LANES)
    bk = _tile(t, 4096, LANES)
    nkv = t // bk if has_lat else 1
    rows = g * bq
    sub = min(ATT_SUB_ROWS, rows)

    def with_ones(v):
        return jnp.concatenate([v, jnp.ones(v.shape[:2] + (1,), v.dtype),
                                jnp.zeros(v.shape[:2] + (LANES - dh - 1,), v.dtype)], axis=2)

    def body(*refs):
        it = iter(refs)
        q_ref, kc_ref, vc_ref = next(it), next(it), next(it)
        kl_ref, vl_ref = (next(it), next(it)) if has_lat else (None, None)
        sink_ref = next(it) if has_sink else None
        o_ref, lse_ref, m_ref, acc_ref = next(it), next(it), next(it), next(it)
        h, kj = pl.program_id(0), pl.program_id(2)
        qv = (q_ref[0].reshape(rows, dh) * ATT_SCALE).astype(BF16)
        is_sum_lane = lax.broadcasted_iota(jnp.int32, (rows, LANES), 1) == dh

        def part(kt_ref, v_ref):
            kt, v = kt_ref[0].astype(BF16), v_ref[0].astype(BF16)
            m_all, acc_all = m_ref[...], acc_ref[...]
            m_out, acc_out = [], []
            for r0 in range(0, rows, sub):
                s = jnp.dot(qv[r0:r0 + sub], kt, preferred_element_type=F32)
                m_old = m_all[r0:r0 + sub]
                m_new = jnp.maximum(m_old, jnp.max(s, axis=-1, keepdims=True))
                p = jnp.exp(s - m_new)
                m_out.append(m_new)
                acc_out.append(jnp.exp(m_old - m_new) * acc_all[r0:r0 + sub]
                               + jnp.dot(p.astype(BF16), v, preferred_element_type=F32))
            m_ref[...] = jnp.concatenate(m_out, axis=0)
            acc_ref[...] = jnp.concatenate(acc_out, axis=0)

        @pl.when(kj == 0)
        def _():
            if has_sink:
                m_ref[...] = _sink_rows(sink_ref, h, bq)
                acc_ref[...] = jnp.where(is_sum_lane, 1.0, 0.0)
            else:
                m_ref[...] = jnp.full_like(m_ref, NEG_BIG)
                acc_ref[...] = jnp.zeros_like(acc_ref)
            part(kc_ref, vc_ref)

        if has_lat:
            part(kl_ref, vl_ref)

        @pl.when(kj == nkv - 1)
        def _():
            acc = acc_ref[...]
            l = jnp.sum(jnp.where(is_sum_lane, acc, 0.0), axis=-1, keepdims=True)
            o_ref[0] = (acc / l).reshape(g, bq, LANES)
            lse_ref[0] = (m_ref[...] + jnp.log(l)).reshape(g, bq, 1)

    q_spec = pl.BlockSpec((1, g, bq, dh), lambda h, i, j: (h, 0, i, 0))
    o_spec = pl.BlockSpec((1, g, bq, LANES), lambda h, i, j: (h, 0, i, 0))
    r_spec = pl.BlockSpec((1, g, bq, 1), lambda h, i, j: (h, 0, i, 0))
    in_specs = [q_spec, pl.BlockSpec((1, dh, c), lambda h, i, j: (h, 0, 0)),
                pl.BlockSpec((1, c, LANES), lambda h, i, j: (h, 0, 0))]
    args = [q, kct, with_ones(vc)]
    if has_lat:
        in_specs += [pl.BlockSpec((1, dh, bk), lambda h, i, j: (h, 0, j)),
                     pl.BlockSpec((1, bk, LANES), lambda h, i, j: (h, j, 0))]
        args += [klt, with_ones(vl)]
    if has_sink:
        in_specs.append(pl.BlockSpec(memory_space=pltpu.SMEM))
        args.append(sink)
    o_wide, lse = pl.pallas_call(
        body, name="attn_fwd", grid=(kvh, t // bq, nkv),
        in_specs=in_specs, out_specs=[o_spec, r_spec],
        out_shape=[jax.ShapeDtypeStruct((kvh, g, t, LANES), F32), jax.ShapeDtypeStruct((kvh, g, t, 1), F32)],
        scratch_shapes=[pltpu.VMEM((rows, 1), F32), pltpu.VMEM((rows, LANES), F32)],
        compiler_params=_cparams(("parallel", "parallel", "arbitrary"), 4 * 4 * rows * max(bk, c)),
    )(*args)
    return o_wide[..., :dh], lse


def _ctx_dq(q, kc, vc, sink, o, do, lse):
    kvh, g, t, dh = q.shape
    c = kc.shape[1]
    has_sink = sink is not None
    bq = _tile(t, ATT_BQ, LANES)
    rows = g * bq

    def body(*refs):
        q_ref, o_ref, do_ref, lse_ref, kc_ref, vc_ref = refs[:6]
        sink_ref = refs[6] if has_sink else None
        dq_ref, delta_ref, dsink_ref = refs[-3:]
        dov = do_ref[0].reshape(rows, dh)
        lse_v = lse_ref[0].reshape(rows, 1)
        delta = jnp.sum(o_ref[0].reshape(rows, dh) * dov, axis=-1, keepdims=True)
        p = jnp.exp(_qk_scores(q_ref[0].reshape(rows, dh), kc_ref[0]) - lse_v)
        dp = lax.dot_general(dov.astype(BF16), vc_ref[0].astype(BF16), (((1,), (1,)), ((), ())),
                             preferred_element_type=F32)
        ds = (p * (dp - delta)).astype(BF16)
        dq = jnp.dot(ds, kc_ref[0].astype(BF16), preferred_element_type=F32) * ATT_SCALE
        dq_ref[0] = dq.reshape(g, bq, dh)
        delta_ref[0] = delta.reshape(g, bq, 1)
        if has_sink:
            p_sink = jnp.exp(_sink_rows(sink_ref, pl.program_id(0), bq) - lse_v)
            dsink_ref[0] = (-p_sink * delta).reshape(g, bq, 1)
        else:
            dsink_ref[0] = jnp.zeros((g, bq, 1), F32)

    q_spec = pl.BlockSpec((1, g, bq, dh), lambda h, i: (h, 0, i, 0))
    r_spec = pl.BlockSpec((1, g, bq, 1), lambda h, i: (h, 0, i, 0))
    c_spec = pl.BlockSpec((1, c, dh), lambda h, i: (h, 0, 0))
    in_specs, args = [q_spec, q_spec, q_spec, r_spec, c_spec, c_spec], [q, o, do, lse, kc, vc]
    if has_sink:
        in_specs.append(pl.BlockSpec(memory_space=pltpu.SMEM))
        args.append(sink)
    row_shape = jax.ShapeDtypeStruct((kvh, g, t, 1), F32)
    return pl.pallas_call(
        body, name="attn_ctx_dq", grid=(kvh, t // bq),
        in_specs=in_specs, out_specs=[q_spec, r_spec, r_spec],
        out_shape=[jax.ShapeDtypeStruct(q.shape, F32), row_shape, row_shape],
        compiler_params=_cparams(("parallel", "parallel"), 4 * 6 * rows * c),
    )(*args)


def _flash_dkv(q, do, lse, delta, k, v):
    kvh, g, t, dh = q.shape
    nk_rows = k.shape[1]
    bq = _tile(t, ATT_BQ, LANES)
    bk = _tile(nk_rows, 1024, LANES)
    nq = t // bq
    rows = g * bq

    def body(q_ref, do_ref, lse_ref, delta_ref, k_ref, v_ref, dk_ref, dv_ref, dk_acc, dv_acc):
        qj = pl.program_id(2)
        qv = q_ref[0].reshape(rows, dh)
        dov = do_ref[0].reshape(rows, dh)

        @pl.when(qj == 0)
        def _():
            dk_acc[...] = jnp.zeros_like(dk_acc)
            dv_acc[...] = jnp.zeros_like(dv_acc)

        s = _qk_scores(qv, k_ref[0])
        p = jnp.exp(s - lse_ref[0].reshape(rows, 1))
        dp = lax.dot_general(dov.astype(BF16), v_ref[0].astype(BF16), (((1,), (1,)), ((), ())),
                             preferred_element_type=F32)
        ds = p * (dp - delta_ref[0].reshape(rows, 1))
        tn = (((0,), (0,)), ((), ()))
        dv_acc[...] += lax.dot_general(p.astype(BF16), dov.astype(BF16), tn, preferred_element_type=F32)
        dk_acc[...] += lax.dot_general(ds.astype(BF16), qv.astype(BF16), tn, preferred_element_type=F32)

        @pl.when(qj == nq - 1)
        def _():
            dk_ref[0] = dk_acc[...] * ATT_SCALE
            dv_ref[0] = dv_acc[...]

    q_spec = pl.BlockSpec((1, g, bq, dh), lambda h, i, j: (h, 0, j, 0))
    r_spec = pl.BlockSpec((1, g, bq, 1), lambda h, i, j: (h, 0, j, 0))
    k_spec = pl.BlockSpec((1, bk, dh), lambda h, i, j: (h, i, 0))
    return pl.pallas_call(
        body, name="attn_dkv", grid=(kvh, nk_rows // bk, nq),
        in_specs=[q_spec, q_spec, r_spec, r_spec, k_spec, k_spec], out_specs=[k_spec, k_spec],
        out_shape=[jax.ShapeDtypeStruct(k.shape, F32), jax.ShapeDtypeStruct(k.shape, F32)],
        scratch_shapes=[pltpu.VMEM((bk, dh), F32), pltpu.VMEM((bk, dh), F32)],
        compiler_params=_cparams(("parallel", "parallel", "arbitrary"), 4 * 6 * rows * bk),
    )(q, do, lse, delta, k, v)


def _flash_bwd_full(q, kl, vl, kc, vc, o, do, lse):
    kvh, g, t, dh = q.shape
    c = kc.shape[1]
    bq, bk = _tile(t, 512, LANES), _tile(t, 1024, LANES)
    nq, nkv = t // bq, t // bk
    rows = g * bq
    klt, vlt = jnp.swapaxes(kl, 1, 2), jnp.swapaxes(vl, 1, 2)
    kct, vct = jnp.swapaxes(kc, 1, 2), jnp.swapaxes(vc, 1, 2)
    tn = (((0,), (0,)), ((), ()))

    def body(q_ref, o_ref, do_ref, lse_ref, kc_ref, kct_ref, vct_ref, kl_ref, klt_ref, vlt_ref,
             dq_ref, delta_ref, dk_hbm, dv_hbm, dq_acc, dl_ref, dk_acc, dv_acc):
        h, qi, kj = pl.program_id(0), pl.program_id(1), pl.program_id(2)
        q_raw = q_ref[0].reshape(rows, dh).astype(BF16)
        qv = (q_ref[0].reshape(rows, dh) * ATT_SCALE).astype(BF16)
        dov = do_ref[0].reshape(rows, dh).astype(BF16)
        lse_v = lse_ref[0].reshape(rows, 1)

        def tile(kt, vt):
            s = jnp.dot(qv, kt.astype(BF16), preferred_element_type=F32)
            p = jnp.exp(s - lse_v)
            dp = jnp.dot(dov, vt.astype(BF16), preferred_element_type=F32)
            return p, (p * (dp - dl_ref[...])).astype(BF16)

        @pl.when((qi == 0) & (kj == 0))
        def _():
            dk_acc[...] = jnp.zeros_like(dk_acc)
            dv_acc[...] = jnp.zeros_like(dv_acc)

        @pl.when(kj == 0)
        def _():
            dl_ref[...] = jnp.sum(o_ref[0].reshape(rows, dh) * do_ref[0].reshape(rows, dh),
                                  axis=-1, keepdims=True)
            _, ds = tile(kct_ref[0], vct_ref[0])
            dq_acc[...] = jnp.dot(ds, kc_ref[0].astype(BF16), preferred_element_type=F32)

        p, ds = tile(klt_ref[0], vlt_ref[0])
        dq_acc[...] += jnp.dot(ds, kl_ref[0].astype(BF16), preferred_element_type=F32)
        ks = pl.ds(pl.multiple_of(kj * bk, bk), bk)
        dv_acc[ks] += lax.dot_general(p.astype(BF16), dov, tn, preferred_element_type=F32)
        dk_acc[ks] += lax.dot_general(ds, q_raw, tn, preferred_element_type=F32)

        @pl.when(kj == nkv - 1)
        def _():
            dq_ref[0] = (dq_acc[...] * ATT_SCALE).reshape(g, bq, dh)
            delta_ref[0] = dl_ref[...].reshape(g, bq, 1)

        @pl.when((qi == nq - 1) & (kj == nkv - 1))
        def _():
            dk_acc[...] = dk_acc[...] * ATT_SCALE
            pltpu.sync_copy(dk_acc, dk_hbm.at[h])
            pltpu.sync_copy(dv_acc, dv_hbm.at[h])

    q_spec = pl.BlockSpec((1, g, bq, dh), lambda h, i, j: (h, 0, i, 0))
    r_spec = pl.BlockSpec((1, g, bq, 1), lambda h, i, j: (h, 0, i, 0))
    c_spec = pl.BlockSpec((1, c, dh), lambda h, i, j: (h, 0, 0))
    ct_spec = pl.BlockSpec((1, dh, c), lambda h, i, j: (h, 0, 0))
    l_spec = pl.BlockSpec((1, bk, dh), lambda h, i, j: (h, j, 0))
    lt_spec = pl.BlockSpec((1, dh, bk), lambda h, i, j: (h, 0, j))
    any_spec = pl.BlockSpec(memory_space=pl.ANY)
    kv_shape = jax.ShapeDtypeStruct((kvh, t, dh), F32)
    return pl.pallas_call(
        body, name="attn_bwd_full", grid=(kvh, nq, nkv),
        in_specs=[q_spec, q_spec, q_spec, r_spec, c_spec, ct_spec, ct_spec, l_spec, lt_spec, lt_spec],
        out_specs=[q_spec, r_spec, any_spec, any_spec],
        out_shape=[jax.ShapeDtypeStruct(q.shape, F32), jax.ShapeDtypeStruct((kvh, g, t, 1), F32),
                   kv_shape, kv_shape],
        scratch_shapes=[pltpu.VMEM((rows, dh), F32), pltpu.VMEM((rows, 1), F32),
                        pltpu.VMEM((t, dh), F32), pltpu.VMEM((t, dh), F32)],
        compiler_params=_cparams(("arbitrary", "arbitrary", "arbitrary"), 4 * 2 * rows * bk,
                                 2 * 4 * t * LANES),
    )(q, o, do, lse, kc, kct, vct, kl, klt, vlt)


BAND_BQ = 512


def _band_specs(t, shape_of):
    per = BAND_BQ // WINDOW
    n_halo = t // WINDOW

    def spec(n, index):
        shape, axis = shape_of(n)

        def index_map(h, i):
            idx = [h] + [0] * (len(shape) - 1)
            idx[axis] = index(i)
            return tuple(idx)

        return pl.BlockSpec(shape, index_map)

    return [spec(WINDOW, lambda i: jnp.maximum(i * per - 1, 0)),
            spec(BAND_BQ, lambda i: i),
            spec(WINDOW, lambda i: jnp.minimum((i + 1) * per, n_halo - 1))]


def _band_visible(i, t, c, rows):
    cols = c + BAND_BQ + 2 * WINDOW
    col = lax.broadcasted_iota(jnp.int32, (rows, cols), 1)
    qpos = i * BAND_BQ + lax.broadcasted_iota(jnp.int32, (rows, cols), 0) % BAND_BQ
    kpos = i * BAND_BQ - WINDOW + (col - c)
    return (col < c) | ((kpos >= 0) & (kpos < t) & (jnp.abs(kpos - qpos) <= WINDOW))


def _band_fwd(q, klt, vl, kct, vc, sink):
    kvh, g, t, dh = q.shape
    c = kct.shape[2]
    rows = g * BAND_BQ

    def body(q_ref, kct_ref, vc_ref, ktp, ktc, ktn, vp, vcur, vn, sink_ref, o_ref, lse_ref):
        h, i = pl.program_id(0), pl.program_id(1)
        qv = (q_ref[0].reshape(rows, dh) * ATT_SCALE).astype(BF16)
        kt = jnp.concatenate([kct_ref[0], ktp[0], ktc[0], ktn[0]], axis=1).astype(BF16)
        v = jnp.concatenate([vc_ref[0], vp[0], vcur[0], vn[0]], axis=0).astype(BF16)
        s = jnp.where(_band_visible(i, t, c, rows), jnp.dot(qv, kt, preferred_element_type=F32), NEG_BIG)
        sink_r = _sink_rows(sink_ref, h, BAND_BQ)
        m = jnp.maximum(sink_r, jnp.max(s, axis=-1, keepdims=True))
        p = jnp.exp(s - m)
        l = jnp.exp(sink_r - m) + jnp.sum(p, axis=-1, keepdims=True)
        o_ref[0] = (jnp.dot(p.astype(BF16), v, preferred_element_type=F32) / l).reshape(g, BAND_BQ, dh)
        lse_ref[0] = (m + jnp.log(l)).reshape(g, BAND_BQ, 1)

    q_spec = pl.BlockSpec((1, g, BAND_BQ, dh), lambda h, i: (h, 0, i, 0))
    r_spec = pl.BlockSpec((1, g, BAND_BQ, 1), lambda h, i: (h, 0, i, 0))
    in_specs = ([q_spec, pl.BlockSpec((1, dh, c), lambda h, i: (h, 0, 0)),
                 pl.BlockSpec((1, c, dh), lambda h, i: (h, 0, 0))]
                + _band_specs(t,lambda n: ((1, dh, n), 2))
                + _band_specs(t,lambda n: ((1, n, dh), 1))
                + [pl.BlockSpec(memory_space=pltpu.SMEM)])
    return pl.pallas_call(
        body, name="attn_band_fwd", grid=(kvh, t // BAND_BQ), in_specs=in_specs, out_specs=[q_spec, r_spec],
        out_shape=[jax.ShapeDtypeStruct(q.shape, F32), jax.ShapeDtypeStruct((kvh, g, t, 1), F32)],
        compiler_params=_cparams(("parallel", "parallel"), 4 * 2 * rows * (c + BAND_BQ + 2 * WINDOW)),
    )(q, kct, vc, klt, klt, klt, vl, vl, vl, sink)


def _band_dq(q, kl, klt, vlt, kc, kct, vct, sink, o, do, lse):
    kvh, g, t, dh = q.shape
    c = kc.shape[1]
    rows = g * BAND_BQ

    def body(q_ref, o_ref, do_ref, lse_ref, kc_ref, kct_ref, vct_ref, kp, kcur, kn, ktp, ktc, ktn,
             vtp, vtc, vtn, sink_ref, dq_ref, delta_ref, dsink_ref):
        h, i = pl.program_id(0), pl.program_id(1)
        qv = (q_ref[0].reshape(rows, dh) * ATT_SCALE).astype(BF16)
        dov = do_ref[0].reshape(rows, dh)
        lse_v = lse_ref[0].reshape(rows, 1)
        kt = jnp.concatenate([kct_ref[0], ktp[0], ktc[0], ktn[0]], axis=1).astype(BF16)
        vt = jnp.concatenate([vct_ref[0], vtp[0], vtc[0], vtn[0]], axis=1).astype(BF16)
        k = jnp.concatenate([kc_ref[0], kp[0], kcur[0], kn[0]], axis=0).astype(BF16)
        s = jnp.where(_band_visible(i, t, c, rows), jnp.dot(qv, kt, preferred_element_type=F32), NEG_BIG)
        p = jnp.exp(s - lse_v)
        delta = jnp.sum(o_ref[0].reshape(rows, dh) * dov, axis=-1, keepdims=True)
        dp = jnp.dot(dov.astype(BF16), vt, preferred_element_type=F32)
        ds = (p * (dp - delta)).astype(BF16)
        dq_ref[0] = (jnp.dot(ds, k, preferred_element_type=F32) * ATT_SCALE).reshape(g, BAND_BQ, dh)
        delta_ref[0] = delta.reshape(g, BAND_BQ, 1)
        p_sink = jnp.exp(_sink_rows(sink_ref, h, BAND_BQ) - lse_v)
        dsink_ref[0] = (-p_sink * delta).reshape(g, BAND_BQ, 1)

    q_spec = pl.BlockSpec((1, g, BAND_BQ, dh), lambda h, i: (h, 0, i, 0))
    r_spec = pl.BlockSpec((1, g, BAND_BQ, 1), lambda h, i: (h, 0, i, 0))
    ct_spec = pl.BlockSpec((1, dh, c), lambda h, i: (h, 0, 0))
    rows_of = lambda n: ((1, n, dh), 1)
    lanes_of = lambda n: ((1, dh, n), 2)
    in_specs = ([q_spec, q_spec, q_spec, r_spec, pl.BlockSpec((1, c, dh), lambda h, i: (h, 0, 0)), ct_spec, ct_spec]
                + _band_specs(t,rows_of) + _band_specs(t,lanes_of) + _band_specs(t,lanes_of)
                + [pl.BlockSpec(memory_space=pltpu.SMEM)])
    row_shape = jax.ShapeDtypeStruct((kvh, g, t, 1), F32)
    return pl.pallas_call(
        body, name="attn_band_dq", grid=(kvh, t // BAND_BQ), in_specs=in_specs,
        out_specs=[q_spec, r_spec, r_spec], out_shape=[jax.ShapeDtypeStruct(q.shape, F32), row_shape, row_shape],
        compiler_params=_cparams(("parallel", "parallel"), 4 * 3 * rows * (c + BAND_BQ + 2 * WINDOW)),
    )(q, o, do, lse, kc, kct, vct, kl, kl, kl, klt, klt, klt, vlt, vlt, vlt, sink)


def _band_dkv(q, do, lse, delta, klt, vlt):
    kvh, g, t, dh = q.shape
    span = BAND_BQ + 2 * WINDOW
    rows = g * span
    tn = (((0,), (0,)), ((), ()))

    def body(qp, qc, qn, dop, doc, don, lp, lc, ln, dp_, dc_, dn_, kt_ref, vt_ref, dk_ref, dv_ref):
        j = pl.program_id(1)

        def stack(a, b, c_):
            return jnp.concatenate([jnp.concatenate([a[0, gi], b[0, gi], c_[0, gi]], axis=0)
                                    for gi in range(g)], axis=0)

        q_all, do_all = stack(qp, qc, qn), stack(dop, doc, don).astype(BF16)
        lse_all, delta_all = stack(lp, lc, ln), stack(dp_, dc_, dn_)
        s = jnp.dot((q_all * ATT_SCALE).astype(BF16), kt_ref[0].astype(BF16), preferred_element_type=F32)
        qpos = j * BAND_BQ - WINDOW + lax.broadcasted_iota(jnp.int32, (rows, BAND_BQ), 0) % span
        kpos = j * BAND_BQ + lax.broadcasted_iota(jnp.int32, (rows, BAND_BQ), 1)
        s = jnp.where((qpos >= 0) & (qpos < t) & (jnp.abs(kpos - qpos) <= WINDOW), s, NEG_BIG)
        p = jnp.exp(s - lse_all)
        dp = jnp.dot(do_all, vt_ref[0].astype(BF16), preferred_element_type=F32)
        ds = (p * (dp - delta_all)).astype(BF16)
        dv_ref[0] = lax.dot_general(p.astype(BF16), do_all, tn, preferred_element_type=F32)
        dk_ref[0] = lax.dot_general(ds, q_all.astype(BF16), tn, preferred_element_type=F32) * ATT_SCALE

    q_specs = _band_specs(t,lambda n: ((1, g, n, dh), 2))
    r_specs = _band_specs(t,lambda n: ((1, g, n, 1), 2))
    kt_spec = pl.BlockSpec((1, dh, BAND_BQ), lambda h, j: (h, 0, j))
    k_spec = pl.BlockSpec((1, BAND_BQ, dh), lambda h, j: (h, j, 0))
    kv_shape = jax.ShapeDtypeStruct((kvh, t, dh), F32)
    return pl.pallas_call(
        body, name="attn_band_dkv", grid=(kvh, t // BAND_BQ),
        in_specs=q_specs + q_specs + r_specs + r_specs + [kt_spec, kt_spec], out_specs=[k_spec, k_spec],
        out_shape=[kv_shape, kv_shape],
        compiler_params=_cparams(("parallel", "parallel"), 4 * 3 * rows * BAND_BQ),
    )(q, q, q, do, do, do, lse, lse, lse, delta, delta, delta, klt, vlt)


def make_attention(kind, has_sink):
    has_lat = kind != "ctx"

    def unpack(args):
        it = iter(args)
        q, kc, vc = next(it), next(it), next(it)
        kl, vl = (next(it), next(it)) if has_lat else (None, None)
        sink = next(it) if has_sink else None
        return q, kl, vl, kc, vc, sink

    def forward(args):
        q, kl, vl, kc, vc, sink = unpack(args)
        klt = jnp.swapaxes(kl, 1, 2) if has_lat else None
        if kind == "window":
            return _band_fwd(q, klt, vl, jnp.swapaxes(kc, 1, 2), vc, sink)
        return _flash_fwd(q, klt, vl, jnp.swapaxes(kc, 1, 2), vc, sink)

    @jax.custom_vjp
    def op(*args):
        return forward(args)[0]

    def fwd(*args):
        o, lse = forward(args)
        return o, (args, o, lse)

    def bwd(res, do):
        args, o, lse = res
        q, kl, vl, kc, vc, sink = unpack(args)
        if kind == "global":
            dq, delta, dkl, dvl = _flash_bwd_full(q, kl, vl, kc, vc, o, do, lse)
            grads = [dq, *_flash_dkv(q, do, lse, delta, kc, vc), dkl, dvl]
        elif kind == "window":
            klt, vlt = jnp.swapaxes(kl, 1, 2), jnp.swapaxes(vl, 1, 2)
            dq, delta, dsink_rows = _band_dq(q, kl, klt, vlt, kc, jnp.swapaxes(kc, 1, 2),
                                             jnp.swapaxes(vc, 1, 2), sink, o, do, lse)
            grads = [dq, *_flash_dkv(q, do, lse, delta, kc, vc), *_band_dkv(q, do, lse, delta, klt, vlt)]
        else:
            dq, delta, dsink_rows = _ctx_dq(q, kc, vc, sink, o, do, lse)
            grads = [dq, *_flash_dkv(q, do, lse, delta, kc, vc)]
        if has_sink:
            grads.append(jnp.sum(dsink_rows, axis=(2, 3)).reshape(-1))
        return tuple(grads)

    op.defvjp(fwd, bwd)
    return op


attn_window = make_attention("window", True)
attn_global = make_attention("global", False)
attn_ctx_sink = make_attention("ctx", True)
attn_ctx = make_attention("ctx", False)


def _to_heads(z, n_heads):
    return z.reshape(z.shape[0], n_heads, HEAD_DIM).transpose(1, 0, 2)


def _q_heads(z):
    return z.reshape(z.shape[0], KV_HEADS, Q_PER_KV, HEAD_DIM).transpose(1, 2, 0, 3)


def _from_q_heads(o):
    return o.transpose(2, 0, 1, 3).reshape(o.shape[2], KV_HEADS * Q_PER_KV * HEAD_DIM)


SCAN_PAIRS = 4


def _scan_tile(t):
    return _tile(t, 512, 2 * SCAN_PAIRS)


def _scan_fwd_call(bre, bim, lre, lim, h0re, h0im, rev):
    t = bre.shape[0]
    tt = _scan_tile(t)
    nb = t // tt
    plane = bre.shape[1:]

    def body(bre_ref, bim_ref, lre_ref, lim_ref, h0re_ref, h0im_ref, sre_ref, sim_ref, h_ref):
        @pl.when(pl.program_id(0) == 0)
        def _():
            h_ref[0] = h0re_ref[...]
            h_ref[1] = h0im_ref[...]

        ar, ai = lre_ref[...], lim_ref[...]
        a2r, a2i = ar * ar - ai * ai, 2.0 * ar * ai

        def step(j, carry):
            hr, hi = carry
            for u in range(SCAN_PAIRS):
                t1 = (tt - 1 - 2 * (SCAN_PAIRS * j + u)) if rev else 2 * (SCAN_PAIRS * j + u)
                t2 = (t1 - 1) if rev else (t1 + 1)
                b1r, b1i, b2r, b2i = bre_ref[t1], bim_ref[t1], bre_ref[t2], bim_ref[t2]
                er = ar * b1r - ai * b1i + b2r
                ei = ar * b1i + ai * b1r + b2i
                sre_ref[t1] = ar * hr - ai * hi + b1r
                sim_ref[t1] = ar * hi + ai * hr + b1i
                hr, hi = a2r * hr - a2i * hi + er, a2r * hi + a2i * hr + ei
                sre_ref[t2] = hr
                sim_ref[t2] = hi
            return hr, hi

        hr, hi = lax.fori_loop(0, tt // (2 * SCAN_PAIRS), step, (h_ref[0], h_ref[1]))
        h_ref[0] = hr
        h_ref[1] = hi

    blk = pl.BlockSpec((tt,) + plane, (lambda i: (nb - 1 - i, 0, 0)) if rev else (lambda i: (i, 0, 0)))
    par = pl.BlockSpec(plane, lambda i: (0, 0))
    return pl.pallas_call(
        body, name="s5_scan_fwd", grid=(nb,), in_specs=[blk, blk, par, par, par, par], out_specs=[blk, blk],
        out_shape=[jax.ShapeDtypeStruct(bre.shape, F32)] * 2,
        scratch_shapes=[pltpu.VMEM((2,) + plane, F32)],
        compiler_params=_cparams(("arbitrary",), 4 * 4 * tt * plane[0] * plane[1]),
    )(bre, bim, lre, lim, h0re, h0im)


def _scan_bwd_call(gre, gim, sre, sim, lre, lim, h0re, h0im, rev):
    t = gre.shape[0]
    tt = _scan_tile(t)
    nb = t // tt
    plane = gre.shape[1:]
    down = not rev

    def body(gre_ref, gim_ref, sre_ref, sim_ref, lre_ref, lim_ref, h0re_ref, h0im_ref,
             dbre_ref, dbim_ref, dare_ref, daim_ref, dh0re_ref, dh0im_ref, carry_ref):
        i = pl.program_id(0)

        @pl.when(i == 0)
        def _():
            carry_ref[...] = jnp.zeros_like(carry_ref)

        ar, ai = lre_ref[...], lim_ref[...]
        a2r, a2i = ar * ar - ai * ai, 2.0 * ar * ai

        def step(j, carry):
            gr, gi, dar, dai = carry
            for u in range(SCAN_PAIRS):
                t1 = (tt - 1 - 2 * (SCAN_PAIRS * j + u)) if down else 2 * (SCAN_PAIRS * j + u)
                t2 = (t1 - 1) if down else (t1 + 1)
                h1r, h1i, h2r, h2i = sre_ref[t1], sim_ref[t1], sre_ref[t2], sim_ref[t2]
                c1r, c1i, c2r, c2i = gre_ref[t1], gim_ref[t1], gre_ref[t2], gim_ref[t2]
                g1r = c1r + ar * gr + ai * gi
                g1i = c1i + ar * gi - ai * gr
                er = c2r + ar * c1r + ai * c1i
                ei = c2i + ar * c1i - ai * c1r
                dar = dar + ((h1r * gr + h1i * gi) + (h2r * g1r + h2i * g1i))
                dai = dai + ((h1r * gi - h1i * gr) + (h2r * g1i - h2i * g1r))
                gr, gi = er + a2r * gr + a2i * gi, ei + a2r * gi - a2i * gr
                dbre_ref[t1] = g1r
                dbim_ref[t1] = g1i
                dbre_ref[t2] = gr
                dbim_ref[t2] = gi
            return gr, gi, dar, dai

        gr, gi, dar, dai = lax.fori_loop(
            0, tt // (2 * SCAN_PAIRS), step, (carry_ref[0], carry_ref[1], carry_ref[2], carry_ref[3]))
        carry_ref[0] = gr
        carry_ref[1] = gi
        carry_ref[2] = dar
        carry_ref[3] = dai

        @pl.when(i == nb - 1)
        def _():
            hr, hi = h0re_ref[...], h0im_ref[...]
            dare_ref[...] = dar + hr * gr + hi * gi
            daim_ref[...] = dai + hr * gi - hi * gr
            dh0re_ref[...] = ar * gr + ai * gi
            dh0im_ref[...] = ar * gi - ai * gr

    blk = pl.BlockSpec((tt,) + plane, (lambda i: (nb - 1 - i, 0, 0)) if down else (lambda i: (i, 0, 0)))
    par = pl.BlockSpec(plane, lambda i: (0, 0))
    return pl.pallas_call(
        body, name="s5_scan_bwd", grid=(nb,), in_specs=[blk, blk, blk, blk, par, par, par, par],
        out_specs=[blk, blk, par, par, par, par],
        out_shape=[jax.ShapeDtypeStruct(gre.shape, F32)] * 2 + [jax.ShapeDtypeStruct(plane, F32)] * 4,
        scratch_shapes=[pltpu.VMEM((4,) + plane, F32)],
        compiler_params=_cparams(("arbitrary",), 4 * 6 * tt * plane[0] * plane[1]),
    )(gre, gim, sre, sim, lre, lim, h0re, h0im)


def make_scan(rev):
    @jax.custom_vjp
    def op(bre, bim, lre, lim, h0re, h0im):
        return tuple(_scan_fwd_call(bre, bim, lre, lim, h0re, h0im, rev))

    def fwd(bre, bim, lre, lim, h0re, h0im):
        sre, sim = _scan_fwd_call(bre, bim, lre, lim, h0re, h0im, rev)
        return (sre, sim), (sre, sim, lre, lim, h0re, h0im)

    def bwd(res, cts):
        sre, sim, lre, lim, h0re, h0im = res
        return tuple(_scan_bwd_call(cts[0], cts[1], sre, sim, lre, lim, h0re, h0im, rev))

    op.defvjp(fwd, bwd)
    return op


scan_up = make_scan(False)
scan_down = make_scan(True)


def _adamw_call(parts, w, m, v, name):
    r, c = w.shape
    tr = _tile(r, 256, SUBLANES)
    nparts = parts.shape[0]
    c1 = 1.0 - ADAM_B1 ** ADAM_STEP
    c2 = 1.0 - ADAM_B2 ** ADAM_STEP

    def body(p_ref, w_ref, m_ref, v_ref, g_ref, d_ref, nm_ref, nv_ref):
        g = p_ref[0].astype(F32)
        for s in range(1, nparts):
            g = g + p_ref[s].astype(F32)
        m1 = ADAM_B1 * m_ref[...] + (1.0 - ADAM_B1) * g
        v1 = ADAM_B2 * v_ref[...] + (1.0 - ADAM_B2) * jnp.square(g)
        g_ref[...] = g
        nm_ref[...] = m1
        nv_ref[...] = v1
        d_ref[...] = -ADAM_LR * ((m1 / c1) / (jnp.sqrt(v1 / c2) + ADAM_EPS) + ADAM_WD * w_ref[...])

    blk = pl.BlockSpec((tr, c), lambda i: (i, 0))
    return pl.pallas_call(
        body, name=name, grid=(r // tr,),
        in_specs=[pl.BlockSpec((nparts, tr, c), lambda i: (0, i, 0)), blk, blk, blk], out_specs=[blk] * 4,
        out_shape=[jax.ShapeDtypeStruct((r, c), F32)] * 4,
        compiler_params=_cparams(("parallel",), 4 * tr * c * (nparts + 7)),
    )(parts, w, m, v)


def _peer(k):
    x, y, c = lax.axis_index("x"), lax.axis_index("y"), lax.axis_index("c")
    px = 1 - x if k & 4 else x
    py = 1 - y if k & 2 else y
    pc = 1 - c if k & 1 else c
    return (px, py, pc), 4 * px + 2 * py + pc


def _my_slot():
    return 4 * lax.axis_index("x") + 2 * lax.axis_index("y") + lax.axis_index("c")


def all_gather(x, name):
    def body(x_ref, out_ref, send_sems, recv_sems, local_sem):
        me = _my_slot()
        mine = pltpu.make_async_copy(x_ref, out_ref.at[me], local_sem)
        mine.start()
        sends = []
        for k in range(1, N_DEV):
            peer, _ = _peer(k)
            cp = pltpu.make_async_remote_copy(
                src_ref=x_ref, dst_ref=out_ref.at[me], send_sem=send_sems.at[k - 1],
                recv_sem=recv_sems.at[k - 1], device_id=peer, device_id_type=MESH)
            cp.start()
            sends.append(cp)
        for k in range(1, N_DEV):
            peer, slot = _peer(k)
            pltpu.make_async_remote_copy(
                src_ref=x_ref, dst_ref=out_ref.at[slot], send_sem=send_sems.at[k - 1],
                recv_sem=recv_sems.at[k - 1], device_id=peer, device_id_type=MESH).wait_recv()
        for cp in sends:
            cp.wait_send()
        mine.wait()

    return pl.pallas_call(
        body, name=name,
        in_specs=[pl.BlockSpec(memory_space=pl.ANY)], out_specs=pl.BlockSpec(memory_space=pl.ANY),
        out_shape=jax.ShapeDtypeStruct((N_DEV,) + tuple(x.shape), x.dtype),
        scratch_shapes=[pltpu.SemaphoreType.DMA((N_DEV - 1,)), pltpu.SemaphoreType.DMA((N_DEV - 1,)),
                        pltpu.SemaphoreType.DMA],
    )(x)


def sibling_swap(x, name):
    def body(x_ref, out_ref, send_sem, recv_sem):
        x_, y_, c_ = lax.axis_index("x"), lax.axis_index("y"), lax.axis_index("c")
        cp = pltpu.make_async_remote_copy(
            src_ref=x_ref.at[1 - c_], dst_ref=out_ref, send_sem=send_sem, recv_sem=recv_sem,
            device_id=(x_, y_, 1 - c_), device_id_type=MESH)
        cp.start()
        cp.wait()

    return pl.pallas_call(
        body, name=name,
        in_specs=[pl.BlockSpec(memory_space=pl.ANY)], out_specs=pl.BlockSpec(memory_space=pl.ANY),
        out_shape=jax.ShapeDtypeStruct(x.shape[1:], x.dtype),
        scratch_shapes=[pltpu.SemaphoreType.DMA, pltpu.SemaphoreType.DMA],
    )(x)


def chip_exchange(x, name):
    def body(x_ref, out_ref, send_sems, recv_sems, local_sem):
        x_, y_, c_ = lax.axis_index("x"), lax.axis_index("y"), lax.axis_index("c")
        mine = 2 * x_ + y_
        local = pltpu.make_async_copy(x_ref.at[mine], out_ref.at[mine], local_sem)
        local.start()

        def copy(k):
            px = 1 - x_ if k & 2 else x_
            py = 1 - y_ if k & 1 else y_
            peer = 2 * px + py
            send = pltpu.make_async_remote_copy(
                src_ref=x_ref.at[peer], dst_ref=out_ref.at[mine], send_sem=send_sems.at[k - 1],
                recv_sem=recv_sems.at[k - 1], device_id=(px, py, c_), device_id_type=MESH)
            recv = pltpu.make_async_remote_copy(
                src_ref=x_ref.at[peer], dst_ref=out_ref.at[peer], send_sem=send_sems.at[k - 1],
                recv_sem=recv_sems.at[k - 1], device_id=(px, py, c_), device_id_type=MESH)
            return send, recv

        copies = [copy(k) for k in range(1, 4)]
        for send, _ in copies:
            send.start()
        for _, recv in copies:
            recv.wait_recv()
        for send, _ in copies:
            send.wait_send()
        local.wait()

    return pl.pallas_call(
        body, name=name,
        in_specs=[pl.BlockSpec(memory_space=pl.ANY)], out_specs=pl.BlockSpec(memory_space=pl.ANY),
        out_shape=jax.ShapeDtypeStruct(x.shape, x.dtype),
        scratch_shapes=[pltpu.SemaphoreType.DMA((3,)), pltpu.SemaphoreType.DMA((3,)),
                        pltpu.SemaphoreType.DMA],
    )(x)


def all_gather_two_level(x, name):
    def body(x_ref, out_ref, send_sems, recv_sems, local_sem):
        x_, y_, c_ = lax.axis_index("x"), lax.axis_index("y"), lax.axis_index("c")
        me, sibling = (x_, y_, c_), (x_, y_, 1 - c_)
        chips = [(1 - x_, y_), (x_, 1 - y_), (1 - x_, 1 - y_)]

        def slot(px, py, pc):
            return out_ref.at[4 * px + 2 * py + pc]

        def copy(k, block, to, src=None):
            return pltpu.make_async_remote_copy(
                src_ref=slot(*block) if src is None else src, dst_ref=slot(*block),
                send_sem=send_sems.at[k], recv_sem=recv_sems.at[k], device_id=to, device_id_type=MESH)

        mine = pltpu.make_async_copy(x_ref, slot(*me), local_sem)
        mine.start()
        first = [copy(0, me, sibling, src=x_ref)]
        first += [copy(1 + j, me, (*chip, c_), src=x_ref) for j, chip in enumerate(chips)]
        for cp in first:
            cp.start()
        passed = [copy(4 + j, (*chip, c_), sibling) for j, chip in enumerate(chips)]
        for j, chip in enumerate(chips):
            copy(1 + j, (*chip, c_), me).wait_recv()
            passed[j].start()
        copy(0, sibling, me).wait_recv()
        for j, chip in enumerate(chips):
            copy(4 + j, (*chip, 1 - c_), me).wait_recv()
        for cp in first + passed:
            cp.wait_send()
        mine.wait()

    return pl.pallas_call(
        body, name=name,
        in_specs=[pl.BlockSpec(memory_space=pl.ANY)], out_specs=pl.BlockSpec(memory_space=pl.ANY),
        out_shape=jax.ShapeDtypeStruct((N_DEV,) + tuple(x.shape), x.dtype),
        scratch_shapes=[pltpu.SemaphoreType.DMA((N_DEV - 1,)), pltpu.SemaphoreType.DMA((N_DEV - 1,)),
                        pltpu.SemaphoreType.DMA],
    )(x)


def _rope_tables(t):
    n_freq = HEAD_DIM // 4
    tok = jnp.arange(t)
    inv = ROPE_THETA ** (-jnp.arange(n_freq, dtype=F32) / n_freq)
    a_row = (tok // GRID_W).astype(F32)[:, None] * inv
    a_col = (tok % GRID_W).astype(F32)[:, None] * inv
    cos = jnp.concatenate([jnp.cos(a_row)] * 2 + [jnp.cos(a_col)] * 2, axis=1)
    sin = jnp.concatenate([-jnp.sin(a_row), jnp.sin(a_row), -jnp.sin(a_col), jnp.sin(a_col)], axis=1)
    return jnp.concatenate([cos, cos], axis=1), jnp.concatenate([sin, sin], axis=1)


def _block_diag(blocks):
    g, a, b = blocks.shape
    eye = jnp.eye(g, dtype=blocks.dtype)
    return jnp.einsum("gab,gk->gakb", blocks, eye).reshape(g * a, g * b)


def _ffn_fwd_calls(x, mod, g, w_in, w_out):
    shift, scale, gate = mod[0:1], mod[1:2], mod[2:3]
    h, = _row_call(lambda xt, gt, sh, sc: (_norm_mod_fn(xt, gt, sh, sc)[0].astype(BF16),),
                   [x], [g, shift, scale], 0, "ffn_norm")
    u = _mm(h, w_in, out_dtype=BF16, name="ffn_up")
    a, = _row_call(lambda ut: (_swiglu_fn(ut.astype(F32))[0].astype(BF16),), [u], [], 0, "ffn_act")
    y = _mm(a, w_out, name="ffn_down")
    out, = _row_call(functools.partial(_resid_fn, 0.5), [x, y], [gate], 0, "ffn_resid")
    return out, (h, u, a, y)


@jax.custom_vjp
def _ffn_half(x, mod, g, w_in, w_out):
    return _ffn_fwd_calls(x, mod, g, w_in, w_out)[0]


def _ffn_half_fwd(x, mod, g, w_in, w_out):
    out, saved = _ffn_fwd_calls(x, mod, g, w_in, w_out)
    return out, (x, mod, g, w_in, w_out, saved)


def _ffn_half_bwd(res, dxn):
    x, mod, g, w_in, w_out, (h, u, a, y) = res
    shift, scale, gate = mod[0:1], mod[1:2], mod[2:3]

    def resid_bwd(dt, yt, gt):
        return (0.5 * gt * dt).astype(BF16), 0.5 * jnp.sum(dt * yt, axis=0, keepdims=True)

    dy, dgate = _row_call(resid_bwd, [dxn, y], [gate], 1, "ffn_resid_bwd")
    da = _mm(dy, w_out, tb=True, out_dtype=BF16, name="ffn_down_dx")
    dw_out = _mm(a, dy, ta=True, out_dtype=w_out.dtype, name="ffn_down_dw")

    def act_bwd(ut, dat):
        return (jax.vjp(_swiglu_fn, ut.astype(F32))[1]((dat.astype(F32),))[0].astype(BF16),)

    du, = _row_call(act_bwd, [u, da], [], 0, "ffn_act_bwd")
    dh = _mm(du, w_in, tb=True, name="ffn_up_dx")
    dw_in = _mm(h, du, ta=True, out_dtype=w_in.dtype, name="ffn_up_dw")

    def norm_bwd(xt, dht, dt, gt, sh, sc):
        dx, dg, dsh, dsc = jax.vjp(_norm_mod_fn, xt, gt, sh, sc)[1]((dht,))
        return dx + dt, dg, dsh, dsc

    dx, dg, dshift, dscale = _row_call(norm_bwd, [x, dh, dxn], [g, shift, scale], 3, "ffn_norm_bwd")
    return dx, jnp.concatenate([dshift, dscale, dgate], axis=0), dg, dw_in, dw_out


_ffn_half.defvjp(_ffn_half_fwd, _ffn_half_bwd)


def _s5_discretize(a_re, a_im, log_dt, b_re, b_im):
    lam = lax.complex(a_re, a_im)
    dt = jnp.exp(log_dt)[:, None]
    lam_bar = jnp.exp(lam * dt)
    b_bar = ((lam_bar - 1.0) / lam)[..., None] * lax.complex(b_re, b_im)
    return lam_bar, b_bar


def _s5_branch(u_lat, u_ctx, w, l, with_ctx_out):
    zero = jnp.zeros((SUBLANES, SSM_LANES // SUBLANES), F32)
    lat_terms, ctx_terms = [], []
    for d, scan in enumerate((scan_up, scan_down)):
        lam_bar, b_bar = _s5_discretize(w["ssm_a_re"][l, d], w["ssm_a_im"][l, d], w["ssm_log_dt"][l, d],
                                        w["ssm_b_re"][l, d], w["ssm_b_im"][l, d])
        lre = jnp.real(lam_bar).reshape(zero.shape)
        lim = jnp.imag(lam_bar).reshape(zero.shape)
        b_t = jnp.swapaxes(b_bar, 1, 2)
        b_re, b_im = _block_diag(jnp.real(b_t)), _block_diag(jnp.imag(b_t))
        c_re = _block_diag(jnp.swapaxes(w["ssm_c_re"][l, d], 1, 2))
        c_im = _block_diag(jnp.swapaxes(w["ssm_c_im"][l, d], 1, 2))
        sc_re, sc_im = scan(linear_to_planes(u_ctx, b_re), linear_to_planes(u_ctx, b_im), lre, lim, zero, zero)
        last = 0 if d == 1 else u_ctx.shape[0] - 1
        sl_re, sl_im = scan(linear_to_planes(u_lat, b_re), linear_to_planes(u_lat, b_im), lre, lim,
                            sc_re[last], sc_im[last])
        lat_terms += [linear_from_planes(sl_re, c_re), linear_from_planes(sl_im, c_im)]
        if with_ctx_out:
            ctx_terms += [linear_from_planes(sc_re, c_re), linear_from_planes(sc_im, c_im)]
    d_skip = w["ssm_d"][l][None, :]

    def out(terms, u):
        y, = s5_pre(*terms, u, d_skip)
        return glu(linear(y, w["glu_w"][l]))[0]

    return out(lat_terms, u_lat), (out(ctx_terms, u_ctx) if with_ctx_out else None)


def _pool_branch(xa, w, l):
    y = linear(pool_diff(xa), _block_diag(w["pool_w"][l]))
    return scale_rows(y, w["pool_scale"][l][None, :])[0]


_CTX_GROUPS = (O_VB - O_KB, O_UC - O_VB, O_KD - O_UC, O_VD - O_KD, CTX_COLS - O_VD)
_ALL_GROUPS = _CTX_GROUPS + (O_QD - O_QB, O_XA - O_QD, O_GATE - O_XA)
project_ctx = make_split_linear(_CTX_GROUPS)
project_all = make_split_linear(_ALL_GROUPS)


def _merge_branches(branches, gate_logits, w, l):
    zs = [linear_b(y, w["branch_w"][l, k]) for k, y in enumerate(branches)]
    return linear(merge(gate_logits, *zs)[0], w["out_w"][l])


def _token_mixer(h, hc, cos, sin, w, l, with_ctx_out):
    w_in, w_gate = w["w_in"][l][:, :O_GATE], w["w_in"][l][:, O_GATE:]
    kb, vb, uc, kd, vd, qb, qd, xa = project_all(h, w_in)
    p_gate = linear_b(h, w_gate)
    if with_ctx_out:
        kb_c, vb_c, uc_c, kd_c, vd_c, qb_c, qd_c, xa_c = project_all(hc, w_in)
        pc_gate = linear_b(hc, w_gate)
    else:
        kb_c, vb_c, uc_c, kd_c, vd_c = project_ctx(hc, w_in[:, :CTX_COLS])
    sink = w["win_sink"][l]
    q_g = jnp.tile(w["qk_norm"][l, 0], KV_HEADS * Q_PER_KV)[None, :]
    k_g = jnp.tile(w["qk_norm"][l, 1], KV_HEADS)[None, :]
    k_win_c = _to_heads(kb_c, KV_HEADS)
    v_win_c = _to_heads(vb_c, KV_HEADS)
    k_glb_c = _to_heads(head_norm(kd_c, k_g)[0], KV_HEADS)
    v_glb_c = _to_heads(vd_c, KV_HEADS)
    y_a = _pool_branch(xa, w, l)
    q_win = _q_heads(rope(qb, cos, sin)[0])
    k_win = _to_heads(rope(kb, cos, sin)[0], KV_HEADS)
    v_win = _to_heads(vb, KV_HEADS)
    y_b = _from_q_heads(attn_window(q_win, k_win_c, v_win_c, k_win, v_win, sink))
    y_c, y_c_ctx = _s5_branch(uc, uc_c, w, l, with_ctx_out)
    q_glb = _q_heads(norm_rope(qd, cos, sin, q_g)[0])
    k_glb = _to_heads(norm_rope(kd, cos, sin, k_g)[0], KV_HEADS)
    v_glb = _to_heads(vd, KV_HEADS)
    y_d = _from_q_heads(attn_global(q_glb, k_glb_c, v_glb_c, k_glb, v_glb))
    y = _merge_branches((y_a, y_b, y_c, y_d), p_gate, w, l)
    if not with_ctx_out:
        return y, None
    y_a_c = _pool_branch(xa_c, w, l)
    y_b_c = _from_q_heads(attn_ctx_sink(_q_heads(qb_c), k_win_c, v_win_c, sink))
    q_glb_c = _q_heads(head_norm(qd_c, q_g)[0])
    y_d_c = _from_q_heads(attn_ctx(q_glb_c, k_glb_c, v_glb_c))
    return y, _merge_branches((y_a_c, y_b_c, y_c_ctx, y_d_c), pc_gate, w, l)


def local_loss(w, x, c, ctx, target):
    depth = w["w_mod"].shape[0]
    cos, sin = _rope_tables(x.shape[0])
    cond = jnp.concatenate([c, w["c_ctx"][None, :], jnp.zeros((COND_ROWS - 2, D_MODEL), F32)], axis=0)
    s_all, = silu_rows(cond)
    for l in range(depth):
        last = l == depth - 1
        m_all = (linear(s_all, w["w_mod"][l]) + w["b_mod"][l][None, :]).reshape(COND_ROWS, N_SUB, 3, D_MODEL)
        m, mc = m_all[0], m_all[1]
        g = w["norm_g"][l][:, None, :]
        x = _ffn_half(x, m[0], g[0], w["ffn_in"][l, 0], w["ffn_out"][l, 0])
        ctx = _ffn_half(ctx, mc[0], g[0], w["ffn_in"][l, 0], w["ffn_out"][l, 0])
        h, = norm_mod(x, g[1], m[1, 0:1], m[1, 1:2])
        hc, = norm_mod(ctx, g[1], mc[1, 0:1], mc[1, 1:2])
        y, y_ctx = _token_mixer(h, hc, cos, sin, w, l, not last)
        x, = resid_full(x, y, m[1, 2:3])
        if not last:
            ctx, = resid_full(ctx, y_ctx, mc[1, 2:3])
        x = _ffn_half(x, m[2], g[2], w["ffn_in"][l, 1], w["ffn_out"][l, 1])
        if not last:
            ctx = _ffn_half(ctx, mc[2], g[2], w["ffn_in"][l, 1], w["ffn_out"][l, 1])
    return jnp.sum(loss_rows(x, target, w["final_g"][None, :])[0])


PACK_COLS = 1024


def _pack(arrays):
    flat = jnp.concatenate([a.reshape(-1) for a in arrays])
    pad = (-flat.shape[0]) % (PACK_COLS * 16)
    return jnp.pad(flat, (0, pad)).reshape(-1, PACK_COLS)


def _unpack(slab, shapes):
    out, off = [], 0
    for s in shapes:
        n = math.prod(s)
        r0, r1 = off // PACK_COLS, -(-(off + n) // PACK_COLS)
        rows, start = slab[r0:r1], off - r0 * PACK_COLS
        if start == 0 and n == (r1 - r0) * PACK_COLS:
            out.append(rows.reshape(s))
        else:
            out.append(rows.reshape(-1)[start:start + n].reshape(s))
        off += n
    return out


def _full_from_shards(gathered, shard_shape, axis):
    z = jnp.moveaxis(gathered.reshape((N_DEV,) + tuple(shard_shape)), 0, axis)
    shape = list(shard_shape)
    shape[axis] *= N_DEV
    return z.reshape(shape)


def _shards_from_full(full, axis):
    shape = list(full.shape)
    shape[axis:axis + 1] = [N_DEV, shape[axis] // N_DEV]
    return jnp.moveaxis(full.reshape(shape), axis, 0)


def kernel(x, c, ctx, c_ctx, w_mod, b_mod, norm_g, ffn_in, ffn_out, w_in, win_sink, qk_norm, pool_w, pool_scale, ssm_a_re, ssm_a_im, ssm_log_dt, ssm_b_re, ssm_b_im, ssm_c_re, ssm_c_im, ssm_d, glu_w, branch_w, out_w, final_g, loss_target, m_c_ctx, m_w_mod, m_b_mod, m_norm_g, m_ffn_in, m_ffn_out, m_w_in, m_win_sink, m_qk_norm, m_pool_w, m_pool_scale, m_ssm_a_re, m_ssm_a_im, m_ssm_log_dt, m_ssm_b_re, m_ssm_b_im, m_ssm_c_re, m_ssm_c_im, m_ssm_d, m_glu_w, m_branch_w, m_out_w, m_final_g, v_c_ctx, v_w_mod, v_b_mod, v_norm_g, v_ffn_in, v_ffn_out, v_w_in, v_win_sink, v_qk_norm, v_pool_w, v_pool_scale, v_ssm_a_re, v_ssm_a_im, v_ssm_log_dt, v_ssm_b_re, v_ssm_b_im, v_ssm_c_re, v_ssm_c_im, v_ssm_d, v_glu_w, v_branch_w, v_out_w, v_final_g):
    given = dict(locals())
    wts = {n: given[n] for n in WEIGHTS}
    mom = {n: given["m_" + n] for n in WEIGHTS}
    var = {n: given["v_" + n] for n in WEIGHTS}
    me = _my_slot()

    shard_shapes = [wts[n].shape for n in SHARDED]
    w_slab = _pack([wts[n] for n in SHARDED])
    gathered = all_gather_two_level(w_slab.astype(BF16), "gather_weights")
    full = dict(wts)
    row = 0
    for n in SHARDED:
        n_rows = math.prod(wts[n].shape) // PACK_COLS
        full[n] = _full_from_shards(gathered[:, row:row + n_rows], wts[n].shape, SHARD_AXIS[n])
        row += n_rows
    g_slab = _pack([norm_g])
    g_all = all_gather(g_slab, "gather_norm_g")
    full["norm_g"] = _full_from_shards(
        jnp.stack([_unpack(g_all[s], [norm_g.shape])[0] for s in range(N_DEV)]), norm_g.shape, 2)

    loss, (gw, gx) = jax.value_and_grad(local_loss, argnums=(0, 1))(full, x[0], c, ctx[0], loss_target[0])
    loss = lax.psum(loss, ("x", "y", "c"))

    dest = [_shards_from_full(gw[n], SHARD_AXIS[n]).reshape(N_DEV // 2, 2, -1, PACK_COLS) for n in SHARDED]
    send = jnp.swapaxes(jnp.concatenate(dest, axis=2), 0, 1)
    from_sibling = sibling_swap(send, "exchange_grads_sibling")
    own = lax.dynamic_index_in_dim(send, lax.axis_index("c"), axis=0, keepdims=False)
    pair, = _row_call(lambda a, b: ((a.astype(F32) + b.astype(F32)).astype(BF16),),
                      [own.reshape(-1, PACK_COLS), from_sibling.reshape(-1, PACK_COLS)], [], 0,
                      "exchange_pair_sum")
    big_parts = chip_exchange(pair.reshape(own.shape), "exchange_grads_chips")
    small_names = SMALL + ("norm_g",)
    small_shapes = [gw[n].shape for n in small_names]
    small_parts = all_gather(_pack([gw[n] for n in small_names]), "gather_small_grads")

    big = _adamw_call(big_parts, w_slab, _pack([mom[n] for n in SHARDED]), _pack([var[n] for n in SHARDED]),
                      "adamw_sharded")
    big = [_unpack(b, shard_shapes) for b in big]
    col = me * norm_g.shape[2]

    def small_slab(src, shard_src):
        padded = jnp.zeros((norm_g.shape[0], norm_g.shape[1], norm_g.shape[2] * N_DEV), F32)
        padded = lax.dynamic_update_slice(padded, shard_src, (0, 0, col))
        return _pack([src[n] for n in SMALL] + [padded])

    small = _adamw_call(small_parts, small_slab(wts, norm_g), small_slab(mom, m_norm_g),
                        small_slab(var, v_norm_g), "adamw_small")
    small = [_unpack(s, small_shapes) for s in small]

    outs = {}
    for kind in range(4):
        for i, n in enumerate(SHARDED):
            outs[(kind, n)] = big[kind][i]
        for i, n in enumerate(small_names):
            val = small[kind][i]
            if n == "norm_g":
                val = lax.dynamic_slice(val, (0, 0, col), norm_g.shape)
            outs[(kind, n)] = val
    return (loss, gx[None], *[outs[(k, n)] for k in range(4) for n in WEIGHTS])
```

```python
import functools
import math

import jax
import jax.numpy as jnp
from jax import lax
from jax.experimental import pallas as pl
from jax.experimental.pallas import tpu as pltpu

F32 = jnp.float32
BF16 = jnp.bfloat16

D_MODEL = 1024
GRID_W = 64
HEAD_DIM = 64
N_BRANCH = 4
BRANCH_W = D_MODEL // N_BRANCH
WINDOW = 128
ROPE_THETA = 10000.0
EPS = 1e-6
D_FF = 2816
N_SUB = 3
POOL_WINDOWS = (2, 4, 8, 16)
POOL_GROUP = BRANCH_W // len(POOL_WINDOWS)
KV_HEADS = 2
Q_PER_KV = 2
SSM_GROUP = 16
SSM_GROUPS = BRANCH_W // SSM_GROUP
SSM_STATE = 64
SSM_LANES = SSM_GROUPS * SSM_STATE
O_KB, O_VB, O_UC, O_KD, O_VD, CTX_COLS = 0, 128, 256, 512, 640, 768
O_QB, O_QD, O_XA, O_GATE = 768, 1024, 1280, 1536
IN_W = O_GATE + N_BRANCH * D_MODEL

ADAM_LR, ADAM_B1, ADAM_B2, ADAM_EPS, ADAM_WD, ADAM_STEP = 0.001, 0.9, 0.999, 1e-08, 0.01, 10

N_DEV = 8
MESH = pl.DeviceIdType.MESH

V7X_VMEM_BYTES = 64 * 1024 * 1024
SUBLANES = 8
LANES = 128
NEG_BIG = -1e30
COND_ROWS = 128

SHARDED = ("w_mod", "ffn_in", "ffn_out", "w_in", "glu_w", "branch_w", "out_w")
SHARD_AXIS = {"w_mod": 2, "ffn_in": 3, "ffn_out": 2, "w_in": 2, "glu_w": 2, "branch_w": 3, "out_w": 1}
SMALL = ("c_ctx", "b_mod", "win_sink", "qk_norm", "pool_w", "pool_scale", "ssm_a_re", "ssm_a_im",
         "ssm_log_dt", "ssm_b_re", "ssm_b_im", "ssm_c_re", "ssm_c_im", "ssm_d", "final_g")
WEIGHTS = ("c_ctx", "w_mod", "b_mod", "norm_g", "ffn_in", "ffn_out", "w_in", "win_sink", "qk_norm",
           "pool_w", "pool_scale", "ssm_a_re", "ssm_a_im", "ssm_log_dt", "ssm_b_re", "ssm_b_im",
           "ssm_c_re", "ssm_c_im", "ssm_d", "glu_w", "branch_w", "out_w", "final_g")


def _tile(n, cap, mult):
    if n <= cap:
        return n
    t = (cap // mult) * mult
    while t >= mult:
        if n % t == 0:
            return t
        t -= mult
    return n


def _cparams(sem, tile_bytes, resident_bytes=0):
    limit = int(min(V7X_VMEM_BYTES - 8 * 2 ** 20,
                    max(32 * 2 ** 20, 3 * tile_bytes + resident_bytes + 8 * 2 ** 20)))
    return pltpu.CompilerParams(dimension_semantics=sem, vmem_limit_bytes=limit)


PLANE = (SUBLANES, 128)
PLANE_COLS = PLANE[0] * PLANE[1]


def _planes_to_rows(ref):
    return jnp.concatenate([ref[:, j, :] for j in range(PLANE[0])], axis=1)


def _mm(a, b, ta=False, tb=False, out_dtype=F32, a_planes=False, b_planes=False, out_planes=False,
        add=None, name="mm"):
    a_shape = (a.shape[0], PLANE_COLS) if a_planes else a.shape
    b_shape = (b.shape[0], PLANE_COLS) if b_planes else b.shape
    m, k = (a_shape[1], a_shape[0]) if ta else a_shape
    n = b_shape[0] if tb else b_shape[1]
    assert (b_shape[1] if tb else b_shape[0]) == k and not (b_planes and tb)
    tm, tn, tk = _tile(m, 1024, LANES), _tile(n, 1536, LANES), _tile(k, 1536, LANES)
    nk = k // tk
    dims = (((0 if ta else 1,), (1 if tb else 0,)), ((), ()))

    def body(a_ref, b_ref, *rest):
        add_ref = rest[0] if add is not None else None
        o_ref, acc_ref = rest[-2], rest[-1]
        kk = pl.program_id(2)

        @pl.when(kk == 0)
        def _():
            acc_ref[...] = jnp.zeros_like(acc_ref)

        av = _planes_to_rows(a_ref) if a_planes else a_ref[...]
        bv = _planes_to_rows(b_ref) if b_planes else b_ref[...]
        acc_ref[...] += lax.dot_general(av.astype(BF16), bv.astype(BF16), dims, preferred_element_type=F32)

        @pl.when(kk == nk - 1)
        def _():
            if out_planes:
                for j in range(PLANE[0]):
                    o_ref[:, j, :] = acc_ref[:, j * PLANE[1]:(j + 1) * PLANE[1]].astype(o_ref.dtype)
            elif add is not None:
                o_ref[...] = (acc_ref[...] + add_ref[...]).astype(o_ref.dtype)
            else:
                o_ref[...] = acc_ref[...].astype(o_ref.dtype)

    if a_planes:
        assert (tm if ta else tk) == PLANE_COLS
        a_spec = pl.BlockSpec(((tk if ta else tm),) + PLANE, (lambda i, j, kk: (kk, 0, 0)) if ta
                              else (lambda i, j, kk: (i, 0, 0)))
    else:
        a_spec = (pl.BlockSpec((tk, tm), lambda i, j, kk: (kk, i)) if ta
                  else pl.BlockSpec((tm, tk), lambda i, j, kk: (i, kk)))
    if b_planes:
        assert tn == PLANE_COLS
        b_spec = pl.BlockSpec((tk,) + PLANE, lambda i, j, kk: (kk, 0, 0))
    else:
        b_spec = (pl.BlockSpec((tn, tk), lambda i, j, kk: (j, kk)) if tb
                  else pl.BlockSpec((tk, tn), lambda i, j, kk: (kk, j)))
    if out_planes:
        assert tn == PLANE_COLS
        o_spec = pl.BlockSpec((tm,) + PLANE, lambda i, j, kk: (i, 0, 0))
        o_shape = jax.ShapeDtypeStruct((m,) + PLANE, out_dtype)
    else:
        o_spec = pl.BlockSpec((tm, tn), lambda i, j, kk: (i, j))
        o_shape = jax.ShapeDtypeStruct((m, n), out_dtype)
    tile_bytes = (a.dtype.itemsize * tm * tk + b.dtype.itemsize * tk * tn
                  + jnp.dtype(out_dtype).itemsize * tm * tn + 2 * tm * tn)
    in_specs, args = [a_spec, b_spec], [a, b]
    if add is not None:
        assert not out_planes and add.shape == (m, n)
        in_specs.append(pl.BlockSpec((tm, tn), lambda i, j, kk: (i, j)))
        args.append(add)
        tile_bytes += add.dtype.itemsize * tm * tn
    return pl.pallas_call(
        body, name=name, grid=(m // tm, n // tn, nk),
        in_specs=in_specs, out_specs=o_spec, out_shape=o_shape,
        scratch_shapes=[pltpu.VMEM((tm, tn), F32)],
        compiler_params=_cparams(("parallel", "parallel", "arbitrary"), tile_bytes),
    )(*args)


def make_linear(out_dtype, x_planes=False, out_planes=False):
    @jax.custom_vjp
    def op(x, w):
        return _mm(x, w, out_dtype=out_dtype, a_planes=x_planes, out_planes=out_planes, name="linear_fwd")

    def fwd(x, w):
        return op(x, w), (x, w)

    def bwd(res, dy):
        x, w = res
        return (_mm(dy, w, tb=True, out_dtype=x.dtype, a_planes=out_planes, out_planes=x_planes,
                    name="linear_dx"),
                _mm(x, dy, ta=True, out_dtype=w.dtype, a_planes=x_planes, b_planes=out_planes,
                    name="linear_dw"))

    op.defvjp(fwd, bwd)
    return op


linear = make_linear(F32)
linear_b = make_linear(BF16)
linear_to_planes = make_linear(F32, out_planes=True)
linear_from_planes = make_linear(F32, x_planes=True)


def _split_rows(t):
    return _tile(t, 512, 2 * SUBLANES)


def _split_fwd(x, w, widths):
    t, k = x.shape
    n = w.shape[1]
    tm = _split_rows(t)

    def body(x_ref, w_ref, *o_refs):
        y = jnp.dot(x_ref[...].astype(BF16), w_ref[...].astype(BF16), preferred_element_type=F32)
        off = 0
        for o_ref, wd in zip(o_refs, widths):
            o_ref[...] = y[:, off:off + wd]
            off += wd

    return pl.pallas_call(
        body, name="split_linear_fwd", grid=(t // tm,),
        in_specs=[pl.BlockSpec((tm, k), lambda i: (i, 0)), pl.BlockSpec((k, n), lambda i: (0, 0))],
        out_specs=[pl.BlockSpec((tm, wd), lambda i: (i, 0)) for wd in widths],
        out_shape=[jax.ShapeDtypeStruct((t, wd), F32) for wd in widths],
        compiler_params=_cparams(("parallel",), 4 * tm * (k + 2 * n) + w.dtype.itemsize * k * n),
    )(x, w)


def _split_dx(cts, w):
    t = cts[0].shape[0]
    k, n = w.shape
    tm = _split_rows(t)

    def body(*refs):
        dy = jnp.concatenate([r[...].astype(BF16) for r in refs[:-2]], axis=1)
        refs[-1][...] = lax.dot_general(dy, refs[-2][...].astype(BF16), (((1,), (1,)), ((), ())),
                                        preferred_element_type=F32)

    return pl.pallas_call(
        body, name="split_linear_dx", grid=(t // tm,),
        in_specs=[pl.BlockSpec((tm, c.shape[1]), lambda i: (i, 0)) for c in cts]
        + [pl.BlockSpec((k, n), lambda i: (0, 0))],
        out_specs=pl.BlockSpec((tm, k), lambda i: (i, 0)), out_shape=jax.ShapeDtypeStruct((t, k), F32),
        compiler_params=_cparams(("parallel",), 4 * tm * (k + 2 * n) + w.dtype.itemsize * k * n),
    )(*cts, w)


def _split_dw(x, cts, out_dtype):
    t, k = x.shape
    n = sum(c.shape[1] for c in cts)
    tk = _split_rows(t)
    steps = t // tk

    def body(x_ref, *refs):
        o_ref, acc_ref = refs[-2], refs[-1]

        @pl.when(pl.program_id(0) == 0)
        def _():
            acc_ref[...] = jnp.zeros_like(acc_ref)

        dy = jnp.concatenate([r[...].astype(BF16) for r in refs[:-2]], axis=1)
        acc_ref[...] += lax.dot_general(x_ref[...].astype(BF16), dy, (((0,), (0,)), ((), ())),
                                        preferred_element_type=F32)

        @pl.when(pl.program_id(0) == steps - 1)
        def _():
            o_ref[...] = acc_ref[...].astype(o_ref.dtype)

    return pl.pallas_call(
        body, name="split_linear_dw", grid=(steps,),
        in_specs=[pl.BlockSpec((tk, k), lambda i: (i, 0))]
        + [pl.BlockSpec((tk, c.shape[1]), lambda i: (i, 0)) for c in cts],
        out_specs=pl.BlockSpec((k, n), lambda i: (0, 0)), out_shape=jax.ShapeDtypeStruct((k, n), out_dtype),
        scratch_shapes=[pltpu.VMEM((k, n), F32)],
        compiler_params=_cparams(("arbitrary",), 4 * tk * (k + n), 3 * 4 * k * n),
    )(x, *cts)


def make_split_linear(widths):
    @jax.custom_vjp
    def op(x, w):
        return tuple(_split_fwd(x, w, widths))

    def fwd(x, w):
        return op(x, w), (x, w)

    def bwd(res, cts):
        x, w = res
        return _split_dx(cts, w), _split_dw(x, cts, w.dtype)

    op.defvjp(fwd, bwd)
    return op


ROW_TILE_BYTES = 6 * 2 ** 20


def _row_tile(t, row_bytes):
    tm = 1024
    while tm > 2 * SUBLANES and tm * row_bytes > ROW_TILE_BYTES:
        tm //= 2
    return _tile(t, tm, 2 * SUBLANES)


def _row_call(fn, rows, params, n_reduce, name):
    t = rows[0].shape[0]
    out_avals = jax.eval_shape(fn, *rows, *params)
    n_out = len(out_avals) - n_reduce
    row_avals, red_avals = out_avals[:n_out], out_avals[n_out:]
    row_bytes = sum(r.shape[1] * r.dtype.itemsize for r in (*rows, *row_avals))
    tm = _row_tile(t, row_bytes)
    n_in = len(rows) + len(params)

    def body(*refs):
        outs = fn(*[r[...] for r in refs[:n_in]])
        o_refs = refs[n_in:]
        for o_ref, o in zip(o_refs[:n_out], outs[:n_out]):
            o_ref[...] = o.astype(o_ref.dtype)
        if n_reduce:
            @pl.when(pl.program_id(0) == 0)
            def _():
                for r in o_refs[n_out:]:
                    r[...] = jnp.zeros_like(r)

            for r, o in zip(o_refs[n_out:], outs[n_out:]):
                r[...] += o.astype(r.dtype)

    in_specs = ([pl.BlockSpec((tm, r.shape[1]), lambda i: (i, 0)) for r in rows]
                + [pl.BlockSpec(p.shape, lambda i: (0, 0)) for p in params])
    out_specs = ([pl.BlockSpec((tm, o.shape[1]), lambda i: (i, 0)) for o in row_avals]
                 + [pl.BlockSpec(o.shape, lambda i: (0, 0)) for o in red_avals])
    return pl.pallas_call(
        body, name=name, grid=(t // tm,), in_specs=in_specs, out_specs=out_specs,
        out_shape=[jax.ShapeDtypeStruct(o.shape, o.dtype) for o in out_avals],
        compiler_params=_cparams(("arbitrary",) if n_reduce else ("parallel",), tm * row_bytes),
    )(*rows, *params)


def rowwise(fn, n_rows, name):
    @jax.custom_vjp
    def op(*args):
        return tuple(_row_call(fn, args[:n_rows], args[n_rows:], 0, name + "_fwd"))

    def fwd(*args):
        return op(*args), args

    def bwd(args, cts):
        n_ct = len(cts)

        def bwd_fn(*a):
            r, ct, p = a[:n_rows], a[n_rows:n_rows + n_ct], a[n_rows + n_ct:]
            return jax.vjp(fn, *r, *p)[1](tuple(ct))

        return tuple(_row_call(bwd_fn, (*args[:n_rows], *cts), args[n_rows:], len(args) - n_rows,
                               name + "_bwd"))

    op.defvjp(fwd, bwd)
    return op


@functools.partial(jax.custom_vjp, nondiff_argnums=(1,))
def _swap_lanes(x, k):
    n = x.shape[-1]
    lane = lax.broadcasted_iota(jnp.int32, x.shape, x.ndim - 1)
    return jnp.where((lane & k) == 0, pltpu.roll(x, n - k, x.ndim - 1), pltpu.roll(x, k, x.ndim - 1))


def _swap_lanes_fwd(x, k):
    return _swap_lanes(x, k), None


def _swap_lanes_bwd(k, _, g):
    return (_swap_lanes(g, k),)


_swap_lanes.defvjp(_swap_lanes_fwd, _swap_lanes_bwd)


def _head_sum(x):
    s = x
    k = 1
    while k < HEAD_DIM:
        s = s + _swap_lanes(s, k)
        k *= 2
    return s


def _rms(x):
    return x * lax.rsqrt(jnp.mean(x * x, axis=-1, keepdims=True) + EPS)


def _norm_mod_fn(x, g, shift, scale):
    return ((_rms(x) * g) * (1.0 + scale) + shift,)


def _swiglu_fn(u):
    gate, up = u[:, :D_FF], u[:, D_FF:]
    return (jax.nn.silu(gate) * up,)


def _resid_fn(coef, x, y, gate):
    return (x + (coef * gate) * y,)


def _scale_fn(y, s):
    return (y * s,)


def _tile_lanes(tab, width):
    return tab if tab.shape[1] == width else jnp.concatenate([tab] * (width // tab.shape[1]), axis=1)


def _rope_fn(x, cos, sin):
    w = x.shape[1]
    return (x * _tile_lanes(cos, w) + _swap_lanes(x, 16) * _tile_lanes(sin, w),)


def _head_norm(x, g):
    ms = _head_sum(x * x) * (1.0 / HEAD_DIM)
    return x * lax.rsqrt(ms + EPS) * g


def _norm_rope_fn(x, cos, sin, g):
    return _rope_fn(_head_norm(x, g), cos, sin)


def _head_norm_fn(x, g):
    return (_head_norm(x, g),)


def _merge_fn(gl, z0, z1, z2, z3):
    zs = (z0, z1, z2, z3)
    terms = [jax.nn.sigmoid(gl[:, k * D_MODEL:(k + 1) * D_MODEL].astype(F32)) * zs[k].astype(F32)
             for k in range(N_BRANCH)]
    return (sum(terms[1:], terms[0]),)


def _s5_pre_fn(y0r, y0i, y1r, y1i, u, d):
    return (jax.nn.gelu(((y0r - y0i) + (y1r - y1i)) + d * u),)


def _glu_fn(z):
    return (z[:, :BRANCH_W] * jax.nn.sigmoid(z[:, BRANCH_W:]),)


def _silu_fn(x):
    return (jax.nn.silu(x),)


def _loss_fn(x, tgt, g):
    err = jnp.square(_rms(x) * g - tgt)
    return (0.5 * jnp.mean(err, axis=-1, keepdims=True),)


norm_mod = rowwise(_norm_mod_fn, 1, "norm_mod")
resid_full = rowwise(functools.partial(_resid_fn, 1.0), 2, "resid_full")
scale_rows = rowwise(_scale_fn, 1, "pool_scale")
rope = rowwise(_rope_fn, 3, "rope")
norm_rope = rowwise(_norm_rope_fn, 3, "norm_rope")
head_norm = rowwise(_head_norm_fn, 1, "head_norm")
merge = rowwise(_merge_fn, 5, "merge")
s5_pre = rowwise(_s5_pre_fn, 5, "s5_pre")
glu = rowwise(_glu_fn, 1, "glu")
silu_rows = rowwise(_silu_fn, 1, "silu")
loss_rows = rowwise(_loss_fn, 2, "loss_head")


POOL_HALO = 16


def _pool_call(xa, adjoint, name):
    n, width = xa.shape
    tm = _tile(n, 512, POOL_HALO)
    halo_blocks = tm // POOL_HALO
    last_halo = n // POOL_HALO - 1
    ext_rows = tm + 2 * POOL_HALO

    def body(prev_ref, cur_ref, next_ref, o_ref, ext_ref):
        i = pl.program_id(0)
        ext_ref[0:POOL_HALO] = prev_ref[...]
        ext_ref[POOL_HALO:POOL_HALO + tm] = cur_ref[...]
        ext_ref[POOL_HALO + tm:ext_rows] = next_ref[...]
        e = ext_ref[...]
        row = lax.broadcasted_iota(jnp.int32, e.shape, 0) + (i * tm - POOL_HALO)
        grp = lax.broadcasted_iota(jnp.int32, e.shape, 1) // POOL_GROUP
        win = jnp.where(grp == 0, POOL_WINDOWS[0],
                        jnp.where(grp == 1, POOL_WINDOWS[1], jnp.where(grp == 2, POOL_WINDOWS[2], POOL_WINDOWS[3])))
        valid = (row >= 0) & (row < n)
        lo = jnp.clip(row - win // 2, 0, n)
        hi = jnp.clip(row - win // 2 + win, 0, n)
        cnt = jnp.maximum((hi - lo).astype(F32), 1.0)
        e0 = jnp.where(valid, e / cnt if adjoint else e, 0.0)

        def shift(z, s):
            return pltpu.roll(z, s % ext_rows, 0)

        s2 = e0 + shift(e0, -1 if adjoint else 1)
        s4 = shift(s2, 1) + shift(s2, -1)
        s8 = shift(s4, 2) + shift(s4, -2)
        s16 = shift(s8, 4) + shift(s8, -4)
        s = jnp.where(grp == 0, s2, jnp.where(grp == 1, s4, jnp.where(grp == 2, s8, s16)))
        out = (s - e) if adjoint else (s / cnt - e)
        o_ref[...] = out[POOL_HALO:POOL_HALO + tm]

    return pl.pallas_call(
        body, name=name, grid=(n // tm,),
        in_specs=[pl.BlockSpec((POOL_HALO, width), lambda i: (jnp.maximum(i * halo_blocks - 1, 0), 0)),
                  pl.BlockSpec((tm, width), lambda i: (i, 0)),
                  pl.BlockSpec((POOL_HALO, width), lambda i: (jnp.minimum((i + 1) * halo_blocks, last_halo), 0))],
        out_specs=pl.BlockSpec((tm, width), lambda i: (i, 0)),
        out_shape=jax.ShapeDtypeStruct((n, width), F32),
        scratch_shapes=[pltpu.VMEM((ext_rows, width), F32)],
        compiler_params=_cparams(("parallel",), 4 * 4 * ext_rows * width),
    )(xa, xa, xa)


@jax.custom_vjp
def pool_diff(xa):
    return _pool_call(xa, False, "pool_fwd")


pool_diff.defvjp(lambda xa: (pool_diff(xa), None), lambda _, g: (_pool_call(g, True, "pool_bwd"),))


ATT_SCALE = HEAD_DIM ** -0.5
ATT_BQ = 512


def _qk_scores(q, k):
    return lax.dot_general((q * ATT_SCALE).astype(BF16), k.astype(BF16), (((1,), (1,)), ((), ())),
                           preferred_element_type=F32)


def _sink_rows(sink_ref, h, bq):
    r = lax.broadcasted_iota(jnp.int32, (Q_PER_KV * bq, 1), 0)
    return jnp.where(r < bq, sink_ref[h * Q_PER_KV], sink_ref[h * Q_PER_KV + 1])


ATT_SUB_ROWS = 256


def _flash_fwd(q, klt, vl, kct, vc, sink):
    kvh, g, t, dh = q.shape
    c = kct.shape[2]
    has_lat = klt is not None
    has_sink = sink is not None
    bq = _tile(t, ATT_BQ, LANES)
    bk = _tile(t, 4096, LANES)
    nkv = t // bk if has_lat else 1
    rows = g * bq
    sub = min(ATT_SUB_ROWS, rows)

    def with_ones(v):
        return jnp.concatenate([v, jnp.ones(v.shape[:2] + (1,), v.dtype),
                                jnp.zeros(v.shape[:2] + (LANES - dh - 1,), v.dtype)], axis=2)

    def body(*refs):
        it = iter(refs)
        q_ref, kc_ref, vc_ref = next(it), next(it), next(it)
        kl_ref, vl_ref = (next(it), next(it)) if has_lat else (None, None)
        sink_ref = next(it) if has_sink else None
        o_ref, lse_ref, m_ref, acc_ref = next(it), next(it), next(it), next(it)
        h, kj = pl.program_id(0), pl.program_id(2)
        qv = (q_ref[0].reshape(rows, dh) * ATT_SCALE).astype(BF16)
        is_sum_lane = lax.broadcasted_iota(jnp.int32, (rows, LANES), 1) == dh

        def part(kt_ref, v_ref):
            kt, v = kt_ref[0].astype(BF16), v_ref[0].astype(BF16)
            m_all, acc_all = m_ref[...], acc_ref[...]
            m_out, acc_out = [], []
            for r0 in range(0, rows, sub):
                s = jnp.dot(qv[r0:r0 + sub], kt, preferred_element_type=F32)
                m_old = m_all[r0:r0 + sub]
                m_new = jnp.maximum(m_old, jnp.max(s, axis=-1, keepdims=True))
                p = jnp.exp(s - m_new)
                m_out.append(m_new)
                acc_out.append(jnp.exp(m_old - m_new) * acc_all[r0:r0 + sub]
                               + jnp.dot(p.astype(BF16), v, preferred_element_type=F32))
            m_ref[...] = jnp.concatenate(m_out, axis=0)
            acc_ref[...] = jnp.concatenate(acc_out, axis=0)

        @pl.when(kj == 0)
        def _():
            if has_sink:
                m_ref[...] = _sink_rows(sink_ref, h, bq)
                acc_ref[...] = jnp.where(is_sum_lane, 1.0, 0.0)
            else:
                m_ref[...] = jnp.full_like(m_ref, NEG_BIG)
                acc_ref[...] = jnp.zeros_like(acc_ref)
            part(kc_ref, vc_ref)

        if has_lat:
            part(kl_ref, vl_ref)

        @pl.when(kj == nkv - 1)
        def _():
            acc = acc_ref[...]
            l = jnp.sum(jnp.where(is_sum_lane, acc, 0.0), axis=-1, keepdims=True)
            o_ref[0] = (acc / l).reshape(g, bq, LANES)
            lse_ref[0] = (m_ref[...] + jnp.log(l)).reshape(g, bq, 1)

    q_spec = pl.BlockSpec((1, g, bq, dh), lambda h, i, j: (h, 0, i, 0))
    o_spec = pl.BlockSpec((1, g, bq, LANES), lambda h, i, j: (h, 0, i, 0))
    r_spec = pl.BlockSpec((1, g, bq, 1), lambda h, i, j: (h, 0, i, 0))
    in_specs = [q_spec, pl.BlockSpec((1, dh, c), lambda h, i, j: (h, 0, 0)),
                pl.BlockSpec((1, c, LANES), lambda h, i, j: (h, 0, 0))]
    args = [q, kct, with_ones(vc)]
    if has_lat:
        in_specs += [pl.BlockSpec((1, dh, bk), lambda h, i, j: (h, 0, j)),
                     pl.BlockSpec((1, bk, LANES), lambda h, i, j: (h, j, 0))]
        args += [klt, with_ones(vl)]
    if has_sink:
        in_specs.append(pl.BlockSpec(memory_space=pltpu.SMEM))
        args.append(sink)
    o_wide, lse = pl.pallas_call(
        body, name="attn_fwd", grid=(kvh, t // bq, nkv),
        in_specs=in_specs, out_specs=[o_spec, r_spec],
        out_shape=[jax.ShapeDtypeStruct((kvh, g, t, LANES), F32), jax.ShapeDtypeStruct((kvh, g, t, 1), F32)],
        scratch_shapes=[pltpu.VMEM((rows, 1), F32), pltpu.VMEM((rows, LANES), F32)],
        compiler_params=_cparams(("parallel", "parallel", "arbitrary"), 4 * 4 * rows * max(bk, c)),
    )(*args)
    return o_wide[..., :dh], lse


def _ctx_dq(q, kc, vc, sink, o, do, lse):
    kvh, g, t, dh = q.shape
    c = kc.shape[1]
    has_sink = sink is not None
    bq = _tile(t, ATT_BQ, LANES)
    rows = g * bq

    def body(*refs):
        q_ref, o_ref, do_ref, lse_ref, kc_ref, vc_ref = refs[:6]
        sink_ref = refs[6] if has_sink else None
        dq_ref, delta_ref, dsink_ref = refs[-3:]
        dov = do_ref[0].reshape(rows, dh)
        lse_v = lse_ref[0].reshape(rows, 1)
        delta = jnp.sum(o_ref[0].reshape(rows, dh) * dov, axis=-1, keepdims=True)
        p = jnp.exp(_qk_scores(q_ref[0].reshape(rows, dh), kc_ref[0]) - lse_v)
        dp = lax.dot_general(dov.astype(BF16), vc_ref[0].astype(BF16), (((1,), (1,)), ((), ())),
                             preferred_element_type=F32)
        ds = (p * (dp - delta)).astype(BF16)
        dq = jnp.dot(ds, kc_ref[0].astype(BF16), preferred_element_type=F32) * ATT_SCALE
        dq_ref[0] = dq.reshape(g, bq, dh)
        delta_ref[0] = delta.reshape(g, bq, 1)
        if has_sink:
            p_sink = jnp.exp(_sink_rows(sink_ref, pl.program_id(0), bq) - lse_v)
            dsink_ref[0] = (-p_sink * delta).reshape(g, bq, 1)
        else:
            dsink_ref[0] = jnp.zeros((g, bq, 1), F32)

    q_spec = pl.BlockSpec((1, g, bq, dh), lambda h, i: (h, 0, i, 0))
    r_spec = pl.BlockSpec((1, g, bq, 1), lambda h, i: (h, 0, i, 0))
    c_spec = pl.BlockSpec((1, c, dh), lambda h, i: (h, 0, 0))
    in_specs, args = [q_spec, q_spec, q_spec, r_spec, c_spec, c_spec], [q, o, do, lse, kc, vc]
    if has_sink:
        in_specs.append(pl.BlockSpec(memory_space=pltpu.SMEM))
        args.append(sink)
    row_shape = jax.ShapeDtypeStruct((kvh, g, t, 1), F32)
    return pl.pallas_call(
        body, name="attn_ctx_dq", grid=(kvh, t // bq),
        in_specs=in_specs, out_specs=[q_spec, r_spec, r_spec],
        out_shape=[jax.ShapeDtypeStruct(q.shape, F32), row_shape, row_shape],
        compiler_params=_cparams(("parallel", "parallel"), 4 * 6 * rows * c),
    )(*args)


def _flash_dkv(q, do, lse, delta, k, v):
    kvh, g, t, dh = q.shape
    nk_rows = k.shape[1]
    bq = _tile(t, ATT_BQ, LANES)
    bk = _tile(nk_rows, 1024, LANES)
    nq = t // bq
    rows = g * bq

    def body(q_ref, do_ref, lse_ref, delta_ref, k_ref, v_ref, dk_ref, dv_ref, dk_acc, dv_acc):
        qj = pl.program_id(2)
        qv = q_ref[0].reshape(rows, dh)
        dov = do_ref[0].reshape(rows, dh)

        @pl.when(qj == 0)
        def _():
            dk_acc[...] = jnp.zeros_like(dk_acc)
            dv_acc[...] = jnp.zeros_like(dv_acc)

        s = _qk_scores(qv, k_ref[0])
        p = jnp.exp(s - lse_ref[0].reshape(rows, 1))
        dp = lax.dot_general(dov.astype(BF16), v_ref[0].astype(BF16), (((1,), (1,)), ((), ())),
                             preferred_element_type=F32)
        ds = p * (dp - delta_ref[0].reshape(rows, 1))
        tn = (((0,), (0,)), ((), ()))
        dv_acc[...] += lax.dot_general(p.astype(BF16), dov.astype(BF16), tn, preferred_element_type=F32)
        dk_acc[...] += lax.dot_general(ds.astype(BF16), qv.astype(BF16), tn, preferred_element_type=F32)

        @pl.when(qj == nq - 1)
        def _():
            dk_ref[0] = dk_acc[...] * ATT_SCALE
            dv_ref[0] = dv_acc[...]

    q_spec = pl.BlockSpec((1, g, bq, dh), lambda h, i, j: (h, 0, j, 0))
    r_spec = pl.BlockSpec((1, g, bq, 1), lambda h, i, j: (h, 0, j, 0))
    k_spec = pl.BlockSpec((1, bk, dh), lambda h, i, j: (h, i, 0))
    return pl.pallas_call(
        body, name="attn_dkv", grid=(kvh, nk_rows // bk, nq),
        in_specs=[q_spec, q_spec, r_spec, r_spec, k_spec, k_spec], out_specs=[k_spec, k_spec],
        out_shape=[jax.ShapeDtypeStruct(k.shape, F32), jax.ShapeDtypeStruct(k.shape, F32)],
        scratch_shapes=[pltpu.VMEM((bk, dh), F32), pltpu.VMEM((bk, dh), F32)],
        compiler_params=_cparams(("parallel", "parallel", "arbitrary"), 4 * 6 * rows * bk),
    )(q, do, lse, delta, k, v)


def _flash_bwd_full(q, kl, vl, kc, vc, o, do, lse):
    kvh, g, t, dh = q.shape
    c = kc.shape[1]
    bq, bk = _tile(t, 512, LANES), _tile(t, 1024, LANES)
    nq, nkv = t // bq, t // bk
    rows = g * bq
    klt, vlt = jnp.swapaxes(kl, 1, 2), jnp.swapaxes(vl, 1, 2)
    kct, vct = jnp.swapaxes(kc, 1, 2), jnp.swapaxes(vc, 1, 2)
    tn = (((0,), (0,)), ((), ()))

    def body(q_ref, o_ref, do_ref, lse_ref, kc_ref, kct_ref, vct_ref, kl_ref, klt_ref, vlt_ref,
             dq_ref, delta_ref, dk_hbm, dv_hbm, dq_acc, dl_ref, dk_acc, dv_acc):
        h, qi, kj = pl.program_id(0), pl.program_id(1), pl.program_id(2)
        q_raw = q_ref[0].reshape(rows, dh).astype(BF16)
        qv = (q_ref[0].reshape(rows, dh) * ATT_SCALE).astype(BF16)
        dov = do_ref[0].reshape(rows, dh).astype(BF16)
        lse_v = lse_ref[0].reshape(rows, 1)

        def tile(kt, vt):
            s = jnp.dot(qv, kt.astype(BF16), preferred_element_type=F32)
            p = jnp.exp(s - lse_v)
            dp = jnp.dot(dov, vt.astype(BF16), preferred_element_type=F32)
            return p, (p * (dp - dl_ref[...])).astype(BF16)

        @pl.when((qi == 0) & (kj == 0))
        def _():
            dk_acc[...] = jnp.zeros_like(dk_acc)
            dv_acc[...] = jnp.zeros_like(dv_acc)

        @pl.when(kj == 0)
        def _():
            dl_ref[...] = jnp.sum(o_ref[0].reshape(rows, dh) * do_ref[0].reshape(rows, dh),
                                  axis=-1, keepdims=True)
            _, ds = tile(kct_ref[0], vct_ref[0])
            dq_acc[...] = jnp.dot(ds, kc_ref[0].astype(BF16), preferred_element_type=F32)

        p, ds = tile(klt_ref[0], vlt_ref[0])
        dq_acc[...] += jnp.dot(ds, kl_ref[0].astype(BF16), preferred_element_type=F32)
        ks = pl.ds(pl.multiple_of(kj * bk, bk), bk)
        dv_acc[ks] += lax.dot_general(p.astype(BF16), dov, tn, preferred_element_type=F32)
        dk_acc[ks] += lax.dot_general(ds, q_raw, tn, preferred_element_type=F32)

        @pl.when(kj == nkv - 1)
        def _():
            dq_ref[0] = (dq_acc[...] * ATT_SCALE).reshape(g, bq, dh)
            delta_ref[0] = dl_ref[...].reshape(g, bq, 1)

        @pl.when((qi == nq - 1) & (kj == nkv - 1))
        def _():
            dk_acc[...] = dk_acc[...] * ATT_SCALE
            pltpu.sync_copy(dk_acc, dk_hbm.at[h])
            pltpu.sync_copy(dv_acc, dv_hbm.at[h])

    q_spec = pl.BlockSpec((1, g, bq, dh), lambda h, i, j: (h, 0, i, 0))
    r_spec = pl.BlockSpec((1, g, bq, 1), lambda h, i, j: (h, 0, i, 0))
    c_spec = pl.BlockSpec((1, c, dh), lambda h, i, j: (h, 0, 0))
    ct_spec = pl.BlockSpec((1, dh, c), lambda h, i, j: (h, 0, 0))
    l_spec = pl.BlockSpec((1, bk, dh), lambda h, i, j: (h, j, 0))
    lt_spec = pl.BlockSpec((1, dh, bk), lambda h, i, j: (h, 0, j))
    any_spec = pl.BlockSpec(memory_space=pl.ANY)
    kv_shape = jax.ShapeDtypeStruct((kvh, t, dh), F32)
    return pl.pallas_call(
        body, name="attn_bwd_full", grid=(kvh, nq, nkv),
        in_specs=[q_spec, q_spec, q_spec, r_spec, c_spec, ct_spec, ct_spec, l_spec, lt_spec, lt_spec],
        out_specs=[q_spec, r_spec, any_spec, any_spec],
        out_shape=[jax.ShapeDtypeStruct(q.shape, F32), jax.ShapeDtypeStruct((kvh, g, t, 1), F32),
                   kv_shape, kv_shape],
        scratch_shapes=[pltpu.VMEM((rows, dh), F32), pltpu.VMEM((rows, 1), F32),
                        pltpu.VMEM((t, dh), F32), pltpu.VMEM((t, dh), F32)],
        compiler_params=_cparams(("arbitrary", "arbitrary", "arbitrary"), 4 * 2 * rows * bk,
                                 2 * 4 * t * LANES),
    )(q, o, do, lse, kc, kct, vct, kl, klt, vlt)


BAND_BQ = 512


def _band_specs(t, shape_of):
    per = BAND_BQ // WINDOW
    n_halo = t // WINDOW

    def spec(n, index):
        shape, axis = shape_of(n)

        def index_map(h, i):
            idx = [h] + [0] * (len(shape) - 1)
            idx[axis] = index(i)
            return tuple(idx)

        return pl.BlockSpec(shape, index_map)

    return [spec(WINDOW, lambda i: jnp.maximum(i * per - 1, 0)),
            spec(BAND_BQ, lambda i: i),
            spec(WINDOW, lambda i: jnp.minimum((i + 1) * per, n_halo - 1))]


def _band_visible(i, t, c, rows):
    cols = c + BAND_BQ + 2 * WINDOW
    col = lax.broadcasted_iota(jnp.int32, (rows, cols), 1)
    qpos = i * BAND_BQ + lax.broadcasted_iota(jnp.int32, (rows, cols), 0) % BAND_BQ
    kpos = i * BAND_BQ - WINDOW + (col - c)
    return (col < c) | ((kpos >= 0) & (kpos < t) & (jnp.abs(kpos - qpos) <= WINDOW))


def _band_fwd(q, klt, vl, kct, vc, sink):
    kvh, g, t, dh = q.shape
    c = kct.shape[2]
    rows = g * BAND_BQ

    def body(q_ref, kct_ref, vc_ref, ktp, ktc, ktn, vp, vcur, vn, sink_ref, o_ref, lse_ref):
        h, i = pl.program_id(0), pl.program_id(1)
        qv = (q_ref[0].reshape(rows, dh) * ATT_SCALE).astype(BF16)
        kt = jnp.concatenate([kct_ref[0], ktp[0], ktc[0], ktn[0]], axis=1).astype(BF16)
        v = jnp.concatenate([vc_ref[0], vp[0], vcur[0], vn[0]], axis=0).astype(BF16)
        s = jnp.where(_band_visible(i, t, c, rows), jnp.dot(qv, kt, preferred_element_type=F32), NEG_BIG)
        sink_r = _sink_rows(sink_ref, h, BAND_BQ)
        m = jnp.maximum(sink_r, jnp.max(s, axis=-1, keepdims=True))
        p = jnp.exp(s - m)
        l = jnp.exp(sink_r - m) + jnp.sum(p, axis=-1, keepdims=True)
        o_ref[0] = (jnp.dot(p.astype(BF16), v, preferred_element_type=F32) / l).reshape(g, BAND_BQ, dh)
        lse_ref[0] = (m + jnp.log(l)).reshape(g, BAND_BQ, 1)

    q_spec = pl.BlockSpec((1, g, BAND_BQ, dh), lambda h, i: (h, 0, i, 0))
    r_spec = pl.BlockSpec((1, g, BAND_BQ, 1), lambda h, i: (h, 0, i, 0))
    in_specs = ([q_spec, pl.BlockSpec((1, dh, c), lambda h, i: (h, 0, 0)),
                 pl.BlockSpec((1, c, dh), lambda h, i: (h, 0, 0))]
                + _band_specs(t,lambda n: ((1, dh, n), 2))
                + _band_specs(t,lambda n: ((1, n, dh), 1))
                + [pl.BlockSpec(memory_space=pltpu.SMEM)])
    return pl.pallas_call(
        body, name="attn_band_fwd", grid=(kvh, t // BAND_BQ), in_specs=in_specs, out_specs=[q_spec, r_spec],
        out_shape=[jax.ShapeDtypeStruct(q.shape, F32), jax.ShapeDtypeStruct((kvh, g, t, 1), F32)],
        compiler_params=_cparams(("parallel", "parallel"), 4 * 2 * rows * (c + BAND_BQ + 2 * WINDOW)),
    )(q, kct, vc, klt, klt, klt, vl, vl, vl, sink)


def _band_dq(q, kl, klt, vlt, kc, kct, vct, sink, o, do, lse):
    kvh, g, t, dh = q.shape
    c = kc.shape[1]
    rows = g * BAND_BQ

    def body(q_ref, o_ref, do_ref, lse_ref, kc_ref, kct_ref, vct_ref, kp, kcur, kn, ktp, ktc, ktn,
             vtp, vtc, vtn, sink_ref, dq_ref, delta_ref, dsink_ref):
        h, i = pl.program_id(0), pl.program_id(1)
        qv = (q_ref[0].reshape(rows, dh) * ATT_SCALE).astype(BF16)
        dov = do_ref[0].reshape(rows, dh)
        lse_v = lse_ref[0].reshape(rows, 1)
        kt = jnp.concatenate([kct_ref[0], ktp[0], ktc[0], ktn[0]], axis=1).astype(BF16)
        vt = jnp.concatenate([vct_ref[0], vtp[0], vtc[0], vtn[0]], axis=1).astype(BF16)
        k = jnp.concatenate([kc_ref[0], kp[0], kcur[0], kn[0]], axis=0).astype(BF16)
        s = jnp.where(_band_visible(i, t, c, rows), jnp.dot(qv, kt, preferred_element_type=F32), NEG_BIG)
        p = jnp.exp(s - lse_v)
        delta = jnp.sum(o_ref[0].reshape(rows, dh) * dov, axis=-1, keepdims=True)
        dp = jnp.dot(dov.astype(BF16), vt, preferred_element_type=F32)
        ds = (p * (dp - delta)).astype(BF16)
        dq_ref[0] = (jnp.dot(ds, k, preferred_element_type=F32) * ATT_SCALE).reshape(g, BAND_BQ, dh)
        delta_ref[0] = delta.reshape(g, BAND_BQ, 1)
        p_sink = jnp.exp(_sink_rows(sink_ref, h, BAND_BQ) - lse_v)
        dsink_ref[0] = (-p_sink * delta).reshape(g, BAND_BQ, 1)

    q_spec = pl.BlockSpec((1, g, BAND_BQ, dh), lambda h, i: (h, 0, i, 0))
    r_spec = pl.BlockSpec((1, g, BAND_BQ, 1), lambda h, i: (h, 0, i, 0))
    ct_spec = pl.BlockSpec((1, dh, c), lambda h, i: (h, 0, 0))
    rows_of = lambda n: ((1, n, dh), 1)
    lanes_of = lambda n: ((1, dh, n), 2)
    in_specs = ([q_spec, q_spec, q_spec, r_spec, pl.BlockSpec((1, c, dh), lambda h, i: (h, 0, 0)), ct_spec, ct_spec]
                + _band_specs(t,rows_of) + _band_specs(t,lanes_of) + _band_specs(t,lanes_of)
                + [pl.BlockSpec(memory_space=pltpu.SMEM)])
    row_shape = jax.ShapeDtypeStruct((kvh, g, t, 1), F32)
    return pl.pallas_call(
        body, name="attn_band_dq", grid=(kvh, t // BAND_BQ), in_specs=in_specs,
        out_specs=[q_spec, r_spec, r_spec], out_shape=[jax.ShapeDtypeStruct(q.shape, F32), row_shape, row_shape],
        compiler_params=_cparams(("parallel", "parallel"), 4 * 3 * rows * (c + BAND_BQ + 2 * WINDOW)),
    )(q, o, do, lse, kc, kct, vct, kl, kl, kl, klt, klt, klt, vlt, vlt, vlt, sink)


def _band_dkv(q, do, lse, delta, klt, vlt):
    kvh, g, t, dh = q.shape
    span = BAND_BQ + 2 * WINDOW
    rows = g * span
    tn = (((0,), (0,)), ((), ()))

    def body(qp, qc, qn, dop, doc, don, lp, lc, ln, dp_, dc_, dn_, kt_ref, vt_ref, dk_ref, dv_ref):
        j = pl.program_id(1)

        def stack(a, b, c_):
            return jnp.concatenate([jnp.concatenate([a[0, gi], b[0, gi], c_[0, gi]], axis=0)
                                    for gi in range(g)], axis=0)

        q_all, do_all = stack(qp, qc, qn), stack(dop, doc, don).astype(BF16)
        lse_all, delta_all = stack(lp, lc, ln), stack(dp_, dc_, dn_)
        s = jnp.dot((q_all * ATT_SCALE).astype(BF16), kt_ref[0].astype(BF16), preferred_element_type=F32)
        qpos = j * BAND_BQ - WINDOW + lax.broadcasted_iota(jnp.int32, (rows, BAND_BQ), 0) % span
        kpos = j * BAND_BQ + lax.broadcasted_iota(jnp.int32, (rows, BAND_BQ), 1)
        s = jnp.where((qpos >= 0) & (qpos < t) & (jnp.abs(kpos - qpos) <= WINDOW), s, NEG_BIG)
        p = jnp.exp(s - lse_all)
        dp = jnp.dot(do_all, vt_ref[0].astype(BF16), preferred_element_type=F32)
        ds = (p * (dp - delta_all)).astype(BF16)
        dv_ref[0] = lax.dot_general(p.astype(BF16), do_all, tn, preferred_element_type=F32)
        dk_ref[0] = lax.dot_general(ds, q_all.astype(BF16), tn, preferred_element_type=F32) * ATT_SCALE

    q_specs = _band_specs(t,lambda n: ((1, g, n, dh), 2))
    r_specs = _band_specs(t,lambda n: ((1, g, n, 1), 2))
    kt_spec = pl.BlockSpec((1, dh, BAND_BQ), lambda h, j: (h, 0, j))
    k_spec = pl.BlockSpec((1, BAND_BQ, dh), lambda h, j: (h, j, 0))
    kv_shape = jax.ShapeDtypeStruct((kvh, t, dh), F32)
    return pl.pallas_call(
        body, name="attn_band_dkv", grid=(kvh, t // BAND_BQ),
        in_specs=q_specs + q_specs + r_specs + r_specs + [kt_spec, kt_spec], out_specs=[k_spec, k_spec],
        out_shape=[kv_shape, kv_shape],
        compiler_params=_cparams(("parallel", "parallel"), 4 * 3 * rows * BAND_BQ),
    )(q, q, q, do, do, do, lse, lse, lse, delta, delta, delta, klt, vlt)


def make_attention(kind, has_sink):
    has_lat = kind != "ctx"

    def unpack(args):
        it = iter(args)
        q, kc, vc = next(it), next(it), next(it)
        kl, vl = (next(it), next(it)) if has_lat else (None, None)
        sink = next(it) if has_sink else None
        return q, kl, vl, kc, vc, sink

    def forward(args):
        q, kl, vl, kc, vc, sink = unpack(args)
        klt = jnp.swapaxes(kl, 1, 2) if has_lat else None
        if kind == "window":
            return _band_fwd(q, klt, vl, jnp.swapaxes(kc, 1, 2), vc, sink)
        return _flash_fwd(q, klt, vl, jnp.swapaxes(kc, 1, 2), vc, sink)

    @jax.custom_vjp
    def op(*args):
        return forward(args)[0]

    def fwd(*args):
        o, lse = forward(args)
        return o, (args, o, lse)

    def bwd(res, do):
        args, o, lse = res
        q, kl, vl, kc, vc, sink = unpack(args)
        if kind == "global":
            dq, delta, dkl, dvl = _flash_bwd_full(q, kl, vl, kc, vc, o, do, lse)
            grads = [dq, *_flash_dkv(q, do, lse, delta, kc, vc), dkl, dvl]
        elif kind == "window":
            klt, vlt = jnp.swapaxes(kl, 1, 2), jnp.swapaxes(vl, 1, 2)
            dq, delta, dsink_rows = _band_dq(q, kl, klt, vlt, kc, jnp.swapaxes(kc, 1, 2),
                                             jnp.swapaxes(vc, 1, 2), sink, o, do, lse)
            grads = [dq, *_flash_dkv(q, do, lse, delta, kc, vc), *_band_dkv(q, do, lse, delta, klt, vlt)]
        else:
            dq, delta, dsink_rows = _ctx_dq(q, kc, vc, sink, o, do, lse)
            grads = [dq, *_flash_dkv(q, do, lse, delta, kc, vc)]
        if has_sink:
            grads.append(jnp.sum(dsink_rows, axis=(2, 3)).reshape(-1))
        return tuple(grads)

    op.defvjp(fwd, bwd)
    return op


attn_window = make_attention("window", True)
attn_global = make_attention("global", False)
attn_ctx_sink = make_attention("ctx", True)
attn_ctx = make_attention("ctx", False)


def _to_heads(z, n_heads):
    return z.reshape(z.shape[0], n_heads, HEAD_DIM).transpose(1, 0, 2)


def _q_heads(z):
    return z.reshape(z.shape[0], KV_HEADS, Q_PER_KV, HEAD_DIM).transpose(1, 2, 0, 3)


def _from_q_heads(o):
    return o.transpose(2, 0, 1, 3).reshape(o.shape[2], KV_HEADS * Q_PER_KV * HEAD_DIM)


SCAN_PAIRS = 4


def _scan_tile(t):
    return _tile(t, 512, 2 * SCAN_PAIRS)


def _scan_fwd_call(bre, bim, lre, lim, h0re, h0im, rev):
    t = bre.shape[0]
    tt = _scan_tile(t)
    nb = t // tt
    plane = bre.shape[1:]

    def body(bre_ref, bim_ref, lre_ref, lim_ref, h0re_ref, h0im_ref, sre_ref, sim_ref, h_ref):
        @pl.when(pl.program_id(0) == 0)
        def _():
            h_ref[0] = h0re_ref[...]
            h_ref[1] = h0im_ref[...]

        ar, ai = lre_ref[...], lim_ref[...]
        a2r, a2i = ar * ar - ai * ai, 2.0 * ar * ai

        def step(j, carry):
            hr, hi = carry
            for u in range(SCAN_PAIRS):
                t1 = (tt - 1 - 2 * (SCAN_PAIRS * j + u)) if rev else 2 * (SCAN_PAIRS * j + u)
                t2 = (t1 - 1) if rev else (t1 + 1)
                b1r, b1i, b2r, b2i = bre_ref[t1], bim_ref[t1], bre_ref[t2], bim_ref[t2]
                er = ar * b1r - ai * b1i + b2r
                ei = ar * b1i + ai * b1r + b2i
                sre_ref[t1] = ar * hr - ai * hi + b1r
                sim_ref[t1] = ar * hi + ai * hr + b1i
                hr, hi = a2r * hr - a2i * hi + er, a2r * hi + a2i * hr + ei
                sre_ref[t2] = hr
                sim_ref[t2] = hi
            return hr, hi

        hr, hi = lax.fori_loop(0, tt // (2 * SCAN_PAIRS), step, (h_ref[0], h_ref[1]))
        h_ref[0] = hr
        h_ref[1] = hi

    blk = pl.BlockSpec((tt,) + plane, (lambda i: (nb - 1 - i, 0, 0)) if rev else (lambda i: (i, 0, 0)))
    par = pl.BlockSpec(plane, lambda i: (0, 0))
    return pl.pallas_call(
        body, name="s5_scan_fwd", grid=(nb,), in_specs=[blk, blk, par, par, par, par], out_specs=[blk, blk],
        out_shape=[jax.ShapeDtypeStruct(bre.shape, F32)] * 2,
        scratch_shapes=[pltpu.VMEM((2,) + plane, F32)],
        compiler_params=_cparams(("arbitrary",), 4 * 4 * tt * plane[0] * plane[1]),
    )(bre, bim, lre, lim, h0re, h0im)


def _scan_bwd_call(gre, gim, sre, sim, lre, lim, h0re, h0im, rev):
    t = gre.shape[0]
    tt = _scan_tile(t)
    nb = t // tt
    plane = gre.shape[1:]
    down = not rev

    def body(gre_ref, gim_ref, sre_ref, sim_ref, lre_ref, lim_ref, h0re_ref, h0im_ref,
             dbre_ref, dbim_ref, dare_ref, daim_ref, dh0re_ref, dh0im_ref, carry_ref):
        i = pl.program_id(0)

        @pl.when(i == 0)
        def _():
            carry_ref[...] = jnp.zeros_like(carry_ref)

        ar, ai = lre_ref[...], lim_ref[...]
        a2r, a2i = ar * ar - ai * ai, 2.0 * ar * ai

        def step(j, carry):
            gr, gi, dar, dai = carry
            for u in range(SCAN_PAIRS):
                t1 = (tt - 1 - 2 * (SCAN_PAIRS * j + u)) if down else 2 * (SCAN_PAIRS * j + u)
                t2 = (t1 - 1) if down else (t1 + 1)
                h1r, h1i, h2r, h2i = sre_ref[t1], sim_ref[t1], sre_ref[t2], sim_ref[t2]
                c1r, c1i, c2r, c2i = gre_ref[t1], gim_ref[t1], gre_ref[t2], gim_ref[t2]
                g1r = c1r + ar * gr + ai * gi
                g1i = c1i + ar * gi - ai * gr
                er = c2r + ar * c1r + ai * c1i
                ei = c2i + ar * c1i - ai * c1r
                dar = dar + ((h1r * gr + h1i * gi) + (h2r * g1r + h2i * g1i))
                dai = dai + ((h1r * gi - h1i * gr) + (h2r * g1i - h2i * g1r))
                gr, gi = er + a2r * gr + a2i * gi, ei + a2r * gi - a2i * gr
                dbre_ref[t1] = g1r
                dbim_ref[t1] = g1i
                dbre_ref[t2] = gr
                dbim_ref[t2] = gi
            return gr, gi, dar, dai

        gr, gi, dar, dai = lax.fori_loop(
            0, tt // (2 * SCAN_PAIRS), step, (carry_ref[0], carry_ref[1], carry_ref[2], carry_ref[3]))
        carry_ref[0] = gr
        carry_ref[1] = gi
        carry_ref[2] = dar
        carry_ref[3] = dai

        @pl.when(i == nb - 1)
        def _():
            hr, hi = h0re_ref[...], h0im_ref[...]
            dare_ref[...] = dar + hr * gr + hi * gi
            daim_ref[...] = dai + hr * gi - hi * gr
            dh0re_ref[...] = ar * gr + ai * gi
            dh0im_ref[...] = ar * gi - ai * gr

    blk = pl.BlockSpec((tt,) + plane, (lambda i: (nb - 1 - i, 0, 0)) if down else (lambda i: (i, 0, 0)))
    par = pl.BlockSpec(plane, lambda i: (0, 0))
    return pl.pallas_call(
        body, name="s5_scan_bwd", grid=(nb,), in_specs=[blk, blk, blk, blk, par, par, par, par],
        out_specs=[blk, blk, par, par, par, par],
        out_shape=[jax.ShapeDtypeStruct(gre.shape, F32)] * 2 + [jax.ShapeDtypeStruct(plane, F32)] * 4,
        scratch_shapes=[pltpu.VMEM((4,) + plane, F32)],
        compiler_params=_cparams(("arbitrary",), 4 * 6 * tt * plane[0] * plane[1]),
    )(gre, gim, sre, sim, lre, lim, h0re, h0im)


def make_scan(rev):
    @jax.custom_vjp
    def op(bre, bim, lre, lim, h0re, h0im):
        return tuple(_scan_fwd_call(bre, bim, lre, lim, h0re, h0im, rev))

    def fwd(bre, bim, lre, lim, h0re, h0im):
        sre, sim = _scan_fwd_call(bre, bim, lre, lim, h0re, h0im, rev)
        return (sre, sim), (sre, sim, lre, lim, h0re, h0im)

    def bwd(res, cts):
        sre, sim, lre, lim, h0re, h0im = res
        return tuple(_scan_bwd_call(cts[0], cts[1], sre, sim, lre, lim, h0re, h0im, rev))

    op.defvjp(fwd, bwd)
    return op


scan_up = make_scan(False)
scan_down = make_scan(True)


def _adamw_call(parts, w, m, v, name):
    r, c = w.shape
    tr = _tile(r, 256, SUBLANES)
    nparts = parts.shape[0]
    c1 = 1.0 - ADAM_B1 ** ADAM_STEP
    c2 = 1.0 - ADAM_B2 ** ADAM_STEP

    def body(p_ref, w_ref, m_ref, v_ref, g_ref, d_ref, nm_ref, nv_ref):
        g = p_ref[0].astype(F32)
        for s in range(1, nparts):
            g = g + p_ref[s].astype(F32)
        m1 = ADAM_B1 * m_ref[...] + (1.0 - ADAM_B1) * g
        v1 = ADAM_B2 * v_ref[...] + (1.0 - ADAM_B2) * jnp.square(g)
        g_ref[...] = g
        nm_ref[...] = m1
        nv_ref[...] = v1
        d_ref[...] = -ADAM_LR * ((m1 / c1) / (jnp.sqrt(v1 / c2) + ADAM_EPS) + ADAM_WD * w_ref[...])

    blk = pl.BlockSpec((tr, c), lambda i: (i, 0))
    return pl.pallas_call(
        body, name=name, grid=(r // tr,),
        in_specs=[pl.BlockSpec((nparts, tr, c), lambda i: (0, i, 0)), blk, blk, blk], out_specs=[blk] * 4,
        out_shape=[jax.ShapeDtypeStruct((r, c), F32)] * 4,
        compiler_params=_cparams(("parallel",), 4 * tr * c * (nparts + 7)),
    )(parts, w, m, v)


def _peer(k):
    x, y, c = lax.axis_index("x"), lax.axis_index("y"), lax.axis_index("c")
    px = 1 - x if k & 4 else x
    py = 1 - y if k & 2 else y
    pc = 1 - c if k & 1 else c
    return (px, py, pc), 4 * px + 2 * py + pc


def _my_slot():
    return 4 * lax.axis_index("x") + 2 * lax.axis_index("y") + lax.axis_index("c")


def all_gather(x, name):
    def body(x_ref, out_ref, send_sems, recv_sems, local_sem):
        me = _my_slot()
        mine = pltpu.make_async_copy(x_ref, out_ref.at[me], local_sem)
        mine.start()
        sends = []
        for k in range(1, N_DEV):
            peer, _ = _peer(k)
            cp = pltpu.make_async_remote_copy(
                src_ref=x_ref, dst_ref=out_ref.at[me], send_sem=send_sems.at[k - 1],
                recv_sem=recv_sems.at[k - 1], device_id=peer, device_id_type=MESH)
            cp.start()
            sends.append(cp)
        for k in range(1, N_DEV):
            peer, slot = _peer(k)
            pltpu.make_async_remote_copy(
                src_ref=x_ref, dst_ref=out_ref.at[slot], send_sem=send_sems.at[k - 1],
                recv_sem=recv_sems.at[k - 1], device_id=peer, device_id_type=MESH).wait_recv()
        for cp in sends:
            cp.wait_send()
        mine.wait()

    return pl.pallas_call(
        body, name=name,
        in_specs=[pl.BlockSpec(memory_space=pl.ANY)], out_specs=pl.BlockSpec(memory_space=pl.ANY),
        out_shape=jax.ShapeDtypeStruct((N_DEV,) + tuple(x.shape), x.dtype),
        scratch_shapes=[pltpu.SemaphoreType.DMA((N_DEV - 1,)), pltpu.SemaphoreType.DMA((N_DEV - 1,)),
                        pltpu.SemaphoreType.DMA],
    )(x)


def sibling_swap(x, name):
    def body(x_ref, out_ref, send_sem, recv_sem):
        x_, y_, c_ = lax.axis_index("x"), lax.axis_index("y"), lax.axis_index("c")
        cp = pltpu.make_async_remote_copy(
            src_ref=x_ref.at[1 - c_], dst_ref=out_ref, send_sem=send_sem, recv_sem=recv_sem,
            device_id=(x_, y_, 1 - c_), device_id_type=MESH)
        cp.start()
        cp.wait()

    return pl.pallas_call(
        body, name=name,
        in_specs=[pl.BlockSpec(memory_space=pl.ANY)], out_specs=pl.BlockSpec(memory_space=pl.ANY),
        out_shape=jax.ShapeDtypeStruct(x.shape[1:], x.dtype),
        scratch_shapes=[pltpu.SemaphoreType.DMA, pltpu.SemaphoreType.DMA],
    )(x)


def chip_exchange(x, name):
    def body(x_ref, out_ref, send_sems, recv_sems, local_sem):
        x_, y_, c_ = lax.axis_index("x"), lax.axis_index("y"), lax.axis_index("c")
        mine = 2 * x_ + y_
        local = pltpu.make_async_copy(x_ref.at[mine], out_ref.at[mine], local_sem)
        local.start()

        def copy(k):
            px = 1 - x_ if k & 2 else x_
            py = 1 - y_ if k & 1 else y_
            peer = 2 * px + py
            send = pltpu.make_async_remote_copy(
                src_ref=x_ref.at[peer], dst_ref=out_ref.at[mine], send_sem=send_sems.at[k - 1],
                recv_sem=recv_sems.at[k - 1], device_id=(px, py, c_), device_id_type=MESH)
            recv = pltpu.make_async_remote_copy(
                src_ref=x_ref.at[peer], dst_ref=out_ref.at[peer], send_sem=send_sems.at[k - 1],
                recv_sem=recv_sems.at[k - 1], device_id=(px, py, c_), device_id_type=MESH)
            return send, recv

        copies = [copy(k) for k in range(1, 4)]
        for send, _ in copies:
            send.start()
        for _, recv in copies:
            recv.wait_recv()
        for send, _ in copies:
            send.wait_send()
        local.wait()

    return pl.pallas_call(
        body, name=name,
        in_specs=[pl.BlockSpec(memory_space=pl.ANY)], out_specs=pl.BlockSpec(memory_space=pl.ANY),
        out_shape=jax.ShapeDtypeStruct(x.shape, x.dtype),
        scratch_shapes=[pltpu.SemaphoreType.DMA((3,)), pltpu.SemaphoreType.DMA((3,)),
                        pltpu.SemaphoreType.DMA],
    )(x)


def all_gather_two_level(x, name):
    def body(x_ref, out_ref, send_sems, recv_sems, local_sem):
        x_, y_, c_ = lax.axis_index("x"), lax.axis_index("y"), lax.axis_index("c")
        me, sibling = (x_, y_, c_), (x_, y_, 1 - c_)
        chips = [(1 - x_, y_), (x_, 1 - y_), (1 - x_, 1 - y_)]

        def slot(px, py, pc):
            return out_ref.at[4 * px + 2 * py + pc]

        def copy(k, block, to, src=None):
            return pltpu.make_async_remote_copy(
                src_ref=slot(*block) if src is None else src, dst_ref=slot(*block),
                send_sem=send_sems.at[k], recv_sem=recv_sems.at[k], device_id=to, device_id_type=MESH)

        mine = pltpu.make_async_copy(x_ref, slot(*me), local_sem)
        mine.start()
        first = [copy(0, me, sibling, src=x_ref)]
        first += [copy(1 + j, me, (*chip, c_), src=x_ref) for j, chip in enumerate(chips)]
        for cp in first:
            cp.start()
        passed = [copy(4 + j, (*chip, c_), sibling) for j, chip in enumerate(chips)]
        for j, chip in enumerate(chips):
            copy(1 + j, (*chip, c_), me).wait_recv()
            passed[j].start()
        copy(0, sibling, me).wait_recv()
        for j, chip in enumerate(chips):
            copy(4 + j, (*chip, 1 - c_), me).wait_recv()
        for cp in first + passed:
            cp.wait_send()
        mine.wait()

    return pl.pallas_call(
        body, name=name,
        in_specs=[pl.BlockSpec(memory_space=pl.ANY)], out_specs=pl.BlockSpec(memory_space=pl.ANY),
        out_shape=jax.ShapeDtypeStruct((N_DEV,) + tuple(x.shape), x.dtype),
        scratch_shapes=[pltpu.SemaphoreType.DMA((N_DEV - 1,)), pltpu.SemaphoreType.DMA((N_DEV - 1,)),
                        pltpu.SemaphoreType.DMA],
    )(x)


def _rope_tables(t):
    n_freq = HEAD_DIM // 4
    tok = jnp.arange(t)
    inv = ROPE_THETA ** (-jnp.arange(n_freq, dtype=F32) / n_freq)
    a_row = (tok // GRID_W).astype(F32)[:, None] * inv
    a_col = (tok % GRID_W).astype(F32)[:, None] * inv
    cos = jnp.concatenate([jnp.cos(a_row)] * 2 + [jnp.cos(a_col)] * 2, axis=1)
    sin = jnp.concatenate([-jnp.sin(a_row), jnp.sin(a_row), -jnp.sin(a_col), jnp.sin(a_col)], axis=1)
    return jnp.concatenate([cos, cos], axis=1), jnp.concatenate([sin, sin], axis=1)


def _block_diag(blocks):
    g, a, b = blocks.shape
    eye = jnp.eye(g, dtype=blocks.dtype)
    return jnp.einsum("gab,gk->gakb", blocks, eye).reshape(g * a, g * b)


def _ffn_fwd_calls(x, mod, g, w_in, w_out):
    shift, scale, gate = mod[0:1], mod[1:2], mod[2:3]
    h, = _row_call(lambda xt, gt, sh, sc: (_norm_mod_fn(xt, gt, sh, sc)[0].astype(BF16),),
                   [x], [g, shift, scale], 0, "ffn_norm")
    u = _mm(h, w_in, out_dtype=BF16, name="ffn_up")
    a, = _row_call(lambda ut: (_swiglu_fn(ut.astype(F32))[0].astype(BF16),), [u], [], 0, "ffn_act")
    y = _mm(a, w_out, name="ffn_down")
    out, = _row_call(functools.partial(_resid_fn, 0.5), [x, y], [gate], 0, "ffn_resid")
    return out, (h, u, a, y)


@jax.custom_vjp
def _ffn_half(x, mod, g, w_in, w_out):
    return _ffn_fwd_calls(x, mod, g, w_in, w_out)[0]


def _ffn_half_fwd(x, mod, g, w_in, w_out):
    out, saved = _ffn_fwd_calls(x, mod, g, w_in, w_out)
    return out, (x, mod, g, w_in, w_out, saved)


def _ffn_half_bwd(res, dxn):
    x, mod, g, w_in, w_out, (h, u, a, y) = res
    shift, scale, gate = mod[0:1], mod[1:2], mod[2:3]

    def resid_bwd(dt, yt, gt):
        return (0.5 * gt * dt).astype(BF16), 0.5 * jnp.sum(dt * yt, axis=0, keepdims=True)

    dy, dgate = _row_call(resid_bwd, [dxn, y], [gate], 1, "ffn_resid_bwd")
    da = _mm(dy, w_out, tb=True, out_dtype=BF16, name="ffn_down_dx")
    dw_out = _mm(a, dy, ta=True, out_dtype=w_out.dtype, name="ffn_down_dw")

    def act_bwd(ut, dat):
        return (jax.vjp(_swiglu_fn, ut.astype(F32))[1]((dat.astype(F32),))[0].astype(BF16),)

    du, = _row_call(act_bwd, [u, da], [], 0, "ffn_act_bwd")
    dh = _mm(du, w_in, tb=True, name="ffn_up_dx")
    dw_in = _mm(h, du, ta=True, out_dtype=w_in.dtype, name="ffn_up_dw")

    def norm_bwd(xt, dht, dt, gt, sh, sc):
        dx, dg, dsh, dsc = jax.vjp(_norm_mod_fn, xt, gt, sh, sc)[1]((dht,))
        return dx + dt, dg, dsh, dsc

    dx, dg, dshift, dscale = _row_call(norm_bwd, [x, dh, dxn], [g, shift, scale], 3, "ffn_norm_bwd")
    return dx, jnp.concatenate([dshift, dscale, dgate], axis=0), dg, dw_in, dw_out


_ffn_half.defvjp(_ffn_half_fwd, _ffn_half_bwd)


def _s5_discretize(a_re, a_im, log_dt, b_re, b_im):
    lam = lax.complex(a_re, a_im)
    dt = jnp.exp(log_dt)[:, None]
    lam_bar = jnp.exp(lam * dt)
    b_bar = ((lam_bar - 1.0) / lam)[..., None] * lax.complex(b_re, b_im)
    return lam_bar, b_bar


def _s5_branch(u_lat, u_ctx, w, l, with_ctx_out):
    zero = jnp.zeros((SUBLANES, SSM_LANES // SUBLANES), F32)
    lat_terms, ctx_terms = [], []
    for d, scan in enumerate((scan_up, scan_down)):
        lam_bar, b_bar = _s5_discretize(w["ssm_a_re"][l, d], w["ssm_a_im"][l, d], w["ssm_log_dt"][l, d],
                                        w["ssm_b_re"][l, d], w["ssm_b_im"][l, d])
        lre = jnp.real(lam_bar).reshape(zero.shape)
        lim = jnp.imag(lam_bar).reshape(zero.shape)
        b_t = jnp.swapaxes(b_bar, 1, 2)
        b_re, b_im = _block_diag(jnp.real(b_t)), _block_diag(jnp.imag(b_t))
        c_re = _block_diag(jnp.swapaxes(w["ssm_c_re"][l, d], 1, 2))
        c_im = _block_diag(jnp.swapaxes(w["ssm_c_im"][l, d], 1, 2))
        sc_re, sc_im = scan(linear_to_planes(u_ctx, b_re), linear_to_planes(u_ctx, b_im), lre, lim, zero, zero)
        last = 0 if d == 1 else u_ctx.shape[0] - 1
        sl_re, sl_im = scan(linear_to_planes(u_lat, b_re), linear_to_planes(u_lat, b_im), lre, lim,
                            sc_re[last], sc_im[last])
        lat_terms += [linear_from_planes(sl_re, c_re), linear_from_planes(sl_im, c_im)]
        if with_ctx_out:
            ctx_terms += [linear_from_planes(sc_re, c_re), linear_from_planes(sc_im, c_im)]
    d_skip = w["ssm_d"][l][None, :]

    def out(terms, u):
        y, = s5_pre(*terms, u, d_skip)
        return glu(linear(y, w["glu_w"][l]))[0]

    return out(lat_terms, u_lat), (out(ctx_terms, u_ctx) if with_ctx_out else None)


def _pool_branch(xa, w, l):
    y = linear(pool_diff(xa), _block_diag(w["pool_w"][l]))
    return scale_rows(y, w["pool_scale"][l][None, :])[0]


_CTX_GROUPS = (O_VB - O_KB, O_UC - O_VB, O_KD - O_UC, O_VD - O_KD, CTX_COLS - O_VD)
_ALL_GROUPS = _CTX_GROUPS + (O_QD - O_QB, O_XA - O_QD, O_GATE - O_XA)
project_ctx = make_split_linear(_CTX_GROUPS)


def make_gated_projection(widths):
    @jax.custom_vjp
    def op(x, w, w_gate):
        return (*_split_fwd(x, w, widths), _mm(x, w_gate, out_dtype=BF16, name="gate_fwd"))

    def fwd(x, w, w_gate):
        return op(x, w, w_gate), (x, w, w_gate)

    def bwd(res, cts):
        x, w, w_gate = res
        *d_groups, d_gate = cts
        dx = _mm(d_gate, w_gate, tb=True, add=_split_dx(d_groups, w), name="gate_dx")
        return (dx, _split_dw(x, d_groups, w.dtype),
                _mm(x, d_gate, ta=True, out_dtype=w_gate.dtype, name="gate_dw"))

    op.defvjp(fwd, bwd)
    return op


project_all = make_gated_projection(_ALL_GROUPS)


def _merge_branches(branches, gate_logits, w, l):
    zs = [linear_b(y, w["branch_w"][l, k]) for k, y in enumerate(branches)]
    return linear(merge(gate_logits, *zs)[0], w["out_w"][l])


def _token_mixer(h, hc, cos, sin, w, l, with_ctx_out):
    w_in, w_gate = w["w_in"][l][:, :O_GATE], w["w_in"][l][:, O_GATE:]
    kb, vb, uc, kd, vd, qb, qd, xa, p_gate = project_all(h, w_in, w_gate)
    if with_ctx_out:
        kb_c, vb_c, uc_c, kd_c, vd_c, qb_c, qd_c, xa_c, pc_gate = project_all(hc, w_in, w_gate)
    else:
        kb_c, vb_c, uc_c, kd_c, vd_c = project_ctx(hc, w_in[:, :CTX_COLS])
    sink = w["win_sink"][l]
    q_g = jnp.tile(w["qk_norm"][l, 0], KV_HEADS * Q_PER_KV)[None, :]
    k_g = jnp.tile(w["qk_norm"][l, 1], KV_HEADS)[None, :]
    k_win_c = _to_heads(kb_c, KV_HEADS)
    v_win_c = _to_heads(vb_c, KV_HEADS)
    k_glb_c = _to_heads(head_norm(kd_c, k_g)[0], KV_HEADS)
    v_glb_c = _to_heads(vd_c, KV_HEADS)
    y_a = _pool_branch(xa, w, l)
    q_win = _q_heads(rope(qb, cos, sin)[0])
    k_win = _to_heads(rope(kb, cos, sin)[0], KV_HEADS)
    v_win = _to_heads(vb, KV_HEADS)
    y_b = _from_q_heads(attn_window(q_win, k_win_c, v_win_c, k_win, v_win, sink))
    y_c, y_c_ctx = _s5_branch(uc, uc_c, w, l, with_ctx_out)
    q_glb = _q_heads(norm_rope(qd, cos, sin, q_g)[0])
    k_glb = _to_heads(norm_rope(kd, cos, sin, k_g)[0], KV_HEADS)
    v_glb = _to_heads(vd, KV_HEADS)
    y_d = _from_q_heads(attn_global(q_glb, k_glb_c, v_glb_c, k_glb, v_glb))
    y = _merge_branches((y_a, y_b, y_c, y_d), p_gate, w, l)
    if not with_ctx_out:
        return y, None
    y_a_c = _pool_branch(xa_c, w, l)
    y_b_c = _from_q_heads(attn_ctx_sink(_q_heads(qb_c), k_win_c, v_win_c, sink))
    q_glb_c = _q_heads(head_norm(qd_c, q_g)[0])
    y_d_c = _from_q_heads(attn_ctx(q_glb_c, k_glb_c, v_glb_c))
    return y, _merge_branches((y_a_c, y_b_c, y_c_ctx, y_d_c), pc_gate, w, l)


def local_loss(w, x, c, ctx, target):
    depth = w["w_mod"].shape[0]
    cos, sin = _rope_tables(x.shape[0])
    cond = jnp.concatenate([c, w["c_ctx"][None, :], jnp.zeros((COND_ROWS - 2, D_MODEL), F32)], axis=0)
    s_all, = silu_rows(cond)
    for l in range(depth):
        last = l == depth - 1
        m_all = (linear(s_all, w["w_mod"][l]) + w["b_mod"][l][None, :]).reshape(COND_ROWS, N_SUB, 3, D_MODEL)
        m, mc = m_all[0], m_all[1]
        g = w["norm_g"][l][:, None, :]
        x = _ffn_half(x, m[0], g[0], w["ffn_in"][l, 0], w["ffn_out"][l, 0])
        ctx = _ffn_half(ctx, mc[0], g[0], w["ffn_in"][l, 0], w["ffn_out"][l, 0])
        h, = norm_mod(x, g[1], m[1, 0:1], m[1, 1:2])
        hc, = norm_mod(ctx, g[1], mc[1, 0:1], mc[1, 1:2])
        y, y_ctx = _token_mixer(h, hc, cos, sin, w, l, not last)
        x, = resid_full(x, y, m[1, 2:3])
        if not last:
            ctx, = resid_full(ctx, y_ctx, mc[1, 2:3])
        x = _ffn_half(x, m[2], g[2], w["ffn_in"][l, 1], w["ffn_out"][l, 1])
        if not last:
            ctx = _ffn_half(ctx, mc[2], g[2], w["ffn_in"][l, 1], w["ffn_out"][l, 1])
    return jnp.sum(loss_rows(x, target, w["final_g"][None, :])[0])


PACK_COLS = 1024


def _pack(arrays):
    flat = jnp.concatenate([a.reshape(-1) for a in arrays])
    pad = (-flat.shape[0]) % (PACK_COLS * 16)
    return jnp.pad(flat, (0, pad)).reshape(-1, PACK_COLS)


def _unpack(slab, shapes):
    out, off = [], 0
    for s in shapes:
        n = math.prod(s)
        r0, r1 = off // PACK_COLS, -(-(off + n) // PACK_COLS)
        rows, start = slab[r0:r1], off - r0 * PACK_COLS
        if start == 0 and n == (r1 - r0) * PACK_COLS:
            out.append(rows.reshape(s))
        else:
            out.append(rows.reshape(-1)[start:start + n].reshape(s))
        off += n
    return out


def _full_from_shards(gathered, shard_shape, axis):
    z = jnp.moveaxis(gathered.reshape((N_DEV,) + tuple(shard_shape)), 0, axis)
    shape = list(shard_shape)
    shape[axis] *= N_DEV
    return z.reshape(shape)


def _shards_from_full(full, axis):
    shape = list(full.shape)
    shape[axis:axis + 1] = [N_DEV, shape[axis] // N_DEV]
    return jnp.moveaxis(full.reshape(shape), axis, 0)


def kernel(x, c, ctx, c_ctx, w_mod, b_mod, norm_g, ffn_in, ffn_out, w_in, win_sink, qk_norm, pool_w, pool_scale, ssm_a_re, ssm_a_im, ssm_log_dt, ssm_b_re, ssm_b_im, ssm_c_re, ssm_c_im, ssm_d, glu_w, branch_w, out_w, final_g, loss_target, m_c_ctx, m_w_mod, m_b_mod, m_norm_g, m_ffn_in, m_ffn_out, m_w_in, m_win_sink, m_qk_norm, m_pool_w, m_pool_scale, m_ssm_a_re, m_ssm_a_im, m_ssm_log_dt, m_ssm_b_re, m_ssm_b_im, m_ssm_c_re, m_ssm_c_im, m_ssm_d, m_glu_w, m_branch_w, m_out_w, m_final_g, v_c_ctx, v_w_mod, v_b_mod, v_norm_g, v_ffn_in, v_ffn_out, v_w_in, v_win_sink, v_qk_norm, v_pool_w, v_pool_scale, v_ssm_a_re, v_ssm_a_im, v_ssm_log_dt, v_ssm_b_re, v_ssm_b_im, v_ssm_c_re, v_ssm_c_im, v_ssm_d, v_glu_w, v_branch_w, v_out_w, v_final_g):
    given = dict(locals())
    wts = {n: given[n] for n in WEIGHTS}
    mom = {n: given["m_" + n] for n in WEIGHTS}
    var = {n: given["v_" + n] for n in WEIGHTS}
    me = _my_slot()

    shard_shapes = [wts[n].shape for n in SHARDED]
    w_slab = _pack([wts[n] for n in SHARDED])
    gathered = all_gather_two_level(w_slab.astype(BF16), "gather_weights")
    full = dict(wts)
    row = 0
    for n in SHARDED:
        n_rows = math.prod(wts[n].shape) // PACK_COLS
        full[n] = _full_from_shards(gathered[:, row:row + n_rows], wts[n].shape, SHARD_AXIS[n])
        row += n_rows
    g_slab = _pack([norm_g])
    g_all = all_gather(g_slab, "gather_norm_g")
    full["norm_g"] = _full_from_shards(
        jnp.stack([_unpack(g_all[s], [norm_g.shape])[0] for s in range(N_DEV)]), norm_g.shape, 2)

    loss, (gw, gx) = jax.value_and_grad(local_loss, argnums=(0, 1))(full, x[0], c, ctx[0], loss_target[0])
    loss = lax.psum(loss, ("x", "y", "c"))

    dest = [_shards_from_full(gw[n], SHARD_AXIS[n]).reshape(N_DEV // 2, 2, -1, PACK_COLS) for n in SHARDED]
    send = jnp.swapaxes(jnp.concatenate(dest, axis=2), 0, 1)
    from_sibling = sibling_swap(send, "exchange_grads_sibling")
    own = lax.dynamic_index_in_dim(send, lax.axis_index("c"), axis=0, keepdims=False)
    pair, = _row_call(lambda a, b: ((a.astype(F32) + b.astype(F32)).astype(BF16),),
                      [own.reshape(-1, PACK_COLS), from_sibling.reshape(-1, PACK_COLS)], [], 0,
                      "exchange_pair_sum")
    big_parts = chip_exchange(pair.reshape(own.shape), "exchange_grads_chips")
    small_names = SMALL + ("norm_g",)
    small_shapes = [gw[n].shape for n in small_names]
    small_parts = all_gather(_pack([gw[n] for n in small_names]), "gather_small_grads")

    big = _adamw_call(big_parts, w_slab, _pack([mom[n] for n in SHARDED]), _pack([var[n] for n in SHARDED]),
                      "adamw_sharded")
    big = [_unpack(b, shard_shapes) for b in big]
    col = me * norm_g.shape[2]

    def small_slab(src, shard_src):
        padded = jnp.zeros((norm_g.shape[0], norm_g.shape[1], norm_g.shape[2] * N_DEV), F32)
        padded = lax.dynamic_update_slice(padded, shard_src, (0, 0, col))
        return _pack([src[n] for n in SMALL] + [padded])

    small = _adamw_call(small_parts, small_slab(wts, norm_g), small_slab(mom, m_norm_g),
                        small_slab(var, v_norm_g), "adamw_small")
    small = [_unpack(s, small_shapes) for s in small]

    outs = {}
    for kind in range(4):
        for i, n in enumerate(SHARDED):
            outs[(kind, n)] = big[kind][i]
        for i, n in enumerate(small_names):
            val = small[kind][i]
            if n == "norm_g":
                val = lax.dynamic_slice(val, (0, 0, col), norm_g.shape)
            outs[(kind, n)] = val
    return (loss, gx[None], *[outs[(k, n)] for k in range(4) for n in WEIGHTS])
```

```python
import functools
import math

import jax
import jax.numpy as jnp
from jax import lax
from jax.experimental import pallas as pl
from jax.experimental.pallas import tpu as pltpu

F32 = jnp.float32
BF16 = jnp.bfloat16

D_MODEL = 1024
GRID_W = 64
HEAD_DIM = 64
N_BRANCH = 4
BRANCH_W = D_MODEL // N_BRANCH
WINDOW = 128
ROPE_THETA = 10000.0
EPS = 1e-6
D_FF = 2816
N_SUB = 3
POOL_WINDOWS = (2, 4, 8, 16)
POOL_GROUP = BRANCH_W // len(POOL_WINDOWS)
KV_HEADS = 2
Q_PER_KV = 2
SSM_GROUP = 16
SSM_GROUPS = BRANCH_W // SSM_GROUP
SSM_STATE = 64
SSM_LANES = SSM_GROUPS * SSM_STATE
O_KB, O_VB, O_UC, O_KD, O_VD, CTX_COLS = 0, 128, 256, 512, 640, 768
O_QB, O_QD, O_XA, O_GATE = 768, 1024, 1280, 1536
IN_W = O_GATE + N_BRANCH * D_MODEL

ADAM_LR, ADAM_B1, ADAM_B2, ADAM_EPS, ADAM_WD, ADAM_STEP = 0.001, 0.9, 0.999, 1e-08, 0.01, 10

N_DEV = 8
MESH = pl.DeviceIdType.MESH

V7X_VMEM_BYTES = 64 * 1024 * 1024
SUBLANES = 8
LANES = 128
NEG_BIG = -1e30
COND_ROWS = 128

SHARDED = ("w_mod", "ffn_in", "ffn_out", "w_in", "glu_w", "branch_w", "out_w")
SHARD_AXIS = {"w_mod": 2, "ffn_in": 3, "ffn_out": 2, "w_in": 2, "glu_w": 2, "branch_w": 3, "out_w": 1}
SMALL = ("c_ctx", "b_mod", "win_sink", "qk_norm", "pool_w", "pool_scale", "ssm_a_re", "ssm_a_im",
         "ssm_log_dt", "ssm_b_re", "ssm_b_im", "ssm_c_re", "ssm_c_im", "ssm_d", "final_g")
WEIGHTS = ("c_ctx", "w_mod", "b_mod", "norm_g", "ffn_in", "ffn_out", "w_in", "win_sink", "qk_norm",
           "pool_w", "pool_scale", "ssm_a_re", "ssm_a_im", "ssm_log_dt", "ssm_b_re", "ssm_b_im",
           "ssm_c_re", "ssm_c_im", "ssm_d", "glu_w", "branch_w", "out_w", "final_g")


def _tile(n, cap, mult):
    if n <= cap:
        return n
    t = (cap // mult) * mult
    while t >= mult:
        if n % t == 0:
            return t
        t -= mult
    return n


def _cparams(sem, tile_bytes, resident_bytes=0):
    limit = int(min(V7X_VMEM_BYTES - 8 * 2 ** 20,
                    max(32 * 2 ** 20, 3 * tile_bytes + resident_bytes + 8 * 2 ** 20)))
    return pltpu.CompilerParams(dimension_semantics=sem, vmem_limit_bytes=limit)


PLANE = (SUBLANES, 128)
PLANE_COLS = PLANE[0] * PLANE[1]


def _planes_to_rows(ref):
    return jnp.concatenate([ref[:, j, :] for j in range(PLANE[0])], axis=1)


def _mm(a, b, ta=False, tb=False, out_dtype=F32, a_planes=False, b_planes=False, out_planes=False,
        add=None, name="mm"):
    a_shape = (a.shape[0], PLANE_COLS) if a_planes else a.shape
    b_shape = (b.shape[0], PLANE_COLS) if b_planes else b.shape
    m, k = (a_shape[1], a_shape[0]) if ta else a_shape
    n = b_shape[0] if tb else b_shape[1]
    assert (b_shape[1] if tb else b_shape[0]) == k and not (b_planes and tb)
    tm, tn, tk = _tile(m, 1024, LANES), _tile(n, 1536, LANES), _tile(k, 1536, LANES)
    nk = k // tk
    dims = (((0 if ta else 1,), (1 if tb else 0,)), ((), ()))

    def body(a_ref, b_ref, *rest):
        add_ref = rest[0] if add is not None else None
        o_ref, acc_ref = rest[-2], rest[-1]
        kk = pl.program_id(2)

        @pl.when(kk == 0)
        def _():
            acc_ref[...] = jnp.zeros_like(acc_ref)

        av = _planes_to_rows(a_ref) if a_planes else a_ref[...]
        bv = _planes_to_rows(b_ref) if b_planes else b_ref[...]
        acc_ref[...] += lax.dot_general(av.astype(BF16), bv.astype(BF16), dims, preferred_element_type=F32)

        @pl.when(kk == nk - 1)
        def _():
            if out_planes:
                for j in range(PLANE[0]):
                    o_ref[:, j, :] = acc_ref[:, j * PLANE[1]:(j + 1) * PLANE[1]].astype(o_ref.dtype)
            elif add is not None:
                o_ref[...] = (acc_ref[...] + add_ref[...]).astype(o_ref.dtype)
            else:
                o_ref[...] = acc_ref[...].astype(o_ref.dtype)

    if a_planes:
        assert (tm if ta else tk) == PLANE_COLS
        a_spec = pl.BlockSpec(((tk if ta else tm),) + PLANE, (lambda i, j, kk: (kk, 0, 0)) if ta
                              else (lambda i, j, kk: (i, 0, 0)))
    else:
        a_spec = (pl.BlockSpec((tk, tm), lambda i, j, kk: (kk, i)) if ta
                  else pl.BlockSpec((tm, tk), lambda i, j, kk: (i, kk)))
    if b_planes:
        assert tn == PLANE_COLS
        b_spec = pl.BlockSpec((tk,) + PLANE, lambda i, j, kk: (kk, 0, 0))
    else:
        b_spec = (pl.BlockSpec((tn, tk), lambda i, j, kk: (j, kk)) if tb
                  else pl.BlockSpec((tk, tn), lambda i, j, kk: (kk, j)))
    if out_planes:
        assert tn == PLANE_COLS
        o_spec = pl.BlockSpec((tm,) + PLANE, lambda i, j, kk: (i, 0, 0))
        o_shape = jax.ShapeDtypeStruct((m,) + PLANE, out_dtype)
    else:
        o_spec = pl.BlockSpec((tm, tn), lambda i, j, kk: (i, j))
        o_shape = jax.ShapeDtypeStruct((m, n), out_dtype)
    tile_bytes = (a.dtype.itemsize * tm * tk + b.dtype.itemsize * tk * tn
                  + jnp.dtype(out_dtype).itemsize * tm * tn + 2 * tm * tn)
    in_specs, args = [a_spec, b_spec], [a, b]
    if add is not None:
        assert not out_planes and add.shape == (m, n)
        in_specs.append(pl.BlockSpec((tm, tn), lambda i, j, kk: (i, j)))
        args.append(add)
        tile_bytes += add.dtype.itemsize * tm * tn
    return pl.pallas_call(
        body, name=name, grid=(m // tm, n // tn, nk),
        in_specs=in_specs, out_specs=o_spec, out_shape=o_shape,
        scratch_shapes=[pltpu.VMEM((tm, tn), F32)],
        compiler_params=_cparams(("parallel", "parallel", "arbitrary"), tile_bytes),
    )(*args)


def make_linear(out_dtype, x_planes=False, out_planes=False):
    @jax.custom_vjp
    def op(x, w):
        return _mm(x, w, out_dtype=out_dtype, a_planes=x_planes, out_planes=out_planes, name="linear_fwd")

    def fwd(x, w):
        return op(x, w), (x, w)

    def bwd(res, dy):
        x, w = res
        return (_mm(dy, w, tb=True, out_dtype=x.dtype, a_planes=out_planes, out_planes=x_planes,
                    name="linear_dx"),
                _mm(x, dy, ta=True, out_dtype=w.dtype, a_planes=x_planes, b_planes=out_planes,
                    name="linear_dw"))

    op.defvjp(fwd, bwd)
    return op


linear = make_linear(F32)
linear_b = make_linear(BF16)
linear_to_planes = make_linear(F32, out_planes=True)
linear_from_planes = make_linear(F32, x_planes=True)


def _split_rows(t):
    return _tile(t, 512, 2 * SUBLANES)


def _split_fwd(x, w, widths):
    t, k = x.shape
    n = w.shape[1]
    tm = _split_rows(t)

    def body(x_ref, w_ref, *o_refs):
        y = jnp.dot(x_ref[...].astype(BF16), w_ref[...].astype(BF16), preferred_element_type=F32)
        off = 0
        for o_ref, wd in zip(o_refs, widths):
            o_ref[...] = y[:, off:off + wd]
            off += wd

    return pl.pallas_call(
        body, name="split_linear_fwd", grid=(t // tm,),
        in_specs=[pl.BlockSpec((tm, k), lambda i: (i, 0)), pl.BlockSpec((k, n), lambda i: (0, 0))],
        out_specs=[pl.BlockSpec((tm, wd), lambda i: (i, 0)) for wd in widths],
        out_shape=[jax.ShapeDtypeStruct((t, wd), F32) for wd in widths],
        compiler_params=_cparams(("parallel",), 4 * tm * (k + 2 * n) + w.dtype.itemsize * k * n),
    )(x, w)


def _split_dx(cts, w):
    t = cts[0].shape[0]
    k, n = w.shape
    tm = _split_rows(t)

    def body(*refs):
        dy = jnp.concatenate([r[...].astype(BF16) for r in refs[:-2]], axis=1)
        refs[-1][...] = lax.dot_general(dy, refs[-2][...].astype(BF16), (((1,), (1,)), ((), ())),
                                        preferred_element_type=F32)

    return pl.pallas_call(
        body, name="split_linear_dx", grid=(t // tm,),
        in_specs=[pl.BlockSpec((tm, c.shape[1]), lambda i: (i, 0)) for c in cts]
        + [pl.BlockSpec((k, n), lambda i: (0, 0))],
        out_specs=pl.BlockSpec((tm, k), lambda i: (i, 0)), out_shape=jax.ShapeDtypeStruct((t, k), F32),
        compiler_params=_cparams(("parallel",), 4 * tm * (k + 2 * n) + w.dtype.itemsize * k * n),
    )(*cts, w)


def _split_dw(x, cts, out_dtype):
    t, k = x.shape
    n = sum(c.shape[1] for c in cts)
    tk = _split_rows(t)
    steps = t // tk

    def body(x_ref, *refs):
        o_ref, acc_ref = refs[-2], refs[-1]

        @pl.when(pl.program_id(0) == 0)
        def _():
            acc_ref[...] = jnp.zeros_like(acc_ref)

        dy = jnp.concatenate([r[...].astype(BF16) for r in refs[:-2]], axis=1)
        acc_ref[...] += lax.dot_general(x_ref[...].astype(BF16), dy, (((0,), (0,)), ((), ())),
                                        preferred_element_type=F32)

        @pl.when(pl.program_id(0) == steps - 1)
        def _():
            o_ref[...] = acc_ref[...].astype(o_ref.dtype)

    return pl.pallas_call(
        body, name="split_linear_dw", grid=(steps,),
        in_specs=[pl.BlockSpec((tk, k), lambda i: (i, 0))]
        + [pl.BlockSpec((tk, c.shape[1]), lambda i: (i, 0)) for c in cts],
        out_specs=pl.BlockSpec((k, n), lambda i: (0, 0)), out_shape=jax.ShapeDtypeStruct((k, n), out_dtype),
        scratch_shapes=[pltpu.VMEM((k, n), F32)],
        compiler_params=_cparams(("arbitrary",), 4 * tk * (k + n), 3 * 4 * k * n),
    )(x, *cts)


def make_split_linear(widths):
    @jax.custom_vjp
    def op(x, w):
        return tuple(_split_fwd(x, w, widths))

    def fwd(x, w):
        return op(x, w), (x, w)

    def bwd(res, cts):
        x, w = res
        return _split_dx(cts, w), _split_dw(x, cts, w.dtype)

    op.defvjp(fwd, bwd)
    return op


ROW_TILE_BYTES = 6 * 2 ** 20


def _row_tile(t, row_bytes):
    tm = 1024
    while tm > 2 * SUBLANES and tm * row_bytes > ROW_TILE_BYTES:
        tm //= 2
    return _tile(t, tm, 2 * SUBLANES)


def _row_call(fn, rows, params, n_reduce, name):
    t = rows[0].shape[0]
    out_avals = jax.eval_shape(fn, *rows, *params)
    n_out = len(out_avals) - n_reduce
    row_avals, red_avals = out_avals[:n_out], out_avals[n_out:]
    row_bytes = sum(r.shape[1] * r.dtype.itemsize for r in (*rows, *row_avals))
    tm = _row_tile(t, row_bytes)
    n_in = len(rows) + len(params)

    def body(*refs):
        outs = fn(*[r[...] for r in refs[:n_in]])
        o_refs = refs[n_in:]
        for o_ref, o in zip(o_refs[:n_out], outs[:n_out]):
            o_ref[...] = o.astype(o_ref.dtype)
        if n_reduce:
            @pl.when(pl.program_id(0) == 0)
            def _():
                for r in o_refs[n_out:]:
                    r[...] = jnp.zeros_like(r)

            for r, o in zip(o_refs[n_out:], outs[n_out:]):
                r[...] += o.astype(r.dtype)

    in_specs = ([pl.BlockSpec((tm, r.shape[1]), lambda i: (i, 0)) for r in rows]
                + [pl.BlockSpec(p.shape, lambda i: (0, 0)) for p in params])
    out_specs = ([pl.BlockSpec((tm, o.shape[1]), lambda i: (i, 0)) for o in row_avals]
                 + [pl.BlockSpec(o.shape, lambda i: (0, 0)) for o in red_avals])
    return pl.pallas_call(
        body, name=name, grid=(t // tm,), in_specs=in_specs, out_specs=out_specs,
        out_shape=[jax.ShapeDtypeStruct(o.shape, o.dtype) for o in out_avals],
        compiler_params=_cparams(("arbitrary",) if n_reduce else ("parallel",), tm * row_bytes),
    )(*rows, *params)


def rowwise(fn, n_rows, name):
    @jax.custom_vjp
    def op(*args):
        return tuple(_row_call(fn, args[:n_rows], args[n_rows:], 0, name + "_fwd"))

    def fwd(*args):
        return op(*args), args

    def bwd(args, cts):
        n_ct = len(cts)

        def bwd_fn(*a):
            r, ct, p = a[:n_rows], a[n_rows:n_rows + n_ct], a[n_rows + n_ct:]
            return jax.vjp(fn, *r, *p)[1](tuple(ct))

        return tuple(_row_call(bwd_fn, (*args[:n_rows], *cts), args[n_rows:], len(args) - n_rows,
                               name + "_bwd"))

    op.defvjp(fwd, bwd)
    return op


@functools.partial(jax.custom_vjp, nondiff_argnums=(1,))
def _swap_lanes(x, k):
    n = x.shape[-1]
    lane = lax.broadcasted_iota(jnp.int32, x.shape, x.ndim - 1)
    return jnp.where((lane & k) == 0, pltpu.roll(x, n - k, x.ndim - 1), pltpu.roll(x, k, x.ndim - 1))


def _swap_lanes_fwd(x, k):
    return _swap_lanes(x, k), None


def _swap_lanes_bwd(k, _, g):
    return (_swap_lanes(g, k),)


_swap_lanes.defvjp(_swap_lanes_fwd, _swap_lanes_bwd)


def _head_sum(x):
    s = x
    k = 1
    while k < HEAD_DIM:
        s = s + _swap_lanes(s, k)
        k *= 2
    return s


def _rms(x):
    return x * lax.rsqrt(jnp.mean(x * x, axis=-1, keepdims=True) + EPS)


def _norm_mod_fn(x, g, shift, scale):
    return ((_rms(x) * g) * (1.0 + scale) + shift,)


def _swiglu_fn(u):
    gate, up = u[:, :D_FF], u[:, D_FF:]
    return (jax.nn.silu(gate) * up,)


def _resid_fn(coef, x, y, gate):
    return (x + (coef * gate) * y,)


def _scale_fn(y, s):
    return (y * s,)


def _tile_lanes(tab, width):
    return tab if tab.shape[1] == width else jnp.concatenate([tab] * (width // tab.shape[1]), axis=1)


def _rope_fn(x, cos, sin):
    w = x.shape[1]
    return (x * _tile_lanes(cos, w) + _swap_lanes(x, 16) * _tile_lanes(sin, w),)


def _head_norm(x, g):
    ms = _head_sum(x * x) * (1.0 / HEAD_DIM)
    return x * lax.rsqrt(ms + EPS) * g


def _norm_rope_fn(x, cos, sin, g):
    return _rope_fn(_head_norm(x, g), cos, sin)


def _head_norm_fn(x, g):
    return (_head_norm(x, g),)


def _merge_fn(gl, z0, z1, z2, z3):
    zs = (z0, z1, z2, z3)
    terms = [jax.nn.sigmoid(gl[:, k * D_MODEL:(k + 1) * D_MODEL].astype(F32)) * zs[k].astype(F32)
             for k in range(N_BRANCH)]
    return (sum(terms[1:], terms[0]),)


def _s5_pre_fn(y0r, y0i, y1r, y1i, u, d):
    return (jax.nn.gelu(((y0r - y0i) + (y1r - y1i)) + d * u),)


def _glu_fn(z):
    return (z[:, :BRANCH_W] * jax.nn.sigmoid(z[:, BRANCH_W:]),)


def _silu_fn(x):
    return (jax.nn.silu(x),)


def _loss_fn(x, tgt, g):
    err = jnp.square(_rms(x) * g - tgt)
    return (0.5 * jnp.mean(err, axis=-1, keepdims=True),)


norm_mod = rowwise(_norm_mod_fn, 1, "norm_mod")
resid_full = rowwise(functools.partial(_resid_fn, 1.0), 2, "resid_full")
scale_rows = rowwise(_scale_fn, 1, "pool_scale")
rope = rowwise(_rope_fn, 3, "rope")
norm_rope = rowwise(_norm_rope_fn, 3, "norm_rope")
head_norm = rowwise(_head_norm_fn, 1, "head_norm")
merge = rowwise(_merge_fn, 5, "merge")
s5_pre = rowwise(_s5_pre_fn, 5, "s5_pre")
glu = rowwise(_glu_fn, 1, "glu")
silu_rows = rowwise(_silu_fn, 1, "silu")
loss_rows = rowwise(_loss_fn, 2, "loss_head")


POOL_HALO = 16


def _pool_call(xa, adjoint, name):
    n, width = xa.shape
    tm = _tile(n, 512, POOL_HALO)
    halo_blocks = tm // POOL_HALO
    last_halo = n // POOL_HALO - 1
    ext_rows = tm + 2 * POOL_HALO

    def body(prev_ref, cur_ref, next_ref, o_ref, ext_ref):
        i = pl.program_id(0)
        ext_ref[0:POOL_HALO] = prev_ref[...]
        ext_ref[POOL_HALO:POOL_HALO + tm] = cur_ref[...]
        ext_ref[POOL_HALO + tm:ext_rows] = next_ref[...]
        e = ext_ref[...]
        row = lax.broadcasted_iota(jnp.int32, e.shape, 0) + (i * tm - POOL_HALO)
        grp = lax.broadcasted_iota(jnp.int32, e.shape, 1) // POOL_GROUP
        win = jnp.where(grp == 0, POOL_WINDOWS[0],
                        jnp.where(grp == 1, POOL_WINDOWS[1], jnp.where(grp == 2, POOL_WINDOWS[2], POOL_WINDOWS[3])))
        valid = (row >= 0) & (row < n)
        lo = jnp.clip(row - win // 2, 0, n)
        hi = jnp.clip(row - win // 2 + win, 0, n)
        cnt = jnp.maximum((hi - lo).astype(F32), 1.0)
        e0 = jnp.where(valid, e / cnt if adjoint else e, 0.0)

        def shift(z, s):
            return pltpu.roll(z, s % ext_rows, 0)

        s2 = e0 + shift(e0, -1 if adjoint else 1)
        s4 = shift(s2, 1) + shift(s2, -1)
        s8 = shift(s4, 2) + shift(s4, -2)
        s16 = shift(s8, 4) + shift(s8, -4)
        s = jnp.where(grp == 0, s2, jnp.where(grp == 1, s4, jnp.where(grp == 2, s8, s16)))
        out = (s - e) if adjoint else (s / cnt - e)
        o_ref[...] = out[POOL_HALO:POOL_HALO + tm]

    return pl.pallas_call(
        body, name=name, grid=(n // tm,),
        in_specs=[pl.BlockSpec((POOL_HALO, width), lambda i: (jnp.maximum(i * halo_blocks - 1, 0), 0)),
                  pl.BlockSpec((tm, width), lambda i: (i, 0)),
                  pl.BlockSpec((POOL_HALO, width), lambda i: (jnp.minimum((i + 1) * halo_blocks, last_halo), 0))],
        out_specs=pl.BlockSpec((tm, width), lambda i: (i, 0)),
        out_shape=jax.ShapeDtypeStruct((n, width), F32),
        scratch_shapes=[pltpu.VMEM((ext_rows, width), F32)],
        compiler_params=_cparams(("parallel",), 4 * 4 * ext_rows * width),
    )(xa, xa, xa)


@jax.custom_vjp
def pool_diff(xa):
    return _pool_call(xa, False, "pool_fwd")


pool_diff.defvjp(lambda xa: (pool_diff(xa), None), lambda _, g: (_pool_call(g, True, "pool_bwd"),))


ATT_SCALE = HEAD_DIM ** -0.5
ATT_BQ = 512


def _qk_scores(q, k):
    return lax.dot_general((q * ATT_SCALE).astype(BF16), k.astype(BF16), (((1,), (1,)), ((), ())),
                           preferred_element_type=F32)


def _sink_rows(sink_ref, h, bq):
    r = lax.broadcasted_iota(jnp.int32, (Q_PER_KV * bq, 1), 0)
    return jnp.where(r < bq, sink_ref[h * Q_PER_KV], sink_ref[h * Q_PER_KV + 1])


ATT_SUB_ROWS = 256


def _flash_fwd(q, klt, vl, kct, vc, sink):
    kvh, g, t, dh = q.shape
    c = kct.shape[2]
    has_lat = klt is not None
    has_sink = sink is not None
    bq = _tile(t, ATT_BQ, LANES)
    bk = _tile(t, 4096, LANES)
    nkv = t // bk if has_lat else 1
    rows = g * bq
    sub = min(ATT_SUB_ROWS, rows)

    def with_ones(v):
        return jnp.concatenate([v, jnp.ones(v.shape[:2] + (1,), v.dtype),
                                jnp.zeros(v.shape[:2] + (LANES - dh - 1,), v.dtype)], axis=2)

    def body(*refs):
        it = iter(refs)
        q_ref, kc_ref, vc_ref = next(it), next(it), next(it)
        kl_ref, vl_ref = (next(it), next(it)) if has_lat else (None, None)
        sink_ref = next(it) if has_sink else None
        o_ref, lse_ref, m_ref, acc_ref = next(it), next(it), next(it), next(it)
        h, kj = pl.program_id(0), pl.program_id(2)
        qv = (q_ref[0].reshape(rows, dh) * ATT_SCALE).astype(BF16)
        is_sum_lane = lax.broadcasted_iota(jnp.int32, (rows, LANES), 1) == dh

        def part(kt_ref, v_ref):
            kt, v = kt_ref[0].astype(BF16), v_ref[0].astype(BF16)
            m_all, acc_all = m_ref[...], acc_ref[...]
            m_out, acc_out = [], []
            for r0 in range(0, rows, sub):
                s = jnp.dot(qv[r0:r0 + sub], kt, preferred_element_type=F32)
                m_old = m_all[r0:r0 + sub]
                m_new = jnp.maximum(m_old, jnp.max(s, axis=-1, keepdims=True))
                p = jnp.exp(s - m_new)
                m_out.append(m_new)
                acc_out.append(jnp.exp(m_old - m_new) * acc_all[r0:r0 + sub]
                               + jnp.dot(p.astype(BF16), v, preferred_element_type=F32))
            m_ref[...] = jnp.concatenate(m_out, axis=0)
            acc_ref[...] = jnp.concatenate(acc_out, axis=0)

        @pl.when(kj == 0)
        def _():
            if has_sink:
                m_ref[...] = _sink_rows(sink_ref, h, bq)
                acc_ref[...] = jnp.where(is_sum_lane, 1.0, 0.0)
            else:
                m_ref[...] = jnp.full_like(m_ref, NEG_BIG)
                acc_ref[...] = jnp.zeros_like(acc_ref)
            part(kc_ref, vc_ref)

        if has_lat:
            part(kl_ref, vl_ref)

        @pl.when(kj == nkv - 1)
        def _():
            acc = acc_ref[...]
            l = jnp.sum(jnp.where(is_sum_lane, acc, 0.0), axis=-1, keepdims=True)
            o_ref[0] = (acc / l).reshape(g, bq, LANES)
            lse_ref[0] = (m_ref[...] + jnp.log(l)).reshape(g, bq, 1)

    q_spec = pl.BlockSpec((1, g, bq, dh), lambda h, i, j: (h, 0, i, 0))
    o_spec = pl.BlockSpec((1, g, bq, LANES), lambda h, i, j: (h, 0, i, 0))
    r_spec = pl.BlockSpec((1, g, bq, 1), lambda h, i, j: (h, 0, i, 0))
    in_specs = [q_spec, pl.BlockSpec((1, dh, c), lambda h, i, j: (h, 0, 0)),
                pl.BlockSpec((1, c, LANES), lambda h, i, j: (h, 0, 0))]
    args = [q, kct, with_ones(vc)]
    if has_lat:
        in_specs += [pl.BlockSpec((1, dh, bk), lambda h, i, j: (h, 0, j)),
                     pl.BlockSpec((1, bk, LANES), lambda h, i, j: (h, j, 0))]
        args += [klt, with_ones(vl)]
    if has_sink:
        in_specs.append(pl.BlockSpec(memory_space=pltpu.SMEM))
        args.append(sink)
    o_wide, lse = pl.pallas_call(
        body, name="attn_fwd", grid=(kvh, t // bq, nkv),
        in_specs=in_specs, out_specs=[o_spec, r_spec],
        out_shape=[jax.ShapeDtypeStruct((kvh, g, t, LANES), F32), jax.ShapeDtypeStruct((kvh, g, t, 1), F32)],
        scratch_shapes=[pltpu.VMEM((rows, 1), F32), pltpu.VMEM((rows, LANES), F32)],
        compiler_params=_cparams(("parallel", "parallel", "arbitrary"), 4 * 4 * rows * max(bk, c)),
    )(*args)
    return o_wide[..., :dh], lse


def _ctx_dq(q, kc, vc, sink, o, do, lse):
    kvh, g, t, dh = q.shape
    c = kc.shape[1]
    has_sink = sink is not None
    bq = _tile(t, ATT_BQ, LANES)
    rows = g * bq

    def body(*refs):
        q_ref, o_ref, do_ref, lse_ref, kc_ref, vc_ref = refs[:6]
        sink_ref = refs[6] if has_sink else None
        dq_ref, delta_ref, dsink_ref = refs[-3:]
        dov = do_ref[0].reshape(rows, dh)
        lse_v = lse_ref[0].reshape(rows, 1)
        delta = jnp.sum(o_ref[0].reshape(rows, dh) * dov, axis=-1, keepdims=True)
        p = jnp.exp(_qk_scores(q_ref[0].reshape(rows, dh), kc_ref[0]) - lse_v)
        dp = lax.dot_general(dov.astype(BF16), vc_ref[0].astype(BF16), (((1,), (1,)), ((), ())),
                             preferred_element_type=F32)
        ds = (p * (dp - delta)).astype(BF16)
        dq = jnp.dot(ds, kc_ref[0].astype(BF16), preferred_element_type=F32) * ATT_SCALE
        dq_ref[0] = dq.reshape(g, bq, dh)
        delta_ref[0] = delta.reshape(g, bq, 1)
        if has_sink:
            p_sink = jnp.exp(_sink_rows(sink_ref, pl.program_id(0), bq) - lse_v)
            dsink_ref[0] = (-p_sink * delta).reshape(g, bq, 1)
        else:
            dsink_ref[0] = jnp.zeros((g, bq, 1), F32)

    q_spec = pl.BlockSpec((1, g, bq, dh), lambda h, i: (h, 0, i, 0))
    r_spec = pl.BlockSpec((1, g, bq, 1), lambda h, i: (h, 0, i, 0))
    c_spec = pl.BlockSpec((1, c, dh), lambda h, i: (h, 0, 0))
    in_specs, args = [q_spec, q_spec, q_spec, r_spec, c_spec, c_spec], [q, o, do, lse, kc, vc]
    if has_sink:
        in_specs.append(pl.BlockSpec(memory_space=pltpu.SMEM))
        args.append(sink)
    row_shape = jax.ShapeDtypeStruct((kvh, g, t, 1), F32)
    return pl.pallas_call(
        body, name="attn_ctx_dq", grid=(kvh, t // bq),
        in_specs=in_specs, out_specs=[q_spec, r_spec, r_spec],
        out_shape=[jax.ShapeDtypeStruct(q.shape, F32), row_shape, row_shape],
        compiler_params=_cparams(("parallel", "parallel"), 4 * 6 * rows * c),
    )(*args)


def _flash_dkv(q, do, lse, delta, k, v):
    kvh, g, t, dh = q.shape
    nk_rows = k.shape[1]
    bq = _tile(t, ATT_BQ, LANES)
    bk = _tile(nk_rows, 1024, LANES)
    nq = t // bq
    rows = g * bq

    def body(q_ref, do_ref, lse_ref, delta_ref, k_ref, v_ref, dk_ref, dv_ref, dk_acc, dv_acc):
        qj = pl.program_id(2)
        qv = q_ref[0].reshape(rows, dh)
        dov = do_ref[0].reshape(rows, dh)

        @pl.when(qj == 0)
        def _():
            dk_acc[...] = jnp.zeros_like(dk_acc)
            dv_acc[...] = jnp.zeros_like(dv_acc)

        s = _qk_scores(qv, k_ref[0])
        p = jnp.exp(s - lse_ref[0].reshape(rows, 1))
        dp = lax.dot_general(dov.astype(BF16), v_ref[0].astype(BF16), (((1,), (1,)), ((), ())),
                             preferred_element_type=F32)
        ds = p * (dp - delta_ref[0].reshape(rows, 1))
        tn = (((0,), (0,)), ((), ()))
        dv_acc[...] += lax.dot_general(p.astype(BF16), dov.astype(BF16), tn, preferred_element_type=F32)
        dk_acc[...] += lax.dot_general(ds.astype(BF16), qv.astype(BF16), tn, preferred_element_type=F32)

        @pl.when(qj == nq - 1)
        def _():
            dk_ref[0] = dk_acc[...] * ATT_SCALE
            dv_ref[0] = dv_acc[...]

    q_spec = pl.BlockSpec((1, g, bq, dh), lambda h, i, j: (h, 0, j, 0))
    r_spec = pl.BlockSpec((1, g, bq, 1), lambda h, i, j: (h, 0, j, 0))
    k_spec = pl.BlockSpec((1, bk, dh), lambda h, i, j: (h, i, 0))
    return pl.pallas_call(
        body, name="attn_dkv", grid=(kvh, nk_rows // bk, nq),
        in_specs=[q_spec, q_spec, r_spec, r_spec, k_spec, k_spec], out_specs=[k_spec, k_spec],
        out_shape=[jax.ShapeDtypeStruct(k.shape, F32), jax.ShapeDtypeStruct(k.shape, F32)],
        scratch_shapes=[pltpu.VMEM((bk, dh), F32), pltpu.VMEM((bk, dh), F32)],
        compiler_params=_cparams(("parallel", "parallel", "arbitrary"), 4 * 6 * rows * bk),
    )(q, do, lse, delta, k, v)


def _flash_bwd_full(q, kl, vl, kc, vc, o, do, lse):
    kvh, g, t, dh = q.shape
    c = kc.shape[1]
    bq, bk = _tile(t, 512, LANES), _tile(t, 1024, LANES)
    nq, nkv = t // bq, t // bk
    rows = g * bq
    klt, vlt = jnp.swapaxes(kl, 1, 2), jnp.swapaxes(vl, 1, 2)
    kct, vct = jnp.swapaxes(kc, 1, 2), jnp.swapaxes(vc, 1, 2)
    tn = (((0,), (0,)), ((), ()))
    chunk = min(bk, 2 * LANES)

    def body(q_ref, o_ref, do_ref, lse_ref, kc_ref, kct_ref, vct_ref, kl_ref, klt_ref, vlt_ref,
             dq_ref, delta_ref, dk_hbm, dv_hbm, dq_acc, dl_ref, dk_acc, dv_acc):
        h, qi, kj = pl.program_id(0), pl.program_id(1), pl.program_id(2)
        q_raw = q_ref[0].reshape(rows, dh).astype(BF16)
        qv = (q_ref[0].reshape(rows, dh) * ATT_SCALE).astype(BF16)
        dov = do_ref[0].reshape(rows, dh).astype(BF16)
        lse_v = lse_ref[0].reshape(rows, 1)

        def tile(kt, vt):
            s = jnp.dot(qv, kt.astype(BF16), preferred_element_type=F32)
            p = jnp.exp(s - lse_v)
            dp = jnp.dot(dov, vt.astype(BF16), preferred_element_type=F32)
            return p, (p * (dp - dl_ref[...])).astype(BF16)

        @pl.when((qi == 0) & (kj == 0))
        def _():
            dk_acc[...] = jnp.zeros_like(dk_acc)
            dv_acc[...] = jnp.zeros_like(dv_acc)

        @pl.when(kj == 0)
        def _():
            dl_ref[...] = jnp.sum(o_ref[0].reshape(rows, dh) * do_ref[0].reshape(rows, dh),
                                  axis=-1, keepdims=True)
            _, ds = tile(kct_ref[0], vct_ref[0])
            dq_acc[...] = jnp.dot(ds, kc_ref[0].astype(BF16), preferred_element_type=F32)

        for c0 in range(0, bk, chunk):
            p, ds = tile(klt_ref[0, :, c0:c0 + chunk], vlt_ref[0, :, c0:c0 + chunk])
            dq_acc[...] += jnp.dot(ds, kl_ref[0, c0:c0 + chunk].astype(BF16), preferred_element_type=F32)
            ks = pl.ds(pl.multiple_of(kj * bk + c0, chunk), chunk)
            dv_acc[ks] += lax.dot_general(p.astype(BF16), dov, tn, preferred_element_type=F32)
            dk_acc[ks] += lax.dot_general(ds, q_raw, tn, preferred_element_type=F32)

        @pl.when(kj == nkv - 1)
        def _():
            dq_ref[0] = (dq_acc[...] * ATT_SCALE).reshape(g, bq, dh)
            delta_ref[0] = dl_ref[...].reshape(g, bq, 1)

        @pl.when((qi == nq - 1) & (kj == nkv - 1))
        def _():
            dk_acc[...] = dk_acc[...] * ATT_SCALE
            pltpu.sync_copy(dk_acc, dk_hbm.at[h])
            pltpu.sync_copy(dv_acc, dv_hbm.at[h])

    q_spec = pl.BlockSpec((1, g, bq, dh), lambda h, i, j: (h, 0, i, 0))
    r_spec = pl.BlockSpec((1, g, bq, 1), lambda h, i, j: (h, 0, i, 0))
    c_spec = pl.BlockSpec((1, c, dh), lambda h, i, j: (h, 0, 0))
    ct_spec = pl.BlockSpec((1, dh, c), lambda h, i, j: (h, 0, 0))
    l_spec = pl.BlockSpec((1, bk, dh), lambda h, i, j: (h, j, 0))
    lt_spec = pl.BlockSpec((1, dh, bk), lambda h, i, j: (h, 0, j))
    any_spec = pl.BlockSpec(memory_space=pl.ANY)
    kv_shape = jax.ShapeDtypeStruct((kvh, t, dh), F32)
    return pl.pallas_call(
        body, name="attn_bwd_full", grid=(kvh, nq, nkv),
        in_specs=[q_spec, q_spec, q_spec, r_spec, c_spec, ct_spec, ct_spec, l_spec, lt_spec, lt_spec],
        out_specs=[q_spec, r_spec, any_spec, any_spec],
        out_shape=[jax.ShapeDtypeStruct(q.shape, F32), jax.ShapeDtypeStruct((kvh, g, t, 1), F32),
                   kv_shape, kv_shape],
        scratch_shapes=[pltpu.VMEM((rows, dh), F32), pltpu.VMEM((rows, 1), F32),
                        pltpu.VMEM((t, dh), F32), pltpu.VMEM((t, dh), F32)],
        compiler_params=_cparams(("arbitrary", "arbitrary", "arbitrary"), 4 * 2 * rows * bk,
                                 2 * 4 * t * LANES),
    )(q, o, do, lse, kc, kct, vct, kl, klt, vlt)


BAND_BQ = 512


def _band_specs(t, shape_of):
    per = BAND_BQ // WINDOW
    n_halo = t // WINDOW

    def spec(n, index):
        shape, axis = shape_of(n)

        def index_map(h, i):
            idx = [h] + [0] * (len(shape) - 1)
            idx[axis] = index(i)
            return tuple(idx)

        return pl.BlockSpec(shape, index_map)

    return [spec(WINDOW, lambda i: jnp.maximum(i * per - 1, 0)),
            spec(BAND_BQ, lambda i: i),
            spec(WINDOW, lambda i: jnp.minimum((i + 1) * per, n_halo - 1))]


def _band_visible(i, t, c, rows):
    cols = c + BAND_BQ + 2 * WINDOW
    col = lax.broadcasted_iota(jnp.int32, (rows, cols), 1)
    qpos = i * BAND_BQ + lax.broadcasted_iota(jnp.int32, (rows, cols), 0) % BAND_BQ
    kpos = i * BAND_BQ - WINDOW + (col - c)
    return (col < c) | ((kpos >= 0) & (kpos < t) & (jnp.abs(kpos - qpos) <= WINDOW))


def _band_fwd(q, klt, vl, kct, vc, sink):
    kvh, g, t, dh = q.shape
    c = kct.shape[2]
    rows = g * BAND_BQ

    def body(q_ref, kct_ref, vc_ref, ktp, ktc, ktn, vp, vcur, vn, sink_ref, o_ref, lse_ref):
        h, i = pl.program_id(0), pl.program_id(1)
        qv = (q_ref[0].reshape(rows, dh) * ATT_SCALE).astype(BF16)
        kt = jnp.concatenate([kct_ref[0], ktp[0], ktc[0], ktn[0]], axis=1).astype(BF16)
        v = jnp.concatenate([vc_ref[0], vp[0], vcur[0], vn[0]], axis=0).astype(BF16)
        s = jnp.where(_band_visible(i, t, c, rows), jnp.dot(qv, kt, preferred_element_type=F32), NEG_BIG)
        sink_r = _sink_rows(sink_ref, h, BAND_BQ)
        m = jnp.maximum(sink_r, jnp.max(s, axis=-1, keepdims=True))
        p = jnp.exp(s - m)
        l = jnp.exp(sink_r - m) + jnp.sum(p, axis=-1, keepdims=True)
        o_ref[0] = (jnp.dot(p.astype(BF16), v, preferred_element_type=F32) / l).reshape(g, BAND_BQ, dh)
        lse_ref[0] = (m + jnp.log(l)).reshape(g, BAND_BQ, 1)

    q_spec = pl.BlockSpec((1, g, BAND_BQ, dh), lambda h, i: (h, 0, i, 0))
    r_spec = pl.BlockSpec((1, g, BAND_BQ, 1), lambda h, i: (h, 0, i, 0))
    in_specs = ([q_spec, pl.BlockSpec((1, dh, c), lambda h, i: (h, 0, 0)),
                 pl.BlockSpec((1, c, dh), lambda h, i: (h, 0, 0))]
                + _band_specs(t,lambda n: ((1, dh, n), 2))
                + _band_specs(t,lambda n: ((1, n, dh), 1))
                + [pl.BlockSpec(memory_space=pltpu.SMEM)])
    return pl.pallas_call(
        body, name="attn_band_fwd", grid=(kvh, t // BAND_BQ), in_specs=in_specs, out_specs=[q_spec, r_spec],
        out_shape=[jax.ShapeDtypeStruct(q.shape, F32), jax.ShapeDtypeStruct((kvh, g, t, 1), F32)],
        compiler_params=_cparams(("parallel", "parallel"), 4 * 2 * rows * (c + BAND_BQ + 2 * WINDOW)),
    )(q, kct, vc, klt, klt, klt, vl, vl, vl, sink)


def _band_dq(q, kl, klt, vlt, kc, kct, vct, sink, o, do, lse):
    kvh, g, t, dh = q.shape
    c = kc.shape[1]
    rows = g * BAND_BQ

    def body(q_ref, o_ref, do_ref, lse_ref, kc_ref, kct_ref, vct_ref, kp, kcur, kn, ktp, ktc, ktn,
             vtp, vtc, vtn, sink_ref, dq_ref, delta_ref, dsink_ref):
        h, i = pl.program_id(0), pl.program_id(1)
        qv = (q_ref[0].reshape(rows, dh) * ATT_SCALE).astype(BF16)
        dov = do_ref[0].reshape(rows, dh)
        lse_v = lse_ref[0].reshape(rows, 1)
        kt = jnp.concatenate([kct_ref[0], ktp[0], ktc[0], ktn[0]], axis=1).astype(BF16)
        vt = jnp.concatenate([vct_ref[0], vtp[0], vtc[0], vtn[0]], axis=1).astype(BF16)
        k = jnp.concatenate([kc_ref[0], kp[0], kcur[0], kn[0]], axis=0).astype(BF16)
        s = jnp.where(_band_visible(i, t, c, rows), jnp.dot(qv, kt, preferred_element_type=F32), NEG_BIG)
        p = jnp.exp(s - lse_v)
        delta = jnp.sum(o_ref[0].reshape(rows, dh) * dov, axis=-1, keepdims=True)
        dp = jnp.dot(dov.astype(BF16), vt, preferred_element_type=F32)
        ds = (p * (dp - delta)).astype(BF16)
        dq_ref[0] = (jnp.dot(ds, k, preferred_element_type=F32) * ATT_SCALE).reshape(g, BAND_BQ, dh)
        delta_ref[0] = delta.reshape(g, BAND_BQ, 1)
        p_sink = jnp.exp(_sink_rows(sink_ref, h, BAND_BQ) - lse_v)
        dsink_ref[0] = (-p_sink * delta).reshape(g, BAND_BQ, 1)

    q_spec = pl.BlockSpec((1, g, BAND_BQ, dh), lambda h, i: (h, 0, i, 0))
    r_spec = pl.BlockSpec((1, g, BAND_BQ, 1), lambda h, i: (h, 0, i, 0))
    ct_spec = pl.BlockSpec((1, dh, c), lambda h, i: (h, 0, 0))
    rows_of = lambda n: ((1, n, dh), 1)
    lanes_of = lambda n: ((1, dh, n), 2)
    in_specs = ([q_spec, q_spec, q_spec, r_spec, pl.BlockSpec((1, c, dh), lambda h, i: (h, 0, 0)), ct_spec, ct_spec]
                + _band_specs(t,rows_of) + _band_specs(t,lanes_of) + _band_specs(t,lanes_of)
                + [pl.BlockSpec(memory_space=pltpu.SMEM)])
    row_shape = jax.ShapeDtypeStruct((kvh, g, t, 1), F32)
    return pl.pallas_call(
        body, name="attn_band_dq", grid=(kvh, t // BAND_BQ), in_specs=in_specs,
        out_specs=[q_spec, r_spec, r_spec], out_shape=[jax.ShapeDtypeStruct(q.shape, F32), row_shape, row_shape],
        compiler_params=_cparams(("parallel", "parallel"), 4 * 3 * rows * (c + BAND_BQ + 2 * WINDOW)),
    )(q, o, do, lse, kc, kct, vct, kl, kl, kl, klt, klt, klt, vlt, vlt, vlt, sink)


def _band_dkv(q, do, lse, delta, klt, vlt):
    kvh, g, t, dh = q.shape
    span = BAND_BQ + 2 * WINDOW
    rows = g * span
    tn = (((0,), (0,)), ((), ()))

    def body(qp, qc, qn, dop, doc, don, lp, lc, ln, dp_, dc_, dn_, kt_ref, vt_ref, dk_ref, dv_ref):
        j = pl.program_id(1)

        def stack(a, b, c_):
            return jnp.concatenate([jnp.concatenate([a[0, gi], b[0, gi], c_[0, gi]], axis=0)
                                    for gi in range(g)], axis=0)

        q_all, do_all = stack(qp, qc, qn), stack(dop, doc, don).astype(BF16)
        lse_all, delta_all = stack(lp, lc, ln), stack(dp_, dc_, dn_)
        s = jnp.dot((q_all * ATT_SCALE).astype(BF16), kt_ref[0].astype(BF16), preferred_element_type=F32)
        qpos = j * BAND_BQ - WINDOW + lax.broadcasted_iota(jnp.int32, (rows, BAND_BQ), 0) % span
        kpos = j * BAND_BQ + lax.broadcasted_iota(jnp.int32, (rows, BAND_BQ), 1)
        s = jnp.where((qpos >= 0) & (qpos < t) & (jnp.abs(kpos - qpos) <= WINDOW), s, NEG_BIG)
        p = jnp.exp(s - lse_all)
        dp = jnp.dot(do_all, vt_ref[0].astype(BF16), preferred_element_type=F32)
        ds = (p * (dp - delta_all)).astype(BF16)
        dv_ref[0] = lax.dot_general(p.astype(BF16), do_all, tn, preferred_element_type=F32)
        dk_ref[0] = lax.dot_general(ds, q_all.astype(BF16), tn, preferred_element_type=F32) * ATT_SCALE

    q_specs = _band_specs(t,lambda n: ((1, g, n, dh), 2))
    r_specs = _band_specs(t,lambda n: ((1, g, n, 1), 2))
    kt_spec = pl.BlockSpec((1, dh, BAND_BQ), lambda h, j: (h, 0, j))
    k_spec = pl.BlockSpec((1, BAND_BQ, dh), lambda h, j: (h, j, 0))
    kv_shape = jax.ShapeDtypeStruct((kvh, t, dh), F32)
    return pl.pallas_call(
        body, name="attn_band_dkv", grid=(kvh, t // BAND_BQ),
        in_specs=q_specs + q_specs + r_specs + r_specs + [kt_spec, kt_spec], out_specs=[k_spec, k_spec],
        out_shape=[kv_shape, kv_shape],
        compiler_params=_cparams(("parallel", "parallel"), 4 * 3 * rows * BAND_BQ),
    )(q, q, q, do, do, do, lse, lse, lse, delta, delta, delta, klt, vlt)


def make_attention(kind, has_sink):
    has_lat = kind != "ctx"

    def unpack(args):
        it = iter(args)
        q, kc, vc = next(it), next(it), next(it)
        kl, vl = (next(it), next(it)) if has_lat else (None, None)
        sink = next(it) if has_sink else None
        return q, kl, vl, kc, vc, sink

    def forward(args):
        q, kl, vl, kc, vc, sink = unpack(args)
        klt = jnp.swapaxes(kl, 1, 2) if has_lat else None
        if kind == "window":
            return _band_fwd(q, klt, vl, jnp.swapaxes(kc, 1, 2), vc, sink)
        return _flash_fwd(q, klt, vl, jnp.swapaxes(kc, 1, 2), vc, sink)

    @jax.custom_vjp
    def op(*args):
        return forward(args)[0]

    def fwd(*args):
        o, lse = forward(args)
        return o, (args, o, lse)

    def bwd(res, do):
        args, o, lse = res
        q, kl, vl, kc, vc, sink = unpack(args)
        if kind == "global":
            dq, delta, dkl, dvl = _flash_bwd_full(q, kl, vl, kc, vc, o, do, lse)
            grads = [dq, *_flash_dkv(q, do, lse, delta, kc, vc), dkl, dvl]
        elif kind == "window":
            klt, vlt = jnp.swapaxes(kl, 1, 2), jnp.swapaxes(vl, 1, 2)
            dq, delta, dsink_rows = _band_dq(q, kl, klt, vlt, kc, jnp.swapaxes(kc, 1, 2),
                                             jnp.swapaxes(vc, 1, 2), sink, o, do, lse)
            grads = [dq, *_flash_dkv(q, do, lse, delta, kc, vc), *_band_dkv(q, do, lse, delta, klt, vlt)]
        else:
            dq, delta, dsink_rows = _ctx_dq(q, kc, vc, sink, o, do, lse)
            grads = [dq, *_flash_dkv(q, do, lse, delta, kc, vc)]
        if has_sink:
            grads.append(jnp.sum(dsink_rows, axis=(2, 3)).reshape(-1))
        return tuple(grads)

    op.defvjp(fwd, bwd)
    return op


attn_window = make_attention("window", True)
attn_global = make_attention("global", False)
attn_ctx_sink = make_attention("ctx", True)
attn_ctx = make_attention("ctx", False)


def _to_heads(z, n_heads):
    return z.reshape(z.shape[0], n_heads, HEAD_DIM).transpose(1, 0, 2)


def _q_heads(z):
    return z.reshape(z.shape[0], KV_HEADS, Q_PER_KV, HEAD_DIM).transpose(1, 2, 0, 3)


def _from_q_heads(o):
    return o.transpose(2, 0, 1, 3).reshape(o.shape[2], KV_HEADS * Q_PER_KV * HEAD_DIM)


SCAN_PAIRS = 4


def _scan_tile(t):
    return _tile(t, 512, 2 * SCAN_PAIRS)


def _scan_fwd_call(bre, bim, lre, lim, h0re, h0im, rev):
    t = bre.shape[0]
    tt = _scan_tile(t)
    nb = t // tt
    plane = bre.shape[1:]

    def body(bre_ref, bim_ref, lre_ref, lim_ref, h0re_ref, h0im_ref, sre_ref, sim_ref, h_ref):
        @pl.when(pl.program_id(0) == 0)
        def _():
            h_ref[0] = h0re_ref[...]
            h_ref[1] = h0im_ref[...]

        ar, ai = lre_ref[...], lim_ref[...]
        a2r, a2i = ar * ar - ai * ai, 2.0 * ar * ai

        def step(j, carry):
            hr, hi = carry
            for u in range(SCAN_PAIRS):
                t1 = (tt - 1 - 2 * (SCAN_PAIRS * j + u)) if rev else 2 * (SCAN_PAIRS * j + u)
                t2 = (t1 - 1) if rev else (t1 + 1)
                b1r, b1i, b2r, b2i = bre_ref[t1], bim_ref[t1], bre_ref[t2], bim_ref[t2]
                er = ar * b1r - ai * b1i + b2r
                ei = ar * b1i + ai * b1r + b2i
                sre_ref[t1] = ar * hr - ai * hi + b1r
                sim_ref[t1] = ar * hi + ai * hr + b1i
                hr, hi = a2r * hr - a2i * hi + er, a2r * hi + a2i * hr + ei
                sre_ref[t2] = hr
                sim_ref[t2] = hi
            return hr, hi

        hr, hi = lax.fori_loop(0, tt // (2 * SCAN_PAIRS), step, (h_ref[0], h_ref[1]))
        h_ref[0] = hr
        h_ref[1] = hi

    blk = pl.BlockSpec((tt,) + plane, (lambda i: (nb - 1 - i, 0, 0)) if rev else (lambda i: (i, 0, 0)))
    par = pl.BlockSpec(plane, lambda i: (0, 0))
    return pl.pallas_call(
        body, name="s5_scan_fwd", grid=(nb,), in_specs=[blk, blk, par, par, par, par], out_specs=[blk, blk],
        out_shape=[jax.ShapeDtypeStruct(bre.shape, F32)] * 2,
        scratch_shapes=[pltpu.VMEM((2,) + plane, F32)],
        compiler_params=_cparams(("arbitrary",), 4 * 4 * tt * plane[0] * plane[1]),
    )(bre, bim, lre, lim, h0re, h0im)


def _scan_bwd_call(gre, gim, sre, sim, lre, lim, h0re, h0im, rev):
    t = gre.shape[0]
    tt = _scan_tile(t)
    nb = t // tt
    plane = gre.shape[1:]
    down = not rev

    def body(gre_ref, gim_ref, sre_ref, sim_ref, lre_ref, lim_ref, h0re_ref, h0im_ref,
             dbre_ref, dbim_ref, dare_ref, daim_ref, dh0re_ref, dh0im_ref, carry_ref):
        i = pl.program_id(0)

        @pl.when(i == 0)
        def _():
            carry_ref[...] = jnp.zeros_like(carry_ref)

        ar, ai = lre_ref[...], lim_ref[...]
        a2r, a2i = ar * ar - ai * ai, 2.0 * ar * ai

        def step(j, carry):
            gr, gi, dar, dai = carry
            for u in range(SCAN_PAIRS):
                t1 = (tt - 1 - 2 * (SCAN_PAIRS * j + u)) if down else 2 * (SCAN_PAIRS * j + u)
                t2 = (t1 - 1) if down else (t1 + 1)
                h1r, h1i, h2r, h2i = sre_ref[t1], sim_ref[t1], sre_ref[t2], sim_ref[t2]
                c1r, c1i, c2r, c2i = gre_ref[t1], gim_ref[t1], gre_ref[t2], gim_ref[t2]
                g1r = c1r + ar * gr + ai * gi
                g1i = c1i + ar * gi - ai * gr
                er = c2r + ar * c1r + ai * c1i
                ei = c2i + ar * c1i - ai * c1r
                dar = dar + ((h1r * gr + h1i * gi) + (h2r * g1r + h2i * g1i))
                dai = dai + ((h1r * gi - h1i * gr) + (h2r * g1i - h2i * g1r))
                gr, gi = er + a2r * gr + a2i * gi, ei + a2r * gi - a2i * gr
                dbre_ref[t1] = g1r
                dbim_ref[t1] = g1i
                dbre_ref[t2] = gr
                dbim_ref[t2] = gi
            return gr, gi, dar, dai

        gr, gi, dar, dai = lax.fori_loop(
            0, tt // (2 * SCAN_PAIRS), step, (carry_ref[0], carry_ref[1], carry_ref[2], carry_ref[3]))
        carry_ref[0] = gr
        carry_ref[1] = gi
        carry_ref[2] = dar
        carry_ref[3] = dai

        @pl.when(i == nb - 1)
        def _():
            hr, hi = h0re_ref[...], h0im_ref[...]
            dare_ref[...] = dar + hr * gr + hi * gi
            daim_ref[...] = dai + hr * gi - hi * gr
            dh0re_ref[...] = ar * gr + ai * gi
            dh0im_ref[...] = ar * gi - ai * gr

    blk = pl.BlockSpec((tt,) + plane, (lambda i: (nb - 1 - i, 0, 0)) if down else (lambda i: (i, 0, 0)))
    par = pl.BlockSpec(plane, lambda i: (0, 0))
    return pl.pallas_call(
        body, name="s5_scan_bwd", grid=(nb,), in_specs=[blk, blk, blk, blk, par, par, par, par],
        out_specs=[blk, blk, par, par, par, par],
        out_shape=[jax.ShapeDtypeStruct(gre.shape, F32)] * 2 + [jax.ShapeDtypeStruct(plane, F32)] * 4,
        scratch_shapes=[pltpu.VMEM((4,) + plane, F32)],
        compiler_params=_cparams(("arbitrary",), 4 * 6 * tt * plane[0] * plane[1]),
    )(gre, gim, sre, sim, lre, lim, h0re, h0im)


def make_scan(rev):
    @jax.custom_vjp
    def op(bre, bim, lre, lim, h0re, h0im):
        return tuple(_scan_fwd_call(bre, bim, lre, lim, h0re, h0im, rev))

    def fwd(bre, bim, lre, lim, h0re, h0im):
        sre, sim = _scan_fwd_call(bre, bim, lre, lim, h0re, h0im, rev)
        return (sre, sim), (sre, sim, lre, lim, h0re, h0im)

    def bwd(res, cts):
        sre, sim, lre, lim, h0re, h0im = res
        return tuple(_scan_bwd_call(cts[0], cts[1], sre, sim, lre, lim, h0re, h0im, rev))

    op.defvjp(fwd, bwd)
    return op


scan_up = make_scan(False)
scan_down = make_scan(True)


def _adamw_call(parts, w, m, v, name):
    r, c = w.shape
    tr = _tile(r, 256, SUBLANES)
    nparts = parts.shape[0]
    c1 = 1.0 - ADAM_B1 ** ADAM_STEP
    c2 = 1.0 - ADAM_B2 ** ADAM_STEP

    def body(p_ref, w_ref, m_ref, v_ref, g_ref, d_ref, nm_ref, nv_ref):
        g = p_ref[0].astype(F32)
        for s in range(1, nparts):
            g = g + p_ref[s].astype(F32)
        m1 = ADAM_B1 * m_ref[...] + (1.0 - ADAM_B1) * g
        v1 = ADAM_B2 * v_ref[...] + (1.0 - ADAM_B2) * jnp.square(g)
        g_ref[...] = g
        nm_ref[...] = m1
        nv_ref[...] = v1
        d_ref[...] = -ADAM_LR * ((m1 / c1) / (jnp.sqrt(v1 / c2) + ADAM_EPS) + ADAM_WD * w_ref[...])

    blk = pl.BlockSpec((tr, c), lambda i: (i, 0))
    return pl.pallas_call(
        body, name=name, grid=(r // tr,),
        in_specs=[pl.BlockSpec((nparts, tr, c), lambda i: (0, i, 0)), blk, blk, blk], out_specs=[blk] * 4,
        out_shape=[jax.ShapeDtypeStruct((r, c), F32)] * 4,
        compiler_params=_cparams(("parallel",), 4 * tr * c * (nparts + 7)),
    )(parts, w, m, v)


def _peer(k):
    x, y, c = lax.axis_index("x"), lax.axis_index("y"), lax.axis_index("c")
    px = 1 - x if k & 4 else x
    py = 1 - y if k & 2 else y
    pc = 1 - c if k & 1 else c
    return (px, py, pc), 4 * px + 2 * py + pc


def _my_slot():
    return 4 * lax.axis_index("x") + 2 * lax.axis_index("y") + lax.axis_index("c")


def all_gather(x, name):
    def body(x_ref, out_ref, send_sems, recv_sems, local_sem):
        me = _my_slot()
        mine = pltpu.make_async_copy(x_ref, out_ref.at[me], local_sem)
        mine.start()
        sends = []
        for k in range(1, N_DEV):
            peer, _ = _peer(k)
            cp = pltpu.make_async_remote_copy(
                src_ref=x_ref, dst_ref=out_ref.at[me], send_sem=send_sems.at[k - 1],
                recv_sem=recv_sems.at[k - 1], device_id=peer, device_id_type=MESH)
            cp.start()
            sends.append(cp)
        for k in range(1, N_DEV):
            peer, slot = _peer(k)
            pltpu.make_async_remote_copy(
                src_ref=x_ref, dst_ref=out_ref.at[slot], send_sem=send_sems.at[k - 1],
                recv_sem=recv_sems.at[k - 1], device_id=peer, device_id_type=MESH).wait_recv()
        for cp in sends:
            cp.wait_send()
        mine.wait()

    return pl.pallas_call(
        body, name=name,
        in_specs=[pl.BlockSpec(memory_space=pl.ANY)], out_specs=pl.BlockSpec(memory_space=pl.ANY),
        out_shape=jax.ShapeDtypeStruct((N_DEV,) + tuple(x.shape), x.dtype),
        scratch_shapes=[pltpu.SemaphoreType.DMA((N_DEV - 1,)), pltpu.SemaphoreType.DMA((N_DEV - 1,)),
                        pltpu.SemaphoreType.DMA],
    )(x)


def sibling_swap(x, name):
    def body(x_ref, out_ref, send_sem, recv_sem):
        x_, y_, c_ = lax.axis_index("x"), lax.axis_index("y"), lax.axis_index("c")
        cp = pltpu.make_async_remote_copy(
            src_ref=x_ref.at[1 - c_], dst_ref=out_ref, send_sem=send_sem, recv_sem=recv_sem,
            device_id=(x_, y_, 1 - c_), device_id_type=MESH)
        cp.start()
        cp.wait()

    return pl.pallas_call(
        body, name=name,
        in_specs=[pl.BlockSpec(memory_space=pl.ANY)], out_specs=pl.BlockSpec(memory_space=pl.ANY),
        out_shape=jax.ShapeDtypeStruct(x.shape[1:], x.dtype),
        scratch_shapes=[pltpu.SemaphoreType.DMA, pltpu.SemaphoreType.DMA],
    )(x)


def chip_exchange(x, name):
    def body(x_ref, out_ref, send_sems, recv_sems, local_sem):
        x_, y_, c_ = lax.axis_index("x"), lax.axis_index("y"), lax.axis_index("c")
        mine = 2 * x_ + y_
        local = pltpu.make_async_copy(x_ref.at[mine], out_ref.at[mine], local_sem)
        local.start()

        def copy(k):
            px = 1 - x_ if k & 2 else x_
            py = 1 - y_ if k & 1 else y_
            peer = 2 * px + py
            send = pltpu.make_async_remote_copy(
                src_ref=x_ref.at[peer], dst_ref=out_ref.at[mine], send_sem=send_sems.at[k - 1],
                recv_sem=recv_sems.at[k - 1], device_id=(px, py, c_), device_id_type=MESH)
            recv = pltpu.make_async_remote_copy(
                src_ref=x_ref.at[peer], dst_ref=out_ref.at[peer], send_sem=send_sems.at[k - 1],
                recv_sem=recv_sems.at[k - 1], device_id=(px, py, c_), device_id_type=MESH)
            return send, recv

        copies = [copy(k) for k in range(1, 4)]
        for send, _ in copies:
            send.start()
        for _, recv in copies:
            recv.wait_recv()
        for send, _ in copies:
            send.wait_send()
        local.wait()

    return pl.pallas_call(
        body, name=name,
        in_specs=[pl.BlockSpec(memory_space=pl.ANY)], out_specs=pl.BlockSpec(memory_space=pl.ANY),
        out_shape=jax.ShapeDtypeStruct(x.shape, x.dtype),
        scratch_shapes=[pltpu.SemaphoreType.DMA((3,)), pltpu.SemaphoreType.DMA((3,)),
                        pltpu.SemaphoreType.DMA],
    )(x)


def all_gather_two_level(x, name):
    def body(x_ref, out_ref, send_sems, recv_sems, local_sem):
        x_, y_, c_ = lax.axis_index("x"), lax.axis_index("y"), lax.axis_index("c")
        me, sibling = (x_, y_, c_), (x_, y_, 1 - c_)
        chips = [(1 - x_, y_), (x_, 1 - y_), (1 - x_, 1 - y_)]

        def slot(px, py, pc):
            return out_ref.at[4 * px + 2 * py + pc]

        def copy(k, block, to, src=None):
            return pltpu.make_async_remote_copy(
                src_ref=slot(*block) if src is None else src, dst_ref=slot(*block),
                send_sem=send_sems.at[k], recv_sem=recv_sems.at[k], device_id=to, device_id_type=MESH)

        mine = pltpu.make_async_copy(x_ref, slot(*me), local_sem)
        mine.start()
        first = [copy(0, me, sibling, src=x_ref)]
        first += [copy(1 + j, me, (*chip, c_), src=x_ref) for j, chip in enumerate(chips)]
        for cp in first:
            cp.start()
        passed = [copy(4 + j, (*chip, c_), sibling) for j, chip in enumerate(chips)]
        for j, chip in enumerate(chips):
            copy(1 + j, (*chip, c_), me).wait_recv()
            passed[j].start()
        copy(0, sibling, me).wait_recv()
        for j, chip in enumerate(chips):
            copy(4 + j, (*chip, 1 - c_), me).wait_recv()
        for cp in first + passed:
            cp.wait_send()
        mine.wait()

    return pl.pallas_call(
        body, name=name,
        in_specs=[pl.BlockSpec(memory_space=pl.ANY)], out_specs=pl.BlockSpec(memory_space=pl.ANY),
        out_shape=jax.ShapeDtypeStruct((N_DEV,) + tuple(x.shape), x.dtype),
        scratch_shapes=[pltpu.SemaphoreType.DMA((N_DEV - 1,)), pltpu.SemaphoreType.DMA((N_DEV - 1,)),
                        pltpu.SemaphoreType.DMA],
    )(x)


def _rope_tables(t):
    n_freq = HEAD_DIM // 4
    tok = jnp.arange(t)
    inv = ROPE_THETA ** (-jnp.arange(n_freq, dtype=F32) / n_freq)
    a_row = (tok // GRID_W).astype(F32)[:, None] * inv
    a_col = (tok % GRID_W).astype(F32)[:, None] * inv
    cos = jnp.concatenate([jnp.cos(a_row)] * 2 + [jnp.cos(a_col)] * 2, axis=1)
    sin = jnp.concatenate([-jnp.sin(a_row), jnp.sin(a_row), -jnp.sin(a_col), jnp.sin(a_col)], axis=1)
    return jnp.concatenate([cos, cos], axis=1), jnp.concatenate([sin, sin], axis=1)


def _block_diag(blocks):
    g, a, b = blocks.shape
    eye = jnp.eye(g, dtype=blocks.dtype)
    return jnp.einsum("gab,gk->gakb", blocks, eye).reshape(g * a, g * b)


def _ffn_fwd_calls(x, mod, g, w_in, w_out):
    shift, scale, gate = mod[0:1], mod[1:2], mod[2:3]
    h, = _row_call(lambda xt, gt, sh, sc: (_norm_mod_fn(xt, gt, sh, sc)[0].astype(BF16),),
                   [x], [g, shift, scale], 0, "ffn_norm")
    u = _mm(h, w_in, out_dtype=BF16, name="ffn_up")
    a, = _row_call(lambda ut: (_swiglu_fn(ut.astype(F32))[0].astype(BF16),), [u], [], 0, "ffn_act")
    y = _mm(a, w_out, name="ffn_down")
    out, = _row_call(functools.partial(_resid_fn, 0.5), [x, y], [gate], 0, "ffn_resid")
    return out, (h, u, a, y)


@jax.custom_vjp
def _ffn_half(x, mod, g, w_in, w_out):
    return _ffn_fwd_calls(x, mod, g, w_in, w_out)[0]


def _ffn_half_fwd(x, mod, g, w_in, w_out):
    out, saved = _ffn_fwd_calls(x, mod, g, w_in, w_out)
    return out, (x, mod, g, w_in, w_out, saved)


def _ffn_half_bwd(res, dxn):
    x, mod, g, w_in, w_out, (h, u, a, y) = res
    shift, scale, gate = mod[0:1], mod[1:2], mod[2:3]

    def resid_bwd(dt, yt, gt):
        return (0.5 * gt * dt).astype(BF16), 0.5 * jnp.sum(dt * yt, axis=0, keepdims=True)

    dy, dgate = _row_call(resid_bwd, [dxn, y], [gate], 1, "ffn_resid_bwd")
    da = _mm(dy, w_out, tb=True, out_dtype=BF16, name="ffn_down_dx")
    dw_out = _mm(a, dy, ta=True, out_dtype=w_out.dtype, name="ffn_down_dw")

    def act_bwd(ut, dat):
        return (jax.vjp(_swiglu_fn, ut.astype(F32))[1]((dat.astype(F32),))[0].astype(BF16),)

    du, = _row_call(act_bwd, [u, da], [], 0, "ffn_act_bwd")
    dh = _mm(du, w_in, tb=True, name="ffn_up_dx")
    dw_in = _mm(h, du, ta=True, out_dtype=w_in.dtype, name="ffn_up_dw")

    def norm_bwd(xt, dht, dt, gt, sh, sc):
        dx, dg, dsh, dsc = jax.vjp(_norm_mod_fn, xt, gt, sh, sc)[1]((dht,))
        return dx + dt, dg, dsh, dsc

    dx, dg, dshift, dscale = _row_call(norm_bwd, [x, dh, dxn], [g, shift, scale], 3, "ffn_norm_bwd")
    return dx, jnp.concatenate([dshift, dscale, dgate], axis=0), dg, dw_in, dw_out


_ffn_half.defvjp(_ffn_half_fwd, _ffn_half_bwd)


def _s5_discretize(a_re, a_im, log_dt, b_re, b_im):
    lam = lax.complex(a_re, a_im)
    dt = jnp.exp(log_dt)[:, None]
    lam_bar = jnp.exp(lam * dt)
    b_bar = ((lam_bar - 1.0) / lam)[..., None] * lax.complex(b_re, b_im)
    return lam_bar, b_bar


def _s5_branch(u_lat, u_ctx, w, l, with_ctx_out):
    zero = jnp.zeros((SUBLANES, SSM_LANES // SUBLANES), F32)
    lat_terms, ctx_terms = [], []
    for d, scan in enumerate((scan_up, scan_down)):
        lam_bar, b_bar = _s5_discretize(w["ssm_a_re"][l, d], w["ssm_a_im"][l, d], w["ssm_log_dt"][l, d],
                                        w["ssm_b_re"][l, d], w["ssm_b_im"][l, d])
        lre = jnp.real(lam_bar).reshape(zero.shape)
        lim = jnp.imag(lam_bar).reshape(zero.shape)
        b_t = jnp.swapaxes(b_bar, 1, 2)
        b_re, b_im = _block_diag(jnp.real(b_t)), _block_diag(jnp.imag(b_t))
        c_re = _block_diag(jnp.swapaxes(w["ssm_c_re"][l, d], 1, 2))
        c_im = _block_diag(jnp.swapaxes(w["ssm_c_im"][l, d], 1, 2))
        sc_re, sc_im = scan(linear_to_planes(u_ctx, b_re), linear_to_planes(u_ctx, b_im), lre, lim, zero, zero)
        last = 0 if d == 1 else u_ctx.shape[0] - 1
        sl_re, sl_im = scan(linear_to_planes(u_lat, b_re), linear_to_planes(u_lat, b_im), lre, lim,
                            sc_re[last], sc_im[last])
        lat_terms += [linear_from_planes(sl_re, c_re), linear_from_planes(sl_im, c_im)]
        if with_ctx_out:
            ctx_terms += [linear_from_planes(sc_re, c_re), linear_from_planes(sc_im, c_im)]
    d_skip = w["ssm_d"][l][None, :]

    def out(terms, u):
        y, = s5_pre(*terms, u, d_skip)
        return glu(linear(y, w["glu_w"][l]))[0]

    return out(lat_terms, u_lat), (out(ctx_terms, u_ctx) if with_ctx_out else None)


def _pool_branch(xa, w, l):
    y = linear(pool_diff(xa), _block_diag(w["pool_w"][l]))
    return scale_rows(y, w["pool_scale"][l][None, :])[0]


_CTX_GROUPS = (O_VB - O_KB, O_UC - O_VB, O_KD - O_UC, O_VD - O_KD, CTX_COLS - O_VD)
_ALL_GROUPS = _CTX_GROUPS + (O_QD - O_QB, O_XA - O_QD, O_GATE - O_XA)
project_ctx = make_split_linear(_CTX_GROUPS)


def make_gated_projection(widths):
    @jax.custom_vjp
    def op(x, w, w_gate):
        return (*_split_fwd(x, w, widths), _mm(x, w_gate, out_dtype=BF16, name="gate_fwd"))

    def fwd(x, w, w_gate):
        return op(x, w, w_gate), (x, w, w_gate)

    def bwd(res, cts):
        x, w, w_gate = res
        *d_groups, d_gate = cts
        dx = _mm(d_gate, w_gate, tb=True, add=_split_dx(d_groups, w), name="gate_dx")
        return (dx, _split_dw(x, d_groups, w.dtype),
                _mm(x, d_gate, ta=True, out_dtype=w_gate.dtype, name="gate_dw"))

    op.defvjp(fwd, bwd)
    return op


project_all = make_gated_projection(_ALL_GROUPS)


def _merge_branches(branches, gate_logits, w, l):
    zs = [linear_b(y, w["branch_w"][l, k]) for k, y in enumerate(branches)]
    return linear(merge(gate_logits, *zs)[0], w["out_w"][l])


def _token_mixer(h, hc, cos, sin, w, l, with_ctx_out):
    w_in, w_gate = w["w_in"][l][:, :O_GATE], w["w_in"][l][:, O_GATE:]
    kb, vb, uc, kd, vd, qb, qd, xa, p_gate = project_all(h, w_in, w_gate)
    if with_ctx_out:
        kb_c, vb_c, uc_c, kd_c, vd_c, qb_c, qd_c, xa_c, pc_gate = project_all(hc, w_in, w_gate)
    else:
        kb_c, vb_c, uc_c, kd_c, vd_c = project_ctx(hc, w_in[:, :CTX_COLS])
    sink = w["win_sink"][l]
    q_g = jnp.tile(w["qk_norm"][l, 0], KV_HEADS * Q_PER_KV)[None, :]
    k_g = jnp.tile(w["qk_norm"][l, 1], KV_HEADS)[None, :]
    k_win_c = _to_heads(kb_c, KV_HEADS)
    v_win_c = _to_heads(vb_c, KV_HEADS)
    k_glb_c = _to_heads(head_norm(kd_c, k_g)[0], KV_HEADS)
    v_glb_c = _to_heads(vd_c, KV_HEADS)
    y_a = _pool_branch(xa, w, l)
    q_win = _q_heads(rope(qb, cos, sin)[0])
    k_win = _to_heads(rope(kb, cos, sin)[0], KV_HEADS)
    v_win = _to_heads(vb, KV_HEADS)
    y_b = _from_q_heads(attn_window(q_win, k_win_c, v_win_c, k_win, v_win, sink))
    y_c, y_c_ctx = _s5_branch(uc, uc_c, w, l, with_ctx_out)
    q_glb = _q_heads(norm_rope(qd, cos, sin, q_g)[0])
    k_glb = _to_heads(norm_rope(kd, cos, sin, k_g)[0], KV_HEADS)
    v_glb = _to_heads(vd, KV_HEADS)
    y_d = _from_q_heads(attn_global(q_glb, k_glb_c, v_glb_c, k_glb, v_glb))
    y = _merge_branches((y_a, y_b, y_c, y_d), p_gate, w, l)
    if not with_ctx_out:
        return y, None
    y_a_c = _pool_branch(xa_c, w, l)
    y_b_c = _from_q_heads(attn_ctx_sink(_q_heads(qb_c), k_win_c, v_win_c, sink))
    q_glb_c = _q_heads(head_norm(qd_c, q_g)[0])
    y_d_c = _from_q_heads(attn_ctx(q_glb_c, k_glb_c, v_glb_c))
    return y, _merge_branches((y_a_c, y_b_c, y_c_ctx, y_d_c), pc_gate, w, l)


def local_loss(w, x, c, ctx, target):
    depth = w["w_mod"].shape[0]
    cos, sin = _rope_tables(x.shape[0])
    cond = jnp.concatenate([c, w["c_ctx"][None, :], jnp.zeros((COND_ROWS - 2, D_MODEL), F32)], axis=0)
    s_all, = silu_rows(cond)
    for l in range(depth):
        last = l == depth - 1
        m_all = (linear(s_all, w["w_mod"][l]) + w["b_mod"][l][None, :]).reshape(COND_ROWS, N_SUB, 3, D_MODEL)
        m, mc = m_all[0], m_all[1]
        g = w["norm_g"][l][:, None, :]
        x = _ffn_half(x, m[0], g[0], w["ffn_in"][l, 0], w["ffn_out"][l, 0])
        ctx = _ffn_half(ctx, mc[0], g[0], w["ffn_in"][l, 0], w["ffn_out"][l, 0])
        h, = norm_mod(x, g[1], m[1, 0:1], m[1, 1:2])
        hc, = norm_mod(ctx, g[1], mc[1, 0:1], mc[1, 1:2])
        y, y_ctx = _token_mixer(h, hc, cos, sin, w, l, not last)
        x, = resid_full(x, y, m[1, 2:3])
        if not last:
            ctx, = resid_full(ctx, y_ctx, mc[1, 2:3])
        x = _ffn_half(x, m[2], g[2], w["ffn_in"][l, 1], w["ffn_out"][l, 1])
        if not last:
            ctx = _ffn_half(ctx, mc[2], g[2], w["ffn_in"][l, 1], w["ffn_out"][l, 1])
    return jnp.sum(loss_rows(x, target, w["final_g"][None, :])[0])


PACK_COLS = 1024


def _pack(arrays):
    flat = jnp.concatenate([a.reshape(-1) for a in arrays])
    pad = (-flat.shape[0]) % (PACK_COLS * 16)
    return jnp.pad(flat, (0, pad)).reshape(-1, PACK_COLS)


def _unpack(slab, shapes):
    out, off = [], 0
    for s in shapes:
        n = math.prod(s)
        r0, r1 = off // PACK_COLS, -(-(off + n) // PACK_COLS)
        rows, start = slab[r0:r1], off - r0 * PACK_COLS
        if start == 0 and n == (r1 - r0) * PACK_COLS:
            out.append(rows.reshape(s))
        else:
            out.append(rows.reshape(-1)[start:start + n].reshape(s))
        off += n
    return out


def _full_from_shards(gathered, shard_shape, axis):
    z = jnp.moveaxis(gathered.reshape((N_DEV,) + tuple(shard_shape)), 0, axis)
    shape = list(shard_shape)
    shape[axis] *= N_DEV
    return z.reshape(shape)


def _shards_from_full(full, axis):
    shape = list(full.shape)
    shape[axis:axis + 1] = [N_DEV, shape[axis] // N_DEV]
    return jnp.moveaxis(full.reshape(shape), axis, 0)


def kernel(x, c, ctx, c_ctx, w_mod, b_mod, norm_g, ffn_in, ffn_out, w_in, win_sink, qk_norm, pool_w, pool_scale, ssm_a_re, ssm_a_im, ssm_log_dt, ssm_b_re, ssm_b_im, ssm_c_re, ssm_c_im, ssm_d, glu_w, branch_w, out_w, final_g, loss_target, m_c_ctx, m_w_mod, m_b_mod, m_norm_g, m_ffn_in, m_ffn_out, m_w_in, m_win_sink, m_qk_norm, m_pool_w, m_pool_scale, m_ssm_a_re, m_ssm_a_im, m_ssm_log_dt, m_ssm_b_re, m_ssm_b_im, m_ssm_c_re, m_ssm_c_im, m_ssm_d, m_glu_w, m_branch_w, m_out_w, m_final_g, v_c_ctx, v_w_mod, v_b_mod, v_norm_g, v_ffn_in, v_ffn_out, v_w_in, v_win_sink, v_qk_norm, v_pool_w, v_pool_scale, v_ssm_a_re, v_ssm_a_im, v_ssm_log_dt, v_ssm_b_re, v_ssm_b_im, v_ssm_c_re, v_ssm_c_im, v_ssm_d, v_glu_w, v_branch_w, v_out_w, v_final_g):
    given = dict(locals())
    wts = {n: given[n] for n in WEIGHTS}
    mom = {n: given["m_" + n] for n in WEIGHTS}
    var = {n: given["v_" + n] for n in WEIGHTS}
    me = _my_slot()

    shard_shapes = [wts[n].shape for n in SHARDED]
    w_slab = _pack([wts[n] for n in SHARDED])
    gathered = all_gather_two_level(w_slab.astype(BF16), "gather_weights")
    full = dict(wts)
    row = 0
    for n in SHARDED:
        n_rows = math.prod(wts[n].shape) // PACK_COLS
        full[n] = _full_from_shards(gathered[:, row:row + n_rows], wts[n].shape, SHARD_AXIS[n])
        row += n_rows
    g_slab = _pack([norm_g])
    g_all = all_gather(g_slab, "gather_norm_g")
    full["norm_g"] = _full_from_shards(
        jnp.stack([_unpack(g_all[s], [norm_g.shape])[0] for s in range(N_DEV)]), norm_g.shape, 2)

    loss, (gw, gx) = jax.value_and_grad(local_loss, argnums=(0, 1))(full, x[0], c, ctx[0], loss_target[0])
    loss = lax.psum(loss, ("x", "y", "c"))

    dest = [_shards_from_full(gw[n], SHARD_AXIS[n]).reshape(N_DEV // 2, 2, -1, PACK_COLS) for n in SHARDED]
    send = jnp.swapaxes(jnp.concatenate(dest, axis=2), 0, 1)
    from_sibling = sibling_swap(send, "exchange_grads_sibling")
    own = lax.dynamic_index_in_dim(send, lax.axis_index("c"), axis=0, keepdims=False)
    pair, = _row_call(lambda a, b: ((a.astype(F32) + b.astype(F32)).astype(BF16),),
                      [own.reshape(-1, PACK_COLS), from_sibling.reshape(-1, PACK_COLS)], [], 0,
                      "exchange_pair_sum")
    big_parts = chip_exchange(pair.reshape(own.shape), "exchange_grads_chips")
    small_names = SMALL + ("norm_g",)
    small_shapes = [gw[n].shape for n in small_names]
    small_parts = all_gather(_pack([gw[n] for n in small_names]), "gather_small_grads")

    big = _adamw_call(big_parts, w_slab, _pack([mom[n] for n in SHARDED]), _pack([var[n] for n in SHARDED]),
                      "adamw_sharded")
    big = [_unpack(b, shard_shapes) for b in big]
    col = me * norm_g.shape[2]

    def small_slab(src, shard_src):
        padded = jnp.zeros((norm_g.shape[0], norm_g.shape[1], norm_g.shape[2] * N_DEV), F32)
        padded = lax.dynamic_update_slice(padded, shard_src, (0, 0, col))
        return _pack([src[n] for n in SMALL] + [padded])

    small = _adamw_call(small_parts, small_slab(wts, norm_g), small_slab(mom, m_norm_g),
                        small_slab(var, v_norm_g), "adamw_small")
    small = [_unpack(s, small_shapes) for s in small]

    outs = {}
    for kind in range(4):
        for i, n in enumerate(SHARDED):
            outs[(kind, n)] = big[kind][i]
        for i, n in enumerate(small_names):
            val = small[kind][i]
            if n == "norm_g":
                val = lax.dynamic_slice(val, (0, 0, col), norm_g.shape)
            outs[(kind, n)] = val
    return (loss, gx[None], *[outs[(k, n)] for k in range(4) for n in WEIGHTS])
```
